```python
import jax, jax.numpy as jnp
from jax import lax
import numpy as np

D_MODEL = 1024
BATCH = 8
SEQ = 2048
DEPTH = 1

N_META = 16
D_FF = 2816
RET_HEADS = 4
RET_DK = 256
RET_DV = 512
RET_CHUNK = 128
ATT_HEADS = 8
ATT_DH = 128
IDX_HEADS = 8
IDX_DH = 64
TOPK_MAX = 256
DSA_QBLOCK = 32
ROPE_THETA = 10000.0
EPS = 1e-6
RET_QK = RET_HEADS * RET_DK
RET_V = RET_HEADS * RET_DV
ATT_W = ATT_HEADS * ATT_DH
IDX_Q = IDX_HEADS * IDX_DH
IN_SPLITS = (RET_QK, RET_QK, RET_V, RET_V, ATT_W, ATT_W, ATT_W, IDX_Q, IDX_DH, IDX_HEADS, D_MODEL, D_MODEL)
D_IN = sum(IN_SPLITS)

kernel_name = 'hybrid_retention_dsa_macaron'


def rmsnorm(x, g):
    xf = x.astype(jnp.float32)
    y = xf * lax.rsqrt(jnp.mean(xf * xf, axis=-1, keepdims=True) + EPS)
    return (y * g.astype(jnp.float32)).astype(x.dtype)


def rope(x, pos):
    d = x.shape[-1]
    inv_freq = ROPE_THETA ** (-jnp.arange(0, d, 2, dtype=jnp.float32) / d)
    ang = pos.astype(jnp.float32)[:, None] * inv_freq[None, :]
    cos = jnp.cos(ang)[:, None, :]
    sin = jnp.sin(ang)[:, None, :]
    xf = x.astype(jnp.float32)
    x1, x2 = xf[..., : d // 2], xf[..., d // 2:]
    return jnp.concatenate([x1 * cos - x2 * sin, x2 * cos + x1 * sin], axis=-1).astype(x.dtype)


def swiglu(u, w_gate, w_up, w_down):
    return (jax.nn.silu(u @ w_gate) * (u @ w_up)) @ w_down


def retention(q, k, v):
    out_dtype = v.dtype
    q, k, v = (a.astype(jnp.float32) for a in (q, k, v))
    B, H, T, _ = q.shape
    C = RET_CHUNK
    n = (T - N_META) // C
    log_g = jnp.log1p(-(2.0 ** (-5.0 - jnp.arange(H, dtype=jnp.float32))))

    def decay_matrix(c):
        i = jnp.arange(c, dtype=jnp.float32)
        diff = i[:, None] - i[None, :]
        return jnp.where(diff >= 0, jnp.exp(log_g[:, None, None] * jnp.maximum(diff, 0.0)), 0.0)

    qm, km, vm = q[:, :, :N_META], k[:, :, :N_META], v[:, :, :N_META]
    y_meta = jnp.einsum('bhqk,bhke->bhqe', jnp.einsum('bhqd,bhkd->bhqk', qm, km) * decay_matrix(N_META), vm)
    zeta_m = jnp.exp(log_g[:, None] * (N_META - 1.0 - jnp.arange(N_META, dtype=jnp.float32)))
    state0 = jnp.einsum('bhkd,bhke->bhde', km * zeta_m[..., None], vm)

    d_in = decay_matrix(C)
    pos_c = jnp.arange(C, dtype=jnp.float32)
    xi = jnp.exp(log_g[:, None] * (pos_c + 1.0))[..., None]
    zeta = jnp.exp(log_g[:, None] * (C - 1.0 - pos_c))[..., None]
    g_chunk = jnp.exp(log_g * C)[:, None, None]

    def to_chunks(a):
        return a[:, :, N_META:].reshape(B, H, n, C, a.shape[-1]).transpose(2, 0, 1, 3, 4)

    def step(state, qkv):
        qc, kc, vc = qkv
        inner = jnp.einsum('bhqk,bhke->bhqe', jnp.einsum('bhqd,bhkd->bhqk', qc, kc) * d_in, vc)
        cross = jnp.einsum('bhqd,bhde->bhqe', qc, state) * xi
        state = g_chunk * state + jnp.einsum('bhkd,bhke->bhde', kc * zeta, vc)
        return state, inner + cross

    _, y_chunks = lax.scan(step, state0, (to_chunks(q), to_chunks(k), to_chunks(v)))
    y_real = y_chunks.transpose(1, 2, 0, 3, 4).reshape(B, H, n * C, v.shape[-1])
    y = jnp.concatenate([y_meta, y_real], axis=2)
    y = y * lax.rsqrt(jnp.mean(y * y, axis=-1, keepdims=True) + EPS)
    return y.astype(out_dtype)


def dsa_attention(q, k, v, qi, ki, wi, k_top):
    B, T, H, dh = q.shape
    nb = -(-T // DSA_QBLOCK)
    t_pad = nb * DSA_QBLOCK
    pad = t_pad - T

    def blocks(a):
        a = jnp.pad(a, [(0, 0), (0, pad)] + [(0, 0)] * (a.ndim - 2))
        return a.reshape((B, nb, DSA_QBLOCK) + a.shape[2:]).swapaxes(0, 1)

    pos_blocks = jnp.arange(t_pad, dtype=jnp.int32).reshape(nb, DSA_QBLOCK)
    key_pos = jnp.arange(T, dtype=jnp.int32)
    meta_pos = jnp.arange(N_META, dtype=jnp.int32)
    k_meta, v_meta = k[:, :N_META], v[:, :N_META]
    scale = ATT_DH ** -0.5
    idx_scale = (IDX_DH ** -0.5) * (IDX_HEADS ** -0.5)
    gather = jax.vmap(lambda a, i: a[i])

    def one_block(args):
        qb, qib, wib, tq = args
        rel = jax.nn.relu(jnp.einsum('bqhd,bsd->bqhs', qib, ki))
        score = jnp.einsum('bqh,bqhs->bqs', wib, rel).astype(jnp.float32) * idx_scale
        admissible = (key_pos[None, :] >= N_META) & (key_pos[None, :] <= tq[:, None])
        score = jnp.where(admissible[None], score, -jnp.inf)
        _, sel = lax.top_k(score, k_top)
        sel_ok = (sel >= N_META) & (sel <= tq[None, :, None])
        k_sel = gather(k, sel)
        v_sel = gather(v, sel)
        s_meta = jnp.einsum('bqhd,bmhd->bhqm', qb, k_meta).astype(jnp.float32) * scale
        s_meta = jnp.where((meta_pos[None, :] <= tq[:, None])[None, None], s_meta, -jnp.inf)
        s_sel = jnp.einsum('bqhd,bqkhd->bhqk', qb, k_sel).astype(jnp.float32) * scale
        s_sel = jnp.where(sel_ok[:, None], s_sel, -jnp.inf)
        p = jax.nn.softmax(jnp.concatenate([s_meta, s_sel], axis=-1), axis=-1).astype(v.dtype)
        return (jnp.einsum('bhqm,bmhd->bqhd', p[..., :N_META], v_meta)
                + jnp.einsum('bhqk,bqkhd->bqhd', p[..., N_META:], v_sel))

    out = lax.map(one_block, (blocks(q), blocks(qi), blocks(wi), pos_blocks))
    return out.swapaxes(0, 1).reshape(B, t_pad, H, dh)[:, :T]


def hybrid_mixer(u, pos, k_top, w_in, w_ret_out, w_att_out, w_mix_out):
    B, T, _ = u.shape
    proj = u @ w_in
    offs = [int(o) for o in np.cumsum(IN_SPLITS)[:-1]]
    rq, rk, rv, rg, aq, ak, av, iq, ik, iw, g_ret, g_att = jnp.split(proj, offs, axis=-1)
    rq = rope(rq.reshape(B, T, RET_HEADS, RET_DK), pos)
    rk = rope(rk.reshape(B, T, RET_HEADS, RET_DK), pos) * (RET_DK ** -0.5)
    rv = rv.reshape(B, T, RET_HEADS, RET_DV)
    yr = retention(rq.transpose(0, 2, 1, 3), rk.transpose(0, 2, 1, 3), rv.transpose(0, 2, 1, 3))
    yr = yr.transpose(0, 2, 1, 3).reshape(B, T, RET_V)
    yr = (jax.nn.silu(rg) * yr) @ w_ret_out
    aq = rope(aq.reshape(B, T, ATT_HEADS, ATT_DH), pos)
    ak = rope(ak.reshape(B, T, ATT_HEADS, ATT_DH), pos)
    av = av.reshape(B, T, ATT_HEADS, ATT_DH)
    iq = rope(iq.reshape(B, T, IDX_HEADS, IDX_DH), pos)
    ik = rope(ik.reshape(B, T, 1, IDX_DH), pos)[:, :, 0]
    ya = dsa_attention(aq, ak, av, iq, ik, iw, k_top).reshape(B, T, ATT_W) @ w_att_out
    merged = jax.nn.sigmoid(g_ret) * yr + jax.nn.sigmoid(g_att) * ya
    return merged @ w_mix_out


def setup_inputs(seed: int = 0) -> dict:
    key = jax.random.key(seed)
    ks = jax.random.split(key, 16)
    f32 = jnp.float32

    def dense(k, fan_in, fan_out):
        return jax.random.normal(k, (DEPTH, fan_in, fan_out), f32) * fan_in ** -0.5

    def gain(k, shape):
        return 1.0 + 0.02 * jax.random.normal(k, shape, f32)

    return {
        'x': jax.random.normal(ks[0], (BATCH, SEQ, D_MODEL), f32),
        'meta_tokens': jax.random.normal(ks[1], (N_META, D_MODEL), f32),
        'ffn1_norm': gain(ks[2], (DEPTH, D_MODEL)),
        'ffn1_w_gate': dense(ks[3], D_MODEL, D_FF),
        'ffn1_w_up': dense(ks[4], D_MODEL, D_FF),
        'ffn1_w_down': dense(ks[5], D_FF, D_MODEL),
        'mix_norm': gain(ks[6], (DEPTH, D_MODEL)),
        'w_in': dense(ks[7], D_MODEL, D_IN),
        'w_ret_out': dense(ks[8], RET_V, D_MODEL),
        'w_att_out': dense(ks[9], ATT_W, D_MODEL),
        'w_mix_out': dense(ks[10], D_MODEL, D_MODEL),
        'ffn2_norm': gain(ks[11], (DEPTH, D_MODEL)),
        'ffn2_w_gate': dense(ks[12], D_MODEL, D_FF),
        'ffn2_w_up': dense(ks[13], D_MODEL, D_FF),
        'ffn2_w_down': dense(ks[14], D_FF, D_MODEL),
        'final_norm': gain(ks[15], (D_MODEL,)),
    }


def reference(x, meta_tokens, ffn1_norm, ffn1_w_gate, ffn1_w_up, ffn1_w_down, mix_norm, w_in, w_ret_out, w_att_out, w_mix_out, ffn2_norm, ffn2_w_gate, ffn2_w_up, ffn2_w_down, final_norm):
    B, L, D = x.shape
    k_top = min(TOPK_MAX, L // 4)
    meta = jnp.broadcast_to(meta_tokens.astype(x.dtype)[None], (B, N_META, D))
    h = jnp.concatenate([meta, x], axis=1)
    pos = jnp.arange(h.shape[1], dtype=jnp.int32)
    for l in range(DEPTH):
        h = h + 0.5 * swiglu(rmsnorm(h, ffn1_norm[l]), ffn1_w_gate[l], ffn1_w_up[l], ffn1_w_down[l])
        h = h + hybrid_mixer(rmsnorm(h, mix_norm[l]), pos, k_top, w_in[l], w_ret_out[l], w_att_out[l], w_mix_out[l])
        h = h + 0.5 * swiglu(rmsnorm(h, ffn2_norm[l]), ffn2_w_gate[l], ffn2_w_up[l], ffn2_w_down[l])
    y = rmsnorm(h, final_norm)
    return y[:, N_META:]
```

```python
import functools

import jax
import jax.numpy as jnp
import numpy as np
from jax import lax
from jax.experimental import pallas as pl
from jax.experimental.pallas import tpu as pltpu

N_META = 16
RET_HEADS = 4
RET_DK = 256
RET_DV = 512
RET_CHUNK = 128
ATT_HEADS = 8
ATT_DH = 128
IDX_HEADS = 8
IDX_DH = 64
TOPK_MAX = 256
ROPE_THETA = 10000.0
EPS = 1e-6
RET_QK = RET_HEADS * RET_DK
RET_V = RET_HEADS * RET_DV
ATT_W = ATT_HEADS * ATT_DH
IDX_Q = IDX_HEADS * IDX_DH

LANES = 128
VMEM_LIMIT = 48 * 1024 * 1024

BF16 = jnp.bfloat16
F32 = jnp.float32
NT_DIMS = (((1,), (1,)), ((), ()))


def _dot(a, b):
    return jnp.dot(a, b, preferred_element_type=F32)


def _dot_nt(a, b):
    return lax.dot_general(a, b, NT_DIMS, preferred_element_type=F32)


def _rms(x, g):
    return x * lax.rsqrt(jnp.mean(x * x, axis=-1, keepdims=True) + EPS) * g


def _sigmoid(x):
    return 1.0 / (1.0 + jnp.exp(-x))


def _params(sem):
    return pltpu.CompilerParams(dimension_semantics=sem, vmem_limit_bytes=VMEM_LIMIT)


def _ffn_kernel(x_ref, g_ref, wg_ref, wu_ref, wd_ref, g2_ref, *rest, nf, emit_h):
    if emit_h:
        h_ref, n_ref, xn_sc, acc_sc = rest
    else:
        n_ref, xn_sc, acc_sc = rest
    f = pl.program_id(1)

    @pl.when(f == 0)
    def _():
        xn_sc[...] = _rms(x_ref[...], g_ref[...]).astype(BF16)
        acc_sc[...] = jnp.zeros_like(acc_sc)

    xn = xn_sc[...]
    gate = _dot(xn, wg_ref[...])
    up = _dot(xn, wu_ref[...])
    act = (gate * _sigmoid(gate)) * up
    acc_sc[...] += _dot(act.astype(BF16), wd_ref[...])

    @pl.when(f == nf - 1)
    def _():
        h = x_ref[...] + 0.5 * acc_sc[...]
        if emit_h:
            h_ref[...] = h
        n_ref[...] = _rms(h, g2_ref[...]).astype(n_ref.dtype)


def _ffn(x, g, wg, wu, wd, g2, *, tm, tf, emit_h, n_dtype):
    M, D = x.shape
    F = wg.shape[1]
    nf = F // tf
    out_shape = [jax.ShapeDtypeStruct((M, D), n_dtype)]
    out_specs = [pl.BlockSpec((tm, D), lambda i, f: (i, 0))]
    if emit_h:
        out_shape = [jax.ShapeDtypeStruct((M, D), F32)] + out_shape
        out_specs = [pl.BlockSpec((tm, D), lambda i, f: (i, 0))] + out_specs
    return pl.pallas_call(
        functools.partial(_ffn_kernel, nf=nf, emit_h=emit_h),
        grid=(M // tm, nf),
        in_specs=[
            pl.BlockSpec((tm, D), lambda i, f: (i, 0)),
            pl.BlockSpec((1, D), lambda i, f: (0, 0)),
            pl.BlockSpec((D, tf), lambda i, f: (0, f)),
            pl.BlockSpec((D, tf), lambda i, f: (0, f)),
            pl.BlockSpec((tf, D), lambda i, f: (f, 0)),
            pl.BlockSpec((1, D), lambda i, f: (0, 0)),
        ],
        out_specs=out_specs,
        out_shape=out_shape,
        scratch_shapes=[pltpu.VMEM((tm, D), BF16), pltpu.VMEM((tm, D), F32)],
        compiler_params=_params(("parallel", "arbitrary")),
        name="ffn",
    )(x, g, wg, wu, wd, g2)


def _rot_half(y, d):
    w = y.shape[-1]
    if d >= w:
        return jnp.concatenate([y[:, w // 2:], y[:, : w // 2]], axis=-1)
    if d == LANES:
        return pltpu.roll(y, d // 2, axis=1)
    fwd = pltpu.roll(y, d // 2, axis=1)
    bwd = pltpu.roll(y, w - d // 2, axis=1)
    lane = lax.broadcasted_iota(jnp.int32, y.shape, 1)
    return jnp.where((lane % d) < d // 2, bwd, fwd)


def _rope_tile(y, cos, sin, d):
    tw = cos.shape[-1]
    outs = []
    for t in range(y.shape[-1] // tw):
        yt = y[:, t * tw:(t + 1) * tw]
        outs.append(yt * cos + _rot_half(yt, d) * sin)
    return outs[0] if len(outs) == 1 else jnp.concatenate(outs, axis=-1)


def _proj_kernel(x_ref, w_ref, *rest, mode, d, scale):
    y = _dot(x_ref[...], w_ref[...])
    if mode == "rope":
        cos_ref, sin_ref, o_ref = rest
        y = _rope_tile(y, cos_ref[...], sin_ref[...], d)
        if scale != 1.0:
            y = y * scale
    else:
        (o_ref,) = rest
        if mode == "silu":
            y = y * _sigmoid(y)
        elif mode == "sigmoid":
            y = _sigmoid(y)
    o_ref[...] = y.astype(o_ref.dtype)


def _proj(x, w, *, tm, tn, mode, out_dtype, tabs=None, d=None, scale=1.0):
    M, K = x.shape
    N = w.shape[1]
    in_specs = [
        pl.BlockSpec((tm, K), lambda i, j: (i, 0)),
        pl.BlockSpec((K, tn), lambda i, j: (0, j)),
    ]
    args = [x, w]
    if mode == "rope":
        cos, sin = tabs
        npos = cos.shape[0] // tm
        tw = cos.shape[1]
        in_specs += [pl.BlockSpec((tm, tw), lambda i, j: (i % npos, 0))] * 2
        args += [cos, sin]
    return pl.pallas_call(
        functools.partial(_proj_kernel, mode=mode, d=d, scale=scale),
        grid=(M // tm, N // tn),
        in_specs=in_specs,
        out_specs=pl.BlockSpec((tm, tn), lambda i, j: (i, j)),
        out_shape=jax.ShapeDtypeStruct((M, N), out_dtype),
        compiler_params=_params(("parallel", "arbitrary")),
        name="proj_" + mode,
    )(*args)


def _idx_proj_kernel(x_ref, w_ref, cos_ref, sin_ref, q_ref, kw_ref):
    y = _dot(x_ref[...], w_ref[...])
    cos, sin = cos_ref[...], sin_ref[...]
    q_ref[...] = _rope_tile(y[:, :IDX_Q], cos, sin, IDX_DH)
    tail = y[:, IDX_Q:]
    lane = lax.broadcasted_iota(jnp.int32, tail.shape, 1)
    kw_ref[...] = jnp.where(lane < IDX_DH, _rope_tile(tail, cos, sin, IDX_DH), tail)


def _idx_proj(x, w, tabs, *, tm):
    M, K = x.shape
    N = w.shape[1]
    cos, sin = tabs
    npos = cos.shape[0] // tm
    return pl.pallas_call(
        _idx_proj_kernel,
        grid=(M // tm,),
        in_specs=[
            pl.BlockSpec((tm, K), lambda i: (i, 0)),
            pl.BlockSpec((K, N), lambda i: (0, 0)),
            pl.BlockSpec((tm, LANES), lambda i: (i % npos, 0)),
            pl.BlockSpec((tm, LANES), lambda i: (i % npos, 0)),
        ],
        out_specs=[
            pl.BlockSpec((tm, IDX_Q), lambda i: (i, 0)),
            pl.BlockSpec((tm, LANES), lambda i: (i, 0)),
        ],
        out_shape=[
            jax.ShapeDtypeStruct((M, IDX_Q), F32),
            jax.ShapeDtypeStruct((M, LANES), F32),
        ],
        compiler_params=_params(("parallel",)),
        name="proj_idx",
    )(x, w, cos, sin)


def _ret_kernel(lg_ref, gc_ref, q_ref, k_ref, v_ref, g_ref, km_ref, vm_ref, o_ref, st_ref, *, C):
    h = pl.program_id(1)
    c = pl.program_id(2)
    lg = lg_ref[h]

    @pl.when(c == 0)
    def _():
        pm = lax.broadcasted_iota(jnp.int32, (N_META, 1), 0).astype(F32)
        zeta_m = jnp.exp(lg * (N_META - 1.0 - pm))
        kz = km_ref[...].astype(F32) * zeta_m
        st_ref[...] = _dot(kz.T.astype(BF16), vm_ref[...])

    q = q_ref[0]
    k = k_ref[0]
    v = v_ref[0]
    ri = lax.broadcasted_iota(jnp.int32, (C, C), 0)
    ci = lax.broadcasted_iota(jnp.int32, (C, C), 1)
    diff = (ri - ci).astype(F32)
    decay = jnp.where(diff >= 0, jnp.exp(lg * jnp.maximum(diff, 0.0)), 0.0)
    pos = lax.broadcasted_iota(jnp.int32, (C, 1), 0).astype(F32)
    xi = jnp.exp(lg * (pos + 1.0))
    zeta = jnp.exp(lg * (C - 1.0 - pos))

    state = st_ref[...]
    inner = _dot((_dot_nt(q, k) * decay).astype(BF16), v)
    cross = _dot(q, state.astype(BF16)) * xi
    y = inner + cross
    y = y * lax.rsqrt(jnp.mean(y * y, axis=-1, keepdims=True) + EPS)
    o_ref[0] = (g_ref[0] * y).astype(o_ref.dtype)
    kz = k.astype(F32) * zeta
    st_ref[...] = gc_ref[h] * state + _dot(kz.T.astype(BF16), v)


def _retention(lg, gc, rq, rk, rv, rg, km, vm, *, C):
    B, S, _ = rq.shape
    H = RET_HEADS
    smem = pl.BlockSpec(memory_space=pltpu.SMEM)
    return pl.pallas_call(
        functools.partial(_ret_kernel, C=C),
        grid=(B, H, S // C),
        in_specs=[
            smem,
            smem,
            pl.BlockSpec((1, C, RET_DK), lambda b, h, c: (b, c, h)),
            pl.BlockSpec((1, C, RET_DK), lambda b, h, c: (b, c, h)),
            pl.BlockSpec((1, C, RET_DV), lambda b, h, c: (b, c, h)),
            pl.BlockSpec((1, C, RET_DV), lambda b, h, c: (b, c, h)),
            pl.BlockSpec((N_META, RET_DK), lambda b, h, c: (0, h)),
            pl.BlockSpec((N_META, RET_DV), lambda b, h, c: (0, h)),
        ],
        out_specs=pl.BlockSpec((1, C, RET_DV), lambda b, h, c: (b, c, h)),
        out_shape=jax.ShapeDtypeStruct((B, S, RET_V), BF16),
        scratch_shapes=[pltpu.VMEM((RET_DK, RET_DV), F32)],
        compiler_params=_params(("parallel", "parallel", "arbitrary")),
        name="retention",
    )(lg, gc, rq, rk, rv, rg, km, vm)


def _dsa_kernel(iq_ref, ikw_ref, wq_ref, aq_ref, ak_ref, av_ref, akm_ref, avm_ref, o_ref,
                sc_ref, tau_ref, m_ref, l_ref, acc_ref, *, tq, k_top):
    tk = tq
    qi = pl.program_id(1)
    nkb = qi + 1
    kf = float(k_top)
    neg_inf = -jnp.inf
    row = qi * tq + lax.broadcasted_iota(jnp.int32, (tq, 1), 0)
    col0 = lax.broadcasted_iota(jnp.int32, (1, tk), 1)

    idx_scale = (IDX_DH ** -0.5) * (IDX_HEADS ** -0.5)
    iq = iq_ref[0].astype(BF16)
    wq = wq_ref[0]
    iq_h = [iq[:, h * IDX_DH:(h + 1) * IDX_DH] for h in range(IDX_HEADS)]
    w_h = [wq[:, IDX_DH + h:IDX_DH + h + 1] for h in range(IDX_HEADS)]

    def score_block(kb, carry):
        rmin, rmax = carry
        start = pl.multiple_of(kb * tk, tk)
        ik = ikw_ref[0, pl.ds(start, tk), :][:, :IDX_DH].astype(BF16)
        acc = jnp.zeros((tq, tk), F32)
        for h in range(IDX_HEADS):
            acc = acc + w_h[h] * jnp.maximum(_dot_nt(iq_h[h], ik), 0.0)
        ok = (col0 + kb * tk) <= row
        s = jnp.where(ok, acc * idx_scale, neg_inf)
        sc_ref[kb] = s
        rmin = jnp.minimum(rmin, jnp.min(jnp.where(ok, s, jnp.inf), axis=-1, keepdims=True))
        rmax = jnp.maximum(rmax, jnp.max(s, axis=-1, keepdims=True))
        return rmin, rmax

    rmin, rmax = lax.fori_loop(
        0, nkb, score_block, (jnp.full((tq, 1), jnp.inf, F32), jnp.full((tq, 1), neg_inf, F32)))

    def count(pred):
        def body(kb, c):
            return c + jnp.sum(jnp.where(pred(sc_ref[kb]), 1.0, 0.0), axis=-1, keepdims=True)
        return lax.fori_loop(0, nkb, body, jnp.zeros((tq, 1), F32))

    def any_row(flag):
        return jnp.max(jnp.where(flag, 1, 0))

    c_max = count(lambda s: s >= rmax)
    top_tied = c_max >= kf
    lo0 = jnp.where(top_tied, rmax, rmin)
    cl0 = jnp.where(top_tied, c_max, (row + 1).astype(F32))

    def midpoint(lo, hi):
        return 0.5 * lo + 0.5 * hi

    def active_rows(lo, hi, cl):
        mid = midpoint(lo, hi)
        return (cl > kf) & (mid > lo) & (mid < hi)

    def bis_cond(st):
        return st[3] > 0

    def bis_body(st):
        lo, hi, cl, _ = st
        mid = midpoint(lo, hi)
        act = active_rows(lo, hi, cl)
        c = count(lambda s: s >= mid)
        up = act & (c >= kf)
        dn = act & (c < kf)
        lo = jnp.where(up, mid, lo)
        cl = jnp.where(up, c, cl)
        hi = jnp.where(dn, mid, hi)
        return lo, hi, cl, any_row(active_rows(lo, hi, cl))

    lo, hi, cl, _ = lax.while_loop(
        bis_cond, bis_body, (lo0, rmax, cl0, any_row(active_rows(lo0, rmax, cl0))))
    tau_ref[...] = lo
    tie = cl > kf

    @pl.when(any_row(tie) > 0)
    def _():
        def max_below(bound):
            def body(kb, v):
                s = sc_ref[kb]
                return jnp.maximum(v, jnp.max(jnp.where(s < bound, s, neg_inf), axis=-1, keepdims=True))
            return lax.fori_loop(0, nkb, body, jnp.full((tq, 1), neg_inf, F32))

        hi_d = jnp.where(lo >= hi, jnp.inf, hi)
        v0 = max_below(hi_d)
        c0 = count(lambda s: s >= v0)

        def walk_body(st):
            hi_w, v, c, _ = st
            hi_w = jnp.where(tie & (c < kf), v, hi_w)
            v = max_below(hi_w)
            c = count(lambda s: s >= v)
            return hi_w, v, c, any_row(tie & (c < kf))

        _, v, _, _ = lax.while_loop(bis_cond, walk_body, (hi_d, v0, c0, any_row(tie & (c0 < kf))))
        keep = kf - count(lambda s: s > v)
        ri = lax.broadcasted_iota(jnp.int32, (tk, tk), 0)
        ci = lax.broadcasted_iota(jnp.int32, (tk, tk), 1)
        before = jnp.where(ri < ci, 1.0, 0.0).astype(BF16)

        def drop_body(kb, seen):
            s = sc_ref[kb]
            eq = tie & (s == v)
            eqf = jnp.where(eq, 1.0, 0.0)
            rank = seen + _dot(eqf.astype(BF16), before)
            sc_ref[kb] = jnp.where(eq & (rank >= keep), neg_inf, s)
            return seen + jnp.sum(eqf, axis=-1, keepdims=True)

        lax.fori_loop(0, nkb, drop_body, jnp.zeros((tq, 1), F32))
        tau_ref[...] = jnp.where(tie, v, lo)

    scale = ATT_DH ** -0.5
    tau = tau_ref[...]
    meta_ok = lax.broadcasted_iota(jnp.int32, (1, LANES), 1) < N_META

    def head(ref, h, rows=None):
        if rows is None:
            return ref[:, h * ATT_DH:(h + 1) * ATT_DH]
        return ref[0, rows, h * ATT_DH:(h + 1) * ATT_DH]

    for h in range(ATT_HEADS):
        q_h = head(aq_ref, h, slice(None))
        s = jnp.where(meta_ok, _dot_nt(q_h, head(akm_ref, h)) * scale, neg_inf)
        m = jnp.max(s, axis=-1, keepdims=True)
        p = jnp.exp(s - m)
        m_ref[h] = jnp.broadcast_to(m, (tq, LANES))
        l_ref[h] = jnp.broadcast_to(jnp.sum(p, axis=-1, keepdims=True), (tq, LANES))
        acc_ref[:, h * ATT_DH:(h + 1) * ATT_DH] = _dot(p.astype(BF16), head(avm_ref, h))

    def att_block(kb, carry):
        start = pl.multiple_of(kb * tk, tk)
        keep_mask = sc_ref[kb] >= tau
        for h in range(ATT_HEADS):
            q_h = head(aq_ref, h, slice(None))
            k_h = head(ak_ref, h, pl.ds(start, tk))
            v_h = head(av_ref, h, pl.ds(start, tk))
            s = jnp.where(keep_mask, _dot_nt(q_h, k_h) * scale, neg_inf)
            m_prev = m_ref[h]
            m_new = jnp.maximum(m_prev, jnp.max(s, axis=-1, keepdims=True))
            alpha = jnp.exp(m_prev - m_new)
            p = jnp.exp(s - m_new[:, :1])
            l_ref[h] = alpha * l_ref[h] + jnp.sum(p, axis=-1, keepdims=True)
            hs = slice(h * ATT_DH, (h + 1) * ATT_DH)
            acc_ref[:, hs] = alpha * acc_ref[:, hs] + _dot(p.astype(BF16), v_h)
            m_ref[h] = m_new
        return carry

    lax.fori_loop(0, nkb, att_block, 0)
    for h in range(ATT_HEADS):
        hs = slice(h * ATT_DH, (h + 1) * ATT_DH)
        o_ref[0, :, hs] = (acc_ref[:, hs] / l_ref[h]).astype(o_ref.dtype)


def _dsa(iq, ikw, aq, ak, av, akm, avm, *, tq, k_top):
    B, S, _ = aq.shape
    nq = S // tq
    return pl.pallas_call(
        functools.partial(_dsa_kernel, tq=tq, k_top=k_top),
        grid=(B, nq),
        in_specs=[
            pl.BlockSpec((1, tq, IDX_Q), lambda b, q: (b, q, 0)),
            pl.BlockSpec((1, S, LANES), lambda b, q: (b, 0, 0)),
            pl.BlockSpec((1, tq, LANES), lambda b, q: (b, q, 0)),
            pl.BlockSpec((1, tq, ATT_W), lambda b, q: (b, q, 0)),
            pl.BlockSpec((1, S, ATT_W), lambda b, q: (b, 0, 0)),
            pl.BlockSpec((1, S, ATT_W), lambda b, q: (b, 0, 0)),
            pl.BlockSpec((LANES, ATT_W), lambda b, q: (0, 0)),
            pl.BlockSpec((LANES, ATT_W), lambda b, q: (0, 0)),
        ],
        out_specs=pl.BlockSpec((1, tq, ATT_W), lambda b, q: (b, q, 0)),
        out_shape=jax.ShapeDtypeStruct((B, S, ATT_W), BF16),
        scratch_shapes=[
            pltpu.VMEM((nq, tq, tq), F32),
            pltpu.VMEM((tq, 1), F32),
            pltpu.VMEM((ATT_HEADS, tq, LANES), F32),
            pltpu.VMEM((ATT_HEADS, tq, LANES), F32),
            pltpu.VMEM((tq, ATT_W), F32),
        ],
        compiler_params=_params(("parallel", "arbitrary")),
        name="dsa",
    )(iq, ikw, ikw, aq, ak, av, akm, avm)


def _merge_kernel(h_ref, yr_ref, ya_ref, gr_ref, ga_ref, wr_ref, wa_ref, wm_ref, o_ref):
    yr = _dot(yr_ref[...], wr_ref[...])
    ya = _dot(ya_ref[...], wa_ref[...])
    merged = gr_ref[...] * yr + ga_ref[...] * ya
    o_ref[...] = h_ref[...] + _dot(merged.astype(BF16), wm_ref[...])


def _merge(h, yr, ya, gates, wr, wa, wm, *, tm):
    M, D = h.shape
    row = lambda i: (i, 0)
    fixed = lambda i: (0, 0)
    return pl.pallas_call(
        _merge_kernel,
        grid=(M // tm,),
        in_specs=[
            pl.BlockSpec((tm, D), row),
            pl.BlockSpec((tm, RET_V), row),
            pl.BlockSpec((tm, ATT_W), row),
            pl.BlockSpec((tm, D), lambda i: (i, 0)),
            pl.BlockSpec((tm, D), lambda i: (i, 1)),
            pl.BlockSpec((RET_V, D), fixed),
            pl.BlockSpec((ATT_W, D), fixed),
            pl.BlockSpec((D, D), fixed),
        ],
        out_specs=pl.BlockSpec((tm, D), row),
        out_shape=jax.ShapeDtypeStruct((M, D), F32),
        compiler_params=_params(("parallel",)),
        name="merge",
    )(h, yr, ya, gates, gates, wr, wa, wm)


def _rope_tables(pos, d, width):
    inv_freq = ROPE_THETA ** (-jnp.arange(0, d, 2, dtype=F32) / d)
    ang = pos.astype(F32)[:, None] * inv_freq[None, :]
    cos, sin = jnp.cos(ang), jnp.sin(ang)
    cos = jnp.concatenate([cos, cos], axis=-1)
    sin = jnp.concatenate([-sin, sin], axis=-1)
    rep = width // d
    return jnp.tile(cos, (1, rep)), jnp.tile(sin, (1, rep))


def _tile(n, cap):
    t = min(n, cap)
    assert n % t == 0, (n, t)
    return t


def kernel(x, meta_tokens, ffn1_norm, ffn1_w_gate, ffn1_w_up, ffn1_w_down, mix_norm, w_in, w_ret_out,
           w_att_out, w_mix_out, ffn2_norm, ffn2_w_gate, ffn2_w_up, ffn2_w_down, final_norm):
    B, S, D = x.shape
    assert ffn1_norm.shape[0] == 1, "single layer only"
    assert meta_tokens.shape[0] == N_META
    k_top = min(TOPK_MAX, (S + 0) // 4)
    M = B * S
    bf = lambda a: a.astype(BF16)
    row = lambda a: a.reshape(1, -1)

    offs = np.cumsum([0, RET_QK, RET_QK, RET_V, RET_V, ATT_W, ATT_W, ATT_W, IDX_Q, IDX_DH, IDX_HEADS, D, D])
    w = bf(w_in[0])
    col = lambda a, b: w[:, offs[a]:offs[b]]
    w_rq, w_rk, w_rv, w_rg, w_aq, w_ak, w_av = (col(i, i + 1) for i in range(7))
    w_idx = jnp.pad(col(7, 10), ((0, 0), (0, LANES - IDX_DH - IDX_HEADS)))
    w_gate = col(10, 12)

    pos_meta = jnp.arange(N_META, dtype=jnp.int32)
    pos_real = N_META + jnp.arange(S, dtype=jnp.int32)
    ffn1 = (row(ffn1_norm[0]), bf(ffn1_w_gate[0]), bf(ffn1_w_up[0]), bf(ffn1_w_down[0]), row(mix_norm[0]))
    ffn2 = (row(ffn2_norm[0]), bf(ffn2_w_gate[0]), bf(ffn2_w_up[0]), bf(ffn2_w_down[0]), row(final_norm))
    tf = 256

    _, un_m = _ffn(meta_tokens.astype(F32), *ffn1, tm=N_META, tf=tf, emit_h=True, n_dtype=BF16)
    pm = dict(tm=N_META, tn=256)
    rk_m = _proj(un_m, w_rk, mode="rope", out_dtype=BF16, tabs=_rope_tables(pos_meta, RET_DK, RET_DK),
                 d=RET_DK, scale=RET_DK ** -0.5, **pm)
    rv_m = _proj(un_m, w_rv, mode="plain", out_dtype=BF16, **pm)
    ak_m = _proj(un_m, w_ak, mode="rope", out_dtype=BF16, tabs=_rope_tables(pos_meta, ATT_DH, LANES),
                 d=ATT_DH, **pm)
    av_m = _proj(un_m, w_av, mode="plain", out_dtype=BF16, **pm)
    pad_meta = lambda a: jnp.pad(a, ((0, LANES - N_META), (0, 0)))

    tm = _tile(S, 1024)
    h1, un = _ffn(x.reshape(M, D), *ffn1, tm=tm, tf=tf, emit_h=True, n_dtype=BF16)
    pr = dict(tm=tm, tn=256)
    tab_r = _rope_tables(pos_real, RET_DK, RET_DK)
    tab_a = _rope_tables(pos_real, ATT_DH, LANES)
    tab_i = _rope_tables(pos_real, IDX_DH, LANES)
    rq = _proj(un, w_rq, mode="rope", out_dtype=BF16, tabs=tab_r, d=RET_DK, **pr)
    rk = _proj(un, w_rk, mode="rope", out_dtype=BF16, tabs=tab_r, d=RET_DK, scale=RET_DK ** -0.5, **pr)
    rv = _proj(un, w_rv, mode="plain", out_dtype=BF16, **pr)
    rg = _proj(un, w_rg, mode="silu", out_dtype=F32, **pr)
    aq = _proj(un, w_aq, mode="rope", out_dtype=BF16, tabs=tab_a, d=ATT_DH, **pr)
    ak = _proj(un, w_ak, mode="rope", out_dtype=BF16, tabs=tab_a, d=ATT_DH, **pr)
    av = _proj(un, w_av, mode="plain", out_dtype=BF16, **pr)
    gates = _proj(un, w_gate, mode="sigmoid", out_dtype=F32, **pr)
    iq, ikw = _idx_proj(un, w_idx, tab_i, tm=tm)

    heads = jnp.arange(RET_HEADS, dtype=F32)
    log_g = jnp.log1p(-(2.0 ** (-5.0 - heads)))
    C = RET_CHUNK
    b3 = lambda a: a.reshape(B, S, a.shape[-1])
    yr = _retention(log_g, jnp.exp(log_g * C), b3(rq), b3(rk), b3(rv), b3(rg), rk_m, rv_m, C=C)
    ya = _dsa(b3(iq), b3(ikw), b3(aq), b3(ak), b3(av), pad_meta(ak_m), pad_meta(av_m),
              tq=_tile(S, 256), k_top=k_top)

    h2 = _merge(h1, yr.reshape(M, RET_V), ya.reshape(M, ATT_W), gates, bf(w_ret_out[0]), bf(w_att_out[0]),
                bf(w_mix_out[0]), tm=_tile(M, 512))
    (y,) = _ffn(h2, *ffn2, tm=tm, tf=tf, emit_h=False, n_dtype=F32)
    return y.reshape(B, S, D)
```

```python
import functools

import jax
import jax.numpy as jnp
import numpy as np
from jax import lax
from jax.experimental import pallas as pl
from jax.experimental.pallas import tpu as pltpu

N_META = 16
RET_HEADS = 4
RET_DK = 256
RET_DV = 512
RET_CHUNK = 128
ATT_HEADS = 8
ATT_DH = 128
IDX_HEADS = 8
IDX_DH = 64
TOPK_MAX = 256
ROPE_THETA = 10000.0
EPS = 1e-6
RET_QK = RET_HEADS * RET_DK
RET_V = RET_HEADS * RET_DV
ATT_W = ATT_HEADS * ATT_DH
IDX_Q = IDX_HEADS * IDX_DH

LANES = 128
VMEM_LIMIT = 48 * 1024 * 1024

BF16 = jnp.bfloat16
F32 = jnp.float32
NT_DIMS = (((1,), (1,)), ((), ()))


def _dot(a, b):
    return jnp.dot(a, b, preferred_element_type=F32)


def _dot_nt(a, b):
    return lax.dot_general(a, b, NT_DIMS, preferred_element_type=F32)


def _rms(x, g):
    return x * lax.rsqrt(jnp.mean(x * x, axis=-1, keepdims=True) + EPS) * g


def _sigmoid(x):
    return 1.0 / (1.0 + jnp.exp(-x))


def _params(sem):
    return pltpu.CompilerParams(dimension_semantics=sem, vmem_limit_bytes=VMEM_LIMIT)


def _ffn_kernel(x_ref, g_ref, wg_ref, wu_ref, wd_ref, g2_ref, *rest, nf, emit_h):
    if emit_h:
        h_ref, n_ref, xn_sc, acc_sc = rest
    else:
        n_ref, xn_sc, acc_sc = rest
    f = pl.program_id(1)

    @pl.when(f == 0)
    def _():
        xn_sc[...] = _rms(x_ref[...], g_ref[...]).astype(BF16)
        acc_sc[...] = jnp.zeros_like(acc_sc)

    xn = xn_sc[...]
    gate = _dot(xn, wg_ref[...])
    up = _dot(xn, wu_ref[...])
    act = (gate * _sigmoid(gate)) * up
    acc_sc[...] += _dot(act.astype(BF16), wd_ref[...])

    @pl.when(f == nf - 1)
    def _():
        h = x_ref[...] + 0.5 * acc_sc[...]
        if emit_h:
            h_ref[...] = h
        n_ref[...] = _rms(h, g2_ref[...]).astype(n_ref.dtype)


def _ffn(x, g, wg, wu, wd, g2, *, tm, tf, emit_h, n_dtype):
    M, D = x.shape
    F = wg.shape[1]
    nf = F // tf
    out_shape = [jax.ShapeDtypeStruct((M, D), n_dtype)]
    out_specs = [pl.BlockSpec((tm, D), lambda i, f: (i, 0))]
    if emit_h:
        out_shape = [jax.ShapeDtypeStruct((M, D), F32)] + out_shape
        out_specs = [pl.BlockSpec((tm, D), lambda i, f: (i, 0))] + out_specs
    return pl.pallas_call(
        functools.partial(_ffn_kernel, nf=nf, emit_h=emit_h),
        grid=(M // tm, nf),
        in_specs=[
            pl.BlockSpec((tm, D), lambda i, f: (i, 0)),
            pl.BlockSpec((1, D), lambda i, f: (0, 0)),
            pl.BlockSpec((D, tf), lambda i, f: (0, f)),
            pl.BlockSpec((D, tf), lambda i, f: (0, f)),
            pl.BlockSpec((tf, D), lambda i, f: (f, 0)),
            pl.BlockSpec((1, D), lambda i, f: (0, 0)),
        ],
        out_specs=out_specs,
        out_shape=out_shape,
        scratch_shapes=[pltpu.VMEM((tm, D), BF16), pltpu.VMEM((tm, D), F32)],
        compiler_params=_params(("parallel", "arbitrary")),
        name="ffn",
    )(x, g, wg, wu, wd, g2)


def _rot_half(y, d):
    w = y.shape[-1]
    if d >= w:
        return jnp.concatenate([y[:, w // 2:], y[:, : w // 2]], axis=-1)
    if d == LANES:
        return pltpu.roll(y, d // 2, axis=1)
    fwd = pltpu.roll(y, d // 2, axis=1)
    bwd = pltpu.roll(y, w - d // 2, axis=1)
    lane = lax.broadcasted_iota(jnp.int32, y.shape, 1)
    return jnp.where((lane % d) < d // 2, bwd, fwd)


def _rope_tile(y, cos, sin, d):
    tw = cos.shape[-1]
    outs = []
    for t in range(y.shape[-1] // tw):
        yt = y[:, t * tw:(t + 1) * tw]
        outs.append(yt * cos + _rot_half(yt, d) * sin)
    return outs[0] if len(outs) == 1 else jnp.concatenate(outs, axis=-1)


def _proj_kernel(x_ref, w_ref, *rest, mode, d, scale):
    y = _dot(x_ref[...], w_ref[...])
    if mode == "rope":
        cos_ref, sin_ref, o_ref = rest
        y = _rope_tile(y, cos_ref[...], sin_ref[...], d)
        if scale != 1.0:
            y = y * scale
    else:
        (o_ref,) = rest
        if mode == "silu":
            y = y * _sigmoid(y)
        elif mode == "sigmoid":
            y = _sigmoid(y)
    o_ref[...] = y.astype(o_ref.dtype)


def _proj(x, w, *, tm, tn, mode, out_dtype, tabs=None, d=None, scale=1.0):
    M, K = x.shape
    N = w.shape[1]
    in_specs = [
        pl.BlockSpec((tm, K), lambda i, j: (i, 0)),
        pl.BlockSpec((K, tn), lambda i, j: (0, j)),
    ]
    args = [x, w]
    if mode == "rope":
        cos, sin = tabs
        npos = cos.shape[0] // tm
        tw = cos.shape[1]
        in_specs += [pl.BlockSpec((tm, tw), lambda i, j: (i % npos, 0))] * 2
        args += [cos, sin]
    return pl.pallas_call(
        functools.partial(_proj_kernel, mode=mode, d=d, scale=scale),
        grid=(M // tm, N // tn),
        in_specs=in_specs,
        out_specs=pl.BlockSpec((tm, tn), lambda i, j: (i, j)),
        out_shape=jax.ShapeDtypeStruct((M, N), out_dtype),
        compiler_params=_params(("parallel", "arbitrary")),
        name="proj_" + mode,
    )(*args)


def _idx_proj_kernel(x_ref, w_ref, cos_ref, sin_ref, q_ref, kw_ref):
    y = _dot(x_ref[...], w_ref[...])
    cos, sin = cos_ref[...], sin_ref[...]
    q_ref[...] = _rope_tile(y[:, :IDX_Q], cos, sin, IDX_DH)
    tail = y[:, IDX_Q:]
    lane = lax.broadcasted_iota(jnp.int32, tail.shape, 1)
    kw_ref[...] = jnp.where(lane < IDX_DH, _rope_tile(tail, cos, sin, IDX_DH), tail)


def _idx_proj(x, w, tabs, *, tm):
    M, K = x.shape
    N = w.shape[1]
    cos, sin = tabs
    npos = cos.shape[0] // tm
    return pl.pallas_call(
        _idx_proj_kernel,
        grid=(M // tm,),
        in_specs=[
            pl.BlockSpec((tm, K), lambda i: (i, 0)),
            pl.BlockSpec((K, N), lambda i: (0, 0)),
            pl.BlockSpec((tm, LANES), lambda i: (i % npos, 0)),
            pl.BlockSpec((tm, LANES), lambda i: (i % npos, 0)),
        ],
        out_specs=[
            pl.BlockSpec((tm, IDX_Q), lambda i: (i, 0)),
            pl.BlockSpec((tm, LANES), lambda i: (i, 0)),
        ],
        out_shape=[
            jax.ShapeDtypeStruct((M, IDX_Q), F32),
            jax.ShapeDtypeStruct((M, LANES), F32),
        ],
        compiler_params=_params(("parallel",)),
        name="proj_idx",
    )(x, w, cos, sin)


def _ret_kernel(lg_ref, gc_ref, q_ref, k_ref, v_ref, g_ref, km_ref, vm_ref, o_ref, st_ref, *, C):
    h = pl.program_id(1)
    c = pl.program_id(2)
    lg = lg_ref[h]

    @pl.when(c == 0)
    def _():
        pm = lax.broadcasted_iota(jnp.int32, (N_META, 1), 0).astype(F32)
        zeta_m = jnp.exp(lg * (N_META - 1.0 - pm))
        kz = km_ref[...].astype(F32) * zeta_m
        st_ref[...] = _dot(kz.T.astype(BF16), vm_ref[...])

    q = q_ref[0]
    k = k_ref[0]
    v = v_ref[0]
    ri = lax.broadcasted_iota(jnp.int32, (C, C), 0)
    ci = lax.broadcasted_iota(jnp.int32, (C, C), 1)
    diff = (ri - ci).astype(F32)
    decay = jnp.where(diff >= 0, jnp.exp(lg * jnp.maximum(diff, 0.0)), 0.0)
    pos = lax.broadcasted_iota(jnp.int32, (C, 1), 0).astype(F32)
    xi = jnp.exp(lg * (pos + 1.0))
    zeta = jnp.exp(lg * (C - 1.0 - pos))

    state = st_ref[...]
    inner = _dot((_dot_nt(q, k) * decay).astype(BF16), v)
    cross = _dot(q, state.astype(BF16)) * xi
    y = inner + cross
    y = y * lax.rsqrt(jnp.mean(y * y, axis=-1, keepdims=True) + EPS)
    o_ref[0] = (g_ref[0] * y).astype(o_ref.dtype)
    kz = k.astype(F32) * zeta
    st_ref[...] = gc_ref[h] * state + _dot(kz.T.astype(BF16), v)


def _retention(lg, gc, rq, rk, rv, rg, km, vm, *, C):
    B, S, _ = rq.shape
    H = RET_HEADS
    smem = pl.BlockSpec(memory_space=pltpu.SMEM)
    return pl.pallas_call(
        functools.partial(_ret_kernel, C=C),
        grid=(B, H, S // C),
        in_specs=[
            smem,
            smem,
            pl.BlockSpec((1, C, RET_DK), lambda b, h, c: (b, c, h)),
            pl.BlockSpec((1, C, RET_DK), lambda b, h, c: (b, c, h)),
            pl.BlockSpec((1, C, RET_DV), lambda b, h, c: (b, c, h)),
            pl.BlockSpec((1, C, RET_DV), lambda b, h, c: (b, c, h)),
            pl.BlockSpec((N_META, RET_DK), lambda b, h, c: (0, h)),
            pl.BlockSpec((N_META, RET_DV), lambda b, h, c: (0, h)),
        ],
        out_specs=pl.BlockSpec((1, C, RET_DV), lambda b, h, c: (b, c, h)),
        out_shape=jax.ShapeDtypeStruct((B, S, RET_V), BF16),
        scratch_shapes=[pltpu.VMEM((RET_DK, RET_DV), F32)],
        compiler_params=_params(("parallel", "parallel", "arbitrary")),
        name="retention",
    )(lg, gc, rq, rk, rv, rg, km, vm)


TINY = float(np.finfo(np.float32).tiny)
TN_DIMS = (((0,), (0,)), ((), ()))


def _dsa_kernel(iq_ref, ikw_ref, wq_ref, aq_ref, ak_ref, av_ref, akm_ref, avm_ref, o_ref,
                sc_ref, tau_ref, m_ref, l_ref, acc_ref, s_ref, bm_ref, *, tq, tk, k_top):
    qi = pl.program_id(1)
    nkb = (qi + 1) * (tq // tk)
    kf = float(k_top)
    neg_inf = -jnp.inf
    qpos = qi * tq + lax.broadcasted_iota(jnp.int32, (1, tq), 1)
    kpos0 = lax.broadcasted_iota(jnp.int32, (tk, 1), 0)

    def fold(x):
        return x.reshape(tk // 8, 8, tq).sum(axis=0)

    idx_scale = (IDX_DH ** -0.5) * (IDX_HEADS ** -0.5)
    iq = iq_ref[0].astype(BF16)
    w_t = wq_ref[0].T
    iq_h = [iq[:, h * IDX_DH:(h + 1) * IDX_DH] for h in range(IDX_HEADS)]
    w_h = [w_t[IDX_DH + h:IDX_DH + h + 1, :] for h in range(IDX_HEADS)]

    def score_block(kb, carry):
        rmin, rmax = carry
        start = pl.multiple_of(kb * tk, tk)
        ik = ikw_ref[0, pl.ds(start, tk), :][:, :IDX_DH].astype(BF16)
        acc = jnp.zeros((tk, tq), F32)
        for h in range(IDX_HEADS):
            acc = acc + w_h[h] * jnp.maximum(_dot_nt(ik, iq_h[h]), 0.0)
        ok = (kpos0 + kb * tk) <= qpos
        s = jnp.where(ok, acc * idx_scale, neg_inf)
        sc_ref[kb] = s
        rmin = jnp.minimum(rmin, jnp.min(jnp.where(ok, s, jnp.inf), axis=0, keepdims=True))
        rmax = jnp.maximum(rmax, jnp.max(s, axis=0, keepdims=True))
        return rmin, rmax

    rmin, rmax = lax.fori_loop(
        0, nkb, score_block, (jnp.full((1, tq), jnp.inf, F32), jnp.full((1, tq), neg_inf, F32)))

    def count(pred):
        def body(kb, c):
            return c + fold(jnp.where(pred(sc_ref[kb]), 1.0, 0.0))
        return jnp.sum(lax.fori_loop(0, nkb, body, jnp.zeros((8, tq), F32)), axis=0, keepdims=True)

    def any_query(flag):
        return jnp.max(jnp.where(flag, 1.0, 0.0))

    c_max = count(lambda s: s >= rmax)
    top_tied = c_max >= kf
    lo0 = jnp.where(top_tied, rmax, rmin)
    cl0 = jnp.where(top_tied, c_max, (qpos + 1).astype(F32))

    def midpoint(lo, hi):
        a = jnp.maximum(jnp.abs(lo), TINY)
        b = jnp.maximum(jnp.abs(hi), TINY)
        geo = jnp.sqrt(a) * jnp.sqrt(b)
        one_sign = (lo >= 0.0) | (hi <= 0.0)
        far = jnp.maximum(a, b) > 4.0 * jnp.minimum(a, b)
        mid = jnp.where(one_sign & far, jnp.where(lo >= 0.0, geo, -geo), 0.5 * lo + 0.5 * hi)
        mid = jnp.where(lo == 0.0, TINY, mid)
        mid = jnp.where(hi == 0.0, -TINY, mid)
        return jnp.where((lo < 0.0) & (hi > 0.0), 0.0, mid)

    def active(lo, hi, cl):
        mid = midpoint(lo, hi)
        return (cl > kf) & (mid > lo) & (mid < hi)

    def go_on(st):
        return st[3] > 0.0

    def shrink_once(lo, hi, cl):
        mid = midpoint(lo, hi)
        act = active(lo, hi, cl)
        c = count(lambda s: s >= mid)
        up = act & (c >= kf)
        dn = act & (c < kf)
        return jnp.where(up, mid, lo), jnp.where(dn, mid, hi), jnp.where(up, c, cl)

    def shrink(st):
        lo, hi, cl = shrink_once(*shrink_once(*st[:3]))
        return lo, hi, cl, any_query(active(lo, hi, cl))

    lo, hi, cl, _ = lax.while_loop(go_on, shrink, (lo0, rmax, cl0, any_query(active(lo0, rmax, cl0))))
    tau_ref[...] = lo
    tie = cl > kf

    @pl.when(any_query(tie) > 0.0)
    def _():
        def max_below(bound):
            def body(kb, v):
                s = sc_ref[kb]
                return jnp.maximum(v, jnp.max(jnp.where(s < bound, s, neg_inf), axis=0, keepdims=True))
            return lax.fori_loop(0, nkb, body, jnp.full((1, tq), neg_inf, F32))

        def count_ge_gt(v):
            def body(kb, c):
                s = sc_ref[kb]
                return c[0] + fold(jnp.where(s >= v, 1.0, 0.0)), c[1] + fold(jnp.where(s > v, 1.0, 0.0))
            zero = jnp.zeros((8, tq), F32)
            ge, gt = lax.fori_loop(0, nkb, body, (zero, zero))
            return jnp.sum(ge, axis=0, keepdims=True), jnp.sum(gt, axis=0, keepdims=True)

        hi_d = jnp.where(lo >= hi, jnp.inf, hi)
        v0 = max_below(hi_d)
        ge0, gt0 = count_ge_gt(v0)

        def walk_on(st):
            return st[4] > 0.0

        def walk(st):
            hi_w, v, ge, _, _ = st
            hi_w = jnp.where(tie & (ge < kf), v, hi_w)
            v = max_below(hi_w)
            ge, gt = count_ge_gt(v)
            return hi_w, v, ge, gt, any_query(tie & (ge < kf))

        _, v, _, gt, _ = lax.while_loop(walk_on, walk, (hi_d, v0, ge0, gt0, any_query(tie & (ge0 < kf))))
        keep = kf - gt
        ri = lax.broadcasted_iota(jnp.int32, (tk, tk), 0)
        ci = lax.broadcasted_iota(jnp.int32, (tk, tk), 1)
        earlier = jnp.where(ci < ri, 1.0, 0.0).astype(BF16)

        def drop(kb, seen):
            s = sc_ref[kb]
            eq = tie & (s == v)
            eqf = jnp.where(eq, 1.0, 0.0)
            rank = seen + _dot(earlier, eqf.astype(BF16))
            sc_ref[kb] = jnp.where(eq & (rank >= keep), neg_inf, s)
            return seen + jnp.sum(eqf, axis=0, keepdims=True)

        lax.fori_loop(0, nkb, drop, jnp.zeros((1, tq), F32))
        tau_ref[...] = jnp.where(tie, v, lo)

    c_exp = (ATT_DH ** -0.5) * float(np.log2(np.e))
    tau = tau_ref[...]
    meta_ok = lax.broadcasted_iota(jnp.int32, (LANES, 1), 0) < N_META

    def head(ref, h, rows=None):
        if rows is None:
            return ref[:, h * ATT_DH:(h + 1) * ATT_DH]
        return ref[0, rows, h * ATT_DH:(h + 1) * ATT_DH]

    for h in range(ATT_HEADS):
        s = jnp.where(meta_ok, _dot_nt(head(akm_ref, h), head(aq_ref, h, slice(None))), neg_inf)
        m = jnp.max(s, axis=0, keepdims=True)
        p = jnp.exp2((s - m) * c_exp)
        m_ref[h:h + 1, :] = m
        l_ref[h:h + 1, :] = jnp.sum(p, axis=0, keepdims=True)
        acc_ref[h] = lax.dot_general(head(avm_ref, h), p.astype(BF16), TN_DIMS, preferred_element_type=F32)

    def att_block(kb, carry):
        start = pl.multiple_of(kb * tk, tk)
        bias = jnp.where(sc_ref[kb] >= tau, 0.0, neg_inf)
        for h in range(ATT_HEADS):
            s = _dot_nt(head(ak_ref, h, pl.ds(start, tk)), head(aq_ref, h, slice(None))) + bias
            s_ref[h] = s
            bm_ref[h:h + 1, :] = jnp.max(s, axis=0, keepdims=True)

        @pl.when(kb < nkb)
        def _():
            for h in range(ATT_HEADS):
                m_new = jnp.maximum(m_ref[h:h + 1, :], bm_ref[h:h + 1, :])
                p = jnp.exp2((s_ref[h] - m_new) * c_exp)
                p_sum = jnp.sum(p, axis=0, keepdims=True)
                pv = lax.dot_general(head(av_ref, h, pl.ds(start, tk)), p.astype(BF16), TN_DIMS,
                                     preferred_element_type=F32)
                alpha = jnp.exp2((m_ref[h:h + 1, :] - m_new) * c_exp)
                l_ref[h:h + 1, :] = alpha * l_ref[h:h + 1, :] + p_sum
                acc_ref[h] = alpha * acc_ref[h] + pv
                m_ref[h:h + 1, :] = m_new
        return carry

    lax.fori_loop(0, nkb, att_block, 0)
    for h in range(ATT_HEADS):
        out_t = acc_ref[h] / l_ref[h:h + 1, :]
        o_ref[0, :, h * ATT_DH:(h + 1) * ATT_DH] = out_t.T.astype(o_ref.dtype)


def _dsa(iq, ikw, aq, ak, av, akm, avm, *, tq, tk, k_top):
    B, S, _ = aq.shape
    nq = S // tq
    assert tq % tk == 0
    return pl.pallas_call(
        functools.partial(_dsa_kernel, tq=tq, tk=tk, k_top=k_top),
        grid=(B, nq),
        in_specs=[
            pl.BlockSpec((1, tq, IDX_Q), lambda b, q: (b, q, 0)),
            pl.BlockSpec((1, S, LANES), lambda b, q: (b, 0, 0)),
            pl.BlockSpec((1, tq, LANES), lambda b, q: (b, q, 0)),
            pl.BlockSpec((1, tq, ATT_W), lambda b, q: (b, q, 0)),
            pl.BlockSpec((1, S, ATT_W), lambda b, q: (b, 0, 0)),
            pl.BlockSpec((1, S, ATT_W), lambda b, q: (b, 0, 0)),
            pl.BlockSpec((LANES, ATT_W), lambda b, q: (0, 0)),
            pl.BlockSpec((LANES, ATT_W), lambda b, q: (0, 0)),
        ],
        out_specs=pl.BlockSpec((1, tq, ATT_W), lambda b, q: (b, q, 0)),
        out_shape=jax.ShapeDtypeStruct((B, S, ATT_W), BF16),
        scratch_shapes=[
            pltpu.VMEM((S // tk, tk, tq), F32),
            pltpu.VMEM((1, tq), F32),
            pltpu.VMEM((ATT_HEADS, tq), F32),
            pltpu.VMEM((ATT_HEADS, tq), F32),
            pltpu.VMEM((ATT_HEADS, ATT_DH, tq), F32),
            pltpu.VMEM((ATT_HEADS, tk, tq), F32),
            pltpu.VMEM((ATT_HEADS, tq), F32),
        ],
        compiler_params=_params(("parallel", "arbitrary")),
        name="dsa",
    )(iq, ikw, ikw, aq, ak, av, akm, avm)


def _merge_kernel(h_ref, yr_ref, ya_ref, gr_ref, ga_ref, wr_ref, wa_ref, wm_ref, o_ref):
    yr = _dot(yr_ref[...], wr_ref[...])
    ya = _dot(ya_ref[...], wa_ref[...])
    merged = gr_ref[...] * yr + ga_ref[...] * ya
    o_ref[...] = h_ref[...] + _dot(merged.astype(BF16), wm_ref[...])


def _merge(h, yr, ya, gates, wr, wa, wm, *, tm):
    M, D = h.shape
    row = lambda i: (i, 0)
    fixed = lambda i: (0, 0)
    return pl.pallas_call(
        _merge_kernel,
        grid=(M // tm,),
        in_specs=[
            pl.BlockSpec((tm, D), row),
            pl.BlockSpec((tm, RET_V), row),
            pl.BlockSpec((tm, ATT_W), row),
            pl.BlockSpec((tm, D), lambda i: (i, 0)),
            pl.BlockSpec((tm, D), lambda i: (i, 1)),
            pl.BlockSpec((RET_V, D), fixed),
            pl.BlockSpec((ATT_W, D), fixed),
            pl.BlockSpec((D, D), fixed),
        ],
        out_specs=pl.BlockSpec((tm, D), row),
        out_shape=jax.ShapeDtypeStruct((M, D), F32),
        compiler_params=_params(("parallel",)),
        name="merge",
    )(h, yr, ya, gates, gates, wr, wa, wm)


def _rope_tables(pos, d, width):
    inv_freq = ROPE_THETA ** (-jnp.arange(0, d, 2, dtype=F32) / d)
    ang = pos.astype(F32)[:, None] * inv_freq[None, :]
    cos, sin = jnp.cos(ang), jnp.sin(ang)
    cos = jnp.concatenate([cos, cos], axis=-1)
    sin = jnp.concatenate([-sin, sin], axis=-1)
    rep = width // d
    return jnp.tile(cos, (1, rep)), jnp.tile(sin, (1, rep))


def _tile(n, cap):
    t = min(n, cap)
    assert n % t == 0, (n, t)
    return t


def kernel(x, meta_tokens, ffn1_norm, ffn1_w_gate, ffn1_w_up, ffn1_w_down, mix_norm, w_in, w_ret_out,
           w_att_out, w_mix_out, ffn2_norm, ffn2_w_gate, ffn2_w_up, ffn2_w_down, final_norm):
    B, S, D = x.shape
    assert ffn1_norm.shape[0] == 1, "single layer only"
    assert meta_tokens.shape[0] == N_META
    k_top = min(TOPK_MAX, (S + 0) // 4)
    M = B * S
    bf = lambda a: a.astype(BF16)
    row = lambda a: a.reshape(1, -1)

    offs = np.cumsum([0, RET_QK, RET_QK, RET_V, RET_V, ATT_W, ATT_W, ATT_W, IDX_Q, IDX_DH, IDX_HEADS, D, D])
    w = bf(w_in[0])
    col = lambda a, b: w[:, offs[a]:offs[b]]
    w_rq, w_rk, w_rv, w_rg, w_aq, w_ak, w_av = (col(i, i + 1) for i in range(7))
    w_idx = jnp.pad(col(7, 10), ((0, 0), (0, LANES - IDX_DH - IDX_HEADS)))
    w_gate = col(10, 12)

    pos_meta = jnp.arange(N_META, dtype=jnp.int32)
    pos_real = N_META + jnp.arange(S, dtype=jnp.int32)
    ffn1 = (row(ffn1_norm[0]), bf(ffn1_w_gate[0]), bf(ffn1_w_up[0]), bf(ffn1_w_down[0]), row(mix_norm[0]))
    ffn2 = (row(ffn2_norm[0]), bf(ffn2_w_gate[0]), bf(ffn2_w_up[0]), bf(ffn2_w_down[0]), row(final_norm))
    tf = 256

    _, un_m = _ffn(meta_tokens.astype(F32), *ffn1, tm=N_META, tf=tf, emit_h=True, n_dtype=BF16)
    pm = dict(tm=N_META, tn=256)
    rk_m = _proj(un_m, w_rk, mode="rope", out_dtype=BF16, tabs=_rope_tables(pos_meta, RET_DK, RET_DK),
                 d=RET_DK, scale=RET_DK ** -0.5, **pm)
    rv_m = _proj(un_m, w_rv, mode="plain", out_dtype=BF16, **pm)
    ak_m = _proj(un_m, w_ak, mode="rope", out_dtype=BF16, tabs=_rope_tables(pos_meta, ATT_DH, LANES),
                 d=ATT_DH, **pm)
    av_m = _proj(un_m, w_av, mode="plain", out_dtype=BF16, **pm)
    pad_meta = lambda a: jnp.pad(a, ((0, LANES - N_META), (0, 0)))

    tm = _tile(S, 1024)
    h1, un = _ffn(x.reshape(M, D), *ffn1, tm=tm, tf=tf, emit_h=True, n_dtype=BF16)
    pr = dict(tm=tm, tn=256)
    tab_r = _rope_tables(pos_real, RET_DK, RET_DK)
    tab_a = _rope_tables(pos_real, ATT_DH, LANES)
    tab_i = _rope_tables(pos_real, IDX_DH, LANES)
    rq = _proj(un, w_rq, mode="rope", out_dtype=BF16, tabs=tab_r, d=RET_DK, **pr)
    rk = _proj(un, w_rk, mode="rope", out_dtype=BF16, tabs=tab_r, d=RET_DK, scale=RET_DK ** -0.5, **pr)
    rv = _proj(un, w_rv, mode="plain", out_dtype=BF16, **pr)
    rg = _proj(un, w_rg, mode="silu", out_dtype=F32, **pr)
    aq = _proj(un, w_aq, mode="rope", out_dtype=BF16, tabs=tab_a, d=ATT_DH, **pr)
    ak = _proj(un, w_ak, mode="rope", out_dtype=BF16, tabs=tab_a, d=ATT_DH, **pr)
    av = _proj(un, w_av, mode="plain", out_dtype=BF16, **pr)
    gates = _proj(un, w_gate, mode="sigmoid", out_dtype=F32, **pr)
    iq, ikw = _idx_proj(un, w_idx, tab_i, tm=tm)

    heads = jnp.arange(RET_HEADS, dtype=F32)
    log_g = jnp.log1p(-(2.0 ** (-5.0 - heads)))
    C = RET_CHUNK
    b3 = lambda a: a.reshape(B, S, a.shape[-1])
    yr = _retention(log_g, jnp.exp(log_g * C), b3(rq), b3(rk), b3(rv), b3(rg), rk_m, rv_m, C=C)
    ya = _dsa(b3(iq), b3(ikw), b3(aq), b3(ak), b3(av), pad_meta(ak_m), pad_meta(av_m),
              tq=_tile(S, 256), tk=_tile(S, 256), k_top=k_top)

    h2 = _merge(h1, yr.reshape(M, RET_V), ya.reshape(M, ATT_W), gates, bf(w_ret_out[0]), bf(w_att_out[0]),
                bf(w_mix_out[0]), tm=_tile(M, 512))
    (y,) = _ffn(h2, *ffn2, tm=tm, tf=tf, emit_h=False, n_dtype=F32)
    return y.reshape(B, S, D)
```

```python
import functools

import jax
import jax.numpy as jnp
import numpy as np
from jax import lax
from jax.experimental import pallas as pl
from jax.experimental.pallas import tpu as pltpu

N_META = 16
RET_HEADS = 4
RET_DK = 256
RET_DV = 512
RET_CHUNK = 128
ATT_HEADS = 8
ATT_DH = 128
IDX_HEADS = 8
IDX_DH = 64
TOPK_MAX = 256
ROPE_THETA = 10000.0
EPS = 1e-6
RET_QK = RET_HEADS * RET_DK
RET_V = RET_HEADS * RET_DV
ATT_W = ATT_HEADS * ATT_DH
IDX_Q = IDX_HEADS * IDX_DH

LANES = 128
VMEM_LIMIT = 48 * 1024 * 1024

BF16 = jnp.bfloat16
F32 = jnp.float32
NT_DIMS = (((1,), (1,)), ((), ()))


def _dot(a, b):
    return jnp.dot(a, b, preferred_element_type=F32)


def _dot_nt(a, b):
    return lax.dot_general(a, b, NT_DIMS, preferred_element_type=F32)


def _rms(x, g):
    return x * lax.rsqrt(jnp.mean(x * x, axis=-1, keepdims=True) + EPS) * g


def _sigmoid(x):
    return 1.0 / (1.0 + jnp.exp(-x))


def _params(sem):
    return pltpu.CompilerParams(dimension_semantics=sem, vmem_limit_bytes=VMEM_LIMIT)


def _ffn_kernel(x_ref, g_ref, wg_ref, wu_ref, wd_ref, g2_ref, *rest, nf, emit_h):
    if emit_h:
        h_ref, n_ref, xn_sc, acc_sc = rest
    else:
        n_ref, xn_sc, acc_sc = rest
    f = pl.program_id(1)

    @pl.when(f == 0)
    def _():
        xn_sc[...] = _rms(x_ref[...], g_ref[...]).astype(BF16)
        acc_sc[...] = jnp.zeros_like(acc_sc)

    xn = xn_sc[...]
    gate = _dot(xn, wg_ref[...])
    up = _dot(xn, wu_ref[...])
    act = (gate * _sigmoid(gate)) * up
    acc_sc[...] += _dot(act.astype(BF16), wd_ref[...])

    @pl.when(f == nf - 1)
    def _():
        h = x_ref[...] + 0.5 * acc_sc[...]
        if emit_h:
            h_ref[...] = h
        n_ref[...] = _rms(h, g2_ref[...]).astype(n_ref.dtype)


def _ffn(x, g, wg, wu, wd, g2, *, tm, emit_h, n_dtype):
    M, D = x.shape
    nf, _, tf = wg.shape
    out_shape = [jax.ShapeDtypeStruct((M, D), n_dtype)]
    out_specs = [pl.BlockSpec((tm, D), lambda i, f: (i, 0))]
    if emit_h:
        out_shape = [jax.ShapeDtypeStruct((M, D), F32)] + out_shape
        out_specs = [pl.BlockSpec((tm, D), lambda i, f: (i, 0))] + out_specs
    return pl.pallas_call(
        functools.partial(_ffn_kernel, nf=nf, emit_h=emit_h),
        grid=(M // tm, nf),
        in_specs=[
            pl.BlockSpec((tm, D), lambda i, f: (i, 0)),
            pl.BlockSpec((1, D), lambda i, f: (0, 0)),
            pl.BlockSpec((None, D, tf), lambda i, f: (f, 0, 0)),
            pl.BlockSpec((None, D, tf), lambda i, f: (f, 0, 0)),
            pl.BlockSpec((tf, D), lambda i, f: (f, 0)),
            pl.BlockSpec((1, D), lambda i, f: (0, 0)),
        ],
        out_specs=out_specs,
        out_shape=out_shape,
        scratch_shapes=[pltpu.VMEM((tm, D), BF16), pltpu.VMEM((tm, D), F32)],
        compiler_params=_params(("parallel", "arbitrary")),
        name="ffn",
    )(x, g, wg, wu, wd, g2)


def _rot_half(y, d):
    w = y.shape[-1]
    if d >= w:
        return jnp.concatenate([y[:, w // 2:], y[:, : w // 2]], axis=-1)
    if d == LANES:
        return pltpu.roll(y, d // 2, axis=1)
    fwd = pltpu.roll(y, d // 2, axis=1)
    bwd = pltpu.roll(y, w - d // 2, axis=1)
    lane = lax.broadcasted_iota(jnp.int32, y.shape, 1)
    return jnp.where((lane % d) < d // 2, bwd, fwd)


def _rope_tile(y, cos, sin, d):
    tw = cos.shape[-1]
    outs = []
    for t in range(y.shape[-1] // tw):
        yt = y[:, t * tw:(t + 1) * tw]
        outs.append(yt * cos + _rot_half(yt, d) * sin)
    return outs[0] if len(outs) == 1 else jnp.concatenate(outs, axis=-1)


def _proj_kernel(x_ref, w_ref, *rest, mode, d, scale):
    y = _dot(x_ref[...], w_ref[...])
    if mode == "rope":
        cos_ref, sin_ref, o_ref = rest
        y = _rope_tile(y, cos_ref[...], sin_ref[...], d)
        if scale != 1.0:
            y = y * scale
    else:
        (o_ref,) = rest
        if mode == "silu":
            y = y * _sigmoid(y)
        elif mode == "sigmoid":
            y = _sigmoid(y)
    o_ref[...] = y.astype(o_ref.dtype)


def _proj(x, w, *, tm, tn, mode, out_dtype, tabs=None, d=None, scale=1.0):
    M, K = x.shape
    N = w.shape[1]
    in_specs = [
        pl.BlockSpec((tm, K), lambda i, j: (i, 0)),
        pl.BlockSpec((K, tn), lambda i, j: (0, j)),
    ]
    args = [x, w]
    if mode == "rope":
        cos, sin = tabs
        npos = cos.shape[0] // tm
        tw = cos.shape[1]
        in_specs += [pl.BlockSpec((tm, tw), lambda i, j: (i % npos, 0))] * 2
        args += [cos, sin]
    return pl.pallas_call(
        functools.partial(_proj_kernel, mode=mode, d=d, scale=scale),
        grid=(M // tm, N // tn),
        in_specs=in_specs,
        out_specs=pl.BlockSpec((tm, tn), lambda i, j: (i, j)),
        out_shape=jax.ShapeDtypeStruct((M, N), out_dtype),
        compiler_params=_params(("parallel", "arbitrary")),
        name="proj_" + mode,
    )(*args)


def _idx_proj_kernel(x_ref, w_ref, cos_ref, sin_ref, q_ref, kw_ref):
    y = _dot(x_ref[...], w_ref[...])
    cos, sin = cos_ref[...], sin_ref[...]
    q_ref[...] = _rope_tile(y[:, :IDX_Q], cos, sin, IDX_DH)
    tail = y[:, IDX_Q:]
    lane = lax.broadcasted_iota(jnp.int32, tail.shape, 1)
    kw_ref[...] = jnp.where(lane < IDX_DH, _rope_tile(tail, cos, sin, IDX_DH), tail)


def _idx_proj(x, w, tabs, *, tm):
    M, K = x.shape
    N = w.shape[1]
    cos, sin = tabs
    npos = cos.shape[0] // tm
    return pl.pallas_call(
        _idx_proj_kernel,
        grid=(M // tm,),
        in_specs=[
            pl.BlockSpec((tm, K), lambda i: (i, 0)),
            pl.BlockSpec((K, N), lambda i: (0, 0)),
            pl.BlockSpec((tm, LANES), lambda i: (i % npos, 0)),
            pl.BlockSpec((tm, LANES), lambda i: (i % npos, 0)),
        ],
        out_specs=[
            pl.BlockSpec((tm, IDX_Q), lambda i: (i, 0)),
            pl.BlockSpec((tm, LANES), lambda i: (i, 0)),
        ],
        out_shape=[
            jax.ShapeDtypeStruct((M, IDX_Q), F32),
            jax.ShapeDtypeStruct((M, LANES), F32),
        ],
        compiler_params=_params(("parallel",)),
        name="proj_idx",
    )(x, w, cos, sin)


def _ret_kernel(lg_ref, gc_ref, q_ref, k_ref, v_ref, g_ref, km_ref, vm_ref, o_ref, st_ref, *, C):
    h = pl.program_id(1)
    c = pl.program_id(2)
    lg = lg_ref[h]

    @pl.when(c == 0)
    def _():
        pm = lax.broadcasted_iota(jnp.int32, (N_META, 1), 0).astype(F32)
        zeta_m = jnp.exp(lg * (N_META - 1.0 - pm))
        kz = km_ref[...].astype(F32) * zeta_m
        st_ref[...] = _dot(kz.T.astype(BF16), vm_ref[...])

    q = q_ref[0]
    k = k_ref[0]
    v = v_ref[0]
    ri = lax.broadcasted_iota(jnp.int32, (C, C), 0)
    ci = lax.broadcasted_iota(jnp.int32, (C, C), 1)
    diff = (ri - ci).astype(F32)
    decay = jnp.where(diff >= 0, jnp.exp(lg * jnp.maximum(diff, 0.0)), 0.0)
    pos = lax.broadcasted_iota(jnp.int32, (C, 1), 0).astype(F32)
    xi = jnp.exp(lg * (pos + 1.0))
    zeta = jnp.exp(lg * (C - 1.0 - pos))

    state = st_ref[...]
    inner = _dot((_dot_nt(q, k) * decay).astype(BF16), v)
    cross = _dot(q, state.astype(BF16)) * xi
    y = inner + cross
    y = y * lax.rsqrt(jnp.mean(y * y, axis=-1, keepdims=True) + EPS)
    o_ref[0] = (g_ref[0] * y).astype(o_ref.dtype)
    kz = k.astype(F32) * zeta
    st_ref[...] = gc_ref[h] * state + _dot(kz.T.astype(BF16), v)


def _retention(lg, gc, rq, rk, rv, rg, km, vm, *, C):
    B, S, _ = rq.shape
    H = RET_HEADS
    smem = pl.BlockSpec(memory_space=pltpu.SMEM)
    return pl.pallas_call(
        functools.partial(_ret_kernel, C=C),
        grid=(B, H, S // C),
        in_specs=[
            smem,
            smem,
            pl.BlockSpec((1, C, RET_DK), lambda b, h, c: (b, c, h)),
            pl.BlockSpec((1, C, RET_DK), lambda b, h, c: (b, c, h)),
            pl.BlockSpec((1, C, RET_DV), lambda b, h, c: (b, c, h)),
            pl.BlockSpec((1, C, RET_DV), lambda b, h, c: (b, c, h)),
            pl.BlockSpec((N_META, RET_DK), lambda b, h, c: (0, h)),
            pl.BlockSpec((N_META, RET_DV), lambda b, h, c: (0, h)),
        ],
        out_specs=pl.BlockSpec((1, C, RET_DV), lambda b, h, c: (b, c, h)),
        out_shape=jax.ShapeDtypeStruct((B, S, RET_V), BF16),
        scratch_shapes=[pltpu.VMEM((RET_DK, RET_DV), F32)],
        compiler_params=_params(("parallel", "parallel", "arbitrary")),
        name="retention",
    )(lg, gc, rq, rk, rv, rg, km, vm)


TINY = float(np.finfo(np.float32).tiny)
TN_DIMS = (((0,), (0,)), ((), ()))


def _dsa_kernel(iq_ref, ikw_ref, wq_ref, aq_ref, ak_ref, av_ref, akm_ref, avm_ref, o_ref,
                sc_ref, tau_ref, m_ref, l_ref, acc_ref, s_ref, bm_ref, *, tq, tk, k_top):
    qi = pl.program_id(1)
    nkb = (qi + 1) * (tq // tk)
    kf = float(k_top)
    neg_inf = -jnp.inf
    qpos = qi * tq + lax.broadcasted_iota(jnp.int32, (1, tq), 1)
    kpos0 = lax.broadcasted_iota(jnp.int32, (tk, 1), 0)

    def fold(x):
        return x.reshape(tk // 8, 8, tq).sum(axis=0)

    idx_scale = (IDX_DH ** -0.5) * (IDX_HEADS ** -0.5)
    iq = iq_ref[0].astype(BF16)
    w_t = wq_ref[0].T
    iq_h = [iq[:, h * IDX_DH:(h + 1) * IDX_DH] for h in range(IDX_HEADS)]
    w_h = [w_t[IDX_DH + h:IDX_DH + h + 1, :] for h in range(IDX_HEADS)]

    def score_block(kb, carry):
        rmin, rmax = carry
        start = pl.multiple_of(kb * tk, tk)
        ik = ikw_ref[0, pl.ds(start, tk), :][:, :IDX_DH].astype(BF16)
        acc = jnp.zeros((tk, tq), F32)
        for h in range(IDX_HEADS):
            acc = acc + w_h[h] * jnp.maximum(_dot_nt(ik, iq_h[h]), 0.0)
        ok = (kpos0 + kb * tk) <= qpos
        s = jnp.where(ok, acc * idx_scale, neg_inf)
        sc_ref[kb] = s
        rmin = jnp.minimum(rmin, jnp.min(jnp.where(ok, s, jnp.inf), axis=0, keepdims=True))
        rmax = jnp.maximum(rmax, jnp.max(s, axis=0, keepdims=True))
        return rmin, rmax

    rmin, rmax = lax.fori_loop(
        0, nkb, score_block, (jnp.full((1, tq), jnp.inf, F32), jnp.full((1, tq), neg_inf, F32)))

    def count(pred):
        def body(kb, c):
            return c + fold(jnp.where(pred(sc_ref[kb]), 1.0, 0.0))
        return jnp.sum(lax.fori_loop(0, nkb, body, jnp.zeros((8, tq), F32)), axis=0, keepdims=True)

    def any_query(flag):
        return jnp.max(jnp.where(flag, 1.0, 0.0))

    c_max = count(lambda s: s >= rmax)
    top_tied = c_max >= kf
    lo0 = jnp.where(top_tied, rmax, rmin)
    cl0 = jnp.where(top_tied, c_max, (qpos + 1).astype(F32))

    def midpoint(lo, hi):
        a = jnp.maximum(jnp.abs(lo), TINY)
        b = jnp.maximum(jnp.abs(hi), TINY)
        geo = jnp.sqrt(a) * jnp.sqrt(b)
        one_sign = (lo >= 0.0) | (hi <= 0.0)
        far = jnp.maximum(a, b) > 4.0 * jnp.minimum(a, b)
        mid = jnp.where(one_sign & far, jnp.where(lo >= 0.0, geo, -geo), 0.5 * lo + 0.5 * hi)
        mid = jnp.where(lo == 0.0, TINY, mid)
        mid = jnp.where(hi == 0.0, -TINY, mid)
        return jnp.where((lo < 0.0) & (hi > 0.0), 0.0, mid)

    def active(lo, hi, cl):
        mid = midpoint(lo, hi)
        return (cl > kf) & (mid > lo) & (mid < hi)

    def go_on(st):
        return st[3] > 0.0

    def shrink_once(lo, hi, cl):
        mid = midpoint(lo, hi)
        act = active(lo, hi, cl)
        c = count(lambda s: s >= mid)
        up = act & (c >= kf)
        dn = act & (c < kf)
        return jnp.where(up, mid, lo), jnp.where(dn, mid, hi), jnp.where(up, c, cl)

    def shrink(st):
        lo, hi, cl = shrink_once(*shrink_once(*st[:3]))
        return lo, hi, cl, any_query(active(lo, hi, cl))

    lo, hi, cl, _ = lax.while_loop(go_on, shrink, (lo0, rmax, cl0, any_query(active(lo0, rmax, cl0))))
    tau_ref[...] = lo
    tie = cl > kf

    @pl.when(any_query(tie) > 0.0)
    def _():
        def max_below(bound):
            def body(kb, v):
                s = sc_ref[kb]
                return jnp.maximum(v, jnp.max(jnp.where(s < bound, s, neg_inf), axis=0, keepdims=True))
            return lax.fori_loop(0, nkb, body, jnp.full((1, tq), neg_inf, F32))

        def count_ge_gt(v):
            def body(kb, c):
                s = sc_ref[kb]
                return c[0] + fold(jnp.where(s >= v, 1.0, 0.0)), c[1] + fold(jnp.where(s > v, 1.0, 0.0))
            zero = jnp.zeros((8, tq), F32)
            ge, gt = lax.fori_loop(0, nkb, body, (zero, zero))
            return jnp.sum(ge, axis=0, keepdims=True), jnp.sum(gt, axis=0, keepdims=True)

        hi_d = jnp.where(lo >= hi, jnp.inf, hi)
        v0 = max_below(hi_d)
        ge0, gt0 = count_ge_gt(v0)

        def walk_on(st):
            return st[4] > 0.0

        def walk(st):
            hi_w, v, ge, _, _ = st
            hi_w = jnp.where(tie & (ge < kf), v, hi_w)
            v = max_below(hi_w)
            ge, gt = count_ge_gt(v)
            return hi_w, v, ge, gt, any_query(tie & (ge < kf))

        _, v, _, gt, _ = lax.while_loop(walk_on, walk, (hi_d, v0, ge0, gt0, any_query(tie & (ge0 < kf))))
        keep = kf - gt
        ri = lax.broadcasted_iota(jnp.int32, (tk, tk), 0)
        ci = lax.broadcasted_iota(jnp.int32, (tk, tk), 1)
        earlier = jnp.where(ci < ri, 1.0, 0.0).astype(BF16)

        def drop(kb, seen):
            s = sc_ref[kb]
            eq = tie & (s == v)
            eqf = jnp.where(eq, 1.0, 0.0)
            rank = seen + _dot(earlier, eqf.astype(BF16))
            sc_ref[kb] = jnp.where(eq & (rank >= keep), neg_inf, s)
            return seen + jnp.sum(eqf, axis=0, keepdims=True)

        lax.fori_loop(0, nkb, drop, jnp.zeros((1, tq), F32))
        tau_ref[...] = jnp.where(tie, v, lo)

    c_exp = (ATT_DH ** -0.5) * float(np.log2(np.e))
    tau = tau_ref[...]
    meta_ok = lax.broadcasted_iota(jnp.int32, (LANES, 1), 0) < N_META

    def head(ref, h, rows=None):
        if rows is None:
            return ref[:, h * ATT_DH:(h + 1) * ATT_DH]
        return ref[0, rows, h * ATT_DH:(h + 1) * ATT_DH]

    for h in range(ATT_HEADS):
        s = jnp.where(meta_ok, _dot_nt(head(akm_ref, h), head(aq_ref, h, slice(None))), neg_inf)
        m = jnp.max(s, axis=0, keepdims=True)
        p = jnp.exp2((s - m) * c_exp)
        m_ref[h:h + 1, :] = m
        l_ref[h:h + 1, :] = jnp.sum(p, axis=0, keepdims=True)
        acc_ref[h] = lax.dot_general(head(avm_ref, h), p.astype(BF16), TN_DIMS, preferred_element_type=F32)

    def att_block(kb, carry):
        start = pl.multiple_of(kb * tk, tk)
        bias = jnp.where(sc_ref[kb] >= tau, 0.0, neg_inf)
        for h in range(ATT_HEADS):
            s = _dot_nt(head(ak_ref, h, pl.ds(start, tk)), head(aq_ref, h, slice(None))) + bias
            s_ref[h] = s
            bm_ref[h:h + 1, :] = jnp.max(s, axis=0, keepdims=True)

        @pl.when(kb < nkb)
        def _():
            for h in range(ATT_HEADS):
                m_new = jnp.maximum(m_ref[h:h + 1, :], bm_ref[h:h + 1, :])
                p = jnp.exp2((s_ref[h] - m_new) * c_exp)
                p_sum = jnp.sum(p, axis=0, keepdims=True)
                pv = lax.dot_general(head(av_ref, h, pl.ds(start, tk)), p.astype(BF16), TN_DIMS,
                                     preferred_element_type=F32)
                alpha = jnp.exp2((m_ref[h:h + 1, :] - m_new) * c_exp)
                l_ref[h:h + 1, :] = alpha * l_ref[h:h + 1, :] + p_sum
                acc_ref[h] = alpha * acc_ref[h] + pv
                m_ref[h:h + 1, :] = m_new
        return carry

    lax.fori_loop(0, nkb, att_block, 0)
    for h in range(ATT_HEADS):
        out_t = acc_ref[h] / l_ref[h:h + 1, :]
        o_ref[0, :, h * ATT_DH:(h + 1) * ATT_DH] = out_t.T.astype(o_ref.dtype)


def _dsa(iq, ikw, aq, ak, av, akm, avm, *, tq, tk, k_top):
    B, S, _ = aq.shape
    nq = S // tq
    assert tq % tk == 0
    return pl.pallas_call(
        functools.partial(_dsa_kernel, tq=tq, tk=tk, k_top=k_top),
        grid=(B, nq),
        in_specs=[
            pl.BlockSpec((1, tq, IDX_Q), lambda b, q: (b, q, 0)),
            pl.BlockSpec((1, S, LANES), lambda b, q: (b, 0, 0)),
            pl.BlockSpec((1, tq, LANES), lambda b, q: (b, q, 0)),
            pl.BlockSpec((1, tq, ATT_W), lambda b, q: (b, q, 0)),
            pl.BlockSpec((1, S, ATT_W), lambda b, q: (b, 0, 0)),
            pl.BlockSpec((1, S, ATT_W), lambda b, q: (b, 0, 0)),
            pl.BlockSpec((LANES, ATT_W), lambda b, q: (0, 0)),
            pl.BlockSpec((LANES, ATT_W), lambda b, q: (0, 0)),
        ],
        out_specs=pl.BlockSpec((1, tq, ATT_W), lambda b, q: (b, q, 0)),
        out_shape=jax.ShapeDtypeStruct((B, S, ATT_W), BF16),
        scratch_shapes=[
            pltpu.VMEM((S // tk, tk, tq), F32),
            pltpu.VMEM((1, tq), F32),
            pltpu.VMEM((ATT_HEADS, tq), F32),
            pltpu.VMEM((ATT_HEADS, tq), F32),
            pltpu.VMEM((ATT_HEADS, ATT_DH, tq), F32),
            pltpu.VMEM((ATT_HEADS, tk, tq), F32),
            pltpu.VMEM((ATT_HEADS, tq), F32),
        ],
        compiler_params=_params(("parallel", "arbitrary")),
        name="dsa",
    )(iq, ikw, ikw, aq, ak, av, akm, avm)


def _merge_kernel(h_ref, yr_ref, ya_ref, gr_ref, ga_ref, wr_ref, wa_ref, wm_ref, o_ref):
    yr = _dot(yr_ref[...], wr_ref[...])
    ya = _dot(ya_ref[...], wa_ref[...])
    merged = gr_ref[...] * yr + ga_ref[...] * ya
    o_ref[...] = h_ref[...] + _dot(merged.astype(BF16), wm_ref[...])


def _merge(h, yr, ya, gates, wr, wa, wm, *, tm):
    M, D = h.shape
    row = lambda i: (i, 0)
    fixed = lambda i: (0, 0)
    return pl.pallas_call(
        _merge_kernel,
        grid=(M // tm,),
        in_specs=[
            pl.BlockSpec((tm, D), row),
            pl.BlockSpec((tm, RET_V), row),
            pl.BlockSpec((tm, ATT_W), row),
            pl.BlockSpec((tm, D), lambda i: (i, 0)),
            pl.BlockSpec((tm, D), lambda i: (i, 1)),
            pl.BlockSpec((RET_V, D), fixed),
            pl.BlockSpec((ATT_W, D), fixed),
            pl.BlockSpec((D, D), fixed),
        ],
        out_specs=pl.BlockSpec((tm, D), row),
        out_shape=jax.ShapeDtypeStruct((M, D), F32),
        compiler_params=_params(("parallel",)),
        name="merge",
    )(h, yr, ya, gates, gates, wr, wa, wm)


def _rope_tables(pos, d, width):
    inv_freq = ROPE_THETA ** (-jnp.arange(0, d, 2, dtype=F32) / d)
    ang = pos.astype(F32)[:, None] * inv_freq[None, :]
    cos, sin = jnp.cos(ang), jnp.sin(ang)
    cos = jnp.concatenate([cos, cos], axis=-1)
    sin = jnp.concatenate([-sin, sin], axis=-1)
    rep = width // d
    return jnp.tile(cos, (1, rep)), jnp.tile(sin, (1, rep))


def _tile(n, cap):
    t = min(n, cap)
    assert n % t == 0, (n, t)
    return t


def kernel(x, meta_tokens, ffn1_norm, ffn1_w_gate, ffn1_w_up, ffn1_w_down, mix_norm, w_in, w_ret_out,
           w_att_out, w_mix_out, ffn2_norm, ffn2_w_gate, ffn2_w_up, ffn2_w_down, final_norm):
    B, S, D = x.shape
    assert ffn1_norm.shape[0] == 1, "single layer only"
    assert meta_tokens.shape[0] == N_META
    k_top = min(TOPK_MAX, (S + 0) // 4)
    M = B * S
    bf = lambda a: a.astype(BF16)
    row = lambda a: a.reshape(1, -1)

    offs = np.cumsum([0, RET_QK, RET_QK, RET_V, RET_V, ATT_W, ATT_W, ATT_W, IDX_Q, IDX_DH, IDX_HEADS, D, D])
    w = bf(w_in[0])
    col = lambda a, b: w[:, offs[a]:offs[b]]
    w_rq, w_rk, w_rv, w_rg, w_aq, w_ak, w_av = (col(i, i + 1) for i in range(7))
    w_idx = jnp.pad(col(7, 10), ((0, 0), (0, LANES - IDX_DH - IDX_HEADS)))
    w_gate = col(10, 12)

    pos_meta = jnp.arange(N_META, dtype=jnp.int32)
    pos_real = N_META + jnp.arange(S, dtype=jnp.int32)
    tf = 256
    slab = lambda a: bf(a).reshape(D, -1, tf).transpose(1, 0, 2)
    ffn1 = (row(ffn1_norm[0]), slab(ffn1_w_gate[0]), slab(ffn1_w_up[0]), bf(ffn1_w_down[0]), row(mix_norm[0]))
    ffn2 = (row(ffn2_norm[0]), slab(ffn2_w_gate[0]), slab(ffn2_w_up[0]), bf(ffn2_w_down[0]), row(final_norm))

    _, un_m = _ffn(meta_tokens.astype(F32), *ffn1, tm=N_META, emit_h=True, n_dtype=BF16)
    pm = dict(tm=N_META, tn=1024)
    rk_m = _proj(un_m, w_rk, mode="rope", out_dtype=BF16, tabs=_rope_tables(pos_meta, RET_DK, RET_DK),
                 d=RET_DK, scale=RET_DK ** -0.5, **pm)
    rv_m = _proj(un_m, w_rv, mode="plain", out_dtype=BF16, **pm)
    ak_m = _proj(un_m, w_ak, mode="rope", out_dtype=BF16, tabs=_rope_tables(pos_meta, ATT_DH, LANES),
                 d=ATT_DH, **pm)
    av_m = _proj(un_m, w_av, mode="plain", out_dtype=BF16, **pm)
    pad_meta = lambda a: jnp.pad(a, ((0, LANES - N_META), (0, 0)))

    tm = _tile(S, 1024)
    h1, un = _ffn(x.reshape(M, D), *ffn1, tm=tm, emit_h=True, n_dtype=BF16)
    pr = dict(tm=tm, tn=1024)
    tab_r = _rope_tables(pos_real, RET_DK, RET_DK)
    tab_a = _rope_tables(pos_real, ATT_DH, LANES)
    tab_i = _rope_tables(pos_real, IDX_DH, LANES)
    rq = _proj(un, w_rq, mode="rope", out_dtype=BF16, tabs=tab_r, d=RET_DK, **pr)
    rk = _proj(un, w_rk, mode="rope", out_dtype=BF16, tabs=tab_r, d=RET_DK, scale=RET_DK ** -0.5, **pr)
    rv = _proj(un, w_rv, mode="plain", out_dtype=BF16, **pr)
    rg = _proj(un, w_rg, mode="silu", out_dtype=F32, **pr)
    aq = _proj(un, w_aq, mode="rope", out_dtype=BF16, tabs=tab_a, d=ATT_DH, **pr)
    ak = _proj(un, w_ak, mode="rope", out_dtype=BF16, tabs=tab_a, d=ATT_DH, **pr)
    av = _proj(un, w_av, mode="plain", out_dtype=BF16, **pr)
    gates = _proj(un, w_gate, mode="sigmoid", out_dtype=F32, **pr)
    iq, ikw = _idx_proj(un, w_idx, tab_i, tm=tm)

    heads = jnp.arange(RET_HEADS, dtype=F32)
    log_g = jnp.log1p(-(2.0 ** (-5.0 - heads)))
    C = _tile(S, 4 * RET_CHUNK)
    b3 = lambda a: a.reshape(B, S, a.shape[-1])
    yr = _retention(log_g, jnp.exp(log_g * C), b3(rq), b3(rk), b3(rv), b3(rg), rk_m, rv_m, C=C)
    ya = _dsa(b3(iq), b3(ikw), b3(aq), b3(ak), b3(av), pad_meta(ak_m), pad_meta(av_m),
              tq=_tile(S, 256), tk=_tile(S, 256), k_top=k_top)

    h2 = _merge(h1, yr.reshape(M, RET_V), ya.reshape(M, ATT_W), gates, bf(w_ret_out[0]), bf(w_att_out[0]),
                bf(w_mix_out[0]), tm=_tile(M, 512))
    (y,) = _ffn(h2, *ffn2, tm=tm, emit_h=False, n_dtype=F32)
    return y.reshape(B, S, D)
```

```python
import functools

import jax
import jax.numpy as jnp
import numpy as np
from jax import lax
from jax.experimental import pallas as pl
from jax.experimental.pallas import tpu as pltpu

N_META = 16
RET_HEADS = 4
RET_DK = 256
RET_DV = 512
RET_CHUNK = 128
ATT_HEADS = 8
ATT_DH = 128
IDX_HEADS = 8
IDX_DH = 64
TOPK_MAX = 256
ROPE_THETA = 10000.0
EPS = 1e-6
RET_QK = RET_HEADS * RET_DK
RET_V = RET_HEADS * RET_DV
ATT_W = ATT_HEADS * ATT_DH
IDX_Q = IDX_HEADS * IDX_DH

LANES = 128
VMEM_LIMIT = 48 * 1024 * 1024

BF16 = jnp.bfloat16
F32 = jnp.float32
NT_DIMS = (((1,), (1,)), ((), ()))


def _dot(a, b):
    return jnp.dot(a, b, preferred_element_type=F32)


def _dot_nt(a, b):
    return lax.dot_general(a, b, NT_DIMS, preferred_element_type=F32)


def _rms(x, g):
    return x * lax.rsqrt(jnp.mean(x * x, axis=-1, keepdims=True) + EPS) * g


def _sigmoid(x):
    return 1.0 / (1.0 + jnp.exp(-x))


def _params(sem):
    return pltpu.CompilerParams(dimension_semantics=sem, vmem_limit_bytes=VMEM_LIMIT)


def _ffn_kernel(x_ref, g_ref, wg_ref, wu_ref, wd_ref, g2_ref, *rest, nf, emit_h):
    if emit_h:
        h_ref, n_ref, xn_sc, acc_sc = rest
    else:
        n_ref, xn_sc, acc_sc = rest
    f = pl.program_id(1)

    @pl.when(f == 0)
    def _():
        xn_sc[...] = _rms(x_ref[...], g_ref[...]).astype(BF16)
        acc_sc[...] = jnp.zeros_like(acc_sc)

    xn = xn_sc[...]
    gate = _dot(xn, wg_ref[...])
    up = _dot(xn, wu_ref[...])
    act = (gate * _sigmoid(gate)) * up
    acc_sc[...] += _dot(act.astype(BF16), wd_ref[...])

    @pl.when(f == nf - 1)
    def _():
        h = x_ref[...] + 0.5 * acc_sc[...]
        if emit_h:
            h_ref[...] = h
        n_ref[...] = _rms(h, g2_ref[...]).astype(n_ref.dtype)


def _ffn(x, g, wg, wu, wd, g2, *, tm, emit_h, n_dtype):
    M, D = x.shape
    nf, _, tf = wg.shape
    out_shape = [jax.ShapeDtypeStruct((M, D), n_dtype)]
    out_specs = [pl.BlockSpec((tm, D), lambda i, f: (i, 0))]
    if emit_h:
        out_shape = [jax.ShapeDtypeStruct((M, D), F32)] + out_shape
        out_specs = [pl.BlockSpec((tm, D), lambda i, f: (i, 0))] + out_specs
    return pl.pallas_call(
        functools.partial(_ffn_kernel, nf=nf, emit_h=emit_h),
        grid=(M // tm, nf),
        in_specs=[
            pl.BlockSpec((tm, D), lambda i, f: (i, 0)),
            pl.BlockSpec((1, D), lambda i, f: (0, 0)),
            pl.BlockSpec((None, D, tf), lambda i, f: (f, 0, 0)),
            pl.BlockSpec((None, D, tf), lambda i, f: (f, 0, 0)),
            pl.BlockSpec((tf, D), lambda i, f: (f, 0)),
            pl.BlockSpec((1, D), lambda i, f: (0, 0)),
        ],
        out_specs=out_specs,
        out_shape=out_shape,
        scratch_shapes=[pltpu.VMEM((tm, D), BF16), pltpu.VMEM((tm, D), F32)],
        compiler_params=_params(("parallel", "arbitrary")),
        name="ffn",
    )(x, g, wg, wu, wd, g2)


def _rot_half(y, d):
    w = y.shape[-1]
    if d >= w:
        return jnp.concatenate([y[:, w // 2:], y[:, : w // 2]], axis=-1)
    if d == LANES:
        return pltpu.roll(y, d // 2, axis=1)
    fwd = pltpu.roll(y, d // 2, axis=1)
    bwd = pltpu.roll(y, w - d // 2, axis=1)
    lane = lax.broadcasted_iota(jnp.int32, y.shape, 1)
    return jnp.where((lane % d) < d // 2, bwd, fwd)


def _rope_tile(y, cos, sin, d):
    tw = cos.shape[-1]
    outs = []
    for t in range(y.shape[-1] // tw):
        yt = y[:, t * tw:(t + 1) * tw]
        outs.append(yt * cos + _rot_half(yt, d) * sin)
    return outs[0] if len(outs) == 1 else jnp.concatenate(outs, axis=-1)


def _proj_kernel(x_ref, w_ref, *rest, mode, d, k_scale, n_q):
    if mode == "rope":
        cos_ref, sin_ref, o_ref, w_sc = rest
    else:
        o_ref, w_sc = rest

    @pl.when(pl.program_id(1) == 0)
    def _():
        w_sc[...] = w_ref[...].astype(BF16)

    y = _dot(x_ref[...], w_sc[...])
    if mode == "rope":
        y = _rope_tile(y, cos_ref[...], sin_ref[...], d)
        if k_scale != 1.0:
            y = y * jnp.where(pl.program_id(0) >= n_q, k_scale, 1.0)
    elif mode == "silu":
        y = y * _sigmoid(y)
    elif mode == "sigmoid":
        y = _sigmoid(y)
    o_ref[...] = y.astype(o_ref.dtype)


def _proj(x, w, *, col0, n_tiles, tm, tn, mode, out_dtype=None, tabs=None, d=None, k_scale=1.0, n_q=0):
    M, K = x.shape
    in_specs = [
        pl.BlockSpec((tm, K), lambda j, i: (i, 0)),
        pl.BlockSpec((K, tn), lambda j, i: (0, col0 + j)),
    ]
    args = [x, w]
    if mode == "rope":
        cos, sin = tabs
        npos = cos.shape[0] // tm
        tw = cos.shape[1]
        in_specs += [pl.BlockSpec((tm, tw), lambda j, i: (i % npos, 0))] * 2
        args += [cos, sin]
    return pl.pallas_call(
        functools.partial(_proj_kernel, mode=mode, d=d, k_scale=k_scale, n_q=n_q),
        grid=(n_tiles, M // tm),
        in_specs=in_specs,
        out_specs=pl.BlockSpec((tm, tn), lambda j, i: (i, j)),
        out_shape=jax.ShapeDtypeStruct((M, n_tiles * tn), out_dtype or BF16),
        scratch_shapes=[pltpu.VMEM((K, tn), BF16)],
        compiler_params=_params(("arbitrary", "arbitrary")),
        name="proj_" + mode,
    )(*args)


def _idx_proj_kernel(x_ref, w_ref, cos_ref, sin_ref, q_ref, kw_ref):
    y = _dot(x_ref[...], w_ref[...])
    cos, sin = cos_ref[...], sin_ref[...]
    q_ref[...] = _rope_tile(y[:, :IDX_Q], cos, sin, IDX_DH)
    tail = y[:, IDX_Q:]
    lane = lax.broadcasted_iota(jnp.int32, tail.shape, 1)
    kw_ref[...] = jnp.where(lane < IDX_DH, _rope_tile(tail, cos, sin, IDX_DH), tail)


def _idx_proj(x, w, tabs, *, tm):
    M, K = x.shape
    N = w.shape[1]
    cos, sin = tabs
    npos = cos.shape[0] // tm
    return pl.pallas_call(
        _idx_proj_kernel,
        grid=(M // tm,),
        in_specs=[
            pl.BlockSpec((tm, K), lambda i: (i, 0)),
            pl.BlockSpec((K, N), lambda i: (0, 0)),
            pl.BlockSpec((tm, LANES), lambda i: (i % npos, 0)),
            pl.BlockSpec((tm, LANES), lambda i: (i % npos, 0)),
        ],
        out_specs=[
            pl.BlockSpec((tm, IDX_Q), lambda i: (i, 0)),
            pl.BlockSpec((tm, LANES), lambda i: (i, 0)),
        ],
        out_shape=[
            jax.ShapeDtypeStruct((M, IDX_Q), F32),
            jax.ShapeDtypeStruct((M, LANES), F32),
        ],
        compiler_params=_params(("parallel",)),
        name="proj_idx",
    )(x, w, cos, sin)


def _ret_kernel(lg_ref, gc_ref, q_ref, k_ref, v_ref, g_ref, km_ref, vm_ref, o_ref, st_ref, *, C):
    h = pl.program_id(1)
    c = pl.program_id(2)
    lg = lg_ref[h]

    @pl.when(c == 0)
    def _():
        pm = lax.broadcasted_iota(jnp.int32, (N_META, 1), 0).astype(F32)
        zeta_m = jnp.exp(lg * (N_META - 1.0 - pm))
        kz = km_ref[...].astype(F32) * zeta_m
        st_ref[...] = _dot(kz.T.astype(BF16), vm_ref[...])

    q = q_ref[0]
    k = k_ref[0]
    v = v_ref[0]
    ri = lax.broadcasted_iota(jnp.int32, (C, C), 0)
    ci = lax.broadcasted_iota(jnp.int32, (C, C), 1)
    diff = (ri - ci).astype(F32)
    decay = jnp.where(diff >= 0, jnp.exp(lg * jnp.maximum(diff, 0.0)), 0.0)
    pos = lax.broadcasted_iota(jnp.int32, (C, 1), 0).astype(F32)
    xi = jnp.exp(lg * (pos + 1.0))
    zeta = jnp.exp(lg * (C - 1.0 - pos))

    state = st_ref[...]
    inner = _dot((_dot_nt(q, k) * decay).astype(BF16), v)
    cross = _dot(q, state.astype(BF16)) * xi
    y = inner + cross
    y = y * lax.rsqrt(jnp.mean(y * y, axis=-1, keepdims=True) + EPS)
    o_ref[0] = (g_ref[0] * y).astype(o_ref.dtype)
    kz = k.astype(F32) * zeta
    st_ref[...] = gc_ref[h] * state + _dot(kz.T.astype(BF16), v)


def _retention(lg, gc, rqk, rv, rg, km, vm, *, C):
    B, S, _ = rv.shape
    H = RET_HEADS
    smem = pl.BlockSpec(memory_space=pltpu.SMEM)
    return pl.pallas_call(
        functools.partial(_ret_kernel, C=C),
        grid=(B, H, S // C),
        in_specs=[
            smem,
            smem,
            pl.BlockSpec((1, C, RET_DK), lambda b, h, c: (b, c, h)),
            pl.BlockSpec((1, C, RET_DK), lambda b, h, c: (b, c, RET_HEADS + h)),
            pl.BlockSpec((1, C, RET_DV), lambda b, h, c: (b, c, h)),
            pl.BlockSpec((1, C, RET_DV), lambda b, h, c: (b, c, h)),
            pl.BlockSpec((N_META, RET_DK), lambda b, h, c: (0, h)),
            pl.BlockSpec((N_META, RET_DV), lambda b, h, c: (0, h)),
        ],
        out_specs=pl.BlockSpec((1, C, RET_DV), lambda b, h, c: (b, c, h)),
        out_shape=jax.ShapeDtypeStruct((B, S, RET_V), BF16),
        scratch_shapes=[pltpu.VMEM((RET_DK, RET_DV), F32)],
        compiler_params=_params(("parallel", "parallel", "arbitrary")),
        name="retention",
    )(lg, gc, rqk, rqk, rv, rg, km, vm)


TINY = float(np.finfo(np.float32).tiny)
TN_DIMS = (((0,), (0,)), ((), ()))


def _dsa_kernel(iq_ref, ikw_ref, wq_ref, aq_ref, ak_ref, av_ref, akm_ref, avm_ref, o_ref,
                sc_ref, tau_ref, m_ref, l_ref, acc_ref, s_ref, bm_ref, *, nq, tq, tk, k_top):
    qi = pl.program_id(1)
    nkb = (qi + 1) * (tq // tk)
    kf = float(k_top)
    neg_inf = -jnp.inf
    qpos = qi * tq + lax.broadcasted_iota(jnp.int32, (1, tq), 1)
    kpos0 = lax.broadcasted_iota(jnp.int32, (tk, 1), 0)

    def fold(x):
        return x.reshape(tk // 8, 8, tq).sum(axis=0)

    idx_scale = (IDX_DH ** -0.5) * (IDX_HEADS ** -0.5)
    iq = iq_ref[0].astype(BF16)
    w_t = wq_ref[0].T
    iq_h = [iq[:, h * IDX_DH:(h + 1) * IDX_DH] for h in range(IDX_HEADS)]
    w_h = [w_t[IDX_DH + h:IDX_DH + h + 1, :] for h in range(IDX_HEADS)]

    def score_block(kb, carry):
        rmin, rmax = carry
        start = pl.multiple_of(kb * tk, tk)
        ik = ikw_ref[0, pl.ds(start, tk), :][:, :IDX_DH].astype(BF16)
        acc = jnp.zeros((tk, tq), F32)
        for h in range(IDX_HEADS):
            acc = acc + w_h[h] * jnp.maximum(_dot_nt(ik, iq_h[h]), 0.0)
        ok = (kpos0 + kb * tk) <= qpos
        s = jnp.where(ok, acc * idx_scale, neg_inf)
        sc_ref[kb] = s
        rmin = jnp.minimum(rmin, jnp.min(jnp.where(ok, s, jnp.inf), axis=0, keepdims=True))
        rmax = jnp.maximum(rmax, jnp.max(s, axis=0, keepdims=True))
        return rmin, rmax

    rmin, rmax = lax.fori_loop(
        0, nkb, score_block, (jnp.full((1, tq), jnp.inf, F32), jnp.full((1, tq), neg_inf, F32)))

    def count(pred):
        def body(kb, c):
            return c + fold(jnp.where(pred(sc_ref[kb]), 1.0, 0.0))
        return jnp.sum(lax.fori_loop(0, nkb, body, jnp.zeros((8, tq), F32)), axis=0, keepdims=True)

    def any_query(flag):
        return jnp.max(jnp.where(flag, 1.0, 0.0))

    c_max = count(lambda s: s >= rmax)
    top_tied = c_max >= kf
    lo0 = jnp.where(top_tied, rmax, rmin)
    cl0 = jnp.where(top_tied, c_max, (qpos + 1).astype(F32))

    def midpoint(lo, hi):
        a = jnp.maximum(jnp.abs(lo), TINY)
        b = jnp.maximum(jnp.abs(hi), TINY)
        geo = jnp.sqrt(a) * jnp.sqrt(b)
        one_sign = (lo >= 0.0) | (hi <= 0.0)
        far = jnp.maximum(a, b) > 4.0 * jnp.minimum(a, b)
        mid = jnp.where(one_sign & far, jnp.where(lo >= 0.0, geo, -geo), 0.5 * lo + 0.5 * hi)
        mid = jnp.where(lo == 0.0, TINY, mid)
        mid = jnp.where(hi == 0.0, -TINY, mid)
        return jnp.where((lo < 0.0) & (hi > 0.0), 0.0, mid)

    def active(lo, hi, cl):
        mid = midpoint(lo, hi)
        return (cl > kf) & (mid > lo) & (mid < hi)

    def go_on(st):
        return st[3] > 0.0

    def shrink_loop(n_blocks, lo, hi, cl):
        def count_ge(mid):
            c = jnp.zeros((8, tq), F32)
            for kb in range(n_blocks):
                c = c + fold(jnp.where(sc_ref[kb] >= mid, 1.0, 0.0))
            return jnp.sum(c, axis=0, keepdims=True)

        def shrink_once(lo, hi, cl):
            mid = midpoint(lo, hi)
            act = active(lo, hi, cl)
            c = count_ge(mid)
            up = act & (c >= kf)
            dn = act & (c < kf)
            return jnp.where(up, mid, lo), jnp.where(dn, mid, hi), jnp.where(up, c, cl)

        def shrink(st):
            pending = any_query(active(*st[:3]))
            return (*shrink_once(*shrink_once(*st[:3])), pending)

        return lax.while_loop(go_on, shrink, (lo, hi, cl, jnp.float32(1.0)))[:3]

    branches = [functools.partial(shrink_loop, (i + 1) * (tq // tk)) for i in range(nq)]
    lo, hi, cl = lax.switch(qi, branches, lo0, rmax, cl0)
    tau_ref[...] = lo
    tie = cl > kf

    @pl.when(any_query(tie) > 0.0)
    def _():
        def max_below(bound):
            def body(kb, v):
                s = sc_ref[kb]
                return jnp.maximum(v, jnp.max(jnp.where(s < bound, s, neg_inf), axis=0, keepdims=True))
            return lax.fori_loop(0, nkb, body, jnp.full((1, tq), neg_inf, F32))

        def count_ge_gt(v):
            def body(kb, c):
                s = sc_ref[kb]
                return c[0] + fold(jnp.where(s >= v, 1.0, 0.0)), c[1] + fold(jnp.where(s > v, 1.0, 0.0))
            zero = jnp.zeros((8, tq), F32)
            ge, gt = lax.fori_loop(0, nkb, body, (zero, zero))
            return jnp.sum(ge, axis=0, keepdims=True), jnp.sum(gt, axis=0, keepdims=True)

        hi_d = jnp.where(lo >= hi, jnp.inf, hi)
        v0 = max_below(hi_d)
        ge0, gt0 = count_ge_gt(v0)

        def walk_on(st):
            return st[4] > 0.0

        def walk(st):
            hi_w, v, ge, _, _ = st
            hi_w = jnp.where(tie & (ge < kf), v, hi_w)
            v = max_below(hi_w)
            ge, gt = count_ge_gt(v)
            return hi_w, v, ge, gt, any_query(tie & (ge < kf))

        _, v, _, gt, _ = lax.while_loop(walk_on, walk, (hi_d, v0, ge0, gt0, any_query(tie & (ge0 < kf))))
        keep = kf - gt
        ri = lax.broadcasted_iota(jnp.int32, (tk, tk), 0)
        ci = lax.broadcasted_iota(jnp.int32, (tk, tk), 1)
        earlier = jnp.where(ci < ri, 1.0, 0.0).astype(BF16)

        def drop(kb, seen):
            s = sc_ref[kb]
            eq = tie & (s == v)
            eqf = jnp.where(eq, 1.0, 0.0)
            rank = seen + _dot(earlier, eqf.astype(BF16))
            sc_ref[kb] = jnp.where(eq & (rank >= keep), neg_inf, s)
            return seen + jnp.sum(eqf, axis=0, keepdims=True)

        lax.fori_loop(0, nkb, drop, jnp.zeros((1, tq), F32))
        tau_ref[...] = jnp.where(tie, v, lo)

    c_exp = (ATT_DH ** -0.5) * float(np.log2(np.e))
    tau = tau_ref[...]
    meta_ok = lax.broadcasted_iota(jnp.int32, (LANES, 1), 0) < N_META

    def head(ref, h, rows=None):
        if rows is None:
            return ref[:, h * ATT_DH:(h + 1) * ATT_DH]
        return ref[0, rows, h * ATT_DH:(h + 1) * ATT_DH]

    for h in range(ATT_HEADS):
        s = jnp.where(meta_ok, _dot_nt(head(akm_ref, h), head(aq_ref, h, slice(None))), neg_inf)
        m = jnp.max(s, axis=0, keepdims=True)
        p = jnp.exp2((s - m) * c_exp)
        m_ref[h:h + 1, :] = m
        l_ref[h:h + 1, :] = jnp.sum(p, axis=0, keepdims=True)
        acc_ref[h] = lax.dot_general(head(avm_ref, h), p.astype(BF16), TN_DIMS, preferred_element_type=F32)

    def att_block(kb, carry):
        start = pl.multiple_of(kb * tk, tk)
        bias = jnp.where(sc_ref[kb] >= tau, 0.0, neg_inf)
        for h in range(ATT_HEADS):
            s = _dot_nt(head(ak_ref, h, pl.ds(start, tk)), head(aq_ref, h, slice(None))) + bias
            s_ref[h] = s
            bm_ref[h:h + 1, :] = jnp.max(s, axis=0, keepdims=True)

        @pl.when(kb < nkb)
        def _():
            for h in range(ATT_HEADS):
                m_new = jnp.maximum(m_ref[h:h + 1, :], bm_ref[h:h + 1, :])
                p = jnp.exp2((s_ref[h] - m_new) * c_exp)
                p_sum = jnp.sum(p, axis=0, keepdims=True)
                pv = lax.dot_general(head(av_ref, h, pl.ds(start, tk)), p.astype(BF16), TN_DIMS,
                                     preferred_element_type=F32)
                alpha = jnp.exp2((m_ref[h:h + 1, :] - m_new) * c_exp)
                l_ref[h:h + 1, :] = alpha * l_ref[h:h + 1, :] + p_sum
                acc_ref[h] = alpha * acc_ref[h] + pv
                m_ref[h:h + 1, :] = m_new
        return carry

    lax.fori_loop(0, nkb, att_block, 0)
    for h in range(ATT_HEADS):
        out_t = acc_ref[h] / l_ref[h:h + 1, :]
        o_ref[0, :, h * ATT_DH:(h + 1) * ATT_DH] = out_t.T.astype(o_ref.dtype)


def _dsa(iq, ikw, aqk, av, akm, avm, *, tq, tk, k_top):
    B, S, _ = av.shape
    nq = S // tq
    assert tq % tk == 0
    return pl.pallas_call(
        functools.partial(_dsa_kernel, nq=nq, tq=tq, tk=tk, k_top=k_top),
        grid=(B, nq),
        in_specs=[
            pl.BlockSpec((1, tq, IDX_Q), lambda b, q: (b, q, 0)),
            pl.BlockSpec((1, S, LANES), lambda b, q: (b, 0, 0)),
            pl.BlockSpec((1, tq, LANES), lambda b, q: (b, q, 0)),
            pl.BlockSpec((1, tq, ATT_W), lambda b, q: (b, q, 0)),
            pl.BlockSpec((1, S, ATT_W), lambda b, q: (b, 0, 1)),
            pl.BlockSpec((1, S, ATT_W), lambda b, q: (b, 0, 0)),
            pl.BlockSpec((LANES, ATT_W), lambda b, q: (0, 0)),
            pl.BlockSpec((LANES, ATT_W), lambda b, q: (0, 0)),
        ],
        out_specs=pl.BlockSpec((1, tq, ATT_W), lambda b, q: (b, q, 0)),
        out_shape=jax.ShapeDtypeStruct((B, S, ATT_W), BF16),
        scratch_shapes=[
            pltpu.VMEM((S // tk, tk, tq), F32),
            pltpu.VMEM((1, tq), F32),
            pltpu.VMEM((ATT_HEADS, tq), F32),
            pltpu.VMEM((ATT_HEADS, tq), F32),
            pltpu.VMEM((ATT_HEADS, ATT_DH, tq), F32),
            pltpu.VMEM((ATT_HEADS, tk, tq), F32),
            pltpu.VMEM((ATT_HEADS, tq), F32),
        ],
        compiler_params=_params(("parallel", "arbitrary")),
        name="dsa",
    )(iq, ikw, ikw, aqk, aqk, av, akm, avm)


def _merge_kernel(h_ref, yr_ref, ya_ref, gr_ref, ga_ref, wr_ref, wa_ref, wm_ref, o_ref):
    yr = _dot(yr_ref[...], wr_ref[...])
    ya = _dot(ya_ref[...], wa_ref[...])
    merged = gr_ref[...] * yr + ga_ref[...] * ya
    o_ref[...] = h_ref[...] + _dot(merged.astype(BF16), wm_ref[...])


def _merge(h, yr, ya, gates, wr, wa, wm, *, tm):
    M, D = h.shape
    row = lambda i: (i, 0)
    fixed = lambda i: (0, 0)
    return pl.pallas_call(
        _merge_kernel,
        grid=(M // tm,),
        in_specs=[
            pl.BlockSpec((tm, D), row),
            pl.BlockSpec((tm, RET_V), row),
            pl.BlockSpec((tm, ATT_W), row),
            pl.BlockSpec((tm, D), lambda i: (i, 0)),
            pl.BlockSpec((tm, D), lambda i: (i, 1)),
            pl.BlockSpec((RET_V, D), fixed),
            pl.BlockSpec((ATT_W, D), fixed),
            pl.BlockSpec((D, D), fixed),
        ],
        out_specs=pl.BlockSpec((tm, D), row),
        out_shape=jax.ShapeDtypeStruct((M, D), F32),
        compiler_params=_params(("parallel",)),
        name="merge",
    )(h, yr, ya, gates, gates, wr, wa, wm)


def _rope_tables(pos, d, width):
    inv_freq = ROPE_THETA ** (-jnp.arange(0, d, 2, dtype=F32) / d)
    ang = pos.astype(F32)[:, None] * inv_freq[None, :]
    cos, sin = jnp.cos(ang), jnp.sin(ang)
    cos = jnp.concatenate([cos, cos], axis=-1)
    sin = jnp.concatenate([-sin, sin], axis=-1)
    rep = width // d
    return jnp.tile(cos, (1, rep)), jnp.tile(sin, (1, rep))


def _tile(n, cap):
    t = min(n, cap)
    assert n % t == 0, (n, t)
    return t


def kernel(x, meta_tokens, ffn1_norm, ffn1_w_gate, ffn1_w_up, ffn1_w_down, mix_norm, w_in, w_ret_out,
           w_att_out, w_mix_out, ffn2_norm, ffn2_w_gate, ffn2_w_up, ffn2_w_down, final_norm):
    B, S, D = x.shape
    assert ffn1_norm.shape[0] == 1, "single layer only"
    assert meta_tokens.shape[0] == N_META
    k_top = min(TOPK_MAX, (S + 0) // 4)
    M = B * S
    bf = lambda a: a.astype(BF16)
    row = lambda a: a.reshape(1, -1)

    w32 = w_in[0]
    offs = np.cumsum([0, RET_QK, RET_QK, RET_V, RET_V, ATT_W, ATT_W, ATT_W, IDX_Q, IDX_DH, IDX_HEADS, D, D])
    tn = 1024
    assert all(int(o) % tn == 0 for o in offs[:8])
    c_rq, c_rk, c_rv, c_rg, c_aq, c_ak, c_av = (int(o) // tn for o in offs[:7])
    w_idx = jnp.pad(bf(w32[:, offs[7]:offs[10]]), ((0, 0), (0, LANES - IDX_DH - IDX_HEADS)))
    w_gate = w32[:, offs[10]:offs[12]]

    pos_meta = jnp.arange(N_META, dtype=jnp.int32)
    pos_real = N_META + jnp.arange(S, dtype=jnp.int32)
    tf = 256
    slab = lambda a: bf(a).reshape(D, -1, tf).transpose(1, 0, 2)
    ffn1 = (row(ffn1_norm[0]), slab(ffn1_w_gate[0]), slab(ffn1_w_up[0]), bf(ffn1_w_down[0]), row(mix_norm[0]))
    ffn2 = (row(ffn2_norm[0]), slab(ffn2_w_gate[0]), slab(ffn2_w_up[0]), bf(ffn2_w_down[0]), row(final_norm))
    k_scale = RET_DK ** -0.5

    _, un_m = _ffn(meta_tokens.astype(F32), *ffn1, tm=N_META, emit_h=True, n_dtype=BF16)
    pm = dict(tm=N_META, tn=tn)
    rk_m = _proj(un_m, w32, col0=c_rk, n_tiles=RET_QK // tn, mode="rope", d=RET_DK, k_scale=k_scale,
                 tabs=_rope_tables(pos_meta, RET_DK, RET_DK), **pm)
    rv_m = _proj(un_m, w32, col0=c_rv, n_tiles=RET_V // tn, mode="plain", **pm)
    ak_m = _proj(un_m, w32, col0=c_ak, n_tiles=ATT_W // tn, mode="rope", d=ATT_DH,
                 tabs=_rope_tables(pos_meta, ATT_DH, LANES), **pm)
    av_m = _proj(un_m, w32, col0=c_av, n_tiles=ATT_W // tn, mode="plain", **pm)
    pad_meta = lambda a: jnp.pad(a, ((0, LANES - N_META), (0, 0)))

    tm = _tile(S, 1024)
    h1, un = _ffn(x.reshape(M, D), *ffn1, tm=tm, emit_h=True, n_dtype=BF16)
    pr = dict(tm=tm, tn=tn)
    rqk = _proj(un, w32, col0=c_rq, n_tiles=2 * RET_QK // tn, mode="rope", d=RET_DK, k_scale=k_scale,
                n_q=RET_QK // tn, tabs=_rope_tables(pos_real, RET_DK, RET_DK), **pr)
    rv = _proj(un, w32, col0=c_rv, n_tiles=RET_V // tn, mode="plain", **pr)
    rg = _proj(un, w32, col0=c_rg, n_tiles=RET_V // tn, mode="silu", **pr)
    aqk = _proj(un, w32, col0=c_aq, n_tiles=2 * ATT_W // tn, mode="rope", d=ATT_DH,
                tabs=_rope_tables(pos_real, ATT_DH, LANES), **pr)
    av = _proj(un, w32, col0=c_av, n_tiles=ATT_W // tn, mode="plain", **pr)
    gates = _proj(un, w_gate, col0=0, n_tiles=2 * D // tn, mode="sigmoid", **pr)
    iq, ikw = _idx_proj(un, w_idx, _rope_tables(pos_real, IDX_DH, LANES), tm=tm)

    heads = jnp.arange(RET_HEADS, dtype=F32)
    log_g = jnp.log1p(-(2.0 ** (-5.0 - heads)))
    C = _tile(S, 4 * RET_CHUNK)
    b3 = lambda a: a.reshape(B, S, a.shape[-1])
    yr = _retention(log_g, jnp.exp(log_g * C), b3(rqk), b3(rv), b3(rg), rk_m, rv_m, C=C)
    ya = _dsa(b3(iq), b3(ikw), b3(aqk), b3(av), pad_meta(ak_m), pad_meta(av_m),
              tq=_tile(S, 256), tk=_tile(S, 256), k_top=k_top)

    h2 = _merge(h1, yr.reshape(M, RET_V), ya.reshape(M, ATT_W), gates, bf(w_ret_out[0]), bf(w_att_out[0]),
                bf(w_mix_out[0]), tm=_tile(M, 512))
    (y,) = _ffn(h2, *ffn2, tm=tm, emit_h=False, n_dtype=F32)
    return y.reshape(B, S, D)
```

```python
import functools

import jax
import jax.numpy as jnp
import numpy as np
from jax import lax
from jax.experimental import pallas as pl
from jax.experimental.pallas import tpu as pltpu

N_META = 16
RET_HEADS = 4
RET_DK = 256
RET_DV = 512
RET_CHUNK = 128
ATT_HEADS = 8
ATT_DH = 128
IDX_HEADS = 8
IDX_DH = 64
TOPK_MAX = 256
ROPE_THETA = 10000.0
EPS = 1e-6
RET_QK = RET_HEADS * RET_DK
RET_V = RET_HEADS * RET_DV
ATT_W = ATT_HEADS * ATT_DH
IDX_Q = IDX_HEADS * IDX_DH

LANES = 128
VMEM_LIMIT = 48 * 1024 * 1024

BF16 = jnp.bfloat16
F32 = jnp.float32
NT_DIMS = (((1,), (1,)), ((), ()))


def _dot(a, b):
    return jnp.dot(a, b, preferred_element_type=F32)


def _dot_nt(a, b):
    return lax.dot_general(a, b, NT_DIMS, preferred_element_type=F32)


def _rms(x, g):
    return x * lax.rsqrt(jnp.mean(x * x, axis=-1, keepdims=True) + EPS) * g


def _sigmoid(x):
    return 1.0 / (1.0 + jnp.exp(-x))


def _params(sem):
    return pltpu.CompilerParams(dimension_semantics=sem, vmem_limit_bytes=VMEM_LIMIT)


def _ffn_kernel(x_ref, g_ref, wg_ref, wu_ref, wd_ref, g2_ref, *rest, nf, emit_h):
    if emit_h:
        h_ref, n_ref, xn_sc, acc_sc = rest
    else:
        n_ref, xn_sc, acc_sc = rest
    f = pl.program_id(1)

    @pl.when(f == 0)
    def _():
        xn_sc[...] = _rms(x_ref[...], g_ref[...]).astype(BF16)
        acc_sc[...] = jnp.zeros_like(acc_sc)

    xn = xn_sc[...]
    gate = _dot(xn, wg_ref[...].astype(BF16))
    up = _dot(xn, wu_ref[...].astype(BF16))
    act = (gate * _sigmoid(gate)) * up
    acc_sc[...] += _dot(act.astype(BF16), wd_ref[...].astype(BF16))

    @pl.when(f == nf - 1)
    def _():
        h = x_ref[...] + 0.5 * acc_sc[...]
        if emit_h:
            h_ref[...] = h
        n_ref[...] = _rms(h, g2_ref[...]).astype(n_ref.dtype)


def _ffn(x, g, wg, wu, wd, g2, *, tm, tf, emit_h, n_dtype):
    M, D = x.shape
    nf = wg.shape[1] // tf
    out_shape = [jax.ShapeDtypeStruct((M, D), n_dtype)]
    out_specs = [pl.BlockSpec((tm, D), lambda i, f: (i, 0))]
    if emit_h:
        out_shape = [jax.ShapeDtypeStruct((M, D), F32)] + out_shape
        out_specs = [pl.BlockSpec((tm, D), lambda i, f: (i, 0))] + out_specs
    return pl.pallas_call(
        functools.partial(_ffn_kernel, nf=nf, emit_h=emit_h),
        grid=(M // tm, nf),
        in_specs=[
            pl.BlockSpec((tm, D), lambda i, f: (i, 0)),
            pl.BlockSpec((1, D), lambda i, f: (0, 0)),
            pl.BlockSpec((D, tf), lambda i, f: (0, f)),
            pl.BlockSpec((D, tf), lambda i, f: (0, f)),
            pl.BlockSpec((tf, D), lambda i, f: (f, 0)),
            pl.BlockSpec((1, D), lambda i, f: (0, 0)),
        ],
        out_specs=out_specs,
        out_shape=out_shape,
        scratch_shapes=[pltpu.VMEM((tm, D), BF16), pltpu.VMEM((tm, D), F32)],
        compiler_params=_params(("parallel", "arbitrary")),
        name="ffn",
    )(x, g, wg, wu, wd, g2)


def _rot_half(y, d):
    w = y.shape[-1]
    if d >= w:
        return jnp.concatenate([y[:, w // 2:], y[:, : w // 2]], axis=-1)
    if d == LANES:
        return pltpu.roll(y, d // 2, axis=1)
    fwd = pltpu.roll(y, d // 2, axis=1)
    bwd = pltpu.roll(y, w - d // 2, axis=1)
    lane = lax.broadcasted_iota(jnp.int32, y.shape, 1)
    return jnp.where((lane % d) < d // 2, bwd, fwd)


def _rope_tile(y, cos, sin, d):
    tw = cos.shape[-1]
    outs = []
    for t in range(y.shape[-1] // tw):
        yt = y[:, t * tw:(t + 1) * tw]
        outs.append(yt * cos + _rot_half(yt, d) * sin)
    return outs[0] if len(outs) == 1 else jnp.concatenate(outs, axis=-1)


def _proj_kernel(x_ref, w_ref, *rest, mode, d, k_scale, n_q):
    if mode == "rope":
        cos_ref, sin_ref, o_ref, w_sc = rest
    else:
        o_ref, w_sc = rest

    @pl.when(pl.program_id(1) == 0)
    def _():
        w_sc[...] = w_ref[...].T.astype(BF16)

    y = _dot(x_ref[...], w_sc[...])
    if mode == "rope":
        y = _rope_tile(y, cos_ref[...], sin_ref[...], d)
        if k_scale != 1.0:
            y = y * jnp.where(pl.program_id(0) >= n_q, k_scale, 1.0)
    elif mode == "silu":
        y = y * _sigmoid(y)
    elif mode == "sigmoid":
        y = _sigmoid(y)
    o_ref[...] = y.astype(o_ref.dtype)


def _proj(x, w_t, *, col0, n_tiles, tm, tn, mode, out_dtype=None, tabs=None, d=None, k_scale=1.0, n_q=0):
    M, K = x.shape
    in_specs = [
        pl.BlockSpec((tm, K), lambda j, i: (i, 0)),
        pl.BlockSpec((tn, K), lambda j, i: (col0 + j, 0)),
    ]
    args = [x, w_t]
    if mode == "rope":
        cos, sin = tabs
        npos = cos.shape[0] // tm
        tw = cos.shape[1]
        in_specs += [pl.BlockSpec((tm, tw), lambda j, i: (i % npos, 0))] * 2
        args += [cos, sin]
    return pl.pallas_call(
        functools.partial(_proj_kernel, mode=mode, d=d, k_scale=k_scale, n_q=n_q),
        grid=(n_tiles, M // tm),
        in_specs=in_specs,
        out_specs=pl.BlockSpec((tm, tn), lambda j, i: (i, j)),
        out_shape=jax.ShapeDtypeStruct((M, n_tiles * tn), out_dtype or BF16),
        scratch_shapes=[pltpu.VMEM((K, tn), BF16)],
        compiler_params=_params(("arbitrary", "arbitrary")),
        name="proj_" + mode,
    )(*args)


def _idx_proj_kernel(x_ref, w_ref, cos_ref, sin_ref, q_ref, kw_ref):
    y = _dot(x_ref[...], w_ref[...])
    cos, sin = cos_ref[...], sin_ref[...]
    q_ref[...] = _rope_tile(y[:, :IDX_Q], cos, sin, IDX_DH)
    tail = y[:, IDX_Q:]
    lane = lax.broadcasted_iota(jnp.int32, tail.shape, 1)
    kw_ref[...] = jnp.where(lane < IDX_DH, _rope_tile(tail, cos, sin, IDX_DH), tail)


def _idx_proj(x, w, tabs, *, tm):
    M, K = x.shape
    N = w.shape[1]
    cos, sin = tabs
    npos = cos.shape[0] // tm
    return pl.pallas_call(
        _idx_proj_kernel,
        grid=(M // tm,),
        in_specs=[
            pl.BlockSpec((tm, K), lambda i: (i, 0)),
            pl.BlockSpec((K, N), lambda i: (0, 0)),
            pl.BlockSpec((tm, LANES), lambda i: (i % npos, 0)),
            pl.BlockSpec((tm, LANES), lambda i: (i % npos, 0)),
        ],
        out_specs=[
            pl.BlockSpec((tm, IDX_Q), lambda i: (i, 0)),
            pl.BlockSpec((tm, LANES), lambda i: (i, 0)),
        ],
        out_shape=[
            jax.ShapeDtypeStruct((M, IDX_Q), F32),
            jax.ShapeDtypeStruct((M, LANES), F32),
        ],
        compiler_params=_params(("parallel",)),
        name="proj_idx",
    )(x, w, cos, sin)


def _ret_kernel(lg_ref, gc_ref, q_ref, k_ref, v_ref, g_ref, km_ref, vm_ref, o_ref, st_ref, decay_ref, *, C):
    h = pl.program_id(1)
    c = pl.program_id(2)
    lg = lg_ref[h]

    @pl.when(c == 0)
    def _():
        pm = lax.broadcasted_iota(jnp.int32, (N_META, 1), 0).astype(F32)
        zeta_m = jnp.exp(lg * (N_META - 1.0 - pm))
        kz = km_ref[...].astype(F32) * zeta_m
        st_ref[...] = _dot(kz.T.astype(BF16), vm_ref[...])
        ri = lax.broadcasted_iota(jnp.int32, (C, C), 0)
        ci = lax.broadcasted_iota(jnp.int32, (C, C), 1)
        diff = (ri - ci).astype(F32)
        decay_ref[...] = jnp.where(diff >= 0, jnp.exp(lg * jnp.maximum(diff, 0.0)), 0.0)

    q = q_ref[0]
    k = k_ref[0]
    v = v_ref[0]
    pos = lax.broadcasted_iota(jnp.int32, (C, 1), 0).astype(F32)
    xi = jnp.exp(lg * (pos + 1.0))
    zeta = jnp.exp(lg * (C - 1.0 - pos))

    state = st_ref[...]
    inner = _dot((_dot_nt(q, k) * decay_ref[...]).astype(BF16), v)
    cross = _dot(q, state.astype(BF16)) * xi
    y = inner + cross
    y = y * lax.rsqrt(jnp.mean(y * y, axis=-1, keepdims=True) + EPS)
    o_ref[0] = (g_ref[0] * y).astype(o_ref.dtype)
    kz = k.astype(F32) * zeta
    st_ref[...] = gc_ref[h] * state + _dot(kz.T.astype(BF16), v)


def _retention(lg, gc, rqk, rv, rg, km, vm, *, C):
    B, S, _ = rv.shape
    H = RET_HEADS
    smem = pl.BlockSpec(memory_space=pltpu.SMEM)
    return pl.pallas_call(
        functools.partial(_ret_kernel, C=C),
        grid=(B, H, S // C),
        in_specs=[
            smem,
            smem,
            pl.BlockSpec((1, C, RET_DK), lambda b, h, c: (b, c, h)),
            pl.BlockSpec((1, C, RET_DK), lambda b, h, c: (b, c, RET_HEADS + h)),
            pl.BlockSpec((1, C, RET_DV), lambda b, h, c: (b, c, h)),
            pl.BlockSpec((1, C, RET_DV), lambda b, h, c: (b, c, h)),
            pl.BlockSpec((N_META, RET_DK), lambda b, h, c: (0, h)),
            pl.BlockSpec((N_META, RET_DV), lambda b, h, c: (0, h)),
        ],
        out_specs=pl.BlockSpec((1, C, RET_DV), lambda b, h, c: (b, c, h)),
        out_shape=jax.ShapeDtypeStruct((B, S, RET_V), BF16),
        scratch_shapes=[pltpu.VMEM((RET_DK, RET_DV), F32), pltpu.VMEM((C, C), F32)],
        compiler_params=_params(("parallel", "parallel", "arbitrary")),
        name="retention",
    )(lg, gc, rqk, rqk, rv, rg, km, vm)


TINY = float(np.finfo(np.float32).tiny)
TN_DIMS = (((0,), (0,)), ((), ()))


def _dsa_kernel(iq_ref, ikw_ref, wq_ref, aq_ref, ak_ref, av_ref, akm_ref, avm_ref, o_ref,
                sc_ref, tau_ref, m_ref, l_ref, acc_ref, s_ref, bm_ref, *, nq, tq, tk, k_top):
    qi = pl.program_id(1)
    nkb = (qi + 1) * (tq // tk)
    kf = float(k_top)
    neg_inf = -jnp.inf
    qpos = qi * tq + lax.broadcasted_iota(jnp.int32, (1, tq), 1)
    kpos0 = lax.broadcasted_iota(jnp.int32, (tk, 1), 0)

    def fold(x):
        return x.reshape(tk // 8, 8, tq).sum(axis=0)

    idx_scale = (IDX_DH ** -0.5) * (IDX_HEADS ** -0.5)
    iq = iq_ref[0].astype(BF16)
    w_t = wq_ref[0].T
    iq_h = [iq[:, h * IDX_DH:(h + 1) * IDX_DH] for h in range(IDX_HEADS)]
    w_h = [w_t[IDX_DH + h:IDX_DH + h + 1, :] for h in range(IDX_HEADS)]

    def score_block(kb, carry):
        rmin, rmax = carry
        start = pl.multiple_of(kb * tk, tk)
        ik = ikw_ref[0, pl.ds(start, tk), :][:, :IDX_DH].astype(BF16)
        acc = jnp.zeros((tk, tq), F32)
        for h in range(IDX_HEADS):
            acc = acc + w_h[h] * jnp.maximum(_dot_nt(ik, iq_h[h]), 0.0)
        ok = (kpos0 + kb * tk) <= qpos
        s = jnp.where(ok, acc * idx_scale, neg_inf)
        sc_ref[kb] = s
        rmin = jnp.minimum(rmin, jnp.min(jnp.where(ok, s, jnp.inf), axis=0, keepdims=True))
        rmax = jnp.maximum(rmax, jnp.max(s, axis=0, keepdims=True))
        return rmin, rmax

    rmin, rmax = lax.fori_loop(
        0, nkb, score_block, (jnp.full((1, tq), jnp.inf, F32), jnp.full((1, tq), neg_inf, F32)))

    def count(pred):
        def body(kb, c):
            return c + fold(jnp.where(pred(sc_ref[kb]), 1.0, 0.0))
        return jnp.sum(lax.fori_loop(0, nkb, body, jnp.zeros((8, tq), F32)), axis=0, keepdims=True)

    def any_query(flag):
        return jnp.max(jnp.where(flag, 1.0, 0.0))

    c_max = count(lambda s: s >= rmax)
    top_tied = c_max >= kf
    lo0 = jnp.where(top_tied, rmax, rmin)
    cl0 = jnp.where(top_tied, c_max, (qpos + 1).astype(F32))

    def midpoint(lo, hi):
        a = jnp.maximum(jnp.abs(lo), TINY)
        b = jnp.maximum(jnp.abs(hi), TINY)
        geo = jnp.sqrt(a) * jnp.sqrt(b)
        one_sign = (lo >= 0.0) | (hi <= 0.0)
        far = jnp.maximum(a, b) > 4.0 * jnp.minimum(a, b)
        mid = jnp.where(one_sign & far, jnp.where(lo >= 0.0, geo, -geo), 0.5 * lo + 0.5 * hi)
        mid = jnp.where(lo == 0.0, TINY, mid)
        mid = jnp.where(hi == 0.0, -TINY, mid)
        return jnp.where((lo < 0.0) & (hi > 0.0), 0.0, mid)

    def active(lo, hi, cl):
        mid = midpoint(lo, hi)
        return (cl > kf) & (mid > lo) & (mid < hi)

    def go_on(st):
        return st[3] > 0.0

    def shrink_loop(n_blocks, lo, hi, cl):
        def count_ge(mid):
            c = jnp.zeros((8, tq), F32)
            for kb in range(n_blocks):
                c = c + fold(jnp.where(sc_ref[kb] >= mid, 1.0, 0.0))
            return jnp.sum(c, axis=0, keepdims=True)

        def shrink_once(lo, hi, cl):
            mid = midpoint(lo, hi)
            act = active(lo, hi, cl)
            c = count_ge(mid)
            up = act & (c >= kf)
            dn = act & (c < kf)
            return jnp.where(up, mid, lo), jnp.where(dn, mid, hi), jnp.where(up, c, cl)

        def shrink(st):
            pending = any_query(active(*st[:3]))
            return (*shrink_once(*shrink_once(*st[:3])), pending)

        return lax.while_loop(go_on, shrink, (lo, hi, cl, jnp.float32(1.0)))[:3]

    branches = [functools.partial(shrink_loop, (i + 1) * (tq // tk)) for i in range(nq)]
    lo, hi, cl = lax.switch(qi, branches, lo0, rmax, cl0)
    tau_ref[...] = lo
    tie = cl > kf

    @pl.when(any_query(tie) > 0.0)
    def _():
        def max_below(bound):
            def body(kb, v):
                s = sc_ref[kb]
                return jnp.maximum(v, jnp.max(jnp.where(s < bound, s, neg_inf), axis=0, keepdims=True))
            return lax.fori_loop(0, nkb, body, jnp.full((1, tq), neg_inf, F32))

        def count_ge_gt(v):
            def body(kb, c):
                s = sc_ref[kb]
                return c[0] + fold(jnp.where(s >= v, 1.0, 0.0)), c[1] + fold(jnp.where(s > v, 1.0, 0.0))
            zero = jnp.zeros((8, tq), F32)
            ge, gt = lax.fori_loop(0, nkb, body, (zero, zero))
            return jnp.sum(ge, axis=0, keepdims=True), jnp.sum(gt, axis=0, keepdims=True)

        hi_d = jnp.where(lo >= hi, jnp.inf, hi)
        v0 = max_below(hi_d)
        ge0, gt0 = count_ge_gt(v0)

        def walk_on(st):
            return st[4] > 0.0

        def walk(st):
            hi_w, v, ge, _, _ = st
            hi_w = jnp.where(tie & (ge < kf), v, hi_w)
            v = max_below(hi_w)
            ge, gt = count_ge_gt(v)
            return hi_w, v, ge, gt, any_query(tie & (ge < kf))

        _, v, _, gt, _ = lax.while_loop(walk_on, walk, (hi_d, v0, ge0, gt0, any_query(tie & (ge0 < kf))))
        keep = kf - gt
        ri = lax.broadcasted_iota(jnp.int32, (tk, tk), 0)
        ci = lax.broadcasted_iota(jnp.int32, (tk, tk), 1)
        earlier = jnp.where(ci < ri, 1.0, 0.0).astype(BF16)

        def drop(kb, seen):
            s = sc_ref[kb]
            eq = tie & (s == v)
            eqf = jnp.where(eq, 1.0, 0.0)
            rank = seen + _dot(earlier, eqf.astype(BF16))
            sc_ref[kb] = jnp.where(eq & (rank >= keep), neg_inf, s)
            return seen + jnp.sum(eqf, axis=0, keepdims=True)

        lax.fori_loop(0, nkb, drop, jnp.zeros((1, tq), F32))
        tau_ref[...] = jnp.where(tie, v, lo)

    c_exp = (ATT_DH ** -0.5) * float(np.log2(np.e))
    tau = tau_ref[...]
    meta_ok = lax.broadcasted_iota(jnp.int32, (LANES, 1), 0) < N_META

    def head(ref, h, rows=None):
        if rows is None:
            return ref[:, h * ATT_DH:(h + 1) * ATT_DH]
        return ref[0, rows, h * ATT_DH:(h + 1) * ATT_DH]

    for h in range(ATT_HEADS):
        s = jnp.where(meta_ok, _dot_nt(head(akm_ref, h), head(aq_ref, h, slice(None))), neg_inf)
        m = jnp.max(s, axis=0, keepdims=True)
        p = jnp.exp2((s - m) * c_exp)
        m_ref[h:h + 1, :] = m
        l_ref[h:h + 1, :] = jnp.sum(p, axis=0, keepdims=True)
        acc_ref[h] = lax.dot_general(head(avm_ref, h), p.astype(BF16), TN_DIMS, preferred_element_type=F32)

    def att_block(kb, carry):
        start = pl.multiple_of(kb * tk, tk)
        bias = jnp.where(sc_ref[kb] >= tau, 0.0, neg_inf)
        for h in range(ATT_HEADS):
            s = _dot_nt(head(ak_ref, h, pl.ds(start, tk)), head(aq_ref, h, slice(None))) + bias
            s_ref[h] = s
            bm_ref[h:h + 1, :] = jnp.max(s, axis=0, keepdims=True)

        @pl.when(kb < nkb)
        def _():
            for h in range(ATT_HEADS):
                m_new = jnp.maximum(m_ref[h:h + 1, :], bm_ref[h:h + 1, :])
                p = jnp.exp2((s_ref[h] - m_new) * c_exp)
                p_sum = jnp.sum(p, axis=0, keepdims=True)
                pv = lax.dot_general(head(av_ref, h, pl.ds(start, tk)), p.astype(BF16), TN_DIMS,
                                     preferred_element_type=F32)
                alpha = jnp.exp2((m_ref[h:h + 1, :] - m_new) * c_exp)
                l_ref[h:h + 1, :] = alpha * l_ref[h:h + 1, :] + p_sum
                acc_ref[h] = alpha * acc_ref[h] + pv
                m_ref[h:h + 1, :] = m_new
        return carry

    lax.fori_loop(0, nkb, att_block, 0)
    for h in range(ATT_HEADS):
        out_t = acc_ref[h] / l_ref[h:h + 1, :]
        o_ref[0, :, h * ATT_DH:(h + 1) * ATT_DH] = out_t.T.astype(o_ref.dtype)


def _dsa(iq, ikw, aqk, av, akm, avm, *, tq, tk, k_top):
    B, S, _ = av.shape
    nq = S // tq
    assert tq % tk == 0
    return pl.pallas_call(
        functools.partial(_dsa_kernel, nq=nq, tq=tq, tk=tk, k_top=k_top),
        grid=(B, nq),
        in_specs=[
            pl.BlockSpec((1, tq, IDX_Q), lambda b, q: (b, q, 0)),
            pl.BlockSpec((1, S, LANES), lambda b, q: (b, 0, 0)),
            pl.BlockSpec((1, tq, LANES), lambda b, q: (b, q, 0)),
            pl.BlockSpec((1, tq, ATT_W), lambda b, q: (b, q, 0)),
            pl.BlockSpec((1, S, ATT_W), lambda b, q: (b, 0, 1)),
            pl.BlockSpec((1, S, ATT_W), lambda b, q: (b, 0, 0)),
            pl.BlockSpec((LANES, ATT_W), lambda b, q: (0, 0)),
            pl.BlockSpec((LANES, ATT_W), lambda b, q: (0, 0)),
        ],
        out_specs=pl.BlockSpec((1, tq, ATT_W), lambda b, q: (b, q, 0)),
        out_shape=jax.ShapeDtypeStruct((B, S, ATT_W), BF16),
        scratch_shapes=[
            pltpu.VMEM((S // tk, tk, tq), F32),
            pltpu.VMEM((1, tq), F32),
            pltpu.VMEM((ATT_HEADS, tq), F32),
            pltpu.VMEM((ATT_HEADS, tq), F32),
            pltpu.VMEM((ATT_HEADS, ATT_DH, tq), F32),
            pltpu.VMEM((ATT_HEADS, tk, tq), F32),
            pltpu.VMEM((ATT_HEADS, tq), F32),
        ],
        compiler_params=_params(("parallel", "arbitrary")),
        name="dsa",
    )(iq, ikw, ikw, aqk, aqk, av, akm, avm)


def _merge_kernel(h_ref, yr_ref, ya_ref, gr_ref, ga_ref, wr_ref, wa_ref, wm_ref, o_ref):
    yr = _dot(yr_ref[...], wr_ref[...])
    ya = _dot(ya_ref[...], wa_ref[...])
    merged = gr_ref[...] * yr + ga_ref[...] * ya
    o_ref[...] = h_ref[...] + _dot(merged.astype(BF16), wm_ref[...])


def _merge(h, yr, ya, gates, wr, wa, wm, *, tm):
    M, D = h.shape
    row = lambda i: (i, 0)
    fixed = lambda i: (0, 0)
    return pl.pallas_call(
        _merge_kernel,
        grid=(M // tm,),
        in_specs=[
            pl.BlockSpec((tm, D), row),
            pl.BlockSpec((tm, RET_V), row),
            pl.BlockSpec((tm, ATT_W), row),
            pl.BlockSpec((tm, D), lambda i: (i, 0)),
            pl.BlockSpec((tm, D), lambda i: (i, 1)),
            pl.BlockSpec((RET_V, D), fixed),
            pl.BlockSpec((ATT_W, D), fixed),
            pl.BlockSpec((D, D), fixed),
        ],
        out_specs=pl.BlockSpec((tm, D), row),
        out_shape=jax.ShapeDtypeStruct((M, D), F32),
        compiler_params=_params(("parallel",)),
        name="merge",
    )(h, yr, ya, gates, gates, wr, wa, wm)


def _rope_tables(pos, d, width):
    inv_freq = ROPE_THETA ** (-jnp.arange(0, d, 2, dtype=F32) / d)
    ang = pos.astype(F32)[:, None] * inv_freq[None, :]
    cos, sin = jnp.cos(ang), jnp.sin(ang)
    cos = jnp.concatenate([cos, cos], axis=-1)
    sin = jnp.concatenate([-sin, sin], axis=-1)
    rep = width // d
    return jnp.tile(cos, (1, rep)), jnp.tile(sin, (1, rep))


def _tile(n, cap):
    t = min(n, cap)
    assert n % t == 0, (n, t)
    return t


def kernel(x, meta_tokens, ffn1_norm, ffn1_w_gate, ffn1_w_up, ffn1_w_down, mix_norm, w_in, w_ret_out,
           w_att_out, w_mix_out, ffn2_norm, ffn2_w_gate, ffn2_w_up, ffn2_w_down, final_norm):
    B, S, D = x.shape
    assert ffn1_norm.shape[0] == 1, "single layer only"
    assert meta_tokens.shape[0] == N_META
    k_top = min(TOPK_MAX, (S + 0) // 4)
    M = B * S
    bf = lambda a: a.astype(BF16)
    row = lambda a: a.reshape(1, -1)

    w32 = w_in[0].T
    offs = np.cumsum([0, RET_QK, RET_QK, RET_V, RET_V, ATT_W, ATT_W, ATT_W, IDX_Q, IDX_DH, IDX_HEADS, D, D])
    tn = 1024
    assert all(int(o) % tn == 0 for o in offs[:8])
    c_rq, c_rk, c_rv, c_rg, c_aq, c_ak, c_av = (int(o) // tn for o in offs[:7])
    w_idx = jnp.pad(bf(w32[offs[7]:offs[10]].T), ((0, 0), (0, LANES - IDX_DH - IDX_HEADS)))
    w_gate = w32[offs[10]:offs[12]]

    pos_meta = jnp.arange(N_META, dtype=jnp.int32)
    pos_real = N_META + jnp.arange(S, dtype=jnp.int32)
    ffn1 = (row(ffn1_norm[0]), ffn1_w_gate[0], ffn1_w_up[0], ffn1_w_down[0], row(mix_norm[0]))
    ffn2 = (row(ffn2_norm[0]), ffn2_w_gate[0], ffn2_w_up[0], ffn2_w_down[0], row(final_norm))
    tf = 256
    k_scale = RET_DK ** -0.5

    _, un_m = _ffn(meta_tokens.astype(F32), *ffn1, tm=N_META, tf=tf, emit_h=True, n_dtype=BF16)
    pm = dict(tm=N_META, tn=tn)
    rk_m = _proj(un_m, w32, col0=c_rk, n_tiles=RET_QK // tn, mode="rope", d=RET_DK, k_scale=k_scale,
                 tabs=_rope_tables(pos_meta, RET_DK, RET_DK), **pm)
    rv_m = _proj(un_m, w32, col0=c_rv, n_tiles=RET_V // tn, mode="plain", **pm)
    ak_m = _proj(un_m, w32, col0=c_ak, n_tiles=ATT_W // tn, mode="rope", d=ATT_DH,
                 tabs=_rope_tables(pos_meta, ATT_DH, LANES), **pm)
    av_m = _proj(un_m, w32, col0=c_av, n_tiles=ATT_W // tn, mode="plain", **pm)
    pad_meta = lambda a: jnp.pad(a, ((0, LANES - N_META), (0, 0)))

    tm = _tile(S, 1024)
    h1, un = _ffn(x.reshape(M, D), *ffn1, tm=tm, tf=tf, emit_h=True, n_dtype=BF16)
    pr = dict(tm=tm, tn=tn)
    rqk = _proj(un, w32, col0=c_rq, n_tiles=2 * RET_QK // tn, mode="rope", d=RET_DK, k_scale=k_scale,
                n_q=RET_QK // tn, tabs=_rope_tables(pos_real, RET_DK, RET_DK), **pr)
    rv = _proj(un, w32, col0=c_rv, n_tiles=RET_V // tn, mode="plain", **pr)
    rg = _proj(un, w32, col0=c_rg, n_tiles=RET_V // tn, mode="silu", **pr)
    aqk = _proj(un, w32, col0=c_aq, n_tiles=2 * ATT_W // tn, mode="rope", d=ATT_DH,
                tabs=_rope_tables(pos_real, ATT_DH, LANES), **pr)
    av = _proj(un, w32, col0=c_av, n_tiles=ATT_W // tn, mode="plain", **pr)
    gates = _proj(un, w_gate, col0=0, n_tiles=2 * D // tn, mode="sigmoid", **pr)
    iq, ikw = _idx_proj(un, w_idx, _rope_tables(pos_real, IDX_DH, LANES), tm=tm)

    heads = jnp.arange(RET_HEADS, dtype=F32)
    log_g = jnp.log1p(-(2.0 ** (-5.0 - heads)))
    C = _tile(S, 4 * RET_CHUNK)
    b3 = lambda a: a.reshape(B, S, a.shape[-1])
    yr = _retention(log_g, jnp.exp(log_g * C), b3(rqk), b3(rv), b3(rg), rk_m, rv_m, C=C)
    ya = _dsa(b3(iq), b3(ikw), b3(aqk), b3(av), pad_meta(ak_m), pad_meta(av_m),
              tq=_tile(S, 256), tk=_tile(S, 256), k_top=k_top)

    h2 = _merge(h1, yr.reshape(M, RET_V), ya.reshape(M, ATT_W), gates, bf(w_ret_out[0]), bf(w_att_out[0]),
                bf(w_mix_out[0]), tm=_tile(M, 512))
    (y,) = _ffn(h2, *ffn2, tm=tm, tf=tf, emit_h=False, n_dtype=F32)
    return y.reshape(B, S, D)
```

```python
import functools

import jax
import jax.numpy as jnp
import numpy as np
from jax import lax
from jax.experimental import pallas as pl
from jax.experimental.pallas import tpu as pltpu

N_META = 16
RET_HEADS = 4
RET_DK = 256
RET_DV = 512
RET_CHUNK = 128
ATT_HEADS = 8
ATT_DH = 128
IDX_HEADS = 8
IDX_DH = 64
TOPK_MAX = 256
ROPE_THETA = 10000.0
EPS = 1e-6
RET_QK = RET_HEADS * RET_DK
RET_V = RET_HEADS * RET_DV
ATT_W = ATT_HEADS * ATT_DH
IDX_Q = IDX_HEADS * IDX_DH
ATT_EXP_SCALE = (ATT_DH ** -0.5) * float(np.log2(np.e))

LANES = 128
VMEM_LIMIT = 48 * 1024 * 1024

BF16 = jnp.bfloat16
F32 = jnp.float32
NT_DIMS = (((1,), (1,)), ((), ()))


def _dot(a, b):
    return jnp.dot(a, b, preferred_element_type=F32)


def _dot_nt(a, b):
    return lax.dot_general(a, b, NT_DIMS, preferred_element_type=F32)


def _rms(x, g):
    return x * lax.rsqrt(jnp.mean(x * x, axis=-1, keepdims=True) + EPS) * g


def _sigmoid(x):
    return 1.0 / (1.0 + jnp.exp(-x))


def _params(sem):
    return pltpu.CompilerParams(dimension_semantics=sem, vmem_limit_bytes=VMEM_LIMIT)


def _ffn_kernel(x_ref, g_ref, wg_ref, wu_ref, wd_ref, g2_ref, *rest, nf, emit_h):
    if emit_h:
        h_ref, n_ref, xn_sc, acc_sc = rest
    else:
        n_ref, xn_sc, acc_sc = rest
    f = pl.program_id(1)

    @pl.when(f == 0)
    def _():
        xn_sc[...] = _rms(x_ref[...], g_ref[...]).astype(BF16)
        acc_sc[...] = jnp.zeros_like(acc_sc)

    xn = xn_sc[...]
    gate = _dot(xn, wg_ref[...].astype(BF16))
    up = _dot(xn, wu_ref[...].astype(BF16))
    act = (gate * _sigmoid(gate)) * up
    acc_sc[...] += _dot(act.astype(BF16), wd_ref[...].astype(BF16))

    @pl.when(f == nf - 1)
    def _():
        h = x_ref[...] + 0.5 * acc_sc[...]
        if emit_h:
            h_ref[...] = h
        n_ref[...] = _rms(h, g2_ref[...]).astype(n_ref.dtype)


def _ffn(x, g, wg, wu, wd, g2, *, tm, tf, emit_h, n_dtype):
    M, D = x.shape
    nf = wg.shape[1] // tf
    out_shape = [jax.ShapeDtypeStruct((M, D), n_dtype)]
    out_specs = [pl.BlockSpec((tm, D), lambda i, f: (i, 0))]
    if emit_h:
        out_shape = [jax.ShapeDtypeStruct((M, D), F32)] + out_shape
        out_specs = [pl.BlockSpec((tm, D), lambda i, f: (i, 0))] + out_specs
    return pl.pallas_call(
        functools.partial(_ffn_kernel, nf=nf, emit_h=emit_h),
        grid=(M // tm, nf),
        in_specs=[
            pl.BlockSpec((tm, D), lambda i, f: (i, 0)),
            pl.BlockSpec((1, D), lambda i, f: (0, 0)),
            pl.BlockSpec((D, tf), lambda i, f: (0, f)),
            pl.BlockSpec((D, tf), lambda i, f: (0, f)),
            pl.BlockSpec((tf, D), lambda i, f: (f, 0)),
            pl.BlockSpec((1, D), lambda i, f: (0, 0)),
        ],
        out_specs=out_specs,
        out_shape=out_shape,
        scratch_shapes=[pltpu.VMEM((tm, D), BF16), pltpu.VMEM((tm, D), F32)],
        compiler_params=_params(("parallel", "arbitrary")),
        name="ffn",
    )(x, g, wg, wu, wd, g2)


def _rot_half(y, d):
    w = y.shape[-1]
    if d >= w:
        return jnp.concatenate([y[:, w // 2:], y[:, : w // 2]], axis=-1)
    if d == LANES:
        return pltpu.roll(y, d // 2, axis=1)
    fwd = pltpu.roll(y, d // 2, axis=1)
    bwd = pltpu.roll(y, w - d // 2, axis=1)
    lane = lax.broadcasted_iota(jnp.int32, y.shape, 1)
    return jnp.where((lane % d) < d // 2, bwd, fwd)


def _rope_tile(y, cos, sin, d):
    tw = cos.shape[-1]
    outs = []
    for t in range(y.shape[-1] // tw):
        yt = y[:, t * tw:(t + 1) * tw]
        outs.append(yt * cos + _rot_half(yt, d) * sin)
    return outs[0] if len(outs) == 1 else jnp.concatenate(outs, axis=-1)


def _proj_kernel(x_ref, w_ref, *rest, mode, d, q_scale, k_scale, n_q):
    if mode == "rope":
        cos_ref, sin_ref, o_ref, w_sc = rest
    else:
        o_ref, w_sc = rest

    @pl.when(pl.program_id(1) == 0)
    def _():
        w_sc[...] = w_ref[...].T.astype(BF16)

    y = _dot(x_ref[...], w_sc[...])
    if mode == "rope":
        y = _rope_tile(y, cos_ref[...], sin_ref[...], d)
        if q_scale != 1.0 or k_scale != 1.0:
            y = y * jnp.where(pl.program_id(0) >= n_q, k_scale, q_scale)
    elif mode == "silu":
        y = y * _sigmoid(y)
    elif mode == "sigmoid":
        y = _sigmoid(y)
    o_ref[...] = y.astype(o_ref.dtype)


def _proj(x, w_t, *, col0, n_tiles, tm, tn, mode, out_dtype=None, tabs=None, d=None, q_scale=1.0,
          k_scale=1.0, n_q=0):
    M, K = x.shape
    in_specs = [
        pl.BlockSpec((tm, K), lambda j, i: (i, 0)),
        pl.BlockSpec((tn, K), lambda j, i: (col0 + j, 0)),
    ]
    args = [x, w_t]
    if mode == "rope":
        cos, sin = tabs
        npos = cos.shape[0] // tm
        tw = cos.shape[1]
        in_specs += [pl.BlockSpec((tm, tw), lambda j, i: (i % npos, 0))] * 2
        args += [cos, sin]
    return pl.pallas_call(
        functools.partial(_proj_kernel, mode=mode, d=d, q_scale=q_scale, k_scale=k_scale, n_q=n_q),
        grid=(n_tiles, M // tm),
        in_specs=in_specs,
        out_specs=pl.BlockSpec((tm, tn), lambda j, i: (i, j)),
        out_shape=jax.ShapeDtypeStruct((M, n_tiles * tn), out_dtype or BF16),
        scratch_shapes=[pltpu.VMEM((K, tn), BF16)],
        compiler_params=_params(("arbitrary", "arbitrary")),
        name="proj_" + mode,
    )(*args)


def _idx_proj_kernel(x_ref, w_ref, cos_ref, sin_ref, q_ref, kw_ref):
    y = _dot(x_ref[...], w_ref[...])
    cos, sin = cos_ref[...], sin_ref[...]
    q_ref[...] = _rope_tile(y[:, :IDX_Q], cos, sin, IDX_DH)
    tail = y[:, IDX_Q:]
    lane = lax.broadcasted_iota(jnp.int32, tail.shape, 1)
    kw_ref[...] = jnp.where(lane < IDX_DH, _rope_tile(tail, cos, sin, IDX_DH), tail)


def _idx_proj(x, w, tabs, *, tm):
    M, K = x.shape
    N = w.shape[1]
    cos, sin = tabs
    npos = cos.shape[0] // tm
    return pl.pallas_call(
        _idx_proj_kernel,
        grid=(M // tm,),
        in_specs=[
            pl.BlockSpec((tm, K), lambda i: (i, 0)),
            pl.BlockSpec((K, N), lambda i: (0, 0)),
            pl.BlockSpec((tm, LANES), lambda i: (i % npos, 0)),
            pl.BlockSpec((tm, LANES), lambda i: (i % npos, 0)),
        ],
        out_specs=[
            pl.BlockSpec((tm, IDX_Q), lambda i: (i, 0)),
            pl.BlockSpec((tm, LANES), lambda i: (i, 0)),
        ],
        out_shape=[
            jax.ShapeDtypeStruct((M, IDX_Q), F32),
            jax.ShapeDtypeStruct((M, LANES), F32),
        ],
        compiler_params=_params(("parallel",)),
        name="proj_idx",
    )(x, w, cos, sin)


def _ret_kernel(lg_ref, gc_ref, q_ref, k_ref, v_ref, g_ref, km_ref, vm_ref, o_ref, st_ref, decay_ref, *, C):
    h = pl.program_id(1)
    c = pl.program_id(2)
    lg = lg_ref[h]

    @pl.when(c == 0)
    def _():
        pm = lax.broadcasted_iota(jnp.int32, (N_META, 1), 0).astype(F32)
        zeta_m = jnp.exp(lg * (N_META - 1.0 - pm))
        kz = km_ref[...].astype(F32) * zeta_m
        st_ref[...] = _dot(kz.T.astype(BF16), vm_ref[...])
        ri = lax.broadcasted_iota(jnp.int32, (C, C), 0)
        ci = lax.broadcasted_iota(jnp.int32, (C, C), 1)
        diff = (ri - ci).astype(F32)
        decay_ref[...] = jnp.where(diff >= 0, jnp.exp(lg * jnp.maximum(diff, 0.0)), 0.0)

    q = q_ref[0]
    k = k_ref[0]
    v = v_ref[0]
    pos = lax.broadcasted_iota(jnp.int32, (C, 1), 0).astype(F32)
    xi = jnp.exp(lg * (pos + 1.0))
    zeta = jnp.exp(lg * (C - 1.0 - pos))

    state = st_ref[...]
    inner = _dot((_dot_nt(q, k) * decay_ref[...]).astype(BF16), v)
    cross = _dot(q, state.astype(BF16)) * xi
    y = inner + cross
    y = y * lax.rsqrt(jnp.mean(y * y, axis=-1, keepdims=True) + EPS)
    o_ref[0] = (g_ref[0] * y).astype(o_ref.dtype)
    kz = k.astype(F32) * zeta
    st_ref[...] = gc_ref[h] * state + _dot(kz.T.astype(BF16), v)


def _retention(lg, gc, rqk, rv, rg, km, vm, *, C):
    B, S, _ = rv.shape
    H = RET_HEADS
    smem = pl.BlockSpec(memory_space=pltpu.SMEM)
    return pl.pallas_call(
        functools.partial(_ret_kernel, C=C),
        grid=(B, H, S // C),
        in_specs=[
            smem,
            smem,
            pl.BlockSpec((1, C, RET_DK), lambda b, h, c: (b, c, h)),
            pl.BlockSpec((1, C, RET_DK), lambda b, h, c: (b, c, RET_HEADS + h)),
            pl.BlockSpec((1, C, RET_DV), lambda b, h, c: (b, c, h)),
            pl.BlockSpec((1, C, RET_DV), lambda b, h, c: (b, c, h)),
            pl.BlockSpec((N_META, RET_DK), lambda b, h, c: (0, h)),
            pl.BlockSpec((N_META, RET_DV), lambda b, h, c: (0, h)),
        ],
        out_specs=pl.BlockSpec((1, C, RET_DV), lambda b, h, c: (b, c, h)),
        out_shape=jax.ShapeDtypeStruct((B, S, RET_V), BF16),
        scratch_shapes=[pltpu.VMEM((RET_DK, RET_DV), F32), pltpu.VMEM((C, C), F32)],
        compiler_params=_params(("parallel", "parallel", "arbitrary")),
        name="retention",
    )(lg, gc, rqk, rqk, rv, rg, km, vm)


TINY = float(np.finfo(np.float32).tiny)
TN_DIMS = (((0,), (0,)), ((), ()))


def _dsa_kernel(iq_ref, ikw_ref, wq_ref, aq_ref, ak_ref, av_ref, akm_ref, avm_ref, o_ref,
                sc_ref, tau_ref, m_ref, l_ref, acc_ref, s_ref, bm_ref, *, nq, tq, tk, k_top):
    qi = pl.program_id(1)
    nkb = (qi + 1) * (tq // tk)
    kf = float(k_top)
    neg_inf = -jnp.inf
    qpos = qi * tq + lax.broadcasted_iota(jnp.int32, (1, tq), 1)

    def fold(x):
        return x.reshape(tk // 8, 8, tq).sum(axis=0)

    idx_scale = (IDX_DH ** -0.5) * (IDX_HEADS ** -0.5)
    iq = iq_ref[0].astype(BF16)
    w_t = wq_ref[0].T
    iq_h = [iq[:, h * IDX_DH:(h + 1) * IDX_DH] for h in range(IDX_HEADS)]
    w_h = [w_t[IDX_DH + h:IDX_DH + h + 1, :] for h in range(IDX_HEADS)]

    def score_rows(kb, n_blocks, carry):
        rmin, rmax = carry
        rows = n_blocks * tk
        ik = ikw_ref[0, pl.ds(pl.multiple_of(kb * tk, tk), rows), :][:, :IDX_DH].astype(BF16)
        acc = jnp.zeros((rows, tq), F32)
        for h in range(IDX_HEADS):
            acc = acc + w_h[h] * jnp.maximum(_dot_nt(ik, iq_h[h]), 0.0)
        ok = (lax.broadcasted_iota(jnp.int32, (rows, 1), 0) + kb * tk) <= qpos
        s = jnp.where(ok, acc * idx_scale, neg_inf)
        for i in range(n_blocks):
            sc_ref[kb + i] = s[i * tk:(i + 1) * tk]
        rmin = jnp.minimum(rmin, jnp.min(jnp.where(ok, s, jnp.inf), axis=0, keepdims=True))
        rmax = jnp.maximum(rmax, jnp.max(s, axis=0, keepdims=True))
        return rmin, rmax

    carry = (jnp.full((1, tq), jnp.inf, F32), jnp.full((1, tq), neg_inf, F32))
    carry = lax.fori_loop(0, nkb // 2, lambda j, c: score_rows(2 * j, 2, c), carry)
    rmin, rmax = lax.cond(nkb % 2 == 1, lambda c: score_rows(nkb - 1, 1, c), lambda c: c, carry)

    def count(pred):
        def body(kb, c):
            return c + fold(jnp.where(pred(sc_ref[kb]), 1.0, 0.0))
        return jnp.sum(lax.fori_loop(0, nkb, body, jnp.zeros((8, tq), F32)), axis=0, keepdims=True)

    def any_query(flag):
        return jnp.max(jnp.where(flag, 1.0, 0.0))

    c_max = count(lambda s: s >= rmax)
    top_tied = c_max >= kf
    lo0 = jnp.where(top_tied, rmax, rmin)
    cl0 = jnp.where(top_tied, c_max, (qpos + 1).astype(F32))

    def midpoint(lo, hi):
        a = jnp.maximum(jnp.abs(lo), TINY)
        b = jnp.maximum(jnp.abs(hi), TINY)
        geo = jnp.sqrt(a) * jnp.sqrt(b)
        one_sign = (lo >= 0.0) | (hi <= 0.0)
        far = jnp.maximum(a, b) > 4.0 * jnp.minimum(a, b)
        mid = jnp.where(one_sign & far, jnp.where(lo >= 0.0, geo, -geo), 0.5 * lo + 0.5 * hi)
        mid = jnp.where(lo == 0.0, TINY, mid)
        mid = jnp.where(hi == 0.0, -TINY, mid)
        return jnp.where((lo < 0.0) & (hi > 0.0), 0.0, mid)

    def active(lo, hi, cl):
        mid = midpoint(lo, hi)
        return (cl > kf) & (mid > lo) & (mid < hi)

    def go_on(st):
        return st[3] > 0.0

    def shrink_loop(n_blocks, lo, hi, cl):
        def count_ge(mid):
            c = jnp.zeros((8, tq), F32)
            for kb in range(n_blocks):
                c = c + fold(jnp.where(sc_ref[kb] >= mid, 1.0, 0.0))
            return jnp.sum(c, axis=0, keepdims=True)

        def shrink_once(lo, hi, cl):
            mid = midpoint(lo, hi)
            act = active(lo, hi, cl)
            c = count_ge(mid)
            up = act & (c >= kf)
            dn = act & (c < kf)
            return jnp.where(up, mid, lo), jnp.where(dn, mid, hi), jnp.where(up, c, cl)

        def shrink(st):
            pending = any_query(active(*st[:3]))
            return (*shrink_once(*shrink_once(*st[:3])), pending)

        return lax.while_loop(go_on, shrink, (lo, hi, cl, jnp.float32(1.0)))[:3]

    branches = [functools.partial(shrink_loop, (i + 1) * (tq // tk)) for i in range(nq)]
    lo, hi, cl = lax.switch(qi, branches, lo0, rmax, cl0)
    tau_ref[...] = lo
    tie = cl > kf

    @pl.when(any_query(tie) > 0.0)
    def _():
        def max_below(bound):
            def body(kb, v):
                s = sc_ref[kb]
                return jnp.maximum(v, jnp.max(jnp.where(s < bound, s, neg_inf), axis=0, keepdims=True))
            return lax.fori_loop(0, nkb, body, jnp.full((1, tq), neg_inf, F32))

        def count_ge_gt(v):
            def body(kb, c):
                s = sc_ref[kb]
                return c[0] + fold(jnp.where(s >= v, 1.0, 0.0)), c[1] + fold(jnp.where(s > v, 1.0, 0.0))
            zero = jnp.zeros((8, tq), F32)
            ge, gt = lax.fori_loop(0, nkb, body, (zero, zero))
            return jnp.sum(ge, axis=0, keepdims=True), jnp.sum(gt, axis=0, keepdims=True)

        hi_d = jnp.where(lo >= hi, jnp.inf, hi)
        v0 = max_below(hi_d)
        ge0, gt0 = count_ge_gt(v0)

        def walk_on(st):
            return st[4] > 0.0

        def walk(st):
            hi_w, v, ge, _, _ = st
            hi_w = jnp.where(tie & (ge < kf), v, hi_w)
            v = max_below(hi_w)
            ge, gt = count_ge_gt(v)
            return hi_w, v, ge, gt, any_query(tie & (ge < kf))

        _, v, _, gt, _ = lax.while_loop(walk_on, walk, (hi_d, v0, ge0, gt0, any_query(tie & (ge0 < kf))))
        keep = kf - gt
        ri = lax.broadcasted_iota(jnp.int32, (tk, tk), 0)
        ci = lax.broadcasted_iota(jnp.int32, (tk, tk), 1)
        earlier = jnp.where(ci < ri, 1.0, 0.0).astype(BF16)

        def drop(kb, seen):
            s = sc_ref[kb]
            eq = tie & (s == v)
            eqf = jnp.where(eq, 1.0, 0.0)
            rank = seen + _dot(earlier, eqf.astype(BF16))
            sc_ref[kb] = jnp.where(eq & (rank >= keep), neg_inf, s)
            return seen + jnp.sum(eqf, axis=0, keepdims=True)

        lax.fori_loop(0, nkb, drop, jnp.zeros((1, tq), F32))
        tau_ref[...] = jnp.where(tie, v, lo)

    tau = tau_ref[...]
    meta_bias = jnp.where(lax.broadcasted_iota(jnp.int32, (tk, 1), 0) < N_META, 0.0, neg_inf)

    def head(ref, h, rows=None):
        if rows is None:
            return ref[:, h * ATT_DH:(h + 1) * ATT_DH]
        return ref[0, rows, h * ATT_DH:(h + 1) * ATT_DH]

    def keys_of(kb):
        if kb is None:
            return (lambda h: head(akm_ref, h)), (lambda h: head(avm_ref, h)), meta_bias
        rows = pl.ds(pl.multiple_of(kb * tk, tk), tk)
        bias = jnp.where(sc_ref[kb] >= tau, 0.0, neg_inf)
        return (lambda h: head(ak_ref, h, rows)), (lambda h: head(av_ref, h, rows)), bias

    def score_phase(kb, buf):
        k_of, _, bias = keys_of(kb)
        for h in range(ATT_HEADS):
            s = _dot_nt(k_of(h), head(aq_ref, h, slice(None))) + bias
            s_ref[buf, h] = s
            bm_ref[buf, h:h + 1, :] = jnp.max(s, axis=0, keepdims=True)

    def value_phase(kb, buf):
        _, v_of, _ = keys_of(kb)
        for h in range(ATT_HEADS):
            m_new = jnp.maximum(m_ref[h:h + 1, :], bm_ref[buf, h:h + 1, :])
            p = jnp.exp2(s_ref[buf, h] - m_new)
            p_sum = jnp.sum(p, axis=0, keepdims=True)
            pv = lax.dot_general(v_of(h), p.astype(BF16), TN_DIMS, preferred_element_type=F32)
            alpha = jnp.exp2(m_ref[h:h + 1, :] - m_new)
            l_ref[h:h + 1, :] = alpha * l_ref[h:h + 1, :] + p_sum
            acc_ref[h] = alpha * acc_ref[h] + pv
            m_ref[h:h + 1, :] = m_new

    m_ref[...] = jnp.full_like(m_ref, neg_inf)
    l_ref[...] = jnp.zeros_like(l_ref)
    acc_ref[...] = jnp.zeros_like(acc_ref)
    score_phase(None, 0)

    @pl.when(nkb > 0)
    def _():
        value_phase(None, 0)
        score_phase(0, 1)

    def att_pair(j, carry):
        kb = 2 * j
        value_phase(kb, 1)
        score_phase(kb + 1, 0)

        @pl.when(kb + 1 < nkb)
        def _():
            value_phase(kb + 1, 0)
            score_phase(jnp.minimum(kb + 2, nkb - 1), 1)
        return carry

    lax.fori_loop(0, nkb // 2, att_pair, 0)

    @pl.when(nkb % 2 == 1)
    def _():
        value_phase(nkb - 1, 1)

    for h in range(ATT_HEADS):
        out_t = acc_ref[h] / l_ref[h:h + 1, :]
        o_ref[0, :, h * ATT_DH:(h + 1) * ATT_DH] = out_t.T.astype(o_ref.dtype)


def _dsa(iq, ikw, aqk, av, akm, avm, *, tq, tk, k_top):
    B, S, _ = av.shape
    nq = S // tq
    assert tq % tk == 0
    return pl.pallas_call(
        functools.partial(_dsa_kernel, nq=nq, tq=tq, tk=tk, k_top=k_top),
        grid=(B, nq),
        in_specs=[
            pl.BlockSpec((1, tq, IDX_Q), lambda b, q: (b, q, 0)),
            pl.BlockSpec((1, S, LANES), lambda b, q: (b, 0, 0)),
            pl.BlockSpec((1, tq, LANES), lambda b, q: (b, q, 0)),
            pl.BlockSpec((1, tq, ATT_W), lambda b, q: (b, q, 0)),
            pl.BlockSpec((1, S, ATT_W), lambda b, q: (b, 0, 1)),
            pl.BlockSpec((1, S, ATT_W), lambda b, q: (b, 0, 0)),
            pl.BlockSpec((tk, ATT_W), lambda b, q: (0, 0)),
            pl.BlockSpec((tk, ATT_W), lambda b, q: (0, 0)),
        ],
        out_specs=pl.BlockSpec((1, tq, ATT_W), lambda b, q: (b, q, 0)),
        out_shape=jax.ShapeDtypeStruct((B, S, ATT_W), BF16),
        scratch_shapes=[
            pltpu.VMEM((S // tk, tk, tq), F32),
            pltpu.VMEM((1, tq), F32),
            pltpu.VMEM((ATT_HEADS, tq), F32),
            pltpu.VMEM((ATT_HEADS, tq), F32),
            pltpu.VMEM((ATT_HEADS, ATT_DH, tq), F32),
            pltpu.VMEM((2, ATT_HEADS, tk, tq), F32),
            pltpu.VMEM((2, ATT_HEADS, tq), F32),
        ],
        compiler_params=_params(("parallel", "arbitrary")),
        name="dsa",
    )(iq, ikw, ikw, aqk, aqk, av, akm, avm)


def _merge_kernel(h_ref, yr_ref, ya_ref, gr_ref, ga_ref, wr_ref, wa_ref, wm_ref, o_ref):
    yr = _dot(yr_ref[...], wr_ref[...])
    ya = _dot(ya_ref[...], wa_ref[...])
    merged = gr_ref[...] * yr + ga_ref[...] * ya
    o_ref[...] = h_ref[...] + _dot(merged.astype(BF16), wm_ref[...])


def _merge(h, yr, ya, gates, wr, wa, wm, *, tm):
    M, D = h.shape
    row = lambda i: (i, 0)
    fixed = lambda i: (0, 0)
    return pl.pallas_call(
        _merge_kernel,
        grid=(M // tm,),
        in_specs=[
            pl.BlockSpec((tm, D), row),
            pl.BlockSpec((tm, RET_V), row),
            pl.BlockSpec((tm, ATT_W), row),
            pl.BlockSpec((tm, D), lambda i: (i, 0)),
            pl.BlockSpec((tm, D), lambda i: (i, 1)),
            pl.BlockSpec((RET_V, D), fixed),
            pl.BlockSpec((ATT_W, D), fixed),
            pl.BlockSpec((D, D), fixed),
        ],
        out_specs=pl.BlockSpec((tm, D), row),
        out_shape=jax.ShapeDtypeStruct((M, D), F32),
        compiler_params=_params(("parallel",)),
        name="merge",
    )(h, yr, ya, gates, gates, wr, wa, wm)


def _rope_tables(pos, d, width):
    inv_freq = ROPE_THETA ** (-jnp.arange(0, d, 2, dtype=F32) / d)
    ang = pos.astype(F32)[:, None] * inv_freq[None, :]
    cos, sin = jnp.cos(ang), jnp.sin(ang)
    cos = jnp.concatenate([cos, cos], axis=-1)
    sin = jnp.concatenate([-sin, sin], axis=-1)
    rep = width // d
    return jnp.tile(cos, (1, rep)), jnp.tile(sin, (1, rep))


def _tile(n, cap):
    t = min(n, cap)
    assert n % t == 0, (n, t)
    return t


def kernel(x, meta_tokens, ffn1_norm, ffn1_w_gate, ffn1_w_up, ffn1_w_down, mix_norm, w_in, w_ret_out,
           w_att_out, w_mix_out, ffn2_norm, ffn2_w_gate, ffn2_w_up, ffn2_w_down, final_norm):
    B, S, D = x.shape
    assert ffn1_norm.shape[0] == 1, "single layer only"
    assert meta_tokens.shape[0] == N_META
    k_top = min(TOPK_MAX, (S + 0) // 4)
    M = B * S
    bf = lambda a: a.astype(BF16)
    row = lambda a: a.reshape(1, -1)

    w32 = w_in[0].T
    offs = np.cumsum([0, RET_QK, RET_QK, RET_V, RET_V, ATT_W, ATT_W, ATT_W, IDX_Q, IDX_DH, IDX_HEADS, D, D])
    tn = 1024
    assert all(int(o) % tn == 0 for o in offs[:8])
    c_rq, c_rk, c_rv, c_rg, c_aq, c_ak, c_av = (int(o) // tn for o in offs[:7])
    w_idx = jnp.pad(bf(w32[offs[7]:offs[10]].T), ((0, 0), (0, LANES - IDX_DH - IDX_HEADS)))
    w_gate = w32[offs[10]:offs[12]]

    pos_meta = jnp.arange(N_META, dtype=jnp.int32)
    pos_real = N_META + jnp.arange(S, dtype=jnp.int32)
    ffn1 = (row(ffn1_norm[0]), ffn1_w_gate[0], ffn1_w_up[0], ffn1_w_down[0], row(mix_norm[0]))
    ffn2 = (row(ffn2_norm[0]), ffn2_w_gate[0], ffn2_w_up[0], ffn2_w_down[0], row(final_norm))
    tf = 256
    k_scale = RET_DK ** -0.5

    _, un_m = _ffn(meta_tokens.astype(F32), *ffn1, tm=N_META, tf=tf, emit_h=True, n_dtype=BF16)
    pm = dict(tm=N_META, tn=tn)
    rk_m = _proj(un_m, w32, col0=c_rk, n_tiles=RET_QK // tn, mode="rope", d=RET_DK, k_scale=k_scale,
                 tabs=_rope_tables(pos_meta, RET_DK, RET_DK), **pm)
    rv_m = _proj(un_m, w32, col0=c_rv, n_tiles=RET_V // tn, mode="plain", **pm)
    ak_m = _proj(un_m, w32, col0=c_ak, n_tiles=ATT_W // tn, mode="rope", d=ATT_DH,
                 tabs=_rope_tables(pos_meta, ATT_DH, LANES), **pm)
    av_m = _proj(un_m, w32, col0=c_av, n_tiles=ATT_W // tn, mode="plain", **pm)

    tm = _tile(S, 1024)
    h1, un = _ffn(x.reshape(M, D), *ffn1, tm=tm, tf=tf, emit_h=True, n_dtype=BF16)
    pr = dict(tm=tm, tn=tn)
    rqk = _proj(un, w32, col0=c_rq, n_tiles=2 * RET_QK // tn, mode="rope", d=RET_DK, k_scale=k_scale,
                n_q=RET_QK // tn, tabs=_rope_tables(pos_real, RET_DK, RET_DK), **pr)
    rv = _proj(un, w32, col0=c_rv, n_tiles=RET_V // tn, mode="plain", **pr)
    rg = _proj(un, w32, col0=c_rg, n_tiles=RET_V // tn, mode="silu", **pr)
    aqk = _proj(un, w32, col0=c_aq, n_tiles=2 * ATT_W // tn, mode="rope", d=ATT_DH, q_scale=ATT_EXP_SCALE,
                n_q=ATT_W // tn, tabs=_rope_tables(pos_real, ATT_DH, LANES), **pr)
    av = _proj(un, w32, col0=c_av, n_tiles=ATT_W // tn, mode="plain", **pr)
    gates = _proj(un, w_gate, col0=0, n_tiles=2 * D // tn, mode="sigmoid", **pr)
    iq, ikw = _idx_proj(un, w_idx, _rope_tables(pos_real, IDX_DH, LANES), tm=tm)

    heads = jnp.arange(RET_HEADS, dtype=F32)
    log_g = jnp.log1p(-(2.0 ** (-5.0 - heads)))
    C = _tile(S, 4 * RET_CHUNK)
    b3 = lambda a: a.reshape(B, S, a.shape[-1])
    yr = _retention(log_g, jnp.exp(log_g * C), b3(rqk), b3(rv), b3(rg), rk_m, rv_m, C=C)
    tk = _tile(S, 256)
    pad_meta = lambda a: jnp.pad(a, ((0, tk - N_META), (0, 0)))
    ya = _dsa(b3(iq), b3(ikw), b3(aqk), b3(av), pad_meta(ak_m), pad_meta(av_m),
              tq=_tile(S, 256), tk=tk, k_top=k_top)

    h2 = _merge(h1, yr.reshape(M, RET_V), ya.reshape(M, ATT_W), gates, bf(w_ret_out[0]), bf(w_att_out[0]),
                bf(w_mix_out[0]), tm=_tile(M, 512))
    (y,) = _ffn(h2, *ffn2, tm=tm, tf=tf, emit_h=False, n_dtype=F32)
    return y.reshape(B, S, D)
```

```python
import functools

import jax
import jax.numpy as jnp
import numpy as np
from jax import lax
from jax.experimental import pallas as pl
from jax.experimental.pallas import tpu as pltpu

N_META = 16
RET_HEADS = 4
RET_DK = 256
RET_DV = 512
RET_CHUNK = 128
ATT_HEADS = 8
ATT_DH = 128
IDX_HEADS = 8
IDX_DH = 64
TOPK_MAX = 256
ROPE_THETA = 10000.0
EPS = 1e-6
RET_QK = RET_HEADS * RET_DK
RET_V = RET_HEADS * RET_DV
ATT_W = ATT_HEADS * ATT_DH
IDX_Q = IDX_HEADS * IDX_DH
ATT_EXP_SCALE = (ATT_DH ** -0.5) * float(np.log2(np.e))

LANES = 128
VMEM_LIMIT = 48 * 1024 * 1024

BF16 = jnp.bfloat16
F32 = jnp.float32
NT_DIMS = (((1,), (1,)), ((), ()))


def _dot(a, b):
    return jnp.dot(a, b, preferred_element_type=F32)


def _dot_nt(a, b):
    return lax.dot_general(a, b, NT_DIMS, preferred_element_type=F32)


def _rms(x, g):
    return x * lax.rsqrt(jnp.mean(x * x, axis=-1, keepdims=True) + EPS) * g


def _sigmoid(x):
    return 0.5 * jnp.tanh(0.5 * x) + 0.5


def _params(sem):
    return pltpu.CompilerParams(dimension_semantics=sem, vmem_limit_bytes=VMEM_LIMIT)


def _ffn_kernel(x_ref, g_ref, wg_ref, wu_ref, wd_ref, g2_ref, *rest, nf, emit_h):
    if emit_h:
        h_ref, n_ref, xn_sc, acc_sc, act_sc = rest
    else:
        n_ref, xn_sc, acc_sc, act_sc = rest
    f = pl.program_id(1)

    def down():
        return _dot(act_sc[...], wd_ref[...].astype(BF16))

    def up():
        xn = xn_sc[...]
        gate = _dot(xn, wg_ref[...].astype(BF16))
        lift = _dot(xn, wu_ref[...].astype(BF16))
        act_sc[...] = ((gate * _sigmoid(gate)) * lift).astype(BF16)

    @pl.when(f == 0)
    def _():
        xn_sc[...] = _rms(x_ref[...], g_ref[...]).astype(BF16)
        up()

    @pl.when(f == 1)
    def _():
        acc_sc[...] = down()
        up()

    @pl.when((f > 1) & (f < nf))
    def _():
        acc_sc[...] += down()
        up()

    @pl.when(f == nf)
    def _():
        h = x_ref[...] + 0.5 * (acc_sc[...] + down())
        if emit_h:
            h_ref[...] = h
        n_ref[...] = _rms(h, g2_ref[...]).astype(n_ref.dtype)


def _ffn(x, g, wg, wu, wd, g2, *, tm, tf, emit_h, n_dtype):
    M, D = x.shape
    nf = wg.shape[1] // tf
    assert nf >= 2
    out_shape = [jax.ShapeDtypeStruct((M, D), n_dtype)]
    out_specs = [pl.BlockSpec((tm, D), lambda i, f: (i, 0))]
    if emit_h:
        out_shape = [jax.ShapeDtypeStruct((M, D), F32)] + out_shape
        out_specs = [pl.BlockSpec((tm, D), lambda i, f: (i, 0))] + out_specs
    up_slab = lambda i, f: (0, jnp.minimum(f, nf - 1))
    return pl.pallas_call(
        functools.partial(_ffn_kernel, nf=nf, emit_h=emit_h),
        grid=(M // tm, nf + 1),
        in_specs=[
            pl.BlockSpec((tm, D), lambda i, f: (i, 0)),
            pl.BlockSpec((1, D), lambda i, f: (0, 0)),
            pl.BlockSpec((D, tf), up_slab),
            pl.BlockSpec((D, tf), up_slab),
            pl.BlockSpec((tf, D), lambda i, f: (jnp.maximum(f - 1, 0), 0)),
            pl.BlockSpec((1, D), lambda i, f: (0, 0)),
        ],
        out_specs=out_specs,
        out_shape=out_shape,
        scratch_shapes=[pltpu.VMEM((tm, D), BF16), pltpu.VMEM((tm, D), F32), pltpu.VMEM((tm, tf), BF16)],
        compiler_params=_params(("parallel", "arbitrary")),
        name="ffn",
    )(x, g, wg, wu, wd, g2)


def _rot_half(y, d):
    w = y.shape[-1]
    if d >= w:
        return jnp.concatenate([y[:, w // 2:], y[:, : w // 2]], axis=-1)
    if d == LANES:
        return pltpu.roll(y, d // 2, axis=1)
    fwd = pltpu.roll(y, d // 2, axis=1)
    bwd = pltpu.roll(y, w - d // 2, axis=1)
    lane = lax.broadcasted_iota(jnp.int32, y.shape, 1)
    return jnp.where((lane % d) < d // 2, bwd, fwd)


def _rope_tile(y, cos, sin, d):
    tw = cos.shape[-1]
    outs = []
    for t in range(y.shape[-1] // tw):
        yt = y[:, t * tw:(t + 1) * tw]
        outs.append(yt * cos + _rot_half(yt, d) * sin)
    return outs[0] if len(outs) == 1 else jnp.concatenate(outs, axis=-1)


def _proj_kernel(x_ref, w_ref, *rest, mode, d, q_scale, k_scale, n_q):
    if mode == "rope":
        cos_ref, sin_ref, o_ref, w_sc = rest
    else:
        o_ref, w_sc = rest

    @pl.when(pl.program_id(1) == 0)
    def _():
        w_sc[...] = w_ref[...].T.astype(BF16)

    y = _dot(x_ref[...], w_sc[...])
    if mode == "rope":
        y = _rope_tile(y, cos_ref[...], sin_ref[...], d)
        if q_scale != 1.0 or k_scale != 1.0:
            y = y * jnp.where(pl.program_id(0) >= n_q, k_scale, q_scale)
    elif mode == "silu":
        y = y * _sigmoid(y)
    elif mode == "sigmoid":
        y = _sigmoid(y)
    o_ref[...] = y.astype(o_ref.dtype)


def _proj(x, w_t, *, col0, n_tiles, tm, tn, mode, out_dtype=None, tabs=None, d=None, q_scale=1.0,
          k_scale=1.0, n_q=0):
    M, K = x.shape
    in_specs = [
        pl.BlockSpec((tm, K), lambda j, i: (i, 0)),
        pl.BlockSpec((tn, K), lambda j, i: (col0 + j, 0)),
    ]
    args = [x, w_t]
    if mode == "rope":
        cos, sin = tabs
        npos = cos.shape[0] // tm
        tw = cos.shape[1]
        in_specs += [pl.BlockSpec((tm, tw), lambda j, i: (i % npos, 0))] * 2
        args += [cos, sin]
    return pl.pallas_call(
        functools.partial(_proj_kernel, mode=mode, d=d, q_scale=q_scale, k_scale=k_scale, n_q=n_q),
        grid=(n_tiles, M // tm),
        in_specs=in_specs,
        out_specs=pl.BlockSpec((tm, tn), lambda j, i: (i, j)),
        out_shape=jax.ShapeDtypeStruct((M, n_tiles * tn), out_dtype or BF16),
        scratch_shapes=[pltpu.VMEM((K, tn), BF16)],
        compiler_params=_params(("arbitrary", "arbitrary")),
        name="proj_" + mode,
    )(*args)


def _idx_proj_kernel(x_ref, w_ref, cos_ref, sin_ref, q_ref, kw_ref):
    y = _dot(x_ref[...], w_ref[...])
    cos, sin = cos_ref[...], sin_ref[...]
    q_ref[...] = _rope_tile(y[:, :IDX_Q], cos, sin, IDX_DH)
    tail = y[:, IDX_Q:]
    lane = lax.broadcasted_iota(jnp.int32, tail.shape, 1)
    kw_ref[...] = jnp.where(lane < IDX_DH, _rope_tile(tail, cos, sin, IDX_DH), tail)


def _idx_proj(x, w, tabs, *, tm):
    M, K = x.shape
    N = w.shape[1]
    cos, sin = tabs
    npos = cos.shape[0] // tm
    return pl.pallas_call(
        _idx_proj_kernel,
        grid=(M // tm,),
        in_specs=[
            pl.BlockSpec((tm, K), lambda i: (i, 0)),
            pl.BlockSpec((K, N), lambda i: (0, 0)),
            pl.BlockSpec((tm, LANES), lambda i: (i % npos, 0)),
            pl.BlockSpec((tm, LANES), lambda i: (i % npos, 0)),
        ],
        out_specs=[
            pl.BlockSpec((tm, IDX_Q), lambda i: (i, 0)),
            pl.BlockSpec((tm, LANES), lambda i: (i, 0)),
        ],
        out_shape=[
            jax.ShapeDtypeStruct((M, IDX_Q), F32),
            jax.ShapeDtypeStruct((M, LANES), F32),
        ],
        compiler_params=_params(("parallel",)),
        name="proj_idx",
    )(x, w, cos, sin)


def _ret_kernel(lg_ref, gc_ref, q_ref, k_ref, v_ref, g_ref, km_ref, vm_ref, o_ref, st_ref, decay_ref, *, C):
    c = pl.program_id(1)
    dk = lambda h: slice(h * RET_DK, (h + 1) * RET_DK)
    dv = lambda h: slice(h * RET_DV, (h + 1) * RET_DV)

    @pl.when(c == 0)
    def _():
        pm = lax.broadcasted_iota(jnp.int32, (N_META, 1), 0).astype(F32)
        ri = lax.broadcasted_iota(jnp.int32, (C, C), 0)
        ci = lax.broadcasted_iota(jnp.int32, (C, C), 1)
        diff = (ri - ci).astype(F32)
        for h in range(RET_HEADS):
            lg = lg_ref[h]
            zeta_m = jnp.exp(lg * (N_META - 1.0 - pm))
            kz = km_ref[:, dk(h)].astype(F32) * zeta_m
            st_ref[h] = _dot(kz.T.astype(BF16), vm_ref[:, dv(h)])
            decay_ref[h] = jnp.where(diff >= 0, jnp.exp(lg * jnp.maximum(diff, 0.0)), 0.0)

    pos = lax.broadcasted_iota(jnp.int32, (C, 1), 0).astype(F32)
    for h in range(RET_HEADS):
        lg = lg_ref[h]
        q = q_ref[0, :, dk(h)]
        k = k_ref[0, :, dk(h)]
        v = v_ref[0, :, dv(h)]
        xi = jnp.exp(lg * (pos + 1.0))
        zeta = jnp.exp(lg * (C - 1.0 - pos))
        state = st_ref[h]
        inner = _dot((_dot_nt(q, k) * decay_ref[h]).astype(BF16), v)
        cross = _dot(q, state.astype(BF16)) * xi
        y = inner + cross
        y = y * lax.rsqrt(jnp.mean(y * y, axis=-1, keepdims=True) + EPS)
        o_ref[0, :, dv(h)] = (g_ref[0, :, dv(h)] * y).astype(o_ref.dtype)
        kz = k.astype(F32) * zeta
        st_ref[h] = gc_ref[h] * state + _dot(kz.T.astype(BF16), v)


def _retention(lg, gc, rqk, rv, rg, km, vm, *, C):
    B, S, _ = rv.shape
    smem = pl.BlockSpec(memory_space=pltpu.SMEM)
    return pl.pallas_call(
        functools.partial(_ret_kernel, C=C),
        grid=(B, S // C),
        in_specs=[
            smem,
            smem,
            pl.BlockSpec((1, C, RET_QK), lambda b, c: (b, c, 0)),
            pl.BlockSpec((1, C, RET_QK), lambda b, c: (b, c, 1)),
            pl.BlockSpec((1, C, RET_V), lambda b, c: (b, c, 0)),
            pl.BlockSpec((1, C, RET_V), lambda b, c: (b, c, 0)),
            pl.BlockSpec((N_META, RET_QK), lambda b, c: (0, 0)),
            pl.BlockSpec((N_META, RET_V), lambda b, c: (0, 0)),
        ],
        out_specs=pl.BlockSpec((1, C, RET_V), lambda b, c: (b, c, 0)),
        out_shape=jax.ShapeDtypeStruct((B, S, RET_V), BF16),
        scratch_shapes=[pltpu.VMEM((RET_HEADS, RET_DK, RET_DV), F32), pltpu.VMEM((RET_HEADS, C, C), F32)],
        compiler_params=_params(("parallel", "arbitrary")),
        name="retention",
    )(lg, gc, rqk, rqk, rv, rg, km, vm)


TINY = float(np.finfo(np.float32).tiny)
TN_DIMS = (((0,), (0,)), ((), ()))


def _dsa_kernel(iq_ref, ikw_ref, wq_ref, aq_ref, ak_ref, av_ref, akm_ref, avm_ref, o_ref,
                sc_ref, tau_ref, m_ref, l_ref, acc_ref, s_ref, bm_ref, *, nq, tq, tk, k_top):
    qi = pl.program_id(1)
    nkb = (qi + 1) * (tq // tk)
    kf = float(k_top)
    neg_inf = -jnp.inf
    qpos = qi * tq + lax.broadcasted_iota(jnp.int32, (1, tq), 1)

    def fold(x):
        return x.reshape(tk // 8, 8, tq).sum(axis=0)

    idx_scale = (IDX_DH ** -0.5) * (IDX_HEADS ** -0.5)
    iq = iq_ref[0].astype(BF16)
    w_t = wq_ref[0].T
    iq_h = [iq[:, h * IDX_DH:(h + 1) * IDX_DH] for h in range(IDX_HEADS)]
    w_h = [w_t[IDX_DH + h:IDX_DH + h + 1, :] for h in range(IDX_HEADS)]

    def score_rows(kb, n_blocks, carry):
        rmin, rmax = carry
        rows = n_blocks * tk
        ik = ikw_ref[0, pl.ds(pl.multiple_of(kb * tk, tk), rows), :][:, :IDX_DH].astype(BF16)
        acc = jnp.zeros((rows, tq), F32)
        for h in range(IDX_HEADS):
            acc = acc + w_h[h] * jnp.maximum(_dot_nt(ik, iq_h[h]), 0.0)
        ok = (lax.broadcasted_iota(jnp.int32, (rows, 1), 0) + kb * tk) <= qpos
        s = jnp.where(ok, acc * idx_scale, neg_inf)
        for i in range(n_blocks):
            sc_ref[kb + i] = s[i * tk:(i + 1) * tk]
        rmin = jnp.minimum(rmin, jnp.min(jnp.where(ok, s, jnp.inf), axis=0, keepdims=True))
        rmax = jnp.maximum(rmax, jnp.max(s, axis=0, keepdims=True))
        return rmin, rmax

    carry = (jnp.full((1, tq), jnp.inf, F32), jnp.full((1, tq), neg_inf, F32))
    carry = lax.fori_loop(0, nkb // 2, lambda j, c: score_rows(2 * j, 2, c), carry)
    rmin, rmax = lax.cond(nkb % 2 == 1, lambda c: score_rows(nkb - 1, 1, c), lambda c: c, carry)

    def count(pred):
        def body(kb, c):
            return c + fold(jnp.where(pred(sc_ref[kb]), 1.0, 0.0))
        return jnp.sum(lax.fori_loop(0, nkb, body, jnp.zeros((8, tq), F32)), axis=0, keepdims=True)

    def any_query(flag):
        return jnp.max(jnp.where(flag, 1.0, 0.0))

    c_max = count(lambda s: s >= rmax)
    top_tied = c_max >= kf
    lo0 = jnp.where(top_tied, rmax, rmin)
    cl0 = jnp.where(top_tied, c_max, (qpos + 1).astype(F32))

    def midpoint(lo, hi):
        a = jnp.maximum(jnp.abs(lo), TINY)
        b = jnp.maximum(jnp.abs(hi), TINY)
        geo = jnp.sqrt(a) * jnp.sqrt(b)
        one_sign = (lo >= 0.0) | (hi <= 0.0)
        far = jnp.maximum(a, b) > 4.0 * jnp.minimum(a, b)
        mid = jnp.where(one_sign & far, jnp.where(lo >= 0.0, geo, -geo), 0.5 * lo + 0.5 * hi)
        mid = jnp.where(lo == 0.0, TINY, mid)
        mid = jnp.where(hi == 0.0, -TINY, mid)
        return jnp.where((lo < 0.0) & (hi > 0.0), 0.0, mid)

    def active(lo, hi, cl):
        mid = midpoint(lo, hi)
        return (cl > kf) & (mid > lo) & (mid < hi)

    def go_on(st):
        return st[3] > 0.0

    def shrink_loop(n_blocks, lo, hi, cl):
        def count_ge(mid):
            c = jnp.zeros((8, tq), F32)
            for kb in range(n_blocks):
                c = c + fold(jnp.where(sc_ref[kb] >= mid, 1.0, 0.0))
            return jnp.sum(c, axis=0, keepdims=True)

        def shrink_once(lo, hi, cl):
            mid = midpoint(lo, hi)
            act = active(lo, hi, cl)
            c = count_ge(mid)
            up = act & (c >= kf)
            dn = act & (c < kf)
            return jnp.where(up, mid, lo), jnp.where(dn, mid, hi), jnp.where(up, c, cl)

        def shrink(st):
            pending = any_query(active(*st[:3]))
            return (*shrink_once(*shrink_once(*st[:3])), pending)

        return lax.while_loop(go_on, shrink, (lo, hi, cl, jnp.float32(1.0)))[:3]

    branches = [functools.partial(shrink_loop, (i + 1) * (tq // tk)) for i in range(nq)]
    lo, hi, cl = lax.switch(qi, branches, lo0, rmax, cl0)
    tau_ref[...] = lo
    tie = cl > kf

    @pl.when(any_query(tie) > 0.0)
    def _():
        def max_below(bound):
            def body(kb, v):
                s = sc_ref[kb]
                return jnp.maximum(v, jnp.max(jnp.where(s < bound, s, neg_inf), axis=0, keepdims=True))
            return lax.fori_loop(0, nkb, body, jnp.full((1, tq), neg_inf, F32))

        def count_ge_gt(v):
            def body(kb, c):
                s = sc_ref[kb]
                return c[0] + fold(jnp.where(s >= v, 1.0, 0.0)), c[1] + fold(jnp.where(s > v, 1.0, 0.0))
            zero = jnp.zeros((8, tq), F32)
            ge, gt = lax.fori_loop(0, nkb, body, (zero, zero))
            return jnp.sum(ge, axis=0, keepdims=True), jnp.sum(gt, axis=0, keepdims=True)

        hi_d = jnp.where(lo >= hi, jnp.inf, hi)
        v0 = max_below(hi_d)
        ge0, gt0 = count_ge_gt(v0)

        def walk_on(st):
            return st[4] > 0.0

        def walk(st):
            hi_w, v, ge, _, _ = st
            hi_w = jnp.where(tie & (ge < kf), v, hi_w)
            v = max_below(hi_w)
            ge, gt = count_ge_gt(v)
            return hi_w, v, ge, gt, any_query(tie & (ge < kf))

        _, v, _, gt, _ = lax.while_loop(walk_on, walk, (hi_d, v0, ge0, gt0, any_query(tie & (ge0 < kf))))
        keep = kf - gt
        ri = lax.broadcasted_iota(jnp.int32, (tk, tk), 0)
        ci = lax.broadcasted_iota(jnp.int32, (tk, tk), 1)
        earlier = jnp.where(ci < ri, 1.0, 0.0).astype(BF16)

        def drop(kb, seen):
            s = sc_ref[kb]
            eq = tie & (s == v)
            eqf = jnp.where(eq, 1.0, 0.0)
            rank = seen + _dot(earlier, eqf.astype(BF16))
            sc_ref[kb] = jnp.where(eq & (rank >= keep), neg_inf, s)
            return seen + jnp.sum(eqf, axis=0, keepdims=True)

        lax.fori_loop(0, nkb, drop, jnp.zeros((1, tq), F32))
        tau_ref[...] = jnp.where(tie, v, lo)

    tau = tau_ref[...]
    meta_bias = jnp.where(lax.broadcasted_iota(jnp.int32, (tk, 1), 0) < N_META, 0.0, neg_inf)

    def head(ref, h, rows=None):
        if rows is None:
            return ref[:, h * ATT_DH:(h + 1) * ATT_DH]
        return ref[0, rows, h * ATT_DH:(h + 1) * ATT_DH]

    def keys_of(kb):
        if kb is None:
            return (lambda h: head(akm_ref, h)), (lambda h: head(avm_ref, h)), meta_bias
        rows = pl.ds(pl.multiple_of(kb * tk, tk), tk)
        bias = jnp.where(sc_ref[kb] >= tau, 0.0, neg_inf)
        return (lambda h: head(ak_ref, h, rows)), (lambda h: head(av_ref, h, rows)), bias

    def score_phase(kb, buf):
        k_of, _, bias = keys_of(kb)
        for h in range(ATT_HEADS):
            s = _dot_nt(k_of(h), head(aq_ref, h, slice(None))) + bias
            s_ref[buf, h] = s
            bm_ref[buf, h:h + 1, :] = jnp.max(s, axis=0, keepdims=True)

    def value_phase(kb, buf):
        _, v_of, _ = keys_of(kb)
        for h in range(ATT_HEADS):
            m_new = jnp.maximum(m_ref[h:h + 1, :], bm_ref[buf, h:h + 1, :])
            p = jnp.exp2(s_ref[buf, h] - m_new)
            p_sum = jnp.sum(p, axis=0, keepdims=True)
            pv = lax.dot_general(v_of(h), p.astype(BF16), TN_DIMS, preferred_element_type=F32)
            alpha = jnp.exp2(m_ref[h:h + 1, :] - m_new)
            l_ref[h:h + 1, :] = alpha * l_ref[h:h + 1, :] + p_sum
            acc_ref[h] = alpha * acc_ref[h] + pv
            m_ref[h:h + 1, :] = m_new

    m_ref[...] = jnp.full_like(m_ref, neg_inf)
    l_ref[...] = jnp.zeros_like(l_ref)
    acc_ref[...] = jnp.zeros_like(acc_ref)
    score_phase(None, 0)

    @pl.when(nkb > 0)
    def _():
        value_phase(None, 0)
        score_phase(0, 1)

    def att_pair(j, carry):
        kb = 2 * j
        value_phase(kb, 1)
        score_phase(kb + 1, 0)

        @pl.when(kb + 1 < nkb)
        def _():
            value_phase(kb + 1, 0)
            score_phase(jnp.minimum(kb + 2, nkb - 1), 1)
        return carry

    lax.fori_loop(0, nkb // 2, att_pair, 0)

    @pl.when(nkb % 2 == 1)
    def _():
        value_phase(nkb - 1, 1)

    for h in range(ATT_HEADS):
        out_t = acc_ref[h] / l_ref[h:h + 1, :]
        o_ref[0, :, h * ATT_DH:(h + 1) * ATT_DH] = out_t.T.astype(o_ref.dtype)


def _dsa(iq, ikw, aqk, av, akm, avm, *, tq, tk, k_top):
    B, S, _ = av.shape
    nq = S // tq
    assert tq % tk == 0
    return pl.pallas_call(
        functools.partial(_dsa_kernel, nq=nq, tq=tq, tk=tk, k_top=k_top),
        grid=(B, nq),
        in_specs=[
            pl.BlockSpec((1, tq, IDX_Q), lambda b, q: (b, q, 0)),
            pl.BlockSpec((1, S, LANES), lambda b, q: (b, 0, 0)),
            pl.BlockSpec((1, tq, LANES), lambda b, q: (b, q, 0)),
            pl.BlockSpec((1, tq, ATT_W), lambda b, q: (b, q, 0)),
            pl.BlockSpec((1, S, ATT_W), lambda b, q: (b, 0, 1)),
            pl.BlockSpec((1, S, ATT_W), lambda b, q: (b, 0, 0)),
            pl.BlockSpec((tk, ATT_W), lambda b, q: (0, 0)),
            pl.BlockSpec((tk, ATT_W), lambda b, q: (0, 0)),
        ],
        out_specs=pl.BlockSpec((1, tq, ATT_W), lambda b, q: (b, q, 0)),
        out_shape=jax.ShapeDtypeStruct((B, S, ATT_W), BF16),
        scratch_shapes=[
            pltpu.VMEM((S // tk, tk, tq), F32),
            pltpu.VMEM((1, tq), F32),
            pltpu.VMEM((ATT_HEADS, tq), F32),
            pltpu.VMEM((ATT_HEADS, tq), F32),
            pltpu.VMEM((ATT_HEADS, ATT_DH, tq), F32),
            pltpu.VMEM((2, ATT_HEADS, tk, tq), F32),
            pltpu.VMEM((2, ATT_HEADS, tq), F32),
        ],
        compiler_params=_params(("parallel", "arbitrary")),
        name="dsa",
    )(iq, ikw, ikw, aqk, aqk, av, akm, avm)


def _merge_kernel(h_ref, yr_ref, ya_ref, gr_ref, ga_ref, wr_ref, wa_ref, wm_ref, o_ref):
    yr = _dot(yr_ref[...], wr_ref[...])
    ya = _dot(ya_ref[...], wa_ref[...])
    merged = gr_ref[...] * yr + ga_ref[...] * ya
    o_ref[...] = h_ref[...] + _dot(merged.astype(BF16), wm_ref[...])


def _merge(h, yr, ya, gates, wr, wa, wm, *, tm):
    M, D = h.shape
    row = lambda i: (i, 0)
    fixed = lambda i: (0, 0)
    return pl.pallas_call(
        _merge_kernel,
        grid=(M // tm,),
        in_specs=[
            pl.BlockSpec((tm, D), row),
            pl.BlockSpec((tm, RET_V), row),
            pl.BlockSpec((tm, ATT_W), row),
            pl.BlockSpec((tm, D), lambda i: (i, 0)),
            pl.BlockSpec((tm, D), lambda i: (i, 1)),
            pl.BlockSpec((RET_V, D), fixed),
            pl.BlockSpec((ATT_W, D), fixed),
            pl.BlockSpec((D, D), fixed),
        ],
        out_specs=pl.BlockSpec((tm, D), row),
        out_shape=jax.ShapeDtypeStruct((M, D), F32),
        compiler_params=_params(("parallel",)),
        name="merge",
    )(h, yr, ya, gates, gates, wr, wa, wm)


def _rope_tables(pos, d, width):
    inv_freq = ROPE_THETA ** (-jnp.arange(0, d, 2, dtype=F32) / d)
    ang = pos.astype(F32)[:, None] * inv_freq[None, :]
    cos, sin = jnp.cos(ang), jnp.sin(ang)
    cos = jnp.concatenate([cos, cos], axis=-1)
    sin = jnp.concatenate([-sin, sin], axis=-1)
    rep = width // d
    return jnp.tile(cos, (1, rep)), jnp.tile(sin, (1, rep))


def _tile(n, cap):
    t = min(n, cap)
    assert n % t == 0, (n, t)
    return t


def kernel(x, meta_tokens, ffn1_norm, ffn1_w_gate, ffn1_w_up, ffn1_w_down, mix_norm, w_in, w_ret_out,
           w_att_out, w_mix_out, ffn2_norm, ffn2_w_gate, ffn2_w_up, ffn2_w_down, final_norm):
    B, S, D = x.shape
    assert ffn1_norm.shape[0] == 1, "single layer only"
    assert meta_tokens.shape[0] == N_META
    k_top = min(TOPK_MAX, (S + 0) // 4)
    M = B * S
    bf = lambda a: a.astype(BF16)
    row = lambda a: a.reshape(1, -1)

    w32 = w_in[0].T
    offs = np.cumsum([0, RET_QK, RET_QK, RET_V, RET_V, ATT_W, ATT_W, ATT_W, IDX_Q, IDX_DH, IDX_HEADS, D, D])
    tn = 1024
    assert all(int(o) % tn == 0 for o in offs[:8])
    c_rq, c_rk, c_rv, c_rg, c_aq, c_ak, c_av = (int(o) // tn for o in offs[:7])
    w_idx = jnp.pad(bf(w32[offs[7]:offs[10]].T), ((0, 0), (0, LANES - IDX_DH - IDX_HEADS)))
    w_gate = w32[offs[10]:offs[12]]

    pos_meta = jnp.arange(N_META, dtype=jnp.int32)
    pos_real = N_META + jnp.arange(S, dtype=jnp.int32)
    ffn1 = (row(ffn1_norm[0]), ffn1_w_gate[0], ffn1_w_up[0], ffn1_w_down[0], row(mix_norm[0]))
    ffn2 = (row(ffn2_norm[0]), ffn2_w_gate[0], ffn2_w_up[0], ffn2_w_down[0], row(final_norm))
    tf = 256
    k_scale = RET_DK ** -0.5

    _, un_m = _ffn(meta_tokens.astype(F32), *ffn1, tm=N_META, tf=tf, emit_h=True, n_dtype=BF16)
    pm = dict(tm=N_META, tn=tn)
    rk_m = _proj(un_m, w32, col0=c_rk, n_tiles=RET_QK // tn, mode="rope", d=RET_DK, k_scale=k_scale,
                 tabs=_rope_tables(pos_meta, RET_DK, RET_DK), **pm)
    rv_m = _proj(un_m, w32, col0=c_rv, n_tiles=RET_V // tn, mode="plain", **pm)
    ak_m = _proj(un_m, w32, col0=c_ak, n_tiles=ATT_W // tn, mode="rope", d=ATT_DH,
                 tabs=_rope_tables(pos_meta, ATT_DH, LANES), **pm)
    av_m = _proj(un_m, w32, col0=c_av, n_tiles=ATT_W // tn, mode="plain", **pm)

    tm = _tile(S, 1024)
    h1, un = _ffn(x.reshape(M, D), *ffn1, tm=tm, tf=tf, emit_h=True, n_dtype=BF16)
    pr = dict(tm=tm, tn=tn)
    rqk = _proj(un, w32, col0=c_rq, n_tiles=2 * RET_QK // tn, mode="rope", d=RET_DK, k_scale=k_scale,
                n_q=RET_QK // tn, tabs=_rope_tables(pos_real, RET_DK, RET_DK), **pr)
    rv = _proj(un, w32, col0=c_rv, n_tiles=RET_V // tn, mode="plain", **pr)
    rg = _proj(un, w32, col0=c_rg, n_tiles=RET_V // tn, mode="silu", **pr)
    aqk = _proj(un, w32, col0=c_aq, n_tiles=2 * ATT_W // tn, mode="rope", d=ATT_DH, q_scale=ATT_EXP_SCALE,
                n_q=ATT_W // tn, tabs=_rope_tables(pos_real, ATT_DH, LANES), **pr)
    av = _proj(un, w32, col0=c_av, n_tiles=ATT_W // tn, mode="plain", **pr)
    gates = _proj(un, w_gate, col0=0, n_tiles=2 * D // tn, mode="sigmoid", **pr)
    iq, ikw = _idx_proj(un, w_idx, _rope_tables(pos_real, IDX_DH, LANES), tm=tm)

    heads = jnp.arange(RET_HEADS, dtype=F32)
    log_g = jnp.log1p(-(2.0 ** (-5.0 - heads)))
    C = _tile(S, 4 * RET_CHUNK)
    b3 = lambda a: a.reshape(B, S, a.shape[-1])
    yr = _retention(log_g, jnp.exp(log_g * C), b3(rqk), b3(rv), b3(rg), rk_m, rv_m, C=C)
    tk = _tile(S, 256)
    pad_meta = lambda a: jnp.pad(a, ((0, tk - N_META), (0, 0)))
    ya = _dsa(b3(iq), b3(ikw), b3(aqk), b3(av), pad_meta(ak_m), pad_meta(av_m),
              tq=_tile(S, 256), tk=tk, k_top=k_top)

    h2 = _merge(h1, yr.reshape(M, RET_V), ya.reshape(M, ATT_W), gates, bf(w_ret_out[0]), bf(w_att_out[0]),
                bf(w_mix_out[0]), tm=_tile(M, 512))
    (y,) = _ffn(h2, *ffn2, tm=tm, tf=tf, emit_h=False, n_dtype=F32)
    return y.reshape(B, S, D)
```

```python
import functools

import jax
import jax.numpy as jnp
import numpy as np
from jax import lax
from jax.experimental import pallas as pl
from jax.experimental.pallas import tpu as pltpu

N_META = 16
RET_HEADS = 4
RET_DK = 256
RET_DV = 512
RET_CHUNK = 128
ATT_HEADS = 8
ATT_DH = 128
IDX_HEADS = 8
IDX_DH = 64
TOPK_MAX = 256
ROPE_THETA = 10000.0
EPS = 1e-6
RET_QK = RET_HEADS * RET_DK
RET_V = RET_HEADS * RET_DV
ATT_W = ATT_HEADS * ATT_DH
IDX_Q = IDX_HEADS * IDX_DH
ATT_EXP_SCALE = (ATT_DH ** -0.5) * float(np.log2(np.e))

LANES = 128
VMEM_LIMIT = 48 * 1024 * 1024
FFN_VMEM_LIMIT = 56 * 1024 * 1024

BF16 = jnp.bfloat16
F32 = jnp.float32
NT_DIMS = (((1,), (1,)), ((), ()))


def _dot(a, b):
    return jnp.dot(a, b, preferred_element_type=F32)


def _dot_nt(a, b):
    return lax.dot_general(a, b, NT_DIMS, preferred_element_type=F32)


def _rms(x, g):
    return x * lax.rsqrt(jnp.mean(x * x, axis=-1, keepdims=True) + EPS) * g


def _sigmoid(x):
    return 0.5 * jnp.tanh(0.5 * x) + 0.5


def _params(sem):
    return pltpu.CompilerParams(dimension_semantics=sem, vmem_limit_bytes=VMEM_LIMIT)


def _ffn_kernel(x_ref, g_ref, wg_ref, wu_ref, wd_ref, g2_ref, *rest, nf, tf, emit_h):
    if emit_h:
        h_ref, n_ref, xn_sc, acc_sc, act_sc = rest
    else:
        n_ref, xn_sc, acc_sc, act_sc = rest
    f = pl.program_id(1)

    def down():
        rows = pl.ds(pl.multiple_of((f - 1) * tf, tf), tf)
        return _dot(act_sc[...], wd_ref[rows, :])

    def up():
        cols = pl.ds(pl.multiple_of(f * tf, tf), tf)
        xn = xn_sc[...]
        gate = _dot(xn, wg_ref[:, cols])
        lift = _dot(xn, wu_ref[:, cols])
        act_sc[...] = ((gate * _sigmoid(gate)) * lift).astype(BF16)

    @pl.when(f == 0)
    def _():
        xn_sc[...] = _rms(x_ref[...], g_ref[...]).astype(BF16)
        up()

    @pl.when(f == 1)
    def _():
        acc_sc[...] = down()
        up()

    @pl.when((f > 1) & (f < nf))
    def _():
        acc_sc[...] += down()
        up()

    @pl.when(f == nf)
    def _():
        h = x_ref[...] + 0.5 * (acc_sc[...] + down())
        if emit_h:
            h_ref[...] = h
        n_ref[...] = _rms(h, g2_ref[...]).astype(n_ref.dtype)


def _ffn(x, g, wg, wu, wd, g2, *, tm, tf, emit_h, n_dtype):
    M, D = x.shape
    F = wg.shape[1]
    nf = F // tf
    assert nf >= 2
    out_shape = [jax.ShapeDtypeStruct((M, D), n_dtype)]
    out_specs = [pl.BlockSpec((tm, D), lambda i, f: (i, 0))]
    if emit_h:
        out_shape = [jax.ShapeDtypeStruct((M, D), F32)] + out_shape
        out_specs = [pl.BlockSpec((tm, D), lambda i, f: (i, 0))] + out_specs
    resident = lambda shape: pl.BlockSpec(shape, lambda i, f: (0, 0), pipeline_mode=pl.Buffered(1))
    return pl.pallas_call(
        functools.partial(_ffn_kernel, nf=nf, tf=tf, emit_h=emit_h),
        grid=(M // tm, nf + 1),
        in_specs=[
            pl.BlockSpec((tm, D), lambda i, f: (i, 0)),
            pl.BlockSpec((1, D), lambda i, f: (0, 0)),
            resident((D, F)),
            resident((D, F)),
            resident((F, D)),
            pl.BlockSpec((1, D), lambda i, f: (0, 0)),
        ],
        out_specs=out_specs,
        out_shape=out_shape,
        scratch_shapes=[pltpu.VMEM((tm, D), BF16), pltpu.VMEM((tm, D), F32), pltpu.VMEM((tm, tf), BF16)],
        compiler_params=pltpu.CompilerParams(
            dimension_semantics=("arbitrary", "arbitrary"), vmem_limit_bytes=FFN_VMEM_LIMIT),
        name="ffn",
    )(x, g, wg, wu, wd, g2)


def _rot_half(y, d):
    w = y.shape[-1]
    if d >= w:
        return jnp.concatenate([y[:, w // 2:], y[:, : w // 2]], axis=-1)
    if d == LANES:
        return pltpu.roll(y, d // 2, axis=1)
    fwd = pltpu.roll(y, d // 2, axis=1)
    bwd = pltpu.roll(y, w - d // 2, axis=1)
    lane = lax.broadcasted_iota(jnp.int32, y.shape, 1)
    return jnp.where((lane % d) < d // 2, bwd, fwd)


def _rope_tile(y, cos, sin, d):
    tw = cos.shape[-1]
    outs = []
    for t in range(y.shape[-1] // tw):
        yt = y[:, t * tw:(t + 1) * tw]
        outs.append(yt * cos + _rot_half(yt, d) * sin)
    return outs[0] if len(outs) == 1 else jnp.concatenate(outs, axis=-1)


def _proj_kernel(x_ref, w_ref, *rest, mode, d, q_scale, k_scale, n_q):
    if mode == "rope":
        cos_ref, sin_ref, o_ref, w_sc = rest
    else:
        o_ref, w_sc = rest

    @pl.when(pl.program_id(1) == 0)
    def _():
        w_sc[...] = w_ref[...].T.astype(BF16)

    y = _dot(x_ref[...], w_sc[...])
    if mode == "rope":
        y = _rope_tile(y, cos_ref[...], sin_ref[...], d)
        if q_scale != 1.0 or k_scale != 1.0:
            y = y * jnp.where(pl.program_id(0) >= n_q, k_scale, q_scale)
    elif mode == "silu":
        y = y * _sigmoid(y)
    elif mode == "sigmoid":
        y = _sigmoid(y)
    o_ref[...] = y.astype(o_ref.dtype)


def _proj(x, w_t, *, col0, n_tiles, tm, tn, mode, out_dtype=None, tabs=None, d=None, q_scale=1.0,
          k_scale=1.0, n_q=0):
    M, K = x.shape
    in_specs = [
        pl.BlockSpec((tm, K), lambda j, i: (i, 0)),
        pl.BlockSpec((tn, K), lambda j, i: (col0 + j, 0)),
    ]
    args = [x, w_t]
    if mode == "rope":
        cos, sin = tabs
        npos = cos.shape[0] // tm
        tw = cos.shape[1]
        in_specs += [pl.BlockSpec((tm, tw), lambda j, i: (i % npos, 0))] * 2
        args += [cos, sin]
    return pl.pallas_call(
        functools.partial(_proj_kernel, mode=mode, d=d, q_scale=q_scale, k_scale=k_scale, n_q=n_q),
        grid=(n_tiles, M // tm),
        in_specs=in_specs,
        out_specs=pl.BlockSpec((tm, tn), lambda j, i: (i, j)),
        out_shape=jax.ShapeDtypeStruct((M, n_tiles * tn), out_dtype or BF16),
        scratch_shapes=[pltpu.VMEM((K, tn), BF16)],
        compiler_params=_params(("arbitrary", "arbitrary")),
        name="proj_" + mode,
    )(*args)


def _idx_proj_kernel(x_ref, w_ref, cos_ref, sin_ref, q_ref, kw_ref):
    y = _dot(x_ref[...], w_ref[...])
    cos, sin = cos_ref[...], sin_ref[...]
    q_ref[...] = _rope_tile(y[:, :IDX_Q], cos, sin, IDX_DH)
    tail = y[:, IDX_Q:]
    lane = lax.broadcasted_iota(jnp.int32, tail.shape, 1)
    kw_ref[...] = jnp.where(lane < IDX_DH, _rope_tile(tail, cos, sin, IDX_DH), tail)


def _idx_proj(x, w, tabs, *, tm):
    M, K = x.shape
    N = w.shape[1]
    cos, sin = tabs
    npos = cos.shape[0] // tm
    return pl.pallas_call(
        _idx_proj_kernel,
        grid=(M // tm,),
        in_specs=[
            pl.BlockSpec((tm, K), lambda i: (i, 0)),
            pl.BlockSpec((K, N), lambda i: (0, 0)),
            pl.BlockSpec((tm, LANES), lambda i: (i % npos, 0)),
            pl.BlockSpec((tm, LANES), lambda i: (i % npos, 0)),
        ],
        out_specs=[
            pl.BlockSpec((tm, IDX_Q), lambda i: (i, 0)),
            pl.BlockSpec((tm, LANES), lambda i: (i, 0)),
        ],
        out_shape=[
            jax.ShapeDtypeStruct((M, IDX_Q), F32),
            jax.ShapeDtypeStruct((M, LANES), F32),
        ],
        compiler_params=_params(("parallel",)),
        name="proj_idx",
    )(x, w, cos, sin)


def _ret_kernel(lg_ref, gc_ref, q_ref, k_ref, v_ref, g_ref, km_ref, vm_ref, o_ref, st_ref, decay_ref, *, C):
    c = pl.program_id(1)
    dk = lambda h: slice(h * RET_DK, (h + 1) * RET_DK)
    dv = lambda h: slice(h * RET_DV, (h + 1) * RET_DV)

    @pl.when(c == 0)
    def _():
        pm = lax.broadcasted_iota(jnp.int32, (N_META, 1), 0).astype(F32)
        ri = lax.broadcasted_iota(jnp.int32, (C, C), 0)
        ci = lax.broadcasted_iota(jnp.int32, (C, C), 1)
        diff = (ri - ci).astype(F32)
        for h in range(RET_HEADS):
            lg = lg_ref[h]
            zeta_m = jnp.exp(lg * (N_META - 1.0 - pm))
            kz = km_ref[:, dk(h)].astype(F32) * zeta_m
            st_ref[h] = _dot(kz.T.astype(BF16), vm_ref[:, dv(h)])
            decay_ref[h] = jnp.where(diff >= 0, jnp.exp(lg * jnp.maximum(diff, 0.0)), 0.0)

    pos = lax.broadcasted_iota(jnp.int32, (C, 1), 0).astype(F32)
    for h in range(RET_HEADS):
        lg = lg_ref[h]
        q = q_ref[0, :, dk(h)]
        k = k_ref[0, :, dk(h)]
        v = v_ref[0, :, dv(h)]
        xi = jnp.exp(lg * (pos + 1.0))
        zeta = jnp.exp(lg * (C - 1.0 - pos))
        state = st_ref[h]
        inner = _dot((_dot_nt(q, k) * decay_ref[h]).astype(BF16), v)
        cross = _dot(q, state.astype(BF16)) * xi
        y = inner + cross
        y = y * lax.rsqrt(jnp.mean(y * y, axis=-1, keepdims=True) + EPS)
        o_ref[0, :, dv(h)] = (g_ref[0, :, dv(h)] * y).astype(o_ref.dtype)
        kz = k.astype(F32) * zeta
        st_ref[h] = gc_ref[h] * state + _dot(kz.T.astype(BF16), v)


def _retention(lg, gc, rqk, rv, rg, km, vm, *, C):
    B, S, _ = rv.shape
    smem = pl.BlockSpec(memory_space=pltpu.SMEM)
    return pl.pallas_call(
        functools.partial(_ret_kernel, C=C),
        grid=(B, S // C),
        in_specs=[
            smem,
            smem,
            pl.BlockSpec((1, C, RET_QK), lambda b, c: (b, c, 0)),
            pl.BlockSpec((1, C, RET_QK), lambda b, c: (b, c, 1)),
            pl.BlockSpec((1, C, RET_V), lambda b, c: (b, c, 0)),
            pl.BlockSpec((1, C, RET_V), lambda b, c: (b, c, 0)),
            pl.BlockSpec((N_META, RET_QK), lambda b, c: (0, 0)),
            pl.BlockSpec((N_META, RET_V), lambda b, c: (0, 0)),
        ],
        out_specs=pl.BlockSpec((1, C, RET_V), lambda b, c: (b, c, 0)),
        out_shape=jax.ShapeDtypeStruct((B, S, RET_V), BF16),
        scratch_shapes=[pltpu.VMEM((RET_HEADS, RET_DK, RET_DV), F32), pltpu.VMEM((RET_HEADS, C, C), F32)],
        compiler_params=_params(("parallel", "arbitrary")),
        name="retention",
    )(lg, gc, rqk, rqk, rv, rg, km, vm)


TINY = float(np.finfo(np.float32).tiny)
TN_DIMS = (((0,), (0,)), ((), ()))


def _dsa_kernel(iq_ref, ikw_ref, wq_ref, aq_ref, ak_ref, av_ref, akm_ref, avm_ref, o_ref,
                sc_ref, tau_ref, m_ref, l_ref, acc_ref, s_ref, bm_ref, *, nq, tq, tk, k_top):
    qi = pl.program_id(1)
    nkb = (qi + 1) * (tq // tk)
    kf = float(k_top)
    neg_inf = -jnp.inf
    qpos = qi * tq + lax.broadcasted_iota(jnp.int32, (1, tq), 1)

    def fold(x):
        return x.reshape(tk // 8, 8, tq).sum(axis=0)

    idx_scale = (IDX_DH ** -0.5) * (IDX_HEADS ** -0.5)
    iq = iq_ref[0].astype(BF16)
    w_t = wq_ref[0].T
    iq_h = [iq[:, h * IDX_DH:(h + 1) * IDX_DH] for h in range(IDX_HEADS)]
    w_h = [w_t[IDX_DH + h:IDX_DH + h + 1, :] for h in range(IDX_HEADS)]

    def score_rows(kb, n_blocks, carry):
        rmin, rmax = carry
        rows = n_blocks * tk
        ik = ikw_ref[0, pl.ds(pl.multiple_of(kb * tk, tk), rows), :][:, :IDX_DH].astype(BF16)
        acc = jnp.zeros((rows, tq), F32)
        for h in range(IDX_HEADS):
            acc = acc + w_h[h] * jnp.maximum(_dot_nt(ik, iq_h[h]), 0.0)
        ok = (lax.broadcasted_iota(jnp.int32, (rows, 1), 0) + kb * tk) <= qpos
        s = jnp.where(ok, acc * idx_scale, neg_inf)
        for i in range(n_blocks):
            sc_ref[kb + i] = s[i * tk:(i + 1) * tk]
        rmin = jnp.minimum(rmin, jnp.min(jnp.where(ok, s, jnp.inf), axis=0, keepdims=True))
        rmax = jnp.maximum(rmax, jnp.max(s, axis=0, keepdims=True))
        return rmin, rmax

    carry = (jnp.full((1, tq), jnp.inf, F32), jnp.full((1, tq), neg_inf, F32))
    carry = lax.fori_loop(0, nkb // 2, lambda j, c: score_rows(2 * j, 2, c), carry)
    rmin, rmax = lax.cond(nkb % 2 == 1, lambda c: score_rows(nkb - 1, 1, c), lambda c: c, carry)

    def count(pred):
        def body(kb, c):
            return c + fold(jnp.where(pred(sc_ref[kb]), 1.0, 0.0))
        return jnp.sum(lax.fori_loop(0, nkb, body, jnp.zeros((8, tq), F32)), axis=0, keepdims=True)

    def any_query(flag):
        return jnp.max(jnp.where(flag, 1.0, 0.0))

    c_max = count(lambda s: s >= rmax)
    top_tied = c_max >= kf
    lo0 = jnp.where(top_tied, rmax, rmin)
    cl0 = jnp.where(top_tied, c_max, (qpos + 1).astype(F32))

    def midpoint(lo, hi):
        a = jnp.maximum(jnp.abs(lo), TINY)
        b = jnp.maximum(jnp.abs(hi), TINY)
        geo = jnp.sqrt(a) * jnp.sqrt(b)
        one_sign = (lo >= 0.0) | (hi <= 0.0)
        far = jnp.maximum(a, b) > 4.0 * jnp.minimum(a, b)
        mid = jnp.where(one_sign & far, jnp.where(lo >= 0.0, geo, -geo), 0.5 * lo + 0.5 * hi)
        mid = jnp.where(lo == 0.0, TINY, mid)
        mid = jnp.where(hi == 0.0, -TINY, mid)
        return jnp.where((lo < 0.0) & (hi > 0.0), 0.0, mid)

    def active(lo, hi, cl):
        mid = midpoint(lo, hi)
        return (cl > kf) & (mid > lo) & (mid < hi)

    def go_on(st):
        return st[3] > 0.0

    def shrink_loop(n_blocks, lo, hi, cl):
        def count_ge(mid):
            c = jnp.zeros((8, tq), F32)
            for kb in range(n_blocks):
                c = c + fold(jnp.where(sc_ref[kb] >= mid, 1.0, 0.0))
            return jnp.sum(c, axis=0, keepdims=True)

        def shrink_once(lo, hi, cl):
            mid = midpoint(lo, hi)
            act = active(lo, hi, cl)
            c = count_ge(mid)
            up = act & (c >= kf)
            dn = act & (c < kf)
            return jnp.where(up, mid, lo), jnp.where(dn, mid, hi), jnp.where(up, c, cl)

        def shrink(st):
            pending = any_query(active(*st[:3]))
            return (*shrink_once(*shrink_once(*st[:3])), pending)

        return lax.while_loop(go_on, shrink, (lo, hi, cl, jnp.float32(1.0)))[:3]

    branches = [functools.partial(shrink_loop, (i + 1) * (tq // tk)) for i in range(nq)]
    lo, hi, cl = lax.switch(qi, branches, lo0, rmax, cl0)
    tau_ref[...] = lo
    tie = cl > kf

    @pl.when(any_query(tie) > 0.0)
    def _():
        def max_below(bound):
            def body(kb, v):
                s = sc_ref[kb]
                return jnp.maximum(v, jnp.max(jnp.where(s < bound, s, neg_inf), axis=0, keepdims=True))
            return lax.fori_loop(0, nkb, body, jnp.full((1, tq), neg_inf, F32))

        def count_ge_gt(v):
            def body(kb, c):
                s = sc_ref[kb]
                return c[0] + fold(jnp.where(s >= v, 1.0, 0.0)), c[1] + fold(jnp.where(s > v, 1.0, 0.0))
            zero = jnp.zeros((8, tq), F32)
            ge, gt = lax.fori_loop(0, nkb, body, (zero, zero))
            return jnp.sum(ge, axis=0, keepdims=True), jnp.sum(gt, axis=0, keepdims=True)

        hi_d = jnp.where(lo >= hi, jnp.inf, hi)
        v0 = max_below(hi_d)
        ge0, gt0 = count_ge_gt(v0)

        def walk_on(st):
            return st[4] > 0.0

        def walk(st):
            hi_w, v, ge, _, _ = st
            hi_w = jnp.where(tie & (ge < kf), v, hi_w)
            v = max_below(hi_w)
            ge, gt = count_ge_gt(v)
            return hi_w, v, ge, gt, any_query(tie & (ge < kf))

        _, v, _, gt, _ = lax.while_loop(walk_on, walk, (hi_d, v0, ge0, gt0, any_query(tie & (ge0 < kf))))
        keep = kf - gt
        ri = lax.broadcasted_iota(jnp.int32, (tk, tk), 0)
        ci = lax.broadcasted_iota(jnp.int32, (tk, tk), 1)
        earlier = jnp.where(ci < ri, 1.0, 0.0).astype(BF16)

        def drop(kb, seen):
            s = sc_ref[kb]
            eq = tie & (s == v)
            eqf = jnp.where(eq, 1.0, 0.0)
            rank = seen + _dot(earlier, eqf.astype(BF16))
            sc_ref[kb] = jnp.where(eq & (rank >= keep), neg_inf, s)
            return seen + jnp.sum(eqf, axis=0, keepdims=True)

        lax.fori_loop(0, nkb, drop, jnp.zeros((1, tq), F32))
        tau_ref[...] = jnp.where(tie, v, lo)

    tau = tau_ref[...]
    meta_bias = jnp.where(lax.broadcasted_iota(jnp.int32, (tk, 1), 0) < N_META, 0.0, neg_inf)

    def head(ref, h, rows=None):
        if rows is None:
            return ref[:, h * ATT_DH:(h + 1) * ATT_DH]
        return ref[0, rows, h * ATT_DH:(h + 1) * ATT_DH]

    def keys_of(kb):
        if kb is None:
            return (lambda h: head(akm_ref, h)), (lambda h: head(avm_ref, h)), meta_bias
        rows = pl.ds(pl.multiple_of(kb * tk, tk), tk)
        bias = jnp.where(sc_ref[kb] >= tau, 0.0, neg_inf)
        return (lambda h: head(ak_ref, h, rows)), (lambda h: head(av_ref, h, rows)), bias

    def score_phase(kb, buf):
        k_of, _, bias = keys_of(kb)
        for h in range(ATT_HEADS):
            s = _dot_nt(k_of(h), head(aq_ref, h, slice(None))) + bias
            s_ref[buf, h] = s
            bm_ref[buf, h:h + 1, :] = jnp.max(s, axis=0, keepdims=True)

    def value_phase(kb, buf):
        _, v_of, _ = keys_of(kb)
        for h in range(ATT_HEADS):
            m_new = jnp.maximum(m_ref[h:h + 1, :], bm_ref[buf, h:h + 1, :])
            p = jnp.exp2(s_ref[buf, h] - m_new)
            p_sum = jnp.sum(p, axis=0, keepdims=True)
            pv = lax.dot_general(v_of(h), p.astype(BF16), TN_DIMS, preferred_element_type=F32)
            alpha = jnp.exp2(m_ref[h:h + 1, :] - m_new)
            l_ref[h:h + 1, :] = alpha * l_ref[h:h + 1, :] + p_sum
            acc_ref[h] = alpha * acc_ref[h] + pv
            m_ref[h:h + 1, :] = m_new

    m_ref[...] = jnp.full_like(m_ref, neg_inf)
    l_ref[...] = jnp.zeros_like(l_ref)
    acc_ref[...] = jnp.zeros_like(acc_ref)
    score_phase(None, 0)

    @pl.when(nkb > 0)
    def _():
        value_phase(None, 0)
        score_phase(0, 1)

    def att_pair(j, carry):
        kb = 2 * j
        value_phase(kb, 1)
        score_phase(kb + 1, 0)

        @pl.when(kb + 1 < nkb)
        def _():
            value_phase(kb + 1, 0)
            score_phase(jnp.minimum(kb + 2, nkb - 1), 1)
        return carry

    lax.fori_loop(0, nkb // 2, att_pair, 0)

    @pl.when(nkb % 2 == 1)
    def _():
        value_phase(nkb - 1, 1)

    for h in range(ATT_HEADS):
        out_t = acc_ref[h] / l_ref[h:h + 1, :]
        o_ref[0, :, h * ATT_DH:(h + 1) * ATT_DH] = out_t.T.astype(o_ref.dtype)


def _dsa(iq, ikw, aqk, av, akm, avm, *, tq, tk, k_top):
    B, S, _ = av.shape
    nq = S // tq
    assert tq % tk == 0
    return pl.pallas_call(
        functools.partial(_dsa_kernel, nq=nq, tq=tq, tk=tk, k_top=k_top),
        grid=(B, nq),
        in_specs=[
            pl.BlockSpec((1, tq, IDX_Q), lambda b, q: (b, q, 0)),
            pl.BlockSpec((1, S, LANES), lambda b, q: (b, 0, 0)),
            pl.BlockSpec((1, tq, LANES), lambda b, q: (b, q, 0)),
            pl.BlockSpec((1, tq, ATT_W), lambda b, q: (b, q, 0)),
            pl.BlockSpec((1, S, ATT_W), lambda b, q: (b, 0, 1)),
            pl.BlockSpec((1, S, ATT_W), lambda b, q: (b, 0, 0)),
            pl.BlockSpec((tk, ATT_W), lambda b, q: (0, 0)),
            pl.BlockSpec((tk, ATT_W), lambda b, q: (0, 0)),
        ],
        out_specs=pl.BlockSpec((1, tq, ATT_W), lambda b, q: (b, q, 0)),
        out_shape=jax.ShapeDtypeStruct((B, S, ATT_W), BF16),
        scratch_shapes=[
            pltpu.VMEM((S // tk, tk, tq), F32),
            pltpu.VMEM((1, tq), F32),
            pltpu.VMEM((ATT_HEADS, tq), F32),
            pltpu.VMEM((ATT_HEADS, tq), F32),
            pltpu.VMEM((ATT_HEADS, ATT_DH, tq), F32),
            pltpu.VMEM((2, ATT_HEADS, tk, tq), F32),
            pltpu.VMEM((2, ATT_HEADS, tq), F32),
        ],
        compiler_params=_params(("parallel", "arbitrary")),
        name="dsa",
    )(iq, ikw, ikw, aqk, aqk, av, akm, avm)


def _merge_kernel(h_ref, yr_ref, ya_ref, gr_ref, ga_ref, wr_ref, wa_ref, wm_ref, o_ref):
    yr = _dot(yr_ref[...], wr_ref[...])
    ya = _dot(ya_ref[...], wa_ref[...])
    merged = gr_ref[...] * yr + ga_ref[...] * ya
    o_ref[...] = h_ref[...] + _dot(merged.astype(BF16), wm_ref[...])


def _merge(h, yr, ya, gates, wr, wa, wm, *, tm):
    M, D = h.shape
    row = lambda i: (i, 0)
    fixed = lambda i: (0, 0)
    return pl.pallas_call(
        _merge_kernel,
        grid=(M // tm,),
        in_specs=[
            pl.BlockSpec((tm, D), row),
            pl.BlockSpec((tm, RET_V), row),
            pl.BlockSpec((tm, ATT_W), row),
            pl.BlockSpec((tm, D), lambda i: (i, 0)),
            pl.BlockSpec((tm, D), lambda i: (i, 1)),
            pl.BlockSpec((RET_V, D), fixed),
            pl.BlockSpec((ATT_W, D), fixed),
            pl.BlockSpec((D, D), fixed),
        ],
        out_specs=pl.BlockSpec((tm, D), row),
        out_shape=jax.ShapeDtypeStruct((M, D), F32),
        compiler_params=_params(("parallel",)),
        name="merge",
    )(h, yr, ya, gates, gates, wr, wa, wm)


def _rope_tables(pos, d, width):
    inv_freq = ROPE_THETA ** (-jnp.arange(0, d, 2, dtype=F32) / d)
    ang = pos.astype(F32)[:, None] * inv_freq[None, :]
    cos, sin = jnp.cos(ang), jnp.sin(ang)
    cos = jnp.concatenate([cos, cos], axis=-1)
    sin = jnp.concatenate([-sin, sin], axis=-1)
    rep = width // d
    return jnp.tile(cos, (1, rep)), jnp.tile(sin, (1, rep))


def _tile(n, cap):
    t = min(n, cap)
    assert n % t == 0, (n, t)
    return t


def kernel(x, meta_tokens, ffn1_norm, ffn1_w_gate, ffn1_w_up, ffn1_w_down, mix_norm, w_in, w_ret_out,
           w_att_out, w_mix_out, ffn2_norm, ffn2_w_gate, ffn2_w_up, ffn2_w_down, final_norm):
    B, S, D = x.shape
    assert ffn1_norm.shape[0] == 1, "single layer only"
    assert meta_tokens.shape[0] == N_META
    k_top = min(TOPK_MAX, (S + 0) // 4)
    M = B * S
    bf = lambda a: a.astype(BF16)
    row = lambda a: a.reshape(1, -1)

    w32 = w_in[0].T
    offs = np.cumsum([0, RET_QK, RET_QK, RET_V, RET_V, ATT_W, ATT_W, ATT_W, IDX_Q, IDX_DH, IDX_HEADS, D, D])
    tn = 1024
    assert all(int(o) % tn == 0 for o in offs[:8])
    c_rq, c_rk, c_rv, c_rg, c_aq, c_ak, c_av = (int(o) // tn for o in offs[:7])
    w_idx = jnp.pad(bf(w32[offs[7]:offs[10]].T), ((0, 0), (0, LANES - IDX_DH - IDX_HEADS)))
    w_gate = w32[offs[10]:offs[12]]

    pos_meta = jnp.arange(N_META, dtype=jnp.int32)
    pos_real = N_META + jnp.arange(S, dtype=jnp.int32)
    ffn1 = (row(ffn1_norm[0]), bf(ffn1_w_gate[0]), bf(ffn1_w_up[0]), bf(ffn1_w_down[0]), row(mix_norm[0]))
    ffn2 = (row(ffn2_norm[0]), bf(ffn2_w_gate[0]), bf(ffn2_w_up[0]), bf(ffn2_w_down[0]), row(final_norm))
    tf = 256
    k_scale = RET_DK ** -0.5

    _, un_m = _ffn(meta_tokens.astype(F32), *ffn1, tm=N_META, tf=tf, emit_h=True, n_dtype=BF16)
    pm = dict(tm=N_META, tn=tn)
    rk_m = _proj(un_m, w32, col0=c_rk, n_tiles=RET_QK // tn, mode="rope", d=RET_DK, k_scale=k_scale,
                 tabs=_rope_tables(pos_meta, RET_DK, RET_DK), **pm)
    rv_m = _proj(un_m, w32, col0=c_rv, n_tiles=RET_V // tn, mode="plain", **pm)
    ak_m = _proj(un_m, w32, col0=c_ak, n_tiles=ATT_W // tn, mode="rope", d=ATT_DH,
                 tabs=_rope_tables(pos_meta, ATT_DH, LANES), **pm)
    av_m = _proj(un_m, w32, col0=c_av, n_tiles=ATT_W // tn, mode="plain", **pm)

    tm = _tile(S, 1024)
    h1, un = _ffn(x.reshape(M, D), *ffn1, tm=tm, tf=tf, emit_h=True, n_dtype=BF16)
    pr = dict(tm=tm, tn=tn)
    rqk = _proj(un, w32, col0=c_rq, n_tiles=2 * RET_QK // tn, mode="rope", d=RET_DK, k_scale=k_scale,
                n_q=RET_QK // tn, tabs=_rope_tables(pos_real, RET_DK, RET_DK), **pr)
    rv = _proj(un, w32, col0=c_rv, n_tiles=RET_V // tn, mode="plain", **pr)
    rg = _proj(un, w32, col0=c_rg, n_tiles=RET_V // tn, mode="silu", **pr)
    aqk = _proj(un, w32, col0=c_aq, n_tiles=2 * ATT_W // tn, mode="rope", d=ATT_DH, q_scale=ATT_EXP_SCALE,
                n_q=ATT_W // tn, tabs=_rope_tables(pos_real, ATT_DH, LANES), **pr)
    av = _proj(un, w32, col0=c_av, n_tiles=ATT_W // tn, mode="plain", **pr)
    gates = _proj(un, w_gate, col0=0, n_tiles=2 * D // tn, mode="sigmoid", **pr)
    iq, ikw = _idx_proj(un, w_idx, _rope_tables(pos_real, IDX_DH, LANES), tm=tm)

    heads = jnp.arange(RET_HEADS, dtype=F32)
    log_g = jnp.log1p(-(2.0 ** (-5.0 - heads)))
    C = _tile(S, 4 * RET_CHUNK)
    b3 = lambda a: a.reshape(B, S, a.shape[-1])
    yr = _retention(log_g, jnp.exp(log_g * C), b3(rqk), b3(rv), b3(rg), rk_m, rv_m, C=C)
    tk = _tile(S, 256)
    pad_meta = lambda a: jnp.pad(a, ((0, tk - N_META), (0, 0)))
    ya = _dsa(b3(iq), b3(ikw), b3(aqk), b3(av), pad_meta(ak_m), pad_meta(av_m),
              tq=_tile(S, 256), tk=tk, k_top=k_top)

    h2 = _merge(h1, yr.reshape(M, RET_V), ya.reshape(M, ATT_W), gates, bf(w_ret_out[0]), bf(w_att_out[0]),
                bf(w_mix_out[0]), tm=_tile(M, 512))
    (y,) = _ffn(h2, *ffn2, tm=tm, tf=tf, emit_h=False, n_dtype=F32)
    return y.reshape(B, S, D)
```

```python
import functools

import jax
import jax.numpy as jnp
import numpy as np
from jax import lax
from jax.experimental import pallas as pl
from jax.experimental.pallas import tpu as pltpu

N_META = 16
RET_HEADS = 4
RET_DK = 256
RET_DV = 512
RET_CHUNK = 128
ATT_HEADS = 8
ATT_DH = 128
IDX_HEADS = 8
IDX_DH = 64
TOPK_MAX = 256
ROPE_THETA = 10000.0
EPS = 1e-6
RET_QK = RET_HEADS * RET_DK
RET_V = RET_HEADS * RET_DV
ATT_W = ATT_HEADS * ATT_DH
IDX_Q = IDX_HEADS * IDX_DH
ATT_EXP_SCALE = (ATT_DH ** -0.5) * float(np.log2(np.e))

LANES = 128
VMEM_LIMIT = 48 * 1024 * 1024
FFN_VMEM_LIMIT = 56 * 1024 * 1024

BF16 = jnp.bfloat16
F32 = jnp.float32
NT_DIMS = (((1,), (1,)), ((), ()))


def _dot(a, b):
    return jnp.dot(a, b, preferred_element_type=F32)


def _dot_nt(a, b):
    return lax.dot_general(a, b, NT_DIMS, preferred_element_type=F32)


def _rms(x, g):
    return x * lax.rsqrt(jnp.mean(x * x, axis=-1, keepdims=True) + EPS) * g


def _sigmoid(x):
    return 0.5 * jnp.tanh(0.5 * x) + 0.5


def _params(sem):
    return pltpu.CompilerParams(dimension_semantics=sem, vmem_limit_bytes=VMEM_LIMIT)


def _ffn_kernel(x_ref, g_ref, wg_ref, wu_ref, wd_ref, g2_ref, *rest, nf, tf, emit_h):
    if emit_h:
        h_ref, n_ref, xn_sc, acc_sc, act_sc = rest
    else:
        n_ref, xn_sc, acc_sc, act_sc = rest

    def down(f):
        rows = pl.ds(pl.multiple_of(f * tf, tf), tf)
        return _dot(act_sc[...], wd_ref[rows, :])

    def up(f):
        cols = pl.ds(pl.multiple_of(f * tf, tf), tf)
        xn = xn_sc[...]
        gate = _dot(xn, wg_ref[:, cols])
        lift = _dot(xn, wu_ref[:, cols])
        act_sc[...] = ((gate * _sigmoid(gate)) * lift).astype(BF16)

    xn_sc[...] = _rms(x_ref[...], g_ref[...]).astype(BF16)
    acc_sc[...] = jnp.zeros_like(acc_sc)
    up(0)

    def slab(f, carry):
        acc_sc[...] += down(f - 1)
        up(f)
        return carry

    lax.fori_loop(1, nf, slab, 0)
    h = x_ref[...] + 0.5 * (acc_sc[...] + down(nf - 1))
    if emit_h:
        h_ref[...] = h
    n_ref[...] = _rms(h, g2_ref[...]).astype(n_ref.dtype)


def _ffn(x, g, wg, wu, wd, g2, *, tm, tf, emit_h, n_dtype):
    M, D = x.shape
    F = wg.shape[1]
    nf = F // tf
    assert nf >= 2
    out_shape = [jax.ShapeDtypeStruct((M, D), n_dtype)]
    out_specs = [pl.BlockSpec((tm, D), lambda i: (i, 0))]
    if emit_h:
        out_shape = [jax.ShapeDtypeStruct((M, D), F32)] + out_shape
        out_specs = [pl.BlockSpec((tm, D), lambda i: (i, 0))] + out_specs
    resident = lambda shape: pl.BlockSpec(shape, lambda i: (0, 0), pipeline_mode=pl.Buffered(1))
    return pl.pallas_call(
        functools.partial(_ffn_kernel, nf=nf, tf=tf, emit_h=emit_h),
        grid=(M // tm,),
        in_specs=[
            pl.BlockSpec((tm, D), lambda i: (i, 0)),
            pl.BlockSpec((1, D), lambda i: (0, 0)),
            resident((D, F)),
            resident((D, F)),
            resident((F, D)),
            pl.BlockSpec((1, D), lambda i: (0, 0)),
        ],
        out_specs=out_specs,
        out_shape=out_shape,
        scratch_shapes=[pltpu.VMEM((tm, D), BF16), pltpu.VMEM((tm, D), F32), pltpu.VMEM((tm, tf), BF16)],
        compiler_params=pltpu.CompilerParams(
            dimension_semantics=("arbitrary",), vmem_limit_bytes=FFN_VMEM_LIMIT),
        name="ffn",
    )(x, g, wg, wu, wd, g2)


def _rot_half(y, d):
    w = y.shape[-1]
    if d >= w:
        return jnp.concatenate([y[:, w // 2:], y[:, : w // 2]], axis=-1)
    if d == LANES:
        return pltpu.roll(y, d // 2, axis=1)
    fwd = pltpu.roll(y, d // 2, axis=1)
    bwd = pltpu.roll(y, w - d // 2, axis=1)
    lane = lax.broadcasted_iota(jnp.int32, y.shape, 1)
    return jnp.where((lane % d) < d // 2, bwd, fwd)


def _rope_tile(y, cos, sin, d):
    tw = cos.shape[-1]
    outs = []
    for t in range(y.shape[-1] // tw):
        yt = y[:, t * tw:(t + 1) * tw]
        outs.append(yt * cos + _rot_half(yt, d) * sin)
    return outs[0] if len(outs) == 1 else jnp.concatenate(outs, axis=-1)


def _proj_kernel(x_ref, w_ref, *rest, mode, d, q_scale, k_scale, n_q):
    if mode == "rope":
        cos_ref, sin_ref, o_ref, w_sc = rest
    else:
        o_ref, w_sc = rest

    @pl.when(pl.program_id(1) == 0)
    def _():
        w_sc[...] = w_ref[...].T.astype(BF16)

    y = _dot(x_ref[...], w_sc[...])
    if mode == "rope":
        y = _rope_tile(y, cos_ref[...], sin_ref[...], d)
        if q_scale != 1.0 or k_scale != 1.0:
            y = y * jnp.where(pl.program_id(0) >= n_q, k_scale, q_scale)
    elif mode == "silu":
        y = y * _sigmoid(y)
    elif mode == "sigmoid":
        y = _sigmoid(y)
    o_ref[...] = y.astype(o_ref.dtype)


def _proj(x, w_t, *, col0, n_tiles, tm, tn, mode, out_dtype=None, tabs=None, d=None, q_scale=1.0,
          k_scale=1.0, n_q=0):
    M, K = x.shape
    in_specs = [
        pl.BlockSpec((tm, K), lambda j, i: (i, 0)),
        pl.BlockSpec((tn, K), lambda j, i: (col0 + j, 0)),
    ]
    args = [x, w_t]
    if mode == "rope":
        cos, sin = tabs
        npos = cos.shape[0] // tm
        tw = cos.shape[1]
        in_specs += [pl.BlockSpec((tm, tw), lambda j, i: (i % npos, 0))] * 2
        args += [cos, sin]
    return pl.pallas_call(
        functools.partial(_proj_kernel, mode=mode, d=d, q_scale=q_scale, k_scale=k_scale, n_q=n_q),
        grid=(n_tiles, M // tm),
        in_specs=in_specs,
        out_specs=pl.BlockSpec((tm, tn), lambda j, i: (i, j)),
        out_shape=jax.ShapeDtypeStruct((M, n_tiles * tn), out_dtype or BF16),
        scratch_shapes=[pltpu.VMEM((K, tn), BF16)],
        compiler_params=_params(("arbitrary", "arbitrary")),
        name="proj_" + mode,
    )(*args)


def _idx_proj_kernel(x_ref, w_ref, cos_ref, sin_ref, q_ref, kw_ref):
    y = _dot(x_ref[...], w_ref[...])
    cos, sin = cos_ref[...], sin_ref[...]
    q_ref[...] = _rope_tile(y[:, :IDX_Q], cos, sin, IDX_DH)
    tail = y[:, IDX_Q:]
    lane = lax.broadcasted_iota(jnp.int32, tail.shape, 1)
    kw_ref[...] = jnp.where(lane < IDX_DH, _rope_tile(tail, cos, sin, IDX_DH), tail)


def _idx_proj(x, w, tabs, *, tm):
    M, K = x.shape
    N = w.shape[1]
    cos, sin = tabs
    npos = cos.shape[0] // tm
    return pl.pallas_call(
        _idx_proj_kernel,
        grid=(M // tm,),
        in_specs=[
            pl.BlockSpec((tm, K), lambda i: (i, 0)),
            pl.BlockSpec((K, N), lambda i: (0, 0)),
            pl.BlockSpec((tm, LANES), lambda i: (i % npos, 0)),
            pl.BlockSpec((tm, LANES), lambda i: (i % npos, 0)),
        ],
        out_specs=[
            pl.BlockSpec((tm, IDX_Q), lambda i: (i, 0)),
            pl.BlockSpec((tm, LANES), lambda i: (i, 0)),
        ],
        out_shape=[
            jax.ShapeDtypeStruct((M, IDX_Q), F32),
            jax.ShapeDtypeStruct((M, LANES), F32),
        ],
        compiler_params=_params(("parallel",)),
        name="proj_idx",
    )(x, w, cos, sin)


def _ret_kernel(lg_ref, gc_ref, q_ref, k_ref, v_ref, g_ref, km_ref, vm_ref, o_ref, st_ref, decay_ref, *, C):
    c = pl.program_id(1)
    dk = lambda h: slice(h * RET_DK, (h + 1) * RET_DK)
    dv = lambda h: slice(h * RET_DV, (h + 1) * RET_DV)

    @pl.when(c == 0)
    def _():
        pm = lax.broadcasted_iota(jnp.int32, (N_META, 1), 0).astype(F32)
        ri = lax.broadcasted_iota(jnp.int32, (C, C), 0)
        ci = lax.broadcasted_iota(jnp.int32, (C, C), 1)
        diff = (ri - ci).astype(F32)
        for h in range(RET_HEADS):
            lg = lg_ref[h]
            zeta_m = jnp.exp(lg * (N_META - 1.0 - pm))
            kz = km_ref[:, dk(h)].astype(F32) * zeta_m
            st_ref[h] = _dot(kz.T.astype(BF16), vm_ref[:, dv(h)])
            decay_ref[h] = jnp.where(diff >= 0, jnp.exp(lg * jnp.maximum(diff, 0.0)), 0.0)

    pos = lax.broadcasted_iota(jnp.int32, (C, 1), 0).astype(F32)
    for h in range(RET_HEADS):
        lg = lg_ref[h]
        q = q_ref[0, :, dk(h)]
        k = k_ref[0, :, dk(h)]
        v = v_ref[0, :, dv(h)]
        xi = jnp.exp(lg * (pos + 1.0))
        zeta = jnp.exp(lg * (C - 1.0 - pos))
        state = st_ref[h]
        inner = _dot((_dot_nt(q, k) * decay_ref[h]).astype(BF16), v)
        cross = _dot(q, state.astype(BF16)) * xi
        y = inner + cross
        y = y * lax.rsqrt(jnp.mean(y * y, axis=-1, keepdims=True) + EPS)
        o_ref[0, :, dv(h)] = (g_ref[0, :, dv(h)] * y).astype(o_ref.dtype)
        kz = k.astype(F32) * zeta
        st_ref[h] = gc_ref[h] * state + _dot(kz.T.astype(BF16), v)


def _retention(lg, gc, rqk, rv, rg, km, vm, *, C):
    B, S, _ = rv.shape
    smem = pl.BlockSpec(memory_space=pltpu.SMEM)
    return pl.pallas_call(
        functools.partial(_ret_kernel, C=C),
        grid=(B, S // C),
        in_specs=[
            smem,
            smem,
            pl.BlockSpec((1, C, RET_QK), lambda b, c: (b, c, 0)),
            pl.BlockSpec((1, C, RET_QK), lambda b, c: (b, c, 1)),
            pl.BlockSpec((1, C, RET_V), lambda b, c: (b, c, 0)),
            pl.BlockSpec((1, C, RET_V), lambda b, c: (b, c, 0)),
            pl.BlockSpec((N_META, RET_QK), lambda b, c: (0, 0)),
            pl.BlockSpec((N_META, RET_V), lambda b, c: (0, 0)),
        ],
        out_specs=pl.BlockSpec((1, C, RET_V), lambda b, c: (b, c, 0)),
        out_shape=jax.ShapeDtypeStruct((B, S, RET_V), BF16),
        scratch_shapes=[pltpu.VMEM((RET_HEADS, RET_DK, RET_DV), F32), pltpu.VMEM((RET_HEADS, C, C), F32)],
        compiler_params=_params(("parallel", "arbitrary")),
        name="retention",
    )(lg, gc, rqk, rqk, rv, rg, km, vm)


TINY = float(np.finfo(np.float32).tiny)
TN_DIMS = (((0,), (0,)), ((), ()))


def _dsa_kernel(iq_ref, ikw_ref, wq_ref, aq_ref, ak_ref, av_ref, akm_ref, avm_ref, o_ref,
                sc_ref, tau_ref, m_ref, l_ref, acc_ref, s_ref, bm_ref, *, nq, tq, tk, k_top):
    qi = pl.program_id(1)
    nkb = (qi + 1) * (tq // tk)
    kf = float(k_top)
    neg_inf = -jnp.inf
    qpos = qi * tq + lax.broadcasted_iota(jnp.int32, (1, tq), 1)

    def fold(x):
        return x.reshape(tk // 8, 8, tq).sum(axis=0)

    idx_scale = (IDX_DH ** -0.5) * (IDX_HEADS ** -0.5)
    iq = iq_ref[0].astype(BF16)
    w_t = wq_ref[0].T
    iq_h = [iq[:, h * IDX_DH:(h + 1) * IDX_DH] for h in range(IDX_HEADS)]
    w_h = [w_t[IDX_DH + h:IDX_DH + h + 1, :] for h in range(IDX_HEADS)]

    def score_rows(kb, n_blocks, carry):
        rmin, rmax = carry
        rows = n_blocks * tk
        ik = ikw_ref[0, pl.ds(pl.multiple_of(kb * tk, tk), rows), :][:, :IDX_DH].astype(BF16)
        acc = jnp.zeros((rows, tq), F32)
        for h in range(IDX_HEADS):
            acc = acc + w_h[h] * jnp.maximum(_dot_nt(ik, iq_h[h]), 0.0)
        ok = (lax.broadcasted_iota(jnp.int32, (rows, 1), 0) + kb * tk) <= qpos
        s = jnp.where(ok, acc * idx_scale, neg_inf)
        for i in range(n_blocks):
            sc_ref[kb + i] = s[i * tk:(i + 1) * tk]
        rmin = jnp.minimum(rmin, jnp.min(jnp.where(ok, s, jnp.inf), axis=0, keepdims=True))
        rmax = jnp.maximum(rmax, jnp.max(s, axis=0, keepdims=True))
        return rmin, rmax

    carry = (jnp.full((1, tq), jnp.inf, F32), jnp.full((1, tq), neg_inf, F32))
    carry = lax.fori_loop(0, nkb // 2, lambda j, c: score_rows(2 * j, 2, c), carry)
    rmin, rmax = lax.cond(nkb % 2 == 1, lambda c: score_rows(nkb - 1, 1, c), lambda c: c, carry)

    def count(pred):
        def body(kb, c):
            return c + fold(jnp.where(pred(sc_ref[kb]), 1.0, 0.0))
        return jnp.sum(lax.fori_loop(0, nkb, body, jnp.zeros((8, tq), F32)), axis=0, keepdims=True)

    def any_query(flag):
        return jnp.max(jnp.where(flag, 1.0, 0.0))

    c_max = count(lambda s: s >= rmax)
    top_tied = c_max >= kf
    lo0 = jnp.where(top_tied, rmax, rmin)
    cl0 = jnp.where(top_tied, c_max, (qpos + 1).astype(F32))

    def midpoint(lo, hi):
        a = jnp.maximum(jnp.abs(lo), TINY)
        b = jnp.maximum(jnp.abs(hi), TINY)
        geo = jnp.sqrt(a) * jnp.sqrt(b)
        one_sign = (lo >= 0.0) | (hi <= 0.0)
        far = jnp.maximum(a, b) > 4.0 * jnp.minimum(a, b)
        mid = jnp.where(one_sign & far, jnp.where(lo >= 0.0, geo, -geo), 0.5 * lo + 0.5 * hi)
        mid = jnp.where(lo == 0.0, TINY, mid)
        mid = jnp.where(hi == 0.0, -TINY, mid)
        return jnp.where((lo < 0.0) & (hi > 0.0), 0.0, mid)

    def active(lo, hi, cl):
        mid = midpoint(lo, hi)
        return (cl > kf) & (mid > lo) & (mid < hi)

    def go_on(st):
        return st[3] > 0.0

    def shrink_loop(n_blocks, lo, hi, cl):
        def count_ge(mid):
            c = jnp.zeros((8, tq), F32)
            for kb in range(n_blocks):
                c = c + fold(jnp.where(sc_ref[kb] >= mid, 1.0, 0.0))
            return jnp.sum(c, axis=0, keepdims=True)

        def shrink_once(lo, hi, cl):
            mid = midpoint(lo, hi)
            act = active(lo, hi, cl)
            c = count_ge(mid)
            up = act & (c >= kf)
            dn = act & (c < kf)
            return jnp.where(up, mid, lo), jnp.where(dn, mid, hi), jnp.where(up, c, cl)

        def shrink(st):
            pending = any_query(active(*st[:3]))
            return (*shrink_once(*shrink_once(*st[:3])), pending)

        return lax.while_loop(go_on, shrink, (lo, hi, cl, jnp.float32(1.0)))[:3]

    branches = [functools.partial(shrink_loop, (i + 1) * (tq // tk)) for i in range(nq)]
    lo, hi, cl = lax.switch(qi, branches, lo0, rmax, cl0)
    tau_ref[...] = lo
    tie = cl > kf

    @pl.when(any_query(tie) > 0.0)
    def _():
        def max_below(bound):
            def body(kb, v):
                s = sc_ref[kb]
                return jnp.maximum(v, jnp.max(jnp.where(s < bound, s, neg_inf), axis=0, keepdims=True))
            return lax.fori_loop(0, nkb, body, jnp.full((1, tq), neg_inf, F32))

        def count_ge_gt(v):
            def body(kb, c):
                s = sc_ref[kb]
                return c[0] + fold(jnp.where(s >= v, 1.0, 0.0)), c[1] + fold(jnp.where(s > v, 1.0, 0.0))
            zero = jnp.zeros((8, tq), F32)
            ge, gt = lax.fori_loop(0, nkb, body, (zero, zero))
            return jnp.sum(ge, axis=0, keepdims=True), jnp.sum(gt, axis=0, keepdims=True)

        hi_d = jnp.where(lo >= hi, jnp.inf, hi)
        v0 = max_below(hi_d)
        ge0, gt0 = count_ge_gt(v0)

        def walk_on(st):
            return st[4] > 0.0

        def walk(st):
            hi_w, v, ge, _, _ = st
            hi_w = jnp.where(tie & (ge < kf), v, hi_w)
            v = max_below(hi_w)
            ge, gt = count_ge_gt(v)
            return hi_w, v, ge, gt, any_query(tie & (ge < kf))

        _, v, _, gt, _ = lax.while_loop(walk_on, walk, (hi_d, v0, ge0, gt0, any_query(tie & (ge0 < kf))))
        keep = kf - gt
        ri = lax.broadcasted_iota(jnp.int32, (tk, tk), 0)
        ci = lax.broadcasted_iota(jnp.int32, (tk, tk), 1)
        earlier = jnp.where(ci < ri, 1.0, 0.0).astype(BF16)

        def drop(kb, seen):
            s = sc_ref[kb]
            eq = tie & (s == v)
            eqf = jnp.where(eq, 1.0, 0.0)
            rank = seen + _dot(earlier, eqf.astype(BF16))
            sc_ref[kb] = jnp.where(eq & (rank >= keep), neg_inf, s)
            return seen + jnp.sum(eqf, axis=0, keepdims=True)

        lax.fori_loop(0, nkb, drop, jnp.zeros((1, tq), F32))
        tau_ref[...] = jnp.where(tie, v, lo)

    tau = tau_ref[...]
    meta_bias = jnp.where(lax.broadcasted_iota(jnp.int32, (tk, 1), 0) < N_META, 0.0, neg_inf)

    def head(ref, h, rows=None):
        if rows is None:
            return ref[:, h * ATT_DH:(h + 1) * ATT_DH]
        return ref[0, rows, h * ATT_DH:(h + 1) * ATT_DH]

    def keys_of(kb):
        if kb is None:
            return (lambda h: head(akm_ref, h)), (lambda h: head(avm_ref, h)), meta_bias
        rows = pl.ds(pl.multiple_of(kb * tk, tk), tk)
        bias = jnp.where(sc_ref[kb] >= tau, 0.0, neg_inf)
        return (lambda h: head(ak_ref, h, rows)), (lambda h: head(av_ref, h, rows)), bias

    def score_phase(kb, buf):
        k_of, _, bias = keys_of(kb)
        for h in range(ATT_HEADS):
            s = _dot_nt(k_of(h), head(aq_ref, h, slice(None))) + bias
            s_ref[buf, h] = s
            bm_ref[buf, h:h + 1, :] = jnp.max(s, axis=0, keepdims=True)

    def value_phase(kb, buf):
        _, v_of, _ = keys_of(kb)
        for h in range(ATT_HEADS):
            m_new = jnp.maximum(m_ref[h:h + 1, :], bm_ref[buf, h:h + 1, :])
            p = jnp.exp2(s_ref[buf, h] - m_new)
            p_sum = jnp.sum(p, axis=0, keepdims=True)
            pv = lax.dot_general(v_of(h), p.astype(BF16), TN_DIMS, preferred_element_type=F32)
            alpha = jnp.exp2(m_ref[h:h + 1, :] - m_new)
            l_ref[h:h + 1, :] = alpha * l_ref[h:h + 1, :] + p_sum
            acc_ref[h] = alpha * acc_ref[h] + pv
            m_ref[h:h + 1, :] = m_new

    m_ref[...] = jnp.full_like(m_ref, neg_inf)
    l_ref[...] = jnp.zeros_like(l_ref)
    acc_ref[...] = jnp.zeros_like(acc_ref)
    score_phase(None, 0)

    @pl.when(nkb > 0)
    def _():
        value_phase(None, 0)
        score_phase(0, 1)

    def att_pair(j, carry):
        kb = 2 * j
        value_phase(kb, 1)
        score_phase(kb + 1, 0)

        @pl.when(kb + 1 < nkb)
        def _():
            value_phase(kb + 1, 0)
            score_phase(jnp.minimum(kb + 2, nkb - 1), 1)
        return carry

    lax.fori_loop(0, nkb // 2, att_pair, 0)

    @pl.when(nkb % 2 == 1)
    def _():
        value_phase(nkb - 1, 1)

    for h in range(ATT_HEADS):
        out_t = acc_ref[h] / l_ref[h:h + 1, :]
        o_ref[0, :, h * ATT_DH:(h + 1) * ATT_DH] = out_t.T.astype(o_ref.dtype)


def _dsa(iq, ikw, aqk, av, akm, avm, *, tq, tk, k_top):
    B, S, _ = av.shape
    nq = S // tq
    assert tq % tk == 0
    return pl.pallas_call(
        functools.partial(_dsa_kernel, nq=nq, tq=tq, tk=tk, k_top=k_top),
        grid=(B, nq),
        in_specs=[
            pl.BlockSpec((1, tq, IDX_Q), lambda b, q: (b, q, 0)),
            pl.BlockSpec((1, S, LANES), lambda b, q: (b, 0, 0)),
            pl.BlockSpec((1, tq, LANES), lambda b, q: (b, q, 0)),
            pl.BlockSpec((1, tq, ATT_W), lambda b, q: (b, q, 0)),
            pl.BlockSpec((1, S, ATT_W), lambda b, q: (b, 0, 1)),
            pl.BlockSpec((1, S, ATT_W), lambda b, q: (b, 0, 0)),
            pl.BlockSpec((tk, ATT_W), lambda b, q: (0, 0)),
            pl.BlockSpec((tk, ATT_W), lambda b, q: (0, 0)),
        ],
        out_specs=pl.BlockSpec((1, tq, ATT_W), lambda b, q: (b, q, 0)),
        out_shape=jax.ShapeDtypeStruct((B, S, ATT_W), BF16),
        scratch_shapes=[
            pltpu.VMEM((S // tk, tk, tq), F32),
            pltpu.VMEM((1, tq), F32),
            pltpu.VMEM((ATT_HEADS, tq), F32),
            pltpu.VMEM((ATT_HEADS, tq), F32),
            pltpu.VMEM((ATT_HEADS, ATT_DH, tq), F32),
            pltpu.VMEM((2, ATT_HEADS, tk, tq), F32),
            pltpu.VMEM((2, ATT_HEADS, tq), F32),
        ],
        compiler_params=_params(("parallel", "arbitrary")),
        name="dsa",
    )(iq, ikw, ikw, aqk, aqk, av, akm, avm)


def _merge_kernel(h_ref, yr_ref, ya_ref, gr_ref, ga_ref, wr_ref, wa_ref, wm_ref, o_ref):
    yr = _dot(yr_ref[...], wr_ref[...])
    ya = _dot(ya_ref[...], wa_ref[...])
    merged = gr_ref[...] * yr + ga_ref[...] * ya
    o_ref[...] = h_ref[...] + _dot(merged.astype(BF16), wm_ref[...])


def _merge(h, yr, ya, gates, wr, wa, wm, *, tm):
    M, D = h.shape
    row = lambda i: (i, 0)
    fixed = lambda i: (0, 0)
    return pl.pallas_call(
        _merge_kernel,
        grid=(M // tm,),
        in_specs=[
            pl.BlockSpec((tm, D), row),
            pl.BlockSpec((tm, RET_V), row),
            pl.BlockSpec((tm, ATT_W), row),
            pl.BlockSpec((tm, D), lambda i: (i, 0)),
            pl.BlockSpec((tm, D), lambda i: (i, 1)),
            pl.BlockSpec((RET_V, D), fixed),
            pl.BlockSpec((ATT_W, D), fixed),
            pl.BlockSpec((D, D), fixed),
        ],
        out_specs=pl.BlockSpec((tm, D), row),
        out_shape=jax.ShapeDtypeStruct((M, D), F32),
        compiler_params=_params(("parallel",)),
        name="merge",
    )(h, yr, ya, gates, gates, wr, wa, wm)


def _rope_tables(pos, d, width):
    inv_freq = ROPE_THETA ** (-jnp.arange(0, d, 2, dtype=F32) / d)
    ang = pos.astype(F32)[:, None] * inv_freq[None, :]
    cos, sin = jnp.cos(ang), jnp.sin(ang)
    cos = jnp.concatenate([cos, cos], axis=-1)
    sin = jnp.concatenate([-sin, sin], axis=-1)
    rep = width // d
    return jnp.tile(cos, (1, rep)), jnp.tile(sin, (1, rep))


def _tile(n, cap):
    t = min(n, cap)
    assert n % t == 0, (n, t)
    return t


def kernel(x, meta_tokens, ffn1_norm, ffn1_w_gate, ffn1_w_up, ffn1_w_down, mix_norm, w_in, w_ret_out,
           w_att_out, w_mix_out, ffn2_norm, ffn2_w_gate, ffn2_w_up, ffn2_w_down, final_norm):
    B, S, D = x.shape
    assert ffn1_norm.shape[0] == 1, "single layer only"
    assert meta_tokens.shape[0] == N_META
    k_top = min(TOPK_MAX, (S + 0) // 4)
    M = B * S
    bf = lambda a: a.astype(BF16)
    row = lambda a: a.reshape(1, -1)

    w32 = w_in[0].T
    offs = np.cumsum([0, RET_QK, RET_QK, RET_V, RET_V, ATT_W, ATT_W, ATT_W, IDX_Q, IDX_DH, IDX_HEADS, D, D])
    tn = 1024
    assert all(int(o) % tn == 0 for o in offs[:8])
    c_rq, c_rk, c_rv, c_rg, c_aq, c_ak, c_av = (int(o) // tn for o in offs[:7])
    w_idx = jnp.pad(bf(w32[offs[7]:offs[10]].T), ((0, 0), (0, LANES - IDX_DH - IDX_HEADS)))
    w_gate = w32[offs[10]:offs[12]]

    pos_meta = jnp.arange(N_META, dtype=jnp.int32)
    pos_real = N_META + jnp.arange(S, dtype=jnp.int32)
    ffn1 = (row(ffn1_norm[0]), bf(ffn1_w_gate[0]), bf(ffn1_w_up[0]), bf(ffn1_w_down[0]), row(mix_norm[0]))
    ffn2 = (row(ffn2_norm[0]), bf(ffn2_w_gate[0]), bf(ffn2_w_up[0]), bf(ffn2_w_down[0]), row(final_norm))
    tf = 256
    k_scale = RET_DK ** -0.5

    _, un_m = _ffn(meta_tokens.astype(F32), *ffn1, tm=N_META, tf=tf, emit_h=True, n_dtype=BF16)
    pm = dict(tm=N_META, tn=tn)
    rk_m = _proj(un_m, w32, col0=c_rk, n_tiles=RET_QK // tn, mode="rope", d=RET_DK, k_scale=k_scale,
                 tabs=_rope_tables(pos_meta, RET_DK, RET_DK), **pm)
    rv_m = _proj(un_m, w32, col0=c_rv, n_tiles=RET_V // tn, mode="plain", **pm)
    ak_m = _proj(un_m, w32, col0=c_ak, n_tiles=ATT_W // tn, mode="rope", d=ATT_DH,
                 tabs=_rope_tables(pos_meta, ATT_DH, LANES), **pm)
    av_m = _proj(un_m, w32, col0=c_av, n_tiles=ATT_W // tn, mode="plain", **pm)

    tm = _tile(S, 1024)
    h1, un = _ffn(x.reshape(M, D), *ffn1, tm=tm, tf=tf, emit_h=True, n_dtype=BF16)
    pr = dict(tm=tm, tn=tn)
    rqk = _proj(un, w32, col0=c_rq, n_tiles=2 * RET_QK // tn, mode="rope", d=RET_DK, k_scale=k_scale,
                n_q=RET_QK // tn, tabs=_rope_tables(pos_real, RET_DK, RET_DK), **pr)
    rv = _proj(un, w32, col0=c_rv, n_tiles=RET_V // tn, mode="plain", **pr)
    rg = _proj(un, w32, col0=c_rg, n_tiles=RET_V // tn, mode="silu", **pr)
    aqk = _proj(un, w32, col0=c_aq, n_tiles=2 * ATT_W // tn, mode="rope", d=ATT_DH, q_scale=ATT_EXP_SCALE,
                n_q=ATT_W // tn, tabs=_rope_tables(pos_real, ATT_DH, LANES), **pr)
    av = _proj(un, w32, col0=c_av, n_tiles=ATT_W // tn, mode="plain", **pr)
    gates = _proj(un, w_gate, col0=0, n_tiles=2 * D // tn, mode="sigmoid", **pr)
    iq, ikw = _idx_proj(un, w_idx, _rope_tables(pos_real, IDX_DH, LANES), tm=tm)

    heads = jnp.arange(RET_HEADS, dtype=F32)
    log_g = jnp.log1p(-(2.0 ** (-5.0 - heads)))
    C = _tile(S, 4 * RET_CHUNK)
    b3 = lambda a: a.reshape(B, S, a.shape[-1])
    yr = _retention(log_g, jnp.exp(log_g * C), b3(rqk), b3(rv), b3(rg), rk_m, rv_m, C=C)
    tk = _tile(S, 256)
    pad_meta = lambda a: jnp.pad(a, ((0, tk - N_META), (0, 0)))
    ya = _dsa(b3(iq), b3(ikw), b3(aqk), b3(av), pad_meta(ak_m), pad_meta(av_m),
              tq=_tile(S, 256), tk=tk, k_top=k_top)

    h2 = _merge(h1, yr.reshape(M, RET_V), ya.reshape(M, ATT_W), gates, bf(w_ret_out[0]), bf(w_att_out[0]),
                bf(w_mix_out[0]), tm=_tile(M, 512))
    (y,) = _ffn(h2, *ffn2, tm=tm, tf=tf, emit_h=False, n_dtype=F32)
    return y.reshape(B, S, D)
```

```python
import functools

import jax
import jax.numpy as jnp
import numpy as np
from jax import lax
from jax.experimental import pallas as pl
from jax.experimental.pallas import tpu as pltpu

N_META = 16
RET_HEADS = 4
RET_DK = 256
RET_DV = 512
RET_CHUNK = 128
ATT_HEADS = 8
ATT_DH = 128
IDX_HEADS = 8
IDX_DH = 64
TOPK_MAX = 256
ROPE_THETA = 10000.0
EPS = 1e-6
RET_QK = RET_HEADS * RET_DK
RET_V = RET_HEADS * RET_DV
ATT_W = ATT_HEADS * ATT_DH
IDX_Q = IDX_HEADS * IDX_DH
ATT_EXP_SCALE = (ATT_DH ** -0.5) * float(np.log2(np.e))

LANES = 128
VMEM_LIMIT = 48 * 1024 * 1024
FFN_VMEM_LIMIT = 56 * 1024 * 1024

BF16 = jnp.bfloat16
F32 = jnp.float32
NT_DIMS = (((1,), (1,)), ((), ()))


def _dot(a, b):
    return jnp.dot(a, b, preferred_element_type=F32)


def _dot_nt(a, b):
    return lax.dot_general(a, b, NT_DIMS, preferred_element_type=F32)


def _rms(x, g):
    return x * lax.rsqrt(jnp.mean(x * x, axis=-1, keepdims=True) + EPS) * g


def _sigmoid(x):
    return 0.5 * jnp.tanh(0.5 * x) + 0.5


def _params(sem):
    return pltpu.CompilerParams(dimension_semantics=sem, vmem_limit_bytes=VMEM_LIMIT)


def _ffn_kernel(x_ref, g_ref, wg_ref, wu_ref, wd_ref, g2_ref, *rest, nf, tf, emit_h):
    if emit_h:
        h_ref, n_ref, xn_sc, acc_sc, act_sc = rest
    else:
        n_ref, xn_sc, acc_sc, act_sc = rest

    def down(f):
        rows = pl.ds(pl.multiple_of(f * tf, tf), tf)
        return _dot(act_sc[...], wd_ref[rows, :])

    def up(f):
        cols = pl.ds(pl.multiple_of(f * tf, tf), tf)
        xn = xn_sc[...]
        gate = _dot(xn, wg_ref[:, cols])
        lift = _dot(xn, wu_ref[:, cols])
        act_sc[...] = ((gate * _sigmoid(gate)) * lift).astype(BF16)

    xn_sc[...] = _rms(x_ref[...], g_ref[...]).astype(BF16)
    acc_sc[...] = jnp.zeros_like(acc_sc)
    up(0)

    def slab(f, carry):
        acc_sc[...] += down(f - 1)
        up(f)
        return carry

    lax.fori_loop(1, nf, slab, 0)
    h = x_ref[...] + 0.5 * (acc_sc[...] + down(nf - 1))
    if emit_h:
        h_ref[...] = h
    n_ref[...] = _rms(h, g2_ref[...]).astype(n_ref.dtype)


def _ffn(x, g, wg, wu, wd, g2, *, tm, tf, emit_h, n_dtype):
    M, D = x.shape
    F = wg.shape[1]
    nf = F // tf
    assert nf >= 2
    out_shape = [jax.ShapeDtypeStruct((M, D), n_dtype)]
    out_specs = [pl.BlockSpec((tm, D), lambda i: (i, 0))]
    if emit_h:
        out_shape = [jax.ShapeDtypeStruct((M, D), F32)] + out_shape
        out_specs = [pl.BlockSpec((tm, D), lambda i: (i, 0))] + out_specs
    resident = lambda shape: pl.BlockSpec(shape, lambda i: (0, 0), pipeline_mode=pl.Buffered(1))
    return pl.pallas_call(
        functools.partial(_ffn_kernel, nf=nf, tf=tf, emit_h=emit_h),
        grid=(M // tm,),
        in_specs=[
            pl.BlockSpec((tm, D), lambda i: (i, 0)),
            pl.BlockSpec((1, D), lambda i: (0, 0)),
            resident((D, F)),
            resident((D, F)),
            resident((F, D)),
            pl.BlockSpec((1, D), lambda i: (0, 0)),
        ],
        out_specs=out_specs,
        out_shape=out_shape,
        scratch_shapes=[pltpu.VMEM((tm, D), BF16), pltpu.VMEM((tm, D), F32), pltpu.VMEM((tm, tf), BF16)],
        compiler_params=pltpu.CompilerParams(
            dimension_semantics=("arbitrary",), vmem_limit_bytes=FFN_VMEM_LIMIT),
        name="ffn",
    )(x, g, wg, wu, wd, g2)


def _rot_half(y, d):
    w = y.shape[-1]
    if d >= w:
        return jnp.concatenate([y[:, w // 2:], y[:, : w // 2]], axis=-1)
    if d == LANES:
        return pltpu.roll(y, d // 2, axis=1)
    fwd = pltpu.roll(y, d // 2, axis=1)
    bwd = pltpu.roll(y, w - d // 2, axis=1)
    lane = lax.broadcasted_iota(jnp.int32, y.shape, 1)
    return jnp.where((lane % d) < d // 2, bwd, fwd)


def _rope_tile(y, cos, sin, d):
    tw = cos.shape[-1]
    outs = []
    for t in range(y.shape[-1] // tw):
        yt = y[:, t * tw:(t + 1) * tw]
        outs.append(yt * cos + _rot_half(yt, d) * sin)
    return outs[0] if len(outs) == 1 else jnp.concatenate(outs, axis=-1)


def _proj_kernel(x_ref, w_ref, *rest, mode, d, q_scale, k_scale, n_q):
    if mode == "rope":
        cos_ref, sin_ref, o_ref, w_sc = rest
    else:
        o_ref, w_sc = rest

    @pl.when(pl.program_id(1) == 0)
    def _():
        w_sc[...] = w_ref[...].T.astype(BF16)

    y = _dot(x_ref[...], w_sc[...])
    if mode == "rope":
        y = _rope_tile(y, cos_ref[...], sin_ref[...], d)
        if q_scale != 1.0 or k_scale != 1.0:
            y = y * jnp.where(pl.program_id(0) >= n_q, k_scale, q_scale)
    elif mode == "silu":
        y = y * _sigmoid(y)
    elif mode == "sigmoid":
        y = _sigmoid(y)
    o_ref[...] = y.astype(o_ref.dtype)


def _proj(x, w_t, *, col0, n_tiles, tm, tn, mode, out_dtype=None, tabs=None, d=None, q_scale=1.0,
          k_scale=1.0, n_q=0):
    M, K = x.shape
    in_specs = [
        pl.BlockSpec((tm, K), lambda j, i: (i, 0)),
        pl.BlockSpec((tn, K), lambda j, i: (col0 + j, 0)),
    ]
    args = [x, w_t]
    if mode == "rope":
        cos, sin = tabs
        npos = cos.shape[0] // tm
        tw = cos.shape[1]
        in_specs += [pl.BlockSpec((tm, tw), lambda j, i: (i % npos, 0))] * 2
        args += [cos, sin]
    return pl.pallas_call(
        functools.partial(_proj_kernel, mode=mode, d=d, q_scale=q_scale, k_scale=k_scale, n_q=n_q),
        grid=(n_tiles, M // tm),
        in_specs=in_specs,
        out_specs=pl.BlockSpec((tm, tn), lambda j, i: (i, j)),
        out_shape=jax.ShapeDtypeStruct((M, n_tiles * tn), out_dtype or BF16),
        scratch_shapes=[pltpu.VMEM((K, tn), BF16)],
        compiler_params=_params(("arbitrary", "arbitrary")),
        name="proj_" + mode,
    )(*args)


def _idx_proj_kernel(x_ref, w_ref, cos_ref, sin_ref, q_ref, kw_ref):
    y = _dot(x_ref[...], w_ref[...])
    cos, sin = cos_ref[...], sin_ref[...]
    q_ref[...] = _rope_tile(y[:, :IDX_Q], cos, sin, IDX_DH)
    tail = y[:, IDX_Q:]
    lane = lax.broadcasted_iota(jnp.int32, tail.shape, 1)
    kw_ref[...] = jnp.where(lane < IDX_DH, _rope_tile(tail, cos, sin, IDX_DH), tail)


def _idx_proj(x, w, tabs, *, tm):
    M, K = x.shape
    N = w.shape[1]
    cos, sin = tabs
    npos = cos.shape[0] // tm
    return pl.pallas_call(
        _idx_proj_kernel,
        grid=(M // tm,),
        in_specs=[
            pl.BlockSpec((tm, K), lambda i: (i, 0)),
            pl.BlockSpec((K, N), lambda i: (0, 0)),
            pl.BlockSpec((tm, LANES), lambda i: (i % npos, 0)),
            pl.BlockSpec((tm, LANES), lambda i: (i % npos, 0)),
        ],
        out_specs=[
            pl.BlockSpec((tm, IDX_Q), lambda i: (i, 0)),
            pl.BlockSpec((tm, LANES), lambda i: (i, 0)),
        ],
        out_shape=[
            jax.ShapeDtypeStruct((M, IDX_Q), F32),
            jax.ShapeDtypeStruct((M, LANES), F32),
        ],
        compiler_params=_params(("parallel",)),
        name="proj_idx",
    )(x, w, cos, sin)


def _ret_kernel(lg_ref, gc_ref, q_ref, k_ref, v_ref, g_ref, km_ref, vm_ref, o_ref, st_ref, decay_ref, *, C):
    c = pl.program_id(1)
    dk = lambda h: slice(h * RET_DK, (h + 1) * RET_DK)
    dv = lambda h: slice(h * RET_DV, (h + 1) * RET_DV)

    @pl.when(c == 0)
    def _():
        pm = lax.broadcasted_iota(jnp.int32, (N_META, 1), 0).astype(F32)
        ri = lax.broadcasted_iota(jnp.int32, (C, C), 0)
        ci = lax.broadcasted_iota(jnp.int32, (C, C), 1)
        diff = (ri - ci).astype(F32)
        for h in range(RET_HEADS):
            lg = lg_ref[h]
            zeta_m = jnp.exp(lg * (N_META - 1.0 - pm))
            kz = km_ref[:, dk(h)].astype(F32) * zeta_m
            st_ref[h] = _dot(kz.T.astype(BF16), vm_ref[:, dv(h)])
            decay_ref[h] = jnp.where(diff >= 0, jnp.exp(lg * jnp.maximum(diff, 0.0)), 0.0)

    pos = lax.broadcasted_iota(jnp.int32, (C, 1), 0).astype(F32)
    for h in range(RET_HEADS):
        lg = lg_ref[h]
        q = q_ref[0, :, dk(h)]
        k = k_ref[0, :, dk(h)]
        v = v_ref[0, :, dv(h)]
        xi = jnp.exp(lg * (pos + 1.0))
        zeta = jnp.exp(lg * (C - 1.0 - pos))
        state = st_ref[h]
        inner = _dot((_dot_nt(q, k) * decay_ref[h]).astype(BF16), v)
        cross = _dot(q, state.astype(BF16)) * xi
        y = inner + cross
        y = y * lax.rsqrt(jnp.mean(y * y, axis=-1, keepdims=True) + EPS)
        o_ref[0, :, dv(h)] = (g_ref[0, :, dv(h)] * y).astype(o_ref.dtype)
        kz = k.astype(F32) * zeta
        st_ref[h] = gc_ref[h] * state + _dot(kz.T.astype(BF16), v)


def _retention(lg, gc, rqk, rv, rg, km, vm, *, C):
    B, S, _ = rv.shape
    smem = pl.BlockSpec(memory_space=pltpu.SMEM)
    return pl.pallas_call(
        functools.partial(_ret_kernel, C=C),
        grid=(B, S // C),
        in_specs=[
            smem,
            smem,
            pl.BlockSpec((1, C, RET_QK), lambda b, c: (b, c, 0)),
            pl.BlockSpec((1, C, RET_QK), lambda b, c: (b, c, 1)),
            pl.BlockSpec((1, C, RET_V), lambda b, c: (b, c, 0)),
            pl.BlockSpec((1, C, RET_V), lambda b, c: (b, c, 0)),
            pl.BlockSpec((N_META, RET_QK), lambda b, c: (0, 0)),
            pl.BlockSpec((N_META, RET_V), lambda b, c: (0, 0)),
        ],
        out_specs=pl.BlockSpec((1, C, RET_V), lambda b, c: (b, c, 0)),
        out_shape=jax.ShapeDtypeStruct((B, S, RET_V), BF16),
        scratch_shapes=[pltpu.VMEM((RET_HEADS, RET_DK, RET_DV), F32), pltpu.VMEM((RET_HEADS, C, C), F32)],
        compiler_params=_params(("parallel", "arbitrary")),
        name="retention",
    )(lg, gc, rqk, rqk, rv, rg, km, vm)


TINY = float(np.finfo(np.float32).tiny)
TN_DIMS = (((0,), (0,)), ((), ()))


def _dsa_kernel(iq_ref, ikw_ref, wq_ref, aq_ref, ak_ref, av_ref, akm_ref, avm_ref, o_ref,
                sc_ref, tau_ref, m_ref, l_ref, acc_ref, s_ref, bm_ref, *, nq, tq, tk, k_top):
    qi = pl.program_id(1)
    nkb = (qi + 1) * (tq // tk)
    kf = float(k_top)
    neg_inf = -jnp.inf
    qpos = qi * tq + lax.broadcasted_iota(jnp.int32, (1, tq), 1)

    def fold(x):
        return x.reshape(tk // 8, 8, tq).sum(axis=0)

    idx_scale = (IDX_DH ** -0.5) * (IDX_HEADS ** -0.5)
    iq = iq_ref[0].astype(BF16)
    w_t = wq_ref[0].T
    iq_h = [iq[:, h * IDX_DH:(h + 1) * IDX_DH] for h in range(IDX_HEADS)]
    w_h = [w_t[IDX_DH + h:IDX_DH + h + 1, :] for h in range(IDX_HEADS)]

    def score_rows(kb, n_blocks, carry):
        rmin, rmax = carry
        rows = n_blocks * tk
        ik = ikw_ref[0, pl.ds(pl.multiple_of(kb * tk, tk), rows), :][:, :IDX_DH].astype(BF16)
        acc = jnp.zeros((rows, tq), F32)
        for h in range(IDX_HEADS):
            acc = acc + w_h[h] * jnp.maximum(_dot_nt(ik, iq_h[h]), 0.0)
        ok = (lax.broadcasted_iota(jnp.int32, (rows, 1), 0) + kb * tk) <= qpos
        s = jnp.where(ok, acc * idx_scale, neg_inf)
        for i in range(n_blocks):
            sc_ref[kb + i] = s[i * tk:(i + 1) * tk]
        rmin = jnp.minimum(rmin, jnp.min(jnp.where(ok, s, jnp.inf), axis=0, keepdims=True))
        rmax = jnp.maximum(rmax, jnp.max(s, axis=0, keepdims=True))
        return rmin, rmax

    carry = (jnp.full((1, tq), jnp.inf, F32), jnp.full((1, tq), neg_inf, F32))
    carry = lax.fori_loop(0, nkb // 2, lambda j, c: score_rows(2 * j, 2, c), carry)
    rmin, rmax = lax.cond(nkb % 2 == 1, lambda c: score_rows(nkb - 1, 1, c), lambda c: c, carry)

    def count(pred):
        def body(kb, c):
            return c + fold(jnp.where(pred(sc_ref[kb]), 1.0, 0.0))
        return jnp.sum(lax.fori_loop(0, nkb, body, jnp.zeros((8, tq), F32)), axis=0, keepdims=True)

    def any_query(flag):
        return jnp.max(jnp.where(flag, 1.0, 0.0))

    c_max = count(lambda s: s >= rmax)
    top_tied = c_max >= kf
    lo0 = jnp.where(top_tied, rmax, rmin)
    cl0 = jnp.where(top_tied, c_max, (qpos + 1).astype(F32))

    def midpoint(lo, hi):
        a = jnp.maximum(jnp.abs(lo), TINY)
        b = jnp.maximum(jnp.abs(hi), TINY)
        geo = jnp.sqrt(a) * jnp.sqrt(b)
        one_sign = (lo >= 0.0) | (hi <= 0.0)
        far = jnp.maximum(a, b) > 4.0 * jnp.minimum(a, b)
        mid = jnp.where(one_sign & far, jnp.where(lo >= 0.0, geo, -geo), 0.5 * lo + 0.5 * hi)
        mid = jnp.where(lo == 0.0, TINY, mid)
        mid = jnp.where(hi == 0.0, -TINY, mid)
        return jnp.where((lo < 0.0) & (hi > 0.0), 0.0, mid)

    def active(lo, hi, cl):
        mid = midpoint(lo, hi)
        return (cl > kf) & (mid > lo) & (mid < hi)

    def go_on(st):
        return st[3] > 0.0

    def threshold(n_blocks, lo, hi, cl):
        blocks = range(n_blocks)

        def over_keys(fn, init, combine):
            acc = init
            for kb in blocks:
                acc = combine(acc, fn(sc_ref[kb]))
            return acc

        def count(pred):
            c = over_keys(lambda s: fold(jnp.where(pred(s), 1.0, 0.0)), jnp.zeros((8, tq), F32), jnp.add)
            return jnp.sum(c, axis=0, keepdims=True)

        def shrink_once(lo, hi, cl):
            mid = midpoint(lo, hi)
            act = active(lo, hi, cl)
            c = count(lambda s: s >= mid)
            up = act & (c >= kf)
            dn = act & (c < kf)
            return jnp.where(up, mid, lo), jnp.where(dn, mid, hi), jnp.where(up, c, cl)

        def shrink(st):
            pending = any_query(active(*st[:3]))
            return (*shrink_once(*shrink_once(*st[:3])), pending)

        lo, hi, cl, _ = lax.while_loop(go_on, shrink, (lo, hi, cl, jnp.float32(1.0)))
        tau_ref[...] = lo
        tie = cl > kf

        @pl.when(any_query(tie) > 0.0)
        def _():
            def max_below(bound):
                part = over_keys(
                    lambda s: jnp.max(jnp.where(s < bound, s, neg_inf).reshape(tk // 8, 8, tq), axis=0),
                    jnp.full((8, tq), neg_inf, F32), jnp.maximum)
                return jnp.max(part, axis=0, keepdims=True)

            hi_d = jnp.where(lo >= hi, jnp.inf, hi)
            v0 = max_below(hi_d)
            ge0 = count(lambda s: s >= v0)

            def walk_on(st):
                return st[3] > 0.0

            def walk(st):
                hi_w, v, ge, _ = st
                hi_w = jnp.where(tie & (ge < kf), v, hi_w)
                v = max_below(hi_w)
                ge = count(lambda s: s >= v)
                return hi_w, v, ge, any_query(tie & (ge < kf))

            _, v, _, _ = lax.while_loop(walk_on, walk, (hi_d, v0, ge0, any_query(tie & (ge0 < kf))))
            keep = kf - count(lambda s: s > v)
            ri = lax.broadcasted_iota(jnp.int32, (tk, tk), 0)
            ci = lax.broadcasted_iota(jnp.int32, (tk, tk), 1)
            earlier = jnp.where(ci < ri, 1.0, 0.0).astype(BF16)
            seen = jnp.zeros((1, tq), F32)
            for kb in blocks:
                s = sc_ref[kb]
                eq = tie & (s == v)
                eqf = jnp.where(eq, 1.0, 0.0)
                rank = seen + _dot(earlier, eqf.astype(BF16))
                sc_ref[kb] = jnp.where(eq & (rank >= keep), neg_inf, s)
                seen = seen + jnp.sum(eqf, axis=0, keepdims=True)
            tau_ref[...] = jnp.where(tie, v, lo)

        return jnp.int32(0)

    branches = [functools.partial(threshold, (i + 1) * (tq // tk)) for i in range(nq)]
    lax.switch(qi, branches, lo0, rmax, cl0)

    tau = tau_ref[...]
    meta_bias = jnp.where(lax.broadcasted_iota(jnp.int32, (tk, 1), 0) < N_META, 0.0, neg_inf)

    def head(ref, h, rows=None):
        if rows is None:
            return ref[:, h * ATT_DH:(h + 1) * ATT_DH]
        return ref[0, rows, h * ATT_DH:(h + 1) * ATT_DH]

    def keys_of(kb):
        if kb is None:
            return (lambda h: head(akm_ref, h)), (lambda h: head(avm_ref, h)), meta_bias
        rows = pl.ds(pl.multiple_of(kb * tk, tk), tk)
        bias = jnp.where(sc_ref[kb] >= tau, 0.0, neg_inf)
        return (lambda h: head(ak_ref, h, rows)), (lambda h: head(av_ref, h, rows)), bias

    def score_phase(kb, buf):
        k_of, _, bias = keys_of(kb)
        for h in range(ATT_HEADS):
            s = _dot_nt(k_of(h), head(aq_ref, h, slice(None))) + bias
            s_ref[buf, h] = s
            bm_ref[buf, h:h + 1, :] = jnp.max(s, axis=0, keepdims=True)

    def value_phase(kb, buf):
        _, v_of, _ = keys_of(kb)
        for h in range(ATT_HEADS):
            m_new = jnp.maximum(m_ref[h:h + 1, :], bm_ref[buf, h:h + 1, :])
            p = jnp.exp2(s_ref[buf, h] - m_new)
            p_sum = jnp.sum(p, axis=0, keepdims=True)
            pv = lax.dot_general(v_of(h), p.astype(BF16), TN_DIMS, preferred_element_type=F32)
            alpha = jnp.exp2(m_ref[h:h + 1, :] - m_new)
            l_ref[h:h + 1, :] = alpha * l_ref[h:h + 1, :] + p_sum
            acc_ref[h] = alpha * acc_ref[h] + pv
            m_ref[h:h + 1, :] = m_new

    m_ref[...] = jnp.full_like(m_ref, neg_inf)
    l_ref[...] = jnp.zeros_like(l_ref)
    acc_ref[...] = jnp.zeros_like(acc_ref)
    score_phase(None, 0)

    @pl.when(nkb > 0)
    def _():
        value_phase(None, 0)
        score_phase(0, 1)

    def att_pair(j, carry):
        kb = 2 * j
        value_phase(kb, 1)
        score_phase(kb + 1, 0)

        @pl.when(kb + 1 < nkb)
        def _():
            value_phase(kb + 1, 0)
            score_phase(jnp.minimum(kb + 2, nkb - 1), 1)
        return carry

    lax.fori_loop(0, nkb // 2, att_pair, 0)

    @pl.when(nkb % 2 == 1)
    def _():
        value_phase(nkb - 1, 1)

    for h in range(ATT_HEADS):
        out_t = acc_ref[h] / l_ref[h:h + 1, :]
        o_ref[0, :, h * ATT_DH:(h + 1) * ATT_DH] = out_t.T.astype(o_ref.dtype)


def _dsa(iq, ikw, aqk, av, akm, avm, *, tq, tk, k_top):
    B, S, _ = av.shape
    nq = S // tq
    assert tq % tk == 0
    return pl.pallas_call(
        functools.partial(_dsa_kernel, nq=nq, tq=tq, tk=tk, k_top=k_top),
        grid=(B, nq),
        in_specs=[
            pl.BlockSpec((1, tq, IDX_Q), lambda b, q: (b, q, 0)),
            pl.BlockSpec((1, S, LANES), lambda b, q: (b, 0, 0)),
            pl.BlockSpec((1, tq, LANES), lambda b, q: (b, q, 0)),
            pl.BlockSpec((1, tq, ATT_W), lambda b, q: (b, q, 0)),
            pl.BlockSpec((1, S, ATT_W), lambda b, q: (b, 0, 1)),
            pl.BlockSpec((1, S, ATT_W), lambda b, q: (b, 0, 0)),
            pl.BlockSpec((tk, ATT_W), lambda b, q: (0, 0)),
            pl.BlockSpec((tk, ATT_W), lambda b, q: (0, 0)),
        ],
        out_specs=pl.BlockSpec((1, tq, ATT_W), lambda b, q: (b, q, 0)),
        out_shape=jax.ShapeDtypeStruct((B, S, ATT_W), BF16),
        scratch_shapes=[
            pltpu.VMEM((S // tk, tk, tq), F32),
            pltpu.VMEM((1, tq), F32),
            pltpu.VMEM((ATT_HEADS, tq), F32),
            pltpu.VMEM((ATT_HEADS, tq), F32),
            pltpu.VMEM((ATT_HEADS, ATT_DH, tq), F32),
            pltpu.VMEM((2, ATT_HEADS, tk, tq), F32),
            pltpu.VMEM((2, ATT_HEADS, tq), F32),
        ],
        compiler_params=_params(("parallel", "arbitrary")),
        name="dsa",
    )(iq, ikw, ikw, aqk, aqk, av, akm, avm)


def _merge_kernel(h_ref, yr_ref, ya_ref, gr_ref, ga_ref, wr_ref, wa_ref, wm_ref, o_ref):
    yr = _dot(yr_ref[...], wr_ref[...])
    ya = _dot(ya_ref[...], wa_ref[...])
    merged = gr_ref[...] * yr + ga_ref[...] * ya
    o_ref[...] = h_ref[...] + _dot(merged.astype(BF16), wm_ref[...])


def _merge(h, yr, ya, gates, wr, wa, wm, *, tm):
    M, D = h.shape
    row = lambda i: (i, 0)
    fixed = lambda i: (0, 0)
    return pl.pallas_call(
        _merge_kernel,
        grid=(M // tm,),
        in_specs=[
            pl.BlockSpec((tm, D), row),
            pl.BlockSpec((tm, RET_V), row),
            pl.BlockSpec((tm, ATT_W), row),
            pl.BlockSpec((tm, D), lambda i: (i, 0)),
            pl.BlockSpec((tm, D), lambda i: (i, 1)),
            pl.BlockSpec((RET_V, D), fixed),
            pl.BlockSpec((ATT_W, D), fixed),
            pl.BlockSpec((D, D), fixed),
        ],
        out_specs=pl.BlockSpec((tm, D), row),
        out_shape=jax.ShapeDtypeStruct((M, D), F32),
        compiler_params=_params(("parallel",)),
        name="merge",
    )(h, yr, ya, gates, gates, wr, wa, wm)


def _rope_tables(pos, d, width):
    inv_freq = ROPE_THETA ** (-jnp.arange(0, d, 2, dtype=F32) / d)
    ang = pos.astype(F32)[:, None] * inv_freq[None, :]
    cos, sin = jnp.cos(ang), jnp.sin(ang)
    cos = jnp.concatenate([cos, cos], axis=-1)
    sin = jnp.concatenate([-sin, sin], axis=-1)
    rep = width // d
    return jnp.tile(cos, (1, rep)), jnp.tile(sin, (1, rep))


def _tile(n, cap):
    t = min(n, cap)
    assert n % t == 0, (n, t)
    return t


def kernel(x, meta_tokens, ffn1_norm, ffn1_w_gate, ffn1_w_up, ffn1_w_down, mix_norm, w_in, w_ret_out,
           w_att_out, w_mix_out, ffn2_norm, ffn2_w_gate, ffn2_w_up, ffn2_w_down, final_norm):
    B, S, D = x.shape
    assert ffn1_norm.shape[0] == 1, "single layer only"
    assert meta_tokens.shape[0] == N_META
    k_top = min(TOPK_MAX, (S + 0) // 4)
    M = B * S
    bf = lambda a: a.astype(BF16)
    row = lambda a: a.reshape(1, -1)

    w32 = w_in[0].T
    offs = np.cumsum([0, RET_QK, RET_QK, RET_V, RET_V, ATT_W, ATT_W, ATT_W, IDX_Q, IDX_DH, IDX_HEADS, D, D])
    tn = 1024
    assert all(int(o) % tn == 0 for o in offs[:8])
    c_rq, c_rk, c_rv, c_rg, c_aq, c_ak, c_av = (int(o) // tn for o in offs[:7])
    w_idx = jnp.pad(bf(w32[offs[7]:offs[10]].T), ((0, 0), (0, LANES - IDX_DH - IDX_HEADS)))
    w_gate = w32[offs[10]:offs[12]]

    pos_meta = jnp.arange(N_META, dtype=jnp.int32)
    pos_real = N_META + jnp.arange(S, dtype=jnp.int32)
    ffn1 = (row(ffn1_norm[0]), bf(ffn1_w_gate[0]), bf(ffn1_w_up[0]), bf(ffn1_w_down[0]), row(mix_norm[0]))
    ffn2 = (row(ffn2_norm[0]), bf(ffn2_w_gate[0]), bf(ffn2_w_up[0]), bf(ffn2_w_down[0]), row(final_norm))
    tf = 256
    k_scale = RET_DK ** -0.5

    _, un_m = _ffn(meta_tokens.astype(F32), *ffn1, tm=N_META, tf=tf, emit_h=True, n_dtype=BF16)
    pm = dict(tm=N_META, tn=tn)
    rk_m = _proj(un_m, w32, col0=c_rk, n_tiles=RET_QK // tn, mode="rope", d=RET_DK, k_scale=k_scale,
                 tabs=_rope_tables(pos_meta, RET_DK, RET_DK), **pm)
    rv_m = _proj(un_m, w32, col0=c_rv, n_tiles=RET_V // tn, mode="plain", **pm)
    ak_m = _proj(un_m, w32, col0=c_ak, n_tiles=ATT_W // tn, mode="rope", d=ATT_DH,
                 tabs=_rope_tables(pos_meta, ATT_DH, LANES), **pm)
    av_m = _proj(un_m, w32, col0=c_av, n_tiles=ATT_W // tn, mode="plain", **pm)

    tm = _tile(S, 1024)
    h1, un = _ffn(x.reshape(M, D), *ffn1, tm=tm, tf=tf, emit_h=True, n_dtype=BF16)
    pr = dict(tm=tm, tn=tn)
    pw = dict(tm=_tile(M, 2 * tm), tn=tn)
    rqk = _proj(un, w32, col0=c_rq, n_tiles=2 * RET_QK // tn, mode="rope", d=RET_DK, k_scale=k_scale,
                n_q=RET_QK // tn, tabs=_rope_tables(pos_real, RET_DK, RET_DK), **pr)
    rv = _proj(un, w32, col0=c_rv, n_tiles=RET_V // tn, mode="plain", **pw)
    rg = _proj(un, w32, col0=c_rg, n_tiles=RET_V // tn, mode="silu", **pw)
    aqk = _proj(un, w32, col0=c_aq, n_tiles=2 * ATT_W // tn, mode="rope", d=ATT_DH, q_scale=ATT_EXP_SCALE,
                n_q=ATT_W // tn, tabs=_rope_tables(pos_real, ATT_DH, LANES), **pr)
    av = _proj(un, w32, col0=c_av, n_tiles=ATT_W // tn, mode="plain", **pw)
    gates = _proj(un, w_gate, col0=0, n_tiles=2 * D // tn, mode="sigmoid", **pw)
    iq, ikw = _idx_proj(un, w_idx, _rope_tables(pos_real, IDX_DH, LANES), tm=tm)

    heads = jnp.arange(RET_HEADS, dtype=F32)
    log_g = jnp.log1p(-(2.0 ** (-5.0 - heads)))
    C = _tile(S, 4 * RET_CHUNK)
    b3 = lambda a: a.reshape(B, S, a.shape[-1])
    yr = _retention(log_g, jnp.exp(log_g * C), b3(rqk), b3(rv), b3(rg), rk_m, rv_m, C=C)
    tk = _tile(S, 256)
    pad_meta = lambda a: jnp.pad(a, ((0, tk - N_META), (0, 0)))
    ya = _dsa(b3(iq), b3(ikw), b3(aqk), b3(av), pad_meta(ak_m), pad_meta(av_m),
              tq=_tile(S, 256), tk=tk, k_top=k_top)

    h2 = _merge(h1, yr.reshape(M, RET_V), ya.reshape(M, ATT_W), gates, bf(w_ret_out[0]), bf(w_att_out[0]),
                bf(w_mix_out[0]), tm=_tile(M, 512))
    (y,) = _ffn(h2, *ffn2, tm=tm, tf=tf, emit_h=False, n_dtype=F32)
    return y.reshape(B, S, D)
```

```python
import functools
from typing import NamedTuple

import jax
import jax.numpy as jnp
import numpy as np
from jax import lax
from jax.experimental import pallas as pl
from jax.experimental.pallas import tpu as pltpu

N_META = 16
RET_HEADS = 4
RET_DK = 256
RET_DV = 512
RET_CHUNK = 128
ATT_HEADS = 8
ATT_DH = 128
IDX_HEADS = 8
IDX_DH = 64
TOPK_MAX = 256
ROPE_THETA = 10000.0
EPS = 1e-6
RET_QK = RET_HEADS * RET_DK
RET_V = RET_HEADS * RET_DV
ATT_W = ATT_HEADS * ATT_DH
IDX_Q = IDX_HEADS * IDX_DH
ATT_EXP_SCALE = (ATT_DH ** -0.5) * float(np.log2(np.e))

LANES = 128
SUBLANES = 8
MXU_DIM = 256
VMEM_LIMIT = 48 * 1024 * 1024
FFN_VMEM_LIMIT = 56 * 1024 * 1024

BF16 = jnp.bfloat16
F32 = jnp.float32
NT_DIMS = (((1,), (1,)), ((), ()))


def _dot(a, b):
    return jnp.dot(a, b, preferred_element_type=F32)


def _dot_nt(a, b):
    return lax.dot_general(a, b, NT_DIMS, preferred_element_type=F32)


def _rms(x, g):
    return x * lax.rsqrt(jnp.mean(x * x, axis=-1, keepdims=True) + EPS) * g


def _sigmoid(x):
    return 0.5 * jnp.tanh(0.5 * x) + 0.5


def _params(sem):
    return pltpu.CompilerParams(dimension_semantics=sem, vmem_limit_bytes=VMEM_LIMIT)


def _ffn_kernel(x_ref, g_ref, wg_ref, wu_ref, wd_ref, g2_ref, *rest, nf, tf, emit_h):
    if emit_h:
        h_ref, n_ref, xn_sc, acc_sc, act_sc = rest
    else:
        n_ref, xn_sc, acc_sc, act_sc = rest

    def down(f):
        rows = pl.ds(pl.multiple_of(f * tf, tf), tf)
        return _dot(act_sc[...], wd_ref[rows, :])

    def up(f):
        cols = pl.ds(pl.multiple_of(f * tf, tf), tf)
        xn = xn_sc[...]
        gate = _dot(xn, wg_ref[:, cols])
        lift = _dot(xn, wu_ref[:, cols])
        act_sc[...] = ((gate * _sigmoid(gate)) * lift).astype(BF16)

    xn_sc[...] = _rms(x_ref[...], g_ref[...]).astype(BF16)
    acc_sc[...] = jnp.zeros_like(acc_sc)
    up(0)

    def slab(f, carry):
        acc_sc[...] += down(f - 1)
        up(f)
        return carry

    lax.fori_loop(1, nf, slab, 0)
    h = x_ref[...] + 0.5 * (acc_sc[...] + down(nf - 1))
    if emit_h:
        h_ref[...] = h
    n_ref[...] = _rms(h, g2_ref[...]).astype(n_ref.dtype)


def _ffn(x, g, wg, wu, wd, g2, *, tm, tf, emit_h, n_dtype):
    M, D = x.shape
    F = wg.shape[1]
    nf = F // tf
    assert nf >= 2
    out_shape = [jax.ShapeDtypeStruct((M, D), n_dtype)]
    out_specs = [pl.BlockSpec((tm, D), lambda i: (i, 0))]
    if emit_h:
        out_shape = [jax.ShapeDtypeStruct((M, D), F32)] + out_shape
        out_specs = [pl.BlockSpec((tm, D), lambda i: (i, 0))] + out_specs
    resident = lambda shape: pl.BlockSpec(shape, lambda i: (0, 0), pipeline_mode=pl.Buffered(1))
    return pl.pallas_call(
        functools.partial(_ffn_kernel, nf=nf, tf=tf, emit_h=emit_h),
        grid=(M // tm,),
        in_specs=[
            pl.BlockSpec((tm, D), lambda i: (i, 0)),
            pl.BlockSpec((1, D), lambda i: (0, 0)),
            resident((D, F)),
            resident((D, F)),
            resident((F, D)),
            pl.BlockSpec((1, D), lambda i: (0, 0)),
        ],
        out_specs=out_specs,
        out_shape=out_shape,
        scratch_shapes=[pltpu.VMEM((tm, D), BF16), pltpu.VMEM((tm, D), F32), pltpu.VMEM((tm, tf), BF16)],
        compiler_params=pltpu.CompilerParams(
            dimension_semantics=("arbitrary",), vmem_limit_bytes=FFN_VMEM_LIMIT),
        name="ffn",
    )(x, g, wg, wu, wd, g2)


def _rot_half(y, d):
    w = y.shape[-1]
    if d >= w:
        return jnp.concatenate([y[:, w // 2:], y[:, : w // 2]], axis=-1)
    if d == LANES:
        return pltpu.roll(y, d // 2, axis=1)
    fwd = pltpu.roll(y, d // 2, axis=1)
    bwd = pltpu.roll(y, w - d // 2, axis=1)
    lane = lax.broadcasted_iota(jnp.int32, y.shape, 1)
    return jnp.where((lane % d) < d // 2, bwd, fwd)


def _rope_tile(y, cos, sin, d):
    tw = cos.shape[-1]
    outs = []
    for t in range(y.shape[-1] // tw):
        yt = y[:, t * tw:(t + 1) * tw]
        outs.append(yt * cos + _rot_half(yt, d) * sin)
    return outs[0] if len(outs) == 1 else jnp.concatenate(outs, axis=-1)


def _proj_kernel(x_ref, w_ref, *rest, mode, d, q_scale, k_scale, n_q):
    if mode == "rope":
        cos_ref, sin_ref, o_ref, w_sc = rest
    else:
        o_ref, w_sc = rest

    @pl.when(pl.program_id(1) == 0)
    def _():
        w_sc[...] = w_ref[...].T.astype(BF16)

    y = _dot(x_ref[...], w_sc[...])
    if mode == "rope":
        y = _rope_tile(y, cos_ref[...], sin_ref[...], d)
        if q_scale != 1.0 or k_scale != 1.0:
            y = y * jnp.where(pl.program_id(0) >= n_q, k_scale, q_scale)
    elif mode == "silu":
        y = y * _sigmoid(y)
    o_ref[...] = y.astype(o_ref.dtype)


def _proj(x, w_t, *, col0, n_tiles, tm, tn, mode, out_dtype=None, tabs=None, d=None, q_scale=1.0,
          k_scale=1.0, n_q=0):
    M, K = x.shape
    in_specs = [
        pl.BlockSpec((tm, K), lambda j, i: (i, 0)),
        pl.BlockSpec((tn, K), lambda j, i: (col0 + j, 0)),
    ]
    args = [x, w_t]
    if mode == "rope":
        cos, sin = tabs
        npos = cos.shape[0] // tm
        tw = cos.shape[1]
        in_specs += [pl.BlockSpec((tm, tw), lambda j, i: (i % npos, 0))] * 2
        args += [cos, sin]
    return pl.pallas_call(
        functools.partial(_proj_kernel, mode=mode, d=d, q_scale=q_scale, k_scale=k_scale, n_q=n_q),
        grid=(n_tiles, M // tm),
        in_specs=in_specs,
        out_specs=pl.BlockSpec((tm, tn), lambda j, i: (i, j)),
        out_shape=jax.ShapeDtypeStruct((M, n_tiles * tn), out_dtype or BF16),
        scratch_shapes=[pltpu.VMEM((K, tn), BF16)],
        compiler_params=_params(("arbitrary", "arbitrary")),
        name="proj_" + mode,
    )(*args)


def _idx_proj_kernel(x_ref, w_ref, cos_ref, sin_ref, q_ref, kw_ref):
    y = _dot(x_ref[...], w_ref[...])
    cos, sin = cos_ref[...], sin_ref[...]
    q_ref[...] = _rope_tile(y[:, :IDX_Q], cos, sin, IDX_DH)
    tail = y[:, IDX_Q:]
    lane = lax.broadcasted_iota(jnp.int32, tail.shape, 1)
    kw_ref[...] = jnp.where(lane < IDX_DH, _rope_tile(tail, cos, sin, IDX_DH), tail)


def _idx_proj(x, w, tabs, *, tm):
    M, K = x.shape
    N = w.shape[1]
    cos, sin = tabs
    npos = cos.shape[0] // tm
    return pl.pallas_call(
        _idx_proj_kernel,
        grid=(M // tm,),
        in_specs=[
            pl.BlockSpec((tm, K), lambda i: (i, 0)),
            pl.BlockSpec((K, N), lambda i: (0, 0)),
            pl.BlockSpec((tm, LANES), lambda i: (i % npos, 0)),
            pl.BlockSpec((tm, LANES), lambda i: (i % npos, 0)),
        ],
        out_specs=[
            pl.BlockSpec((tm, IDX_Q), lambda i: (i, 0)),
            pl.BlockSpec((tm, LANES), lambda i: (i, 0)),
        ],
        out_shape=[
            jax.ShapeDtypeStruct((M, IDX_Q), F32),
            jax.ShapeDtypeStruct((M, LANES), F32),
        ],
        compiler_params=_params(("parallel",)),
        name="proj_idx",
    )(x, w, cos, sin)


def _ret_kernel(lg_ref, gc_ref, q_ref, k_ref, v_ref, g_ref, km_ref, vm_ref, o_ref, st_ref, decay_ref, *, C):
    c = pl.program_id(1)
    dk = lambda h: slice(h * RET_DK, (h + 1) * RET_DK)
    dv = lambda h: slice(h * RET_DV, (h + 1) * RET_DV)

    @pl.when(c == 0)
    def _():
        pm = lax.broadcasted_iota(jnp.int32, (N_META, 1), 0).astype(F32)
        ri = lax.broadcasted_iota(jnp.int32, (C, C), 0)
        ci = lax.broadcasted_iota(jnp.int32, (C, C), 1)
        diff = (ri - ci).astype(F32)
        for h in range(RET_HEADS):
            lg = lg_ref[h]
            zeta_m = jnp.exp(lg * (N_META - 1.0 - pm))
            kz = km_ref[:, dk(h)].astype(F32) * zeta_m
            st_ref[h] = _dot(kz.T.astype(BF16), vm_ref[:, dv(h)])
            decay_ref[h] = jnp.where(diff >= 0, jnp.exp(lg * jnp.maximum(diff, 0.0)), 0.0)

    pos = lax.broadcasted_iota(jnp.int32, (C, 1), 0).astype(F32)
    for h in range(RET_HEADS):
        lg = lg_ref[h]
        q = q_ref[0, :, dk(h)]
        k = k_ref[0, :, dk(h)]
        v = v_ref[0, :, dv(h)]
        xi = jnp.exp(lg * (pos + 1.0))
        zeta = jnp.exp(lg * (C - 1.0 - pos))
        state = st_ref[h]
        inner = _dot((_dot_nt(q, k) * decay_ref[h]).astype(BF16), v)
        cross = _dot(q, state.astype(BF16)) * xi
        y = inner + cross
        y = y * lax.rsqrt(jnp.mean(y * y, axis=-1, keepdims=True) + EPS)
        o_ref[0, :, dv(h)] = (g_ref[0, :, dv(h)] * y).astype(o_ref.dtype)
        kz = k.astype(F32) * zeta
        st_ref[h] = gc_ref[h] * state + _dot(kz.T.astype(BF16), v)


def _retention(lg, gc, rqk, rv, rg, km, vm, *, C):
    B, S, _ = rv.shape
    smem = pl.BlockSpec(memory_space=pltpu.SMEM)
    return pl.pallas_call(
        functools.partial(_ret_kernel, C=C),
        grid=(B, S // C),
        in_specs=[
            smem,
            smem,
            pl.BlockSpec((1, C, RET_QK), lambda b, c: (b, c, 0)),
            pl.BlockSpec((1, C, RET_QK), lambda b, c: (b, c, 1)),
            pl.BlockSpec((1, C, RET_V), lambda b, c: (b, c, 0)),
            pl.BlockSpec((1, C, RET_V), lambda b, c: (b, c, 0)),
            pl.BlockSpec((N_META, RET_QK), lambda b, c: (0, 0)),
            pl.BlockSpec((N_META, RET_V), lambda b, c: (0, 0)),
        ],
        out_specs=pl.BlockSpec((1, C, RET_V), lambda b, c: (b, c, 0)),
        out_shape=jax.ShapeDtypeStruct((B, S, RET_V), BF16),
        scratch_shapes=[pltpu.VMEM((RET_HEADS, RET_DK, RET_DV), F32), pltpu.VMEM((RET_HEADS, C, C), F32)],
        compiler_params=_params(("parallel", "arbitrary")),
        name="retention",
    )(lg, gc, rqk, rqk, rv, rg, km, vm)


TINY = float(np.finfo(np.float32).tiny)
TN_DIMS = (((0,), (0,)), ((), ()))


def _dsa_kernel(iq_ref, ikw_ref, wq_ref, aq_ref, ak_ref, av_ref, akm_ref, avm_ref, o_ref,
                sc_ref, tau_ref, m_ref, l_ref, acc_ref, s_ref, bm_ref, *, nq, tq, tk, k_top):
    qi = pl.program_id(1)
    nkb = (qi + 1) * (tq // tk)
    kf = float(k_top)
    neg_inf = -jnp.inf
    qpos = qi * tq + lax.broadcasted_iota(jnp.int32, (1, tq), 1)

    part_shape = (SUBLANES, tq)

    def fold(x):
        return x.reshape(tk // SUBLANES, SUBLANES, tq).sum(axis=0)

    idx_scale = (IDX_DH ** -0.5) * (IDX_HEADS ** -0.5)
    iq = iq_ref[0].astype(BF16)
    w_t = wq_ref[0].T
    iq_h = [iq[:, h * IDX_DH:(h + 1) * IDX_DH] for h in range(IDX_HEADS)]
    w_h = [w_t[IDX_DH + h:IDX_DH + h + 1, :] for h in range(IDX_HEADS)]

    def score_rows(kb, n_blocks, carry):
        rmin, rmax = carry
        rows = n_blocks * tk
        ik = ikw_ref[0, pl.ds(pl.multiple_of(kb * tk, tk), rows), :][:, :IDX_DH].astype(BF16)
        acc = jnp.zeros((rows, tq), F32)
        for h in range(IDX_HEADS):
            acc = acc + w_h[h] * jnp.maximum(_dot_nt(ik, iq_h[h]), 0.0)
        ok = (lax.broadcasted_iota(jnp.int32, (rows, 1), 0) + kb * tk) <= qpos
        s = jnp.where(ok, acc * idx_scale, neg_inf)
        for i in range(n_blocks):
            sc_ref[kb + i] = s[i * tk:(i + 1) * tk]
        rmin = jnp.minimum(rmin, jnp.min(jnp.where(ok, s, jnp.inf), axis=0, keepdims=True))
        rmax = jnp.maximum(rmax, jnp.max(s, axis=0, keepdims=True))
        return rmin, rmax

    carry = (jnp.full((1, tq), jnp.inf, F32), jnp.full((1, tq), neg_inf, F32))
    carry = lax.fori_loop(0, nkb // 2, lambda j, c: score_rows(2 * j, 2, c), carry)
    rmin, rmax = lax.cond(nkb % 2 == 1, lambda c: score_rows(nkb - 1, 1, c), lambda c: c, carry)

    def count(pred):
        def body(kb, c):
            return c + fold(jnp.where(pred(sc_ref[kb]), 1.0, 0.0))
        return jnp.sum(lax.fori_loop(0, nkb, body, jnp.zeros(part_shape, F32)), axis=0, keepdims=True)

    def any_query(flag):
        return jnp.max(jnp.where(flag, 1.0, 0.0))

    c_max = count(lambda s: s >= rmax)
    top_tied = c_max >= kf
    lo0 = jnp.where(top_tied, rmax, rmin)
    cl0 = jnp.where(top_tied, c_max, (qpos + 1).astype(F32))

    def midpoint(lo, hi):
        a = jnp.maximum(jnp.abs(lo), TINY)
        b = jnp.maximum(jnp.abs(hi), TINY)
        geo = jnp.sqrt(a) * jnp.sqrt(b)
        one_sign = (lo >= 0.0) | (hi <= 0.0)
        far = jnp.maximum(a, b) > 4.0 * jnp.minimum(a, b)
        mid = jnp.where(one_sign & far, jnp.where(lo >= 0.0, geo, -geo), 0.5 * lo + 0.5 * hi)
        mid = jnp.where(lo == 0.0, TINY, mid)
        mid = jnp.where(hi == 0.0, -TINY, mid)
        return jnp.where((lo < 0.0) & (hi > 0.0), 0.0, mid)

    def active(lo, hi, cl):
        mid = midpoint(lo, hi)
        return (cl > kf) & (mid > lo) & (mid < hi)

    def go_on(st):
        return st[3] > 0.0

    def threshold(n_blocks, lo, hi, cl):
        blocks = range(n_blocks)

        def over_keys(fn, init, combine):
            acc = init
            for kb in blocks:
                acc = combine(acc, fn(sc_ref[kb]))
            return acc

        def count(pred):
            c = over_keys(lambda s: fold(jnp.where(pred(s), 1.0, 0.0)), jnp.zeros(part_shape, F32), jnp.add)
            return jnp.sum(c, axis=0, keepdims=True)

        def shrink_once(lo, hi, cl):
            mid = midpoint(lo, hi)
            act = active(lo, hi, cl)
            c = count(lambda s: s >= mid)
            up = act & (c >= kf)
            dn = act & (c < kf)
            return jnp.where(up, mid, lo), jnp.where(dn, mid, hi), jnp.where(up, c, cl)

        def shrink(st):
            pending = any_query(active(*st[:3]))
            return (*shrink_once(*shrink_once(*st[:3])), pending)

        lo, hi, cl, _ = lax.while_loop(go_on, shrink, (lo, hi, cl, jnp.float32(1.0)))
        tau_ref[...] = lo
        tie = cl > kf

        @pl.when(any_query(tie) > 0.0)
        def _():
            def max_below(bound):
                part = over_keys(
                    lambda s: jnp.max(jnp.where(s < bound, s, neg_inf).reshape(tk // SUBLANES, SUBLANES, tq), axis=0),
                    jnp.full(part_shape, neg_inf, F32), jnp.maximum)
                return jnp.max(part, axis=0, keepdims=True)

            hi_d = jnp.where(lo >= hi, jnp.inf, hi)
            v0 = max_below(hi_d)
            ge0 = count(lambda s: s >= v0)

            def walk_on(st):
                return st[3] > 0.0

            def walk(st):
                hi_w, v, ge, _ = st
                hi_w = jnp.where(tie & (ge < kf), v, hi_w)
                v = max_below(hi_w)
                ge = count(lambda s: s >= v)
                return hi_w, v, ge, any_query(tie & (ge < kf))

            _, v, _, _ = lax.while_loop(walk_on, walk, (hi_d, v0, ge0, any_query(tie & (ge0 < kf))))
            keep = kf - count(lambda s: s > v)
            ri = lax.broadcasted_iota(jnp.int32, (tk, tk), 0)
            ci = lax.broadcasted_iota(jnp.int32, (tk, tk), 1)
            earlier = jnp.where(ci < ri, 1.0, 0.0).astype(BF16)
            seen = jnp.zeros((1, tq), F32)
            for kb in blocks:
                s = sc_ref[kb]
                eq = tie & (s == v)
                eqf = jnp.where(eq, 1.0, 0.0)
                rank = seen + _dot(earlier, eqf.astype(BF16))
                sc_ref[kb] = jnp.where(eq & (rank >= keep), neg_inf, s)
                seen = seen + jnp.sum(eqf, axis=0, keepdims=True)
            tau_ref[...] = jnp.where(tie, v, lo)

        return jnp.int32(0)

    branches = [functools.partial(threshold, (i + 1) * (tq // tk)) for i in range(nq)]
    lax.switch(qi, branches, lo0, rmax, cl0)

    tau = tau_ref[...]
    meta_bias = jnp.where(lax.broadcasted_iota(jnp.int32, (tk, 1), 0) < N_META, 0.0, neg_inf)

    def head(ref, h, rows=None):
        if rows is None:
            return ref[:, h * ATT_DH:(h + 1) * ATT_DH]
        return ref[0, rows, h * ATT_DH:(h + 1) * ATT_DH]

    def keys_of(kb):
        if kb is None:
            return (lambda h: head(akm_ref, h)), (lambda h: head(avm_ref, h)), meta_bias
        rows = pl.ds(pl.multiple_of(kb * tk, tk), tk)
        bias = jnp.where(sc_ref[kb] >= tau, 0.0, neg_inf)
        return (lambda h: head(ak_ref, h, rows)), (lambda h: head(av_ref, h, rows)), bias

    def score_phase(kb, buf):
        k_of, _, bias = keys_of(kb)
        for h in range(ATT_HEADS):
            s = _dot_nt(k_of(h), head(aq_ref, h, slice(None))) + bias
            s_ref[buf, h] = s
            bm_ref[buf, h:h + 1, :] = jnp.max(s, axis=0, keepdims=True)

    def value_phase(kb, buf):
        _, v_of, _ = keys_of(kb)
        for h in range(ATT_HEADS):
            m_new = jnp.maximum(m_ref[h:h + 1, :], bm_ref[buf, h:h + 1, :])
            p = jnp.exp2(s_ref[buf, h] - m_new)
            p_sum = jnp.sum(p, axis=0, keepdims=True)
            pv = lax.dot_general(v_of(h), p.astype(BF16), TN_DIMS, preferred_element_type=F32)
            alpha = jnp.exp2(m_ref[h:h + 1, :] - m_new)
            l_ref[h:h + 1, :] = alpha * l_ref[h:h + 1, :] + p_sum
            acc_ref[h] = alpha * acc_ref[h] + pv
            m_ref[h:h + 1, :] = m_new

    m_ref[...] = jnp.full_like(m_ref, neg_inf)
    l_ref[...] = jnp.zeros_like(l_ref)
    acc_ref[...] = jnp.zeros_like(acc_ref)
    score_phase(None, 0)

    @pl.when(nkb > 0)
    def _():
        value_phase(None, 0)
        score_phase(0, 1)

    def att_pair(j, carry):
        kb = 2 * j
        value_phase(kb, 1)
        score_phase(kb + 1, 0)

        @pl.when(kb + 1 < nkb)
        def _():
            value_phase(kb + 1, 0)
            score_phase(jnp.minimum(kb + 2, nkb - 1), 1)
        return carry

    lax.fori_loop(0, nkb // 2, att_pair, 0)

    @pl.when(nkb % 2 == 1)
    def _():
        value_phase(nkb - 1, 1)

    for h in range(ATT_HEADS):
        out_t = acc_ref[h] / l_ref[h:h + 1, :]
        o_ref[0, :, h * ATT_DH:(h + 1) * ATT_DH] = out_t.T.astype(o_ref.dtype)


def _dsa(iq, ikw, aqk, av, akm, avm, *, tq, tk, k_top):
    B, S, _ = av.shape
    nq = S // tq
    assert tq % tk == 0
    return pl.pallas_call(
        functools.partial(_dsa_kernel, nq=nq, tq=tq, tk=tk, k_top=k_top),
        grid=(B, nq),
        in_specs=[
            pl.BlockSpec((1, tq, IDX_Q), lambda b, q: (b, q, 0)),
            pl.BlockSpec((1, S, LANES), lambda b, q: (b, 0, 0)),
            pl.BlockSpec((1, tq, LANES), lambda b, q: (b, q, 0)),
            pl.BlockSpec((1, tq, ATT_W), lambda b, q: (b, q, 0)),
            pl.BlockSpec((1, S, ATT_W), lambda b, q: (b, 0, 1)),
            pl.BlockSpec((1, S, ATT_W), lambda b, q: (b, 0, 0)),
            pl.BlockSpec((tk, ATT_W), lambda b, q: (0, 0)),
            pl.BlockSpec((tk, ATT_W), lambda b, q: (0, 0)),
        ],
        out_specs=pl.BlockSpec((1, tq, ATT_W), lambda b, q: (b, q, 0)),
        out_shape=jax.ShapeDtypeStruct((B, S, ATT_W), BF16),
        scratch_shapes=[
            pltpu.VMEM((S // tk, tk, tq), F32),
            pltpu.VMEM((1, tq), F32),
            pltpu.VMEM((ATT_HEADS, tq), F32),
            pltpu.VMEM((ATT_HEADS, tq), F32),
            pltpu.VMEM((ATT_HEADS, ATT_DH, tq), F32),
            pltpu.VMEM((2, ATT_HEADS, tk, tq), F32),
            pltpu.VMEM((2, ATT_HEADS, tq), F32),
        ],
        compiler_params=_params(("parallel", "arbitrary")),
        name="dsa",
    )(iq, ikw, ikw, aqk, aqk, av, akm, avm)


def _merge_kernel(h_ref, yr_ref, ya_ref, gr_ref, ga_ref, wr_ref, wa_ref, wm_ref, o_ref):
    yr = _dot(yr_ref[...], wr_ref[...])
    ya = _dot(ya_ref[...], wa_ref[...])
    merged = _sigmoid(gr_ref[...].astype(F32)) * yr + _sigmoid(ga_ref[...].astype(F32)) * ya
    o_ref[...] = h_ref[...] + _dot(merged.astype(BF16), wm_ref[...])


def _merge(h, yr, ya, gates, wr, wa, wm, *, tm):
    M, D = h.shape
    row = lambda i: (i, 0)
    fixed = lambda i: (0, 0)
    return pl.pallas_call(
        _merge_kernel,
        grid=(M // tm,),
        in_specs=[
            pl.BlockSpec((tm, D), row),
            pl.BlockSpec((tm, RET_V), row),
            pl.BlockSpec((tm, ATT_W), row),
            pl.BlockSpec((tm, D), lambda i: (i, 0)),
            pl.BlockSpec((tm, D), lambda i: (i, 1)),
            pl.BlockSpec((RET_V, D), fixed),
            pl.BlockSpec((ATT_W, D), fixed),
            pl.BlockSpec((D, D), fixed),
        ],
        out_specs=pl.BlockSpec((tm, D), row),
        out_shape=jax.ShapeDtypeStruct((M, D), F32),
        compiler_params=_params(("parallel",)),
        name="merge",
    )(h, yr, ya, gates, gates, wr, wa, wm)


def _rope_tables(pos, d, width):
    inv_freq = ROPE_THETA ** (-jnp.arange(0, d, 2, dtype=F32) / d)
    ang = pos.astype(F32)[:, None] * inv_freq[None, :]
    cos, sin = jnp.cos(ang), jnp.sin(ang)
    cos = jnp.concatenate([cos, cos], axis=-1)
    sin = jnp.concatenate([-sin, sin], axis=-1)
    rep = width // d
    return jnp.tile(cos, (1, rep)), jnp.tile(sin, (1, rep))


def _tile(n, cap):
    t = min(n, cap)
    assert n % t == 0, (n, t)
    return t


class _Plan(NamedTuple):
    rows: int
    rows_wide: int
    rows_merge: int
    cols: int
    ffn_slab: int
    att: int
    chunk: int


def _plan(S, M):
    return _Plan(rows=_tile(S, 4 * MXU_DIM), rows_wide=_tile(M, 8 * MXU_DIM), rows_merge=_tile(M, 2 * MXU_DIM),
                 cols=4 * MXU_DIM, ffn_slab=MXU_DIM, att=_tile(S, MXU_DIM), chunk=_tile(S, 4 * RET_CHUNK))


def kernel(x, meta_tokens, ffn1_norm, ffn1_w_gate, ffn1_w_up, ffn1_w_down, mix_norm, w_in, w_ret_out,
           w_att_out, w_mix_out, ffn2_norm, ffn2_w_gate, ffn2_w_up, ffn2_w_down, final_norm):
    B, S, D = x.shape
    assert ffn1_norm.shape[0] == 1, "single layer only"
    assert meta_tokens.shape[0] == N_META
    k_top = min(TOPK_MAX, S // 4)
    M = B * S
    plan = _plan(S, M)
    bf = lambda a: a.astype(BF16)
    row = lambda a: a.reshape(1, -1)

    w32 = w_in[0].T
    offs = np.cumsum([0, RET_QK, RET_QK, RET_V, RET_V, ATT_W, ATT_W, ATT_W, IDX_Q, IDX_DH, IDX_HEADS, D, D])
    tn = plan.cols
    assert all(int(o) % tn == 0 for o in offs[:8])
    c_rq, c_rk, c_rv, c_rg, c_aq, c_ak, c_av = (int(o) // tn for o in offs[:7])
    w_idx = jnp.pad(bf(w32[offs[7]:offs[10]].T), ((0, 0), (0, LANES - IDX_DH - IDX_HEADS)))
    w_gate = w32[offs[10]:offs[12]]

    pos_meta = jnp.arange(N_META, dtype=jnp.int32)
    pos_real = N_META + jnp.arange(S, dtype=jnp.int32)
    ffn1 = (row(ffn1_norm[0]), bf(ffn1_w_gate[0]), bf(ffn1_w_up[0]), bf(ffn1_w_down[0]), row(mix_norm[0]))
    ffn2 = (row(ffn2_norm[0]), bf(ffn2_w_gate[0]), bf(ffn2_w_up[0]), bf(ffn2_w_down[0]), row(final_norm))
    tf = plan.ffn_slab
    k_scale = RET_DK ** -0.5

    _, un_m = _ffn(meta_tokens.astype(F32), *ffn1, tm=N_META, tf=tf, emit_h=True, n_dtype=BF16)
    pm = dict(tm=N_META, tn=tn)
    rk_m = _proj(un_m, w32, col0=c_rk, n_tiles=RET_QK // tn, mode="rope", d=RET_DK, k_scale=k_scale,
                 tabs=_rope_tables(pos_meta, RET_DK, RET_DK), **pm)
    rv_m = _proj(un_m, w32, col0=c_rv, n_tiles=RET_V // tn, mode="plain", **pm)
    ak_m = _proj(un_m, w32, col0=c_ak, n_tiles=ATT_W // tn, mode="rope", d=ATT_DH,
                 tabs=_rope_tables(pos_meta, ATT_DH, LANES), **pm)
    av_m = _proj(un_m, w32, col0=c_av, n_tiles=ATT_W // tn, mode="plain", **pm)

    tm = plan.rows
    h1, un = _ffn(x.reshape(M, D), *ffn1, tm=tm, tf=tf, emit_h=True, n_dtype=BF16)
    pr = dict(tm=tm, tn=tn)
    pw = dict(tm=plan.rows_wide, tn=tn)
    rqk = _proj(un, w32, col0=c_rq, n_tiles=2 * RET_QK // tn, mode="rope", d=RET_DK, k_scale=k_scale,
                n_q=RET_QK // tn, tabs=_rope_tables(pos_real, RET_DK, RET_DK), **pr)
    rv = _proj(un, w32, col0=c_rv, n_tiles=RET_V // tn, mode="plain", **pw)
    rg = _proj(un, w32, col0=c_rg, n_tiles=RET_V // tn, mode="silu", **pw)
    aqk = _proj(un, w32, col0=c_aq, n_tiles=2 * ATT_W // tn, mode="rope", d=ATT_DH, q_scale=ATT_EXP_SCALE,
                n_q=ATT_W // tn, tabs=_rope_tables(pos_real, ATT_DH, LANES), **pr)
    av = _proj(un, w32, col0=c_av, n_tiles=ATT_W // tn, mode="plain", **pw)
    gates = _proj(un, w_gate, col0=0, n_tiles=2 * D // tn, mode="plain", **pw)
    iq, ikw = _idx_proj(un, w_idx, _rope_tables(pos_real, IDX_DH, LANES), tm=tm)

    heads = jnp.arange(RET_HEADS, dtype=F32)
    log_g = jnp.log1p(-(2.0 ** (-5.0 - heads)))
    C = plan.chunk
    b3 = lambda a: a.reshape(B, S, a.shape[-1])
    yr = _retention(log_g, jnp.exp(log_g * C), b3(rqk), b3(rv), b3(rg), rk_m, rv_m, C=C)
    tk = plan.att
    pad_meta = lambda a: jnp.pad(a, ((0, tk - N_META), (0, 0)))
    ya = _dsa(b3(iq), b3(ikw), b3(aqk), b3(av), pad_meta(ak_m), pad_meta(av_m),
              tq=plan.att, tk=tk, k_top=k_top)

    h2 = _merge(h1, yr.reshape(M, RET_V), ya.reshape(M, ATT_W), gates, bf(w_ret_out[0]), bf(w_att_out[0]),
                bf(w_mix_out[0]), tm=plan.rows_merge)
    (y,) = _ffn(h2, *ffn2, tm=tm, tf=tf, emit_h=False, n_dtype=F32)
    return y.reshape(B, S, D)
```

```python
import functools
from typing import NamedTuple

import jax
import jax.numpy as jnp
import numpy as np
from jax import lax
from jax.experimental import pallas as pl
from jax.experimental.pallas import tpu as pltpu

N_META = 16
RET_HEADS = 4
RET_DK = 256
RET_DV = 512
RET_CHUNK = 128
ATT_HEADS = 8
ATT_DH = 128
IDX_HEADS = 8
IDX_DH = 64
TOPK_MAX = 256
ROPE_THETA = 10000.0
EPS = 1e-6
RET_QK = RET_HEADS * RET_DK
RET_V = RET_HEADS * RET_DV
ATT_W = ATT_HEADS * ATT_DH
IDX_Q = IDX_HEADS * IDX_DH
ATT_EXP_SCALE = (ATT_DH ** -0.5) * float(np.log2(np.e))

LANES = 128
SUBLANES = 8
MXU_DIM = 256
VMEM_LIMIT = 48 * 1024 * 1024
FFN_VMEM_LIMIT = 56 * 1024 * 1024

BF16 = jnp.bfloat16
F32 = jnp.float32
NT_DIMS = (((1,), (1,)), ((), ()))


def _dot(a, b):
    return jnp.dot(a, b, preferred_element_type=F32)


def _dot_nt(a, b):
    return lax.dot_general(a, b, NT_DIMS, preferred_element_type=F32)


def _rms(x, g):
    return x * lax.rsqrt(jnp.mean(x * x, axis=-1, keepdims=True) + EPS) * g


def _sigmoid(x):
    return 0.5 * jnp.tanh(0.5 * x) + 0.5


def _params(sem):
    return pltpu.CompilerParams(dimension_semantics=sem, vmem_limit_bytes=VMEM_LIMIT)


def _ffn_kernel(x_ref, g_ref, wg_ref, wu_ref, wd_ref, g2_ref, *rest, nf, tf, emit_h):
    if emit_h:
        h_ref, n_ref, xn_sc, acc_sc, act_sc = rest
    else:
        n_ref, xn_sc, acc_sc, act_sc = rest

    def down(f):
        rows = pl.ds(pl.multiple_of(f * tf, tf), tf)
        return _dot(act_sc[...], wd_ref[rows, :])

    def up(f):
        cols = pl.ds(pl.multiple_of(f * tf, tf), tf)
        xn = xn_sc[...]
        gate = _dot(xn, wg_ref[:, cols])
        lift = _dot(xn, wu_ref[:, cols])
        act_sc[...] = ((gate * _sigmoid(gate)) * lift).astype(BF16)

    xn_sc[...] = _rms(x_ref[...], g_ref[...]).astype(BF16)
    acc_sc[...] = jnp.zeros_like(acc_sc)
    up(0)

    def slab(f, carry):
        acc_sc[...] += down(f - 1)
        up(f)
        return carry

    lax.fori_loop(1, nf, slab, 0)
    h = x_ref[...] + 0.5 * (acc_sc[...] + down(nf - 1))
    if emit_h:
        h_ref[...] = h
    n_ref[...] = _rms(h, g2_ref[...]).astype(n_ref.dtype)


def _ffn(x, g, wg, wu, wd, g2, *, tm, tf, emit_h, n_dtype):
    M, D = x.shape
    F = wg.shape[1]
    nf = F // tf
    assert nf >= 2
    out_shape = [jax.ShapeDtypeStruct((M, D), n_dtype)]
    out_specs = [pl.BlockSpec((tm, D), lambda i: (i, 0))]
    if emit_h:
        out_shape = [jax.ShapeDtypeStruct((M, D), F32)] + out_shape
        out_specs = [pl.BlockSpec((tm, D), lambda i: (i, 0))] + out_specs
    resident = lambda shape: pl.BlockSpec(shape, lambda i: (0, 0), pipeline_mode=pl.Buffered(1))
    return pl.pallas_call(
        functools.partial(_ffn_kernel, nf=nf, tf=tf, emit_h=emit_h),
        grid=(M // tm,),
        in_specs=[
            pl.BlockSpec((tm, D), lambda i: (i, 0)),
            pl.BlockSpec((1, D), lambda i: (0, 0)),
            resident((D, F)),
            resident((D, F)),
            resident((F, D)),
            pl.BlockSpec((1, D), lambda i: (0, 0)),
        ],
        out_specs=out_specs,
        out_shape=out_shape,
        scratch_shapes=[pltpu.VMEM((tm, D), BF16), pltpu.VMEM((tm, D), F32), pltpu.VMEM((tm, tf), BF16)],
        compiler_params=pltpu.CompilerParams(
            dimension_semantics=("arbitrary",), vmem_limit_bytes=FFN_VMEM_LIMIT),
        name="ffn",
    )(x, g, wg, wu, wd, g2)


def _rot_half(y, d):
    w = y.shape[-1]
    if d >= w:
        return jnp.concatenate([y[:, w // 2:], y[:, : w // 2]], axis=-1)
    if d == LANES:
        return pltpu.roll(y, d // 2, axis=1)
    fwd = pltpu.roll(y, d // 2, axis=1)
    bwd = pltpu.roll(y, w - d // 2, axis=1)
    lane = lax.broadcasted_iota(jnp.int32, y.shape, 1)
    return jnp.where((lane % d) < d // 2, bwd, fwd)


def _rope_tile(y, cos, sin, d):
    tw = cos.shape[-1]
    outs = []
    for t in range(y.shape[-1] // tw):
        yt = y[:, t * tw:(t + 1) * tw]
        outs.append(yt * cos + _rot_half(yt, d) * sin)
    return outs[0] if len(outs) == 1 else jnp.concatenate(outs, axis=-1)


def _proj_kernel(x_ref, w_ref, *rest, mode, d, q_scale, k_scale, n_q):
    if mode == "rope":
        cos_ref, sin_ref, o_ref, w_sc = rest
    else:
        o_ref, w_sc = rest

    @pl.when(pl.program_id(1) == 0)
    def _():
        w_sc[...] = w_ref[...].T.astype(BF16)

    y = _dot(x_ref[...], w_sc[...])
    if mode == "rope":
        y = _rope_tile(y, cos_ref[...], sin_ref[...], d)
        if q_scale != 1.0 or k_scale != 1.0:
            y = y * jnp.where(pl.program_id(0) >= n_q, k_scale, q_scale)
    elif mode == "silu":
        y = y * _sigmoid(y)
    o_ref[...] = y.astype(o_ref.dtype)


def _proj(x, w_t, *, col0, n_tiles, tm, tn, mode, out_dtype=None, tabs=None, d=None, q_scale=1.0,
          k_scale=1.0, n_q=0):
    M, K = x.shape
    in_specs = [
        pl.BlockSpec((tm, K), lambda j, i: (i, 0)),
        pl.BlockSpec((tn, K), lambda j, i: (col0 + j, 0)),
    ]
    args = [x, w_t]
    if mode == "rope":
        cos, sin = tabs
        npos = cos.shape[0] // tm
        tw = cos.shape[1]
        in_specs += [pl.BlockSpec((tm, tw), lambda j, i: (i % npos, 0))] * 2
        args += [cos, sin]
    return pl.pallas_call(
        functools.partial(_proj_kernel, mode=mode, d=d, q_scale=q_scale, k_scale=k_scale, n_q=n_q),
        grid=(n_tiles, M // tm),
        in_specs=in_specs,
        out_specs=pl.BlockSpec((tm, tn), lambda j, i: (i, j)),
        out_shape=jax.ShapeDtypeStruct((M, n_tiles * tn), out_dtype or BF16),
        scratch_shapes=[pltpu.VMEM((K, tn), BF16)],
        compiler_params=_params(("arbitrary", "arbitrary")),
        name="proj_" + mode,
    )(*args)


def _idx_proj_kernel(x_ref, w_ref, cos_ref, sin_ref, q_ref, kw_ref):
    y = _dot(x_ref[...], w_ref[...])
    cos, sin = cos_ref[...], sin_ref[...]
    q_ref[...] = _rope_tile(y[:, :IDX_Q], cos, sin, IDX_DH)
    tail = y[:, IDX_Q:]
    lane = lax.broadcasted_iota(jnp.int32, tail.shape, 1)
    kw_ref[...] = jnp.where(lane < IDX_DH, _rope_tile(tail, cos, sin, IDX_DH), tail)


def _idx_proj(x, w, tabs, *, tm):
    M, K = x.shape
    N = w.shape[1]
    cos, sin = tabs
    npos = cos.shape[0] // tm
    return pl.pallas_call(
        _idx_proj_kernel,
        grid=(M // tm,),
        in_specs=[
            pl.BlockSpec((tm, K), lambda i: (i, 0)),
            pl.BlockSpec((K, N), lambda i: (0, 0)),
            pl.BlockSpec((tm, LANES), lambda i: (i % npos, 0)),
            pl.BlockSpec((tm, LANES), lambda i: (i % npos, 0)),
        ],
        out_specs=[
            pl.BlockSpec((tm, IDX_Q), lambda i: (i, 0)),
            pl.BlockSpec((tm, LANES), lambda i: (i, 0)),
        ],
        out_shape=[
            jax.ShapeDtypeStruct((M, IDX_Q), F32),
            jax.ShapeDtypeStruct((M, LANES), F32),
        ],
        compiler_params=_params(("parallel",)),
        name="proj_idx",
    )(x, w, cos, sin)


def _ret_kernel(lg_ref, gc_ref, q_ref, k_ref, v_ref, g_ref, km_ref, vm_ref, o_ref, st_ref, decay_ref, *, C):
    c = pl.program_id(1)
    dk = lambda h: slice(h * RET_DK, (h + 1) * RET_DK)
    dv = lambda h: slice(h * RET_DV, (h + 1) * RET_DV)

    @pl.when(c == 0)
    def _():
        pm = lax.broadcasted_iota(jnp.int32, (N_META, 1), 0).astype(F32)
        ri = lax.broadcasted_iota(jnp.int32, (C, C), 0)
        ci = lax.broadcasted_iota(jnp.int32, (C, C), 1)
        diff = (ri - ci).astype(F32)
        for h in range(RET_HEADS):
            lg = lg_ref[h]
            zeta_m = jnp.exp(lg * (N_META - 1.0 - pm))
            kz = km_ref[:, dk(h)].astype(F32) * zeta_m
            st_ref[h] = _dot(kz.T.astype(BF16), vm_ref[:, dv(h)])
            decay_ref[h] = jnp.where(diff >= 0, jnp.exp(lg * jnp.maximum(diff, 0.0)), 0.0)

    pos = lax.broadcasted_iota(jnp.int32, (C, 1), 0).astype(F32)
    for h in range(RET_HEADS):
        lg = lg_ref[h]
        q = q_ref[0, :, dk(h)]
        k = k_ref[0, :, dk(h)]
        v = v_ref[0, :, dv(h)]
        xi = jnp.exp(lg * (pos + 1.0))
        zeta = jnp.exp(lg * (C - 1.0 - pos))
        state = st_ref[h]
        inner = _dot((_dot_nt(q, k) * decay_ref[h]).astype(BF16), v)
        cross = _dot(q, state.astype(BF16)) * xi
        y = inner + cross
        y = y * lax.rsqrt(jnp.mean(y * y, axis=-1, keepdims=True) + EPS)
        o_ref[0, :, dv(h)] = (g_ref[0, :, dv(h)] * y).astype(o_ref.dtype)
        kz = k.astype(F32) * zeta
        st_ref[h] = gc_ref[h] * state + _dot(kz.T.astype(BF16), v)


def _retention(lg, gc, rqk, rv, rg, km, vm, *, C):
    B, S, _ = rv.shape
    smem = pl.BlockSpec(memory_space=pltpu.SMEM)
    return pl.pallas_call(
        functools.partial(_ret_kernel, C=C),
        grid=(B, S // C),
        in_specs=[
            smem,
            smem,
            pl.BlockSpec((1, C, RET_QK), lambda b, c: (b, c, 0)),
            pl.BlockSpec((1, C, RET_QK), lambda b, c: (b, c, 1)),
            pl.BlockSpec((1, C, RET_V), lambda b, c: (b, c, 0)),
            pl.BlockSpec((1, C, RET_V), lambda b, c: (b, c, 0)),
            pl.BlockSpec((N_META, RET_QK), lambda b, c: (0, 0)),
            pl.BlockSpec((N_META, RET_V), lambda b, c: (0, 0)),
        ],
        out_specs=pl.BlockSpec((1, C, RET_V), lambda b, c: (b, c, 0)),
        out_shape=jax.ShapeDtypeStruct((B, S, RET_V), BF16),
        scratch_shapes=[pltpu.VMEM((RET_HEADS, RET_DK, RET_DV), F32), pltpu.VMEM((RET_HEADS, C, C), F32)],
        compiler_params=_params(("parallel", "arbitrary")),
        name="retention",
    )(lg, gc, rqk, rqk, rv, rg, km, vm)


TINY = float(np.finfo(np.float32).tiny)
TN_DIMS = (((0,), (0,)), ((), ()))


def _dsa_kernel(iq_ref, ikw_ref, wq_ref, aq_ref, ak_ref, av_ref, akm_ref, avm_ref, o_ref,
                sc_ref, tau_ref, m_ref, l_ref, acc_ref, s_ref, bm_ref, *, nq, tq, tk, k_top):
    qi = pl.program_id(1)
    nkb = (qi + 1) * (tq // tk)
    kf = float(k_top)
    neg_inf = -jnp.inf
    qpos = qi * tq + lax.broadcasted_iota(jnp.int32, (1, tq), 1)

    part_shape = (SUBLANES, tq)

    def fold(x):
        return x.reshape(tk // SUBLANES, SUBLANES, tq).sum(axis=0)

    idx_scale = (IDX_DH ** -0.5) * (IDX_HEADS ** -0.5)
    iq = iq_ref[0].astype(BF16)
    w_t = wq_ref[0].T
    iq_h = [iq[:, h * IDX_DH:(h + 1) * IDX_DH] for h in range(IDX_HEADS)]
    w_h = [w_t[IDX_DH + h:IDX_DH + h + 1, :] for h in range(IDX_HEADS)]

    def score_rows(kb, n_blocks, carry):
        rmin, rmax = carry
        rows = n_blocks * tk
        ik = ikw_ref[0, pl.ds(pl.multiple_of(kb * tk, tk), rows), :][:, :IDX_DH].astype(BF16)
        acc = jnp.zeros((rows, tq), F32)
        for h in range(IDX_HEADS):
            acc = acc + w_h[h] * jnp.maximum(_dot_nt(ik, iq_h[h]), 0.0)
        ok = (lax.broadcasted_iota(jnp.int32, (rows, 1), 0) + kb * tk) <= qpos
        s = jnp.where(ok, acc * idx_scale, neg_inf)
        for i in range(n_blocks):
            sc_ref[kb + i] = s[i * tk:(i + 1) * tk]
        rmin = jnp.minimum(rmin, jnp.min(jnp.where(ok, s, jnp.inf), axis=0, keepdims=True))
        rmax = jnp.maximum(rmax, jnp.max(s, axis=0, keepdims=True))
        return rmin, rmax

    carry = (jnp.full((1, tq), jnp.inf, F32), jnp.full((1, tq), neg_inf, F32))
    carry = lax.fori_loop(0, nkb // 4, lambda j, c: score_rows(4 * j, 4, c), carry)
    carry = lax.cond(nkb % 4 >= 2, lambda c: score_rows((nkb // 4) * 4, 2, c), lambda c: c, carry)
    rmin, rmax = lax.cond(nkb % 2 == 1, lambda c: score_rows(nkb - 1, 1, c), lambda c: c, carry)

    def count(pred):
        def body(kb, c):
            return c + fold(jnp.where(pred(sc_ref[kb]), 1.0, 0.0))
        return jnp.sum(lax.fori_loop(0, nkb, body, jnp.zeros(part_shape, F32)), axis=0, keepdims=True)

    def any_query(flag):
        return jnp.max(jnp.where(flag, 1.0, 0.0))

    c_max = count(lambda s: s >= rmax)
    top_tied = c_max >= kf
    lo0 = jnp.where(top_tied, rmax, rmin)
    cl0 = jnp.where(top_tied, c_max, (qpos + 1).astype(F32))

    def midpoint(lo, hi):
        a = jnp.maximum(jnp.abs(lo), TINY)
        b = jnp.maximum(jnp.abs(hi), TINY)
        geo = jnp.sqrt(a) * jnp.sqrt(b)
        one_sign = (lo >= 0.0) | (hi <= 0.0)
        far = jnp.maximum(a, b) > 4.0 * jnp.minimum(a, b)
        mid = jnp.where(one_sign & far, jnp.where(lo >= 0.0, geo, -geo), 0.5 * lo + 0.5 * hi)
        mid = jnp.where(lo == 0.0, TINY, mid)
        mid = jnp.where(hi == 0.0, -TINY, mid)
        return jnp.where((lo < 0.0) & (hi > 0.0), 0.0, mid)

    def active(lo, hi, cl):
        mid = midpoint(lo, hi)
        return (cl > kf) & (mid > lo) & (mid < hi)

    def go_on(st):
        return st[3] > 0.0

    def threshold(n_blocks, lo, hi, cl):
        blocks = range(n_blocks)

        def over_keys(fn, init, combine):
            acc = init
            for kb in blocks:
                acc = combine(acc, fn(sc_ref[kb]))
            return acc

        def count(pred):
            c = over_keys(lambda s: fold(jnp.where(pred(s), 1.0, 0.0)), jnp.zeros(part_shape, F32), jnp.add)
            return jnp.sum(c, axis=0, keepdims=True)

        def shrink_once(lo, hi, cl):
            mid = midpoint(lo, hi)
            act = active(lo, hi, cl)
            c = count(lambda s: s >= mid)
            up = act & (c >= kf)
            dn = act & (c < kf)
            return jnp.where(up, mid, lo), jnp.where(dn, mid, hi), jnp.where(up, c, cl)

        def shrink(st):
            pending = any_query(active(*st[:3]))
            return (*shrink_once(*shrink_once(*st[:3])), pending)

        lo, hi, cl, _ = lax.while_loop(go_on, shrink, (lo, hi, cl, jnp.float32(1.0)))
        tau_ref[...] = lo
        tie = cl > kf

        @pl.when(any_query(tie) > 0.0)
        def _():
            def max_below(bound):
                part = over_keys(
                    lambda s: jnp.max(jnp.where(s < bound, s, neg_inf).reshape(tk // SUBLANES, SUBLANES, tq), axis=0),
                    jnp.full(part_shape, neg_inf, F32), jnp.maximum)
                return jnp.max(part, axis=0, keepdims=True)

            hi_d = jnp.where(lo >= hi, jnp.inf, hi)
            v0 = max_below(hi_d)
            ge0 = count(lambda s: s >= v0)

            def walk_on(st):
                return st[3] > 0.0

            def walk(st):
                hi_w, v, ge, _ = st
                hi_w = jnp.where(tie & (ge < kf), v, hi_w)
                v = max_below(hi_w)
                ge = count(lambda s: s >= v)
                return hi_w, v, ge, any_query(tie & (ge < kf))

            _, v, _, _ = lax.while_loop(walk_on, walk, (hi_d, v0, ge0, any_query(tie & (ge0 < kf))))
            keep = kf - count(lambda s: s > v)
            ri = lax.broadcasted_iota(jnp.int32, (tk, tk), 0)
            ci = lax.broadcasted_iota(jnp.int32, (tk, tk), 1)
            earlier = jnp.where(ci < ri, 1.0, 0.0).astype(BF16)
            seen = jnp.zeros((1, tq), F32)
            for kb in blocks:
                s = sc_ref[kb]
                eq = tie & (s == v)
                eqf = jnp.where(eq, 1.0, 0.0)
                rank = seen + _dot(earlier, eqf.astype(BF16))
                sc_ref[kb] = jnp.where(eq & (rank >= keep), neg_inf, s)
                seen = seen + jnp.sum(eqf, axis=0, keepdims=True)
            tau_ref[...] = jnp.where(tie, v, lo)

        return jnp.int32(0)

    branches = [functools.partial(threshold, (i + 1) * (tq // tk)) for i in range(nq)]
    lax.switch(qi, branches, lo0, rmax, cl0)

    tau = tau_ref[...]
    meta_bias = jnp.where(lax.broadcasted_iota(jnp.int32, (tk, 1), 0) < N_META, 0.0, neg_inf)

    def head(ref, h, rows=None):
        if rows is None:
            return ref[:, h * ATT_DH:(h + 1) * ATT_DH]
        return ref[0, rows, h * ATT_DH:(h + 1) * ATT_DH]

    def keys_of(kb):
        if kb is None:
            return (lambda h: head(akm_ref, h)), (lambda h: head(avm_ref, h)), meta_bias
        rows = pl.ds(pl.multiple_of(kb * tk, tk), tk)
        bias = jnp.where(sc_ref[kb] >= tau, 0.0, neg_inf)
        return (lambda h: head(ak_ref, h, rows)), (lambda h: head(av_ref, h, rows)), bias

    def score_phase(kb, buf):
        k_of, _, bias = keys_of(kb)
        for h in range(ATT_HEADS):
            s = _dot_nt(k_of(h), head(aq_ref, h, slice(None))) + bias
            s_ref[buf, h] = s
            bm_ref[buf, h:h + 1, :] = jnp.max(s, axis=0, keepdims=True)

    def value_phase(kb, buf):
        _, v_of, _ = keys_of(kb)
        for h in range(ATT_HEADS):
            m_new = jnp.maximum(m_ref[h:h + 1, :], bm_ref[buf, h:h + 1, :])
            p = jnp.exp2(s_ref[buf, h] - m_new)
            p_sum = jnp.sum(p, axis=0, keepdims=True)
            pv = lax.dot_general(v_of(h), p.astype(BF16), TN_DIMS, preferred_element_type=F32)
            alpha = jnp.exp2(m_ref[h:h + 1, :] - m_new)
            l_ref[h:h + 1, :] = alpha * l_ref[h:h + 1, :] + p_sum
            acc_ref[h] = alpha * acc_ref[h] + pv
            m_ref[h:h + 1, :] = m_new

    m_ref[...] = jnp.full_like(m_ref, neg_inf)
    l_ref[...] = jnp.zeros_like(l_ref)
    acc_ref[...] = jnp.zeros_like(acc_ref)
    score_phase(None, 0)

    value_phase(None, 0)
    score_phase(0, 1)

    def att_pair(j, carry):
        kb = 2 * j
        value_phase(kb, 1)
        score_phase(kb + 1, 0)

        value_phase(kb + 1, 0)
        score_phase(jnp.minimum(kb + 2, nkb - 1), 1)
        return carry

    lax.fori_loop(0, nkb // 2, att_pair, 0)

    @pl.when(nkb % 2 == 1)
    def _():
        value_phase(nkb - 1, 1)

    for h in range(ATT_HEADS):
        out_t = acc_ref[h] / l_ref[h:h + 1, :]
        o_ref[0, :, h * ATT_DH:(h + 1) * ATT_DH] = out_t.T.astype(o_ref.dtype)


def _dsa(iq, ikw, aqk, av, akm, avm, *, tq, tk, k_top):
    B, S, _ = av.shape
    nq = S // tq
    assert tq % tk == 0
    return pl.pallas_call(
        functools.partial(_dsa_kernel, nq=nq, tq=tq, tk=tk, k_top=k_top),
        grid=(B, nq),
        in_specs=[
            pl.BlockSpec((1, tq, IDX_Q), lambda b, q: (b, q, 0)),
            pl.BlockSpec((1, S, LANES), lambda b, q: (b, 0, 0)),
            pl.BlockSpec((1, tq, LANES), lambda b, q: (b, q, 0)),
            pl.BlockSpec((1, tq, ATT_W), lambda b, q: (b, q, 0)),
            pl.BlockSpec((1, S, ATT_W), lambda b, q: (b, 0, 1)),
            pl.BlockSpec((1, S, ATT_W), lambda b, q: (b, 0, 0)),
            pl.BlockSpec((tk, ATT_W), lambda b, q: (0, 0)),
            pl.BlockSpec((tk, ATT_W), lambda b, q: (0, 0)),
        ],
        out_specs=pl.BlockSpec((1, tq, ATT_W), lambda b, q: (b, q, 0)),
        out_shape=jax.ShapeDtypeStruct((B, S, ATT_W), BF16),
        scratch_shapes=[
            pltpu.VMEM((S // tk, tk, tq), F32),
            pltpu.VMEM((1, tq), F32),
            pltpu.VMEM((ATT_HEADS, tq), F32),
            pltpu.VMEM((ATT_HEADS, tq), F32),
            pltpu.VMEM((ATT_HEADS, ATT_DH, tq), F32),
            pltpu.VMEM((2, ATT_HEADS, tk, tq), F32),
            pltpu.VMEM((2, ATT_HEADS, tq), F32),
        ],
        compiler_params=_params(("parallel", "arbitrary")),
        name="dsa",
    )(iq, ikw, ikw, aqk, aqk, av, akm, avm)


def _merge_kernel(h_ref, yr_ref, ya_ref, gr_ref, ga_ref, wr_ref, wa_ref, wm_ref, o_ref):
    yr = _dot(yr_ref[...], wr_ref[...])
    ya = _dot(ya_ref[...], wa_ref[...])
    merged = _sigmoid(gr_ref[...].astype(F32)) * yr + _sigmoid(ga_ref[...].astype(F32)) * ya
    o_ref[...] = h_ref[...] + _dot(merged.astype(BF16), wm_ref[...])


def _merge(h, yr, ya, gates, wr, wa, wm, *, tm):
    M, D = h.shape
    row = lambda i: (i, 0)
    fixed = lambda i: (0, 0)
    return pl.pallas_call(
        _merge_kernel,
        grid=(M // tm,),
        in_specs=[
            pl.BlockSpec((tm, D), row),
            pl.BlockSpec((tm, RET_V), row),
            pl.BlockSpec((tm, ATT_W), row),
            pl.BlockSpec((tm, D), lambda i: (i, 0)),
            pl.BlockSpec((tm, D), lambda i: (i, 1)),
            pl.BlockSpec((RET_V, D), fixed),
            pl.BlockSpec((ATT_W, D), fixed),
            pl.BlockSpec((D, D), fixed),
        ],
        out_specs=pl.BlockSpec((tm, D), row),
        out_shape=jax.ShapeDtypeStruct((M, D), F32),
        compiler_params=_params(("parallel",)),
        name="merge",
    )(h, yr, ya, gates, gates, wr, wa, wm)


def _rope_tables(pos, d, width):
    inv_freq = ROPE_THETA ** (-jnp.arange(0, d, 2, dtype=F32) / d)
    ang = pos.astype(F32)[:, None] * inv_freq[None, :]
    cos, sin = jnp.cos(ang), jnp.sin(ang)
    cos = jnp.concatenate([cos, cos], axis=-1)
    sin = jnp.concatenate([-sin, sin], axis=-1)
    rep = width // d
    return jnp.tile(cos, (1, rep)), jnp.tile(sin, (1, rep))


def _tile(n, cap):
    t = min(n, cap)
    assert n % t == 0, (n, t)
    return t


class _Plan(NamedTuple):
    rows: int
    rows_wide: int
    rows_merge: int
    cols: int
    ffn_slab: int
    att: int
    chunk: int


def _plan(S, M):
    return _Plan(rows=_tile(S, 4 * MXU_DIM), rows_wide=_tile(M, 8 * MXU_DIM), rows_merge=_tile(M, 2 * MXU_DIM),
                 cols=4 * MXU_DIM, ffn_slab=MXU_DIM, att=_tile(S, MXU_DIM), chunk=_tile(S, 4 * RET_CHUNK))


def kernel(x, meta_tokens, ffn1_norm, ffn1_w_gate, ffn1_w_up, ffn1_w_down, mix_norm, w_in, w_ret_out,
           w_att_out, w_mix_out, ffn2_norm, ffn2_w_gate, ffn2_w_up, ffn2_w_down, final_norm):
    B, S, D = x.shape
    assert ffn1_norm.shape[0] == 1, "single layer only"
    assert meta_tokens.shape[0] == N_META
    k_top = min(TOPK_MAX, S // 4)
    M = B * S
    plan = _plan(S, M)
    bf = lambda a: a.astype(BF16)
    row = lambda a: a.reshape(1, -1)

    w32 = w_in[0].T
    offs = np.cumsum([0, RET_QK, RET_QK, RET_V, RET_V, ATT_W, ATT_W, ATT_W, IDX_Q, IDX_DH, IDX_HEADS, D, D])
    tn = plan.cols
    assert all(int(o) % tn == 0 for o in offs[:8])
    c_rq, c_rk, c_rv, c_rg, c_aq, c_ak, c_av = (int(o) // tn for o in offs[:7])
    w_idx = jnp.pad(bf(w32[offs[7]:offs[10]].T), ((0, 0), (0, LANES - IDX_DH - IDX_HEADS)))
    w_gate = w32[offs[10]:offs[12]]

    pos_meta = jnp.arange(N_META, dtype=jnp.int32)
    pos_real = N_META + jnp.arange(S, dtype=jnp.int32)
    ffn1 = (row(ffn1_norm[0]), bf(ffn1_w_gate[0]), bf(ffn1_w_up[0]), bf(ffn1_w_down[0]), row(mix_norm[0]))
    ffn2 = (row(ffn2_norm[0]), bf(ffn2_w_gate[0]), bf(ffn2_w_up[0]), bf(ffn2_w_down[0]), row(final_norm))
    tf = plan.ffn_slab
    k_scale = RET_DK ** -0.5

    _, un_m = _ffn(meta_tokens.astype(F32), *ffn1, tm=N_META, tf=tf, emit_h=True, n_dtype=BF16)
    pm = dict(tm=N_META, tn=tn)
    rk_m = _proj(un_m, w32, col0=c_rk, n_tiles=RET_QK // tn, mode="rope", d=RET_DK, k_scale=k_scale,
                 tabs=_rope_tables(pos_meta, RET_DK, RET_DK), **pm)
    rv_m = _proj(un_m, w32, col0=c_rv, n_tiles=RET_V // tn, mode="plain", **pm)
    ak_m = _proj(un_m, w32, col0=c_ak, n_tiles=ATT_W // tn, mode="rope", d=ATT_DH,
                 tabs=_rope_tables(pos_meta, ATT_DH, LANES), **pm)
    av_m = _proj(un_m, w32, col0=c_av, n_tiles=ATT_W // tn, mode="plain", **pm)

    tm = plan.rows
    h1, un = _ffn(x.reshape(M, D), *ffn1, tm=tm, tf=tf, emit_h=True, n_dtype=BF16)
    pr = dict(tm=tm, tn=tn)
    pw = dict(tm=plan.rows_wide, tn=tn)
    rqk = _proj(un, w32, col0=c_rq, n_tiles=2 * RET_QK // tn, mode="rope", d=RET_DK, k_scale=k_scale,
                n_q=RET_QK // tn, tabs=_rope_tables(pos_real, RET_DK, RET_DK), **pr)
    rv = _proj(un, w32, col0=c_rv, n_tiles=RET_V // tn, mode="plain", **pw)
    rg = _proj(un, w32, col0=c_rg, n_tiles=RET_V // tn, mode="silu", **pw)
    aqk = _proj(un, w32, col0=c_aq, n_tiles=2 * ATT_W // tn, mode="rope", d=ATT_DH, q_scale=ATT_EXP_SCALE,
                n_q=ATT_W // tn, tabs=_rope_tables(pos_real, ATT_DH, LANES), **pr)
    av = _proj(un, w32, col0=c_av, n_tiles=ATT_W // tn, mode="plain", **pw)
    gates = _proj(un, w_gate, col0=0, n_tiles=2 * D // tn, mode="plain", **pw)
    iq, ikw = _idx_proj(un, w_idx, _rope_tables(pos_real, IDX_DH, LANES), tm=tm)

    heads = jnp.arange(RET_HEADS, dtype=F32)
    log_g = jnp.log1p(-(2.0 ** (-5.0 - heads)))
    C = plan.chunk
    b3 = lambda a: a.reshape(B, S, a.shape[-1])
    yr = _retention(log_g, jnp.exp(log_g * C), b3(rqk), b3(rv), b3(rg), rk_m, rv_m, C=C)
    tk = plan.att
    pad_meta = lambda a: jnp.pad(a, ((0, tk - N_META), (0, 0)))
    ya = _dsa(b3(iq), b3(ikw), b3(aqk), b3(av), pad_meta(ak_m), pad_meta(av_m),
              tq=plan.att, tk=tk, k_top=k_top)

    h2 = _merge(h1, yr.reshape(M, RET_V), ya.reshape(M, ATT_W), gates, bf(w_ret_out[0]), bf(w_att_out[0]),
                bf(w_mix_out[0]), tm=plan.rows_merge)
    (y,) = _ffn(h2, *ffn2, tm=tm, tf=tf, emit_h=False, n_dtype=F32)
    return y.reshape(B, S, D)
```

```python
import functools
from typing import NamedTuple

import jax
import jax.numpy as jnp
import numpy as np
from jax import lax
from jax.experimental import pallas as pl
from jax.experimental.pallas import tpu as pltpu

N_META = 16
RET_HEADS = 4
RET_DK = 256
RET_DV = 512
RET_CHUNK = 128
ATT_HEADS = 8
ATT_DH = 128
IDX_HEADS = 8
IDX_DH = 64
TOPK_MAX = 256
ROPE_THETA = 10000.0
EPS = 1e-6
RET_QK = RET_HEADS * RET_DK
RET_V = RET_HEADS * RET_DV
ATT_W = ATT_HEADS * ATT_DH
IDX_Q = IDX_HEADS * IDX_DH
ATT_EXP_SCALE = (ATT_DH ** -0.5) * float(np.log2(np.e))

LANES = 128
SUBLANES = 8
MXU_DIM = 256
VMEM_LIMIT = 48 * 1024 * 1024
FFN_VMEM_LIMIT = 56 * 1024 * 1024

BF16 = jnp.bfloat16
F32 = jnp.float32
NT_DIMS = (((1,), (1,)), ((), ()))


def _dot(a, b):
    return jnp.dot(a, b, preferred_element_type=F32)


def _dot_nt(a, b):
    return lax.dot_general(a, b, NT_DIMS, preferred_element_type=F32)


def _rms(x, g):
    return x * lax.rsqrt(jnp.mean(x * x, axis=-1, keepdims=True) + EPS) * g


def _sigmoid(x):
    return 0.5 * jnp.tanh(0.5 * x) + 0.5


def _params(sem):
    return pltpu.CompilerParams(dimension_semantics=sem, vmem_limit_bytes=VMEM_LIMIT)


def _ffn_kernel(x_ref, g_ref, wg_ref, wu_ref, wd_ref, g2_ref, *rest, nf, tf, emit_h):
    if emit_h:
        h_ref, n_ref, xn_sc, acc_sc, act_sc = rest
    else:
        n_ref, xn_sc, acc_sc, act_sc = rest

    def down(f):
        rows = pl.ds(pl.multiple_of(f * tf, tf), tf)
        return _dot(act_sc[...], wd_ref[rows, :])

    def up(f):
        cols = pl.ds(pl.multiple_of(f * tf, tf), tf)
        xn = xn_sc[...]
        gate = _dot(xn, wg_ref[:, cols])
        lift = _dot(xn, wu_ref[:, cols])
        act_sc[...] = ((gate * _sigmoid(gate)) * lift).astype(BF16)

    xn_sc[...] = _rms(x_ref[...], g_ref[...]).astype(BF16)
    acc_sc[...] = jnp.zeros_like(acc_sc)
    up(0)

    def slab(f, carry):
        acc_sc[...] += down(f - 1)
        up(f)
        return carry

    lax.fori_loop(1, nf, slab, 0, unroll=True)
    h = x_ref[...] + 0.5 * (acc_sc[...] + down(nf - 1))
    if emit_h:
        h_ref[...] = h
    n_ref[...] = _rms(h, g2_ref[...]).astype(n_ref.dtype)


def _ffn(x, g, wg, wu, wd, g2, *, tm, tf, emit_h, n_dtype):
    M, D = x.shape
    F = wg.shape[1]
    nf = F // tf
    assert nf >= 2
    out_shape = [jax.ShapeDtypeStruct((M, D), n_dtype)]
    out_specs = [pl.BlockSpec((tm, D), lambda i: (i, 0))]
    if emit_h:
        out_shape = [jax.ShapeDtypeStruct((M, D), F32)] + out_shape
        out_specs = [pl.BlockSpec((tm, D), lambda i: (i, 0))] + out_specs
    resident = lambda shape: pl.BlockSpec(shape, lambda i: (0, 0), pipeline_mode=pl.Buffered(1))
    return pl.pallas_call(
        functools.partial(_ffn_kernel, nf=nf, tf=tf, emit_h=emit_h),
        grid=(M // tm,),
        in_specs=[
            pl.BlockSpec((tm, D), lambda i: (i, 0)),
            pl.BlockSpec((1, D), lambda i: (0, 0)),
            resident((D, F)),
            resident((D, F)),
            resident((F, D)),
            pl.BlockSpec((1, D), lambda i: (0, 0)),
        ],
        out_specs=out_specs,
        out_shape=out_shape,
        scratch_shapes=[pltpu.VMEM((tm, D), BF16), pltpu.VMEM((tm, D), F32), pltpu.VMEM((tm, tf), BF16)],
        compiler_params=pltpu.CompilerParams(
            dimension_semantics=("arbitrary",), vmem_limit_bytes=FFN_VMEM_LIMIT),
        name="ffn",
    )(x, g, wg, wu, wd, g2)


def _rot_half(y, d):
    w = y.shape[-1]
    if d >= w:
        return jnp.concatenate([y[:, w // 2:], y[:, : w // 2]], axis=-1)
    if d == LANES:
        return pltpu.roll(y, d // 2, axis=1)
    fwd = pltpu.roll(y, d // 2, axis=1)
    bwd = pltpu.roll(y, w - d // 2, axis=1)
    lane = lax.broadcasted_iota(jnp.int32, y.shape, 1)
    return jnp.where((lane % d) < d // 2, bwd, fwd)


def _rope_tile(y, cos, sin, d):
    tw = cos.shape[-1]
    outs = []
    for t in range(y.shape[-1] // tw):
        yt = y[:, t * tw:(t + 1) * tw]
        outs.append(yt * cos + _rot_half(yt, d) * sin)
    return outs[0] if len(outs) == 1 else jnp.concatenate(outs, axis=-1)


def _proj_kernel(x_ref, w_ref, *rest, mode, d, q_scale, k_scale, n_q):
    if mode == "rope":
        cos_ref, sin_ref, o_ref, w_sc = rest
    else:
        o_ref, w_sc = rest

    @pl.when(pl.program_id(1) == 0)
    def _():
        w_sc[...] = w_ref[...].T.astype(BF16)

    y = _dot(x_ref[...], w_sc[...])
    if mode == "rope":
        y = _rope_tile(y, cos_ref[...], sin_ref[...], d)
        if q_scale != 1.0 or k_scale != 1.0:
            y = y * jnp.where(pl.program_id(0) >= n_q, k_scale, q_scale)
    elif mode == "silu":
        y = y * _sigmoid(y)
    o_ref[...] = y.astype(o_ref.dtype)


def _proj(x, w_t, *, col0, n_tiles, tm, tn, mode, out_dtype=None, tabs=None, d=None, q_scale=1.0,
          k_scale=1.0, n_q=0):
    M, K = x.shape
    in_specs = [
        pl.BlockSpec((tm, K), lambda j, i: (i, 0)),
        pl.BlockSpec((tn, K), lambda j, i: (col0 + j, 0)),
    ]
    args = [x, w_t]
    if mode == "rope":
        cos, sin = tabs
        npos = cos.shape[0] // tm
        tw = cos.shape[1]
        in_specs += [pl.BlockSpec((tm, tw), lambda j, i: (i % npos, 0))] * 2
        args += [cos, sin]
    return pl.pallas_call(
        functools.partial(_proj_kernel, mode=mode, d=d, q_scale=q_scale, k_scale=k_scale, n_q=n_q),
        grid=(n_tiles, M // tm),
        in_specs=in_specs,
        out_specs=pl.BlockSpec((tm, tn), lambda j, i: (i, j)),
        out_shape=jax.ShapeDtypeStruct((M, n_tiles * tn), out_dtype or BF16),
        scratch_shapes=[pltpu.VMEM((K, tn), BF16)],
        compiler_params=_params(("arbitrary", "arbitrary")),
        name="proj_" + mode,
    )(*args)


def _idx_proj_kernel(x_ref, w_ref, cos_ref, sin_ref, q_ref, kw_ref):
    y = _dot(x_ref[...], w_ref[...])
    cos, sin = cos_ref[...], sin_ref[...]
    q_ref[...] = _rope_tile(y[:, :IDX_Q], cos, sin, IDX_DH)
    tail = y[:, IDX_Q:]
    lane = lax.broadcasted_iota(jnp.int32, tail.shape, 1)
    kw_ref[...] = jnp.where(lane < IDX_DH, _rope_tile(tail, cos, sin, IDX_DH), tail)


def _idx_proj(x, w, tabs, *, tm):
    M, K = x.shape
    N = w.shape[1]
    cos, sin = tabs
    npos = cos.shape[0] // tm
    return pl.pallas_call(
        _idx_proj_kernel,
        grid=(M // tm,),
        in_specs=[
            pl.BlockSpec((tm, K), lambda i: (i, 0)),
            pl.BlockSpec((K, N), lambda i: (0, 0)),
            pl.BlockSpec((tm, LANES), lambda i: (i % npos, 0)),
            pl.BlockSpec((tm, LANES), lambda i: (i % npos, 0)),
        ],
        out_specs=[
            pl.BlockSpec((tm, IDX_Q), lambda i: (i, 0)),
            pl.BlockSpec((tm, LANES), lambda i: (i, 0)),
        ],
        out_shape=[
            jax.ShapeDtypeStruct((M, IDX_Q), F32),
            jax.ShapeDtypeStruct((M, LANES), F32),
        ],
        compiler_params=_params(("parallel",)),
        name="proj_idx",
    )(x, w, cos, sin)


def _ret_kernel(lg_ref, gc_ref, q_ref, k_ref, v_ref, g_ref, km_ref, vm_ref, o_ref, st_ref, decay_ref, *, C):
    c = pl.program_id(1)
    dk = lambda h: slice(h * RET_DK, (h + 1) * RET_DK)
    dv = lambda h: slice(h * RET_DV, (h + 1) * RET_DV)

    @pl.when(c == 0)
    def _():
        pm = lax.broadcasted_iota(jnp.int32, (N_META, 1), 0).astype(F32)
        ri = lax.broadcasted_iota(jnp.int32, (C, C), 0)
        ci = lax.broadcasted_iota(jnp.int32, (C, C), 1)
        diff = (ri - ci).astype(F32)
        for h in range(RET_HEADS):
            lg = lg_ref[h]
            zeta_m = jnp.exp(lg * (N_META - 1.0 - pm))
            kz = km_ref[:, dk(h)].astype(F32) * zeta_m
            st_ref[h] = _dot(kz.T.astype(BF16), vm_ref[:, dv(h)])
            decay_ref[h] = jnp.where(diff >= 0, jnp.exp(lg * jnp.maximum(diff, 0.0)), 0.0)

    pos = lax.broadcasted_iota(jnp.int32, (C, 1), 0).astype(F32)
    for h in range(RET_HEADS):
        lg = lg_ref[h]
        q = q_ref[0, :, dk(h)]
        k = k_ref[0, :, dk(h)]
        v = v_ref[0, :, dv(h)]
        xi = jnp.exp(lg * (pos + 1.0))
        zeta = jnp.exp(lg * (C - 1.0 - pos))
        state = st_ref[h]
        inner = _dot((_dot_nt(q, k) * decay_ref[h]).astype(BF16), v)
        cross = _dot(q, state.astype(BF16)) * xi
        y = inner + cross
        y = y * lax.rsqrt(jnp.mean(y * y, axis=-1, keepdims=True) + EPS)
        o_ref[0, :, dv(h)] = (g_ref[0, :, dv(h)] * y).astype(o_ref.dtype)
        kz = k.astype(F32) * zeta
        st_ref[h] = gc_ref[h] * state + _dot(kz.T.astype(BF16), v)


def _retention(lg, gc, rqk, rv, rg, km, vm, *, C):
    B, S, _ = rv.shape
    smem = pl.BlockSpec(memory_space=pltpu.SMEM)
    return pl.pallas_call(
        functools.partial(_ret_kernel, C=C),
        grid=(B, S // C),
        in_specs=[
            smem,
            smem,
            pl.BlockSpec((1, C, RET_QK), lambda b, c: (b, c, 0)),
            pl.BlockSpec((1, C, RET_QK), lambda b, c: (b, c, 1)),
            pl.BlockSpec((1, C, RET_V), lambda b, c: (b, c, 0)),
            pl.BlockSpec((1, C, RET_V), lambda b, c: (b, c, 0)),
            pl.BlockSpec((N_META, RET_QK), lambda b, c: (0, 0)),
            pl.BlockSpec((N_META, RET_V), lambda b, c: (0, 0)),
        ],
        out_specs=pl.BlockSpec((1, C, RET_V), lambda b, c: (b, c, 0)),
        out_shape=jax.ShapeDtypeStruct((B, S, RET_V), BF16),
        scratch_shapes=[pltpu.VMEM((RET_HEADS, RET_DK, RET_DV), F32), pltpu.VMEM((RET_HEADS, C, C), F32)],
        compiler_params=_params(("parallel", "arbitrary")),
        name="retention",
    )(lg, gc, rqk, rqk, rv, rg, km, vm)


TINY = float(np.finfo(np.float32).tiny)
TN_DIMS = (((0,), (0,)), ((), ()))


def _dsa_kernel(iq_ref, ikw_ref, wq_ref, aq_ref, ak_ref, av_ref, akm_ref, avm_ref, o_ref,
                sc_ref, tau_ref, m_ref, l_ref, acc_ref, s_ref, bm_ref, *, nq, tq, tk, k_top):
    qi = pl.program_id(1)
    nkb = (qi + 1) * (tq // tk)
    kf = float(k_top)
    neg_inf = -jnp.inf
    qpos = qi * tq + lax.broadcasted_iota(jnp.int32, (1, tq), 1)

    part_shape = (SUBLANES, tq)

    def fold(x):
        return x.reshape(tk // SUBLANES, SUBLANES, tq).sum(axis=0)

    idx_scale = (IDX_DH ** -0.5) * (IDX_HEADS ** -0.5)
    iq = iq_ref[0].astype(BF16)
    w_t = wq_ref[0].T
    iq_h = [iq[:, h * IDX_DH:(h + 1) * IDX_DH] for h in range(IDX_HEADS)]
    w_h = [w_t[IDX_DH + h:IDX_DH + h + 1, :] for h in range(IDX_HEADS)]

    def score_rows(kb, n_blocks, carry):
        rmin, rmax = carry
        rows = n_blocks * tk
        ik = ikw_ref[0, pl.ds(pl.multiple_of(kb * tk, tk), rows), :][:, :IDX_DH].astype(BF16)
        acc = jnp.zeros((rows, tq), F32)
        for h in range(IDX_HEADS):
            acc = acc + w_h[h] * jnp.maximum(_dot_nt(ik, iq_h[h]), 0.0)
        ok = (lax.broadcasted_iota(jnp.int32, (rows, 1), 0) + kb * tk) <= qpos
        s = jnp.where(ok, acc * idx_scale, neg_inf)
        for i in range(n_blocks):
            sc_ref[kb + i] = s[i * tk:(i + 1) * tk]
        rmin = jnp.minimum(rmin, jnp.min(jnp.where(ok, s, jnp.inf), axis=0, keepdims=True))
        rmax = jnp.maximum(rmax, jnp.max(s, axis=0, keepdims=True))
        return rmin, rmax

    carry = (jnp.full((1, tq), jnp.inf, F32), jnp.full((1, tq), neg_inf, F32))
    carry = lax.fori_loop(0, nkb // 4, lambda j, c: score_rows(4 * j, 4, c), carry)
    carry = lax.cond(nkb % 4 >= 2, lambda c: score_rows((nkb // 4) * 4, 2, c), lambda c: c, carry)
    rmin, rmax = lax.cond(nkb % 2 == 1, lambda c: score_rows(nkb - 1, 1, c), lambda c: c, carry)

    def count(pred):
        def body(kb, c):
            return c + fold(jnp.where(pred(sc_ref[kb]), 1.0, 0.0))
        return jnp.sum(lax.fori_loop(0, nkb, body, jnp.zeros(part_shape, F32)), axis=0, keepdims=True)

    def any_query(flag):
        return jnp.max(jnp.where(flag, 1.0, 0.0))

    c_max = count(lambda s: s >= rmax)
    top_tied = c_max >= kf
    lo0 = jnp.where(top_tied, rmax, rmin)
    cl0 = jnp.where(top_tied, c_max, (qpos + 1).astype(F32))

    def midpoint(lo, hi):
        a = jnp.maximum(jnp.abs(lo), TINY)
        b = jnp.maximum(jnp.abs(hi), TINY)
        geo = jnp.sqrt(a) * jnp.sqrt(b)
        one_sign = (lo >= 0.0) | (hi <= 0.0)
        far = jnp.maximum(a, b) > 4.0 * jnp.minimum(a, b)
        mid = jnp.where(one_sign & far, jnp.where(lo >= 0.0, geo, -geo), 0.5 * lo + 0.5 * hi)
        mid = jnp.where(lo == 0.0, TINY, mid)
        mid = jnp.where(hi == 0.0, -TINY, mid)
        return jnp.where((lo < 0.0) & (hi > 0.0), 0.0, mid)

    def active(lo, hi, cl):
        mid = midpoint(lo, hi)
        return (cl > kf) & (mid > lo) & (mid < hi)

    def go_on(st):
        return st[3] > 0.0

    def threshold(n_blocks, lo, hi, cl):
        blocks = range(n_blocks)

        def over_keys(fn, init, combine):
            acc = init
            for kb in blocks:
                acc = combine(acc, fn(sc_ref[kb]))
            return acc

        def count(pred):
            c = over_keys(lambda s: fold(jnp.where(pred(s), 1.0, 0.0)), jnp.zeros(part_shape, F32), jnp.add)
            return jnp.sum(c, axis=0, keepdims=True)

        def shrink_once(lo, hi, cl):
            mid = midpoint(lo, hi)
            act = active(lo, hi, cl)
            c = count(lambda s: s >= mid)
            up = act & (c >= kf)
            dn = act & (c < kf)
            return jnp.where(up, mid, lo), jnp.where(dn, mid, hi), jnp.where(up, c, cl)

        def shrink(st):
            pending = any_query(active(*st[:3]))
            return (*shrink_once(*shrink_once(*st[:3])), pending)

        lo, hi, cl, _ = lax.while_loop(go_on, shrink, (lo, hi, cl, jnp.float32(1.0)))
        tau_ref[...] = lo
        tie = cl > kf

        @pl.when(any_query(tie) > 0.0)
        def _():
            def max_below(bound):
                part = over_keys(
                    lambda s: jnp.max(jnp.where(s < bound, s, neg_inf).reshape(tk // SUBLANES, SUBLANES, tq), axis=0),
                    jnp.full(part_shape, neg_inf, F32), jnp.maximum)
                return jnp.max(part, axis=0, keepdims=True)

            hi_d = jnp.where(lo >= hi, jnp.inf, hi)
            v0 = max_below(hi_d)
            ge0 = count(lambda s: s >= v0)

            def walk_on(st):
                return st[3] > 0.0

            def walk(st):
                hi_w, v, ge, _ = st
                hi_w = jnp.where(tie & (ge < kf), v, hi_w)
                v = max_below(hi_w)
                ge = count(lambda s: s >= v)
                return hi_w, v, ge, any_query(tie & (ge < kf))

            _, v, _, _ = lax.while_loop(walk_on, walk, (hi_d, v0, ge0, any_query(tie & (ge0 < kf))))
            keep = kf - count(lambda s: s > v)
            ri = lax.broadcasted_iota(jnp.int32, (tk, tk), 0)
            ci = lax.broadcasted_iota(jnp.int32, (tk, tk), 1)
            earlier = jnp.where(ci < ri, 1.0, 0.0).astype(BF16)
            seen = jnp.zeros((1, tq), F32)
            for kb in blocks:
                s = sc_ref[kb]
                eq = tie & (s == v)
                eqf = jnp.where(eq, 1.0, 0.0)
                rank = seen + _dot(earlier, eqf.astype(BF16))
                sc_ref[kb] = jnp.where(eq & (rank >= keep), neg_inf, s)
                seen = seen + jnp.sum(eqf, axis=0, keepdims=True)
            tau_ref[...] = jnp.where(tie, v, lo)

        return jnp.int32(0)

    branches = [functools.partial(threshold, (i + 1) * (tq // tk)) for i in range(nq)]
    lax.switch(qi, branches, lo0, rmax, cl0)

    tau = tau_ref[...]
    meta_bias = jnp.where(lax.broadcasted_iota(jnp.int32, (tk, 1), 0) < N_META, 0.0, neg_inf)

    def head(ref, h, rows=None):
        if rows is None:
            return ref[:, h * ATT_DH:(h + 1) * ATT_DH]
        return ref[0, rows, h * ATT_DH:(h + 1) * ATT_DH]

    def keys_of(kb):
        if kb is None:
            return (lambda h: head(akm_ref, h)), (lambda h: head(avm_ref, h)), meta_bias
        rows = pl.ds(pl.multiple_of(kb * tk, tk), tk)
        bias = jnp.where(sc_ref[kb] >= tau, 0.0, neg_inf)
        return (lambda h: head(ak_ref, h, rows)), (lambda h: head(av_ref, h, rows)), bias

    def score_phase(kb, buf):
        k_of, _, bias = keys_of(kb)
        for h in range(ATT_HEADS):
            s = _dot_nt(k_of(h), head(aq_ref, h, slice(None))) + bias
            s_ref[buf, h] = s
            bm_ref[buf, h:h + 1, :] = jnp.max(s, axis=0, keepdims=True)

    def value_phase(kb, buf):
        _, v_of, _ = keys_of(kb)
        for h in range(ATT_HEADS):
            m_new = jnp.maximum(m_ref[h:h + 1, :], bm_ref[buf, h:h + 1, :])
            p = jnp.exp2(s_ref[buf, h] - m_new)
            p_sum = jnp.sum(p, axis=0, keepdims=True)
            pv = lax.dot_general(v_of(h), p.astype(BF16), TN_DIMS, preferred_element_type=F32)
            alpha = jnp.exp2(m_ref[h:h + 1, :] - m_new)
            l_ref[h:h + 1, :] = alpha * l_ref[h:h + 1, :] + p_sum
            acc_ref[h] = alpha * acc_ref[h] + pv
            m_ref[h:h + 1, :] = m_new

    m_ref[...] = jnp.full_like(m_ref, neg_inf)
    l_ref[...] = jnp.zeros_like(l_ref)
    acc_ref[...] = jnp.zeros_like(acc_ref)
    score_phase(None, 0)

    value_phase(None, 0)
    score_phase(0, 1)

    def att_pair(j, carry):
        kb = 2 * j
        value_phase(kb, 1)
        score_phase(kb + 1, 0)

        value_phase(kb + 1, 0)
        score_phase(jnp.minimum(kb + 2, nkb - 1), 1)
        return carry

    lax.fori_loop(0, nkb // 2, att_pair, 0)

    @pl.when(nkb % 2 == 1)
    def _():
        value_phase(nkb - 1, 1)

    for h in range(ATT_HEADS):
        out_t = acc_ref[h] / l_ref[h:h + 1, :]
        o_ref[0, :, h * ATT_DH:(h + 1) * ATT_DH] = out_t.T.astype(o_ref.dtype)


def _dsa(iq, ikw, aqk, av, akm, avm, *, tq, tk, k_top):
    B, S, _ = av.shape
    nq = S // tq
    assert tq % tk == 0
    return pl.pallas_call(
        functools.partial(_dsa_kernel, nq=nq, tq=tq, tk=tk, k_top=k_top),
        grid=(B, nq),
        in_specs=[
            pl.BlockSpec((1, tq, IDX_Q), lambda b, q: (b, q, 0)),
            pl.BlockSpec((1, S, LANES), lambda b, q: (b, 0, 0)),
            pl.BlockSpec((1, tq, LANES), lambda b, q: (b, q, 0)),
            pl.BlockSpec((1, tq, ATT_W), lambda b, q: (b, q, 0)),
            pl.BlockSpec((1, S, ATT_W), lambda b, q: (b, 0, 1)),
            pl.BlockSpec((1, S, ATT_W), lambda b, q: (b, 0, 0)),
            pl.BlockSpec((tk, ATT_W), lambda b, q: (0, 0)),
            pl.BlockSpec((tk, ATT_W), lambda b, q: (0, 0)),
        ],
        out_specs=pl.BlockSpec((1, tq, ATT_W), lambda b, q: (b, q, 0)),
        out_shape=jax.ShapeDtypeStruct((B, S, ATT_W), BF16),
        scratch_shapes=[
            pltpu.VMEM((S // tk, tk, tq), F32),
            pltpu.VMEM((1, tq), F32),
            pltpu.VMEM((ATT_HEADS, tq), F32),
            pltpu.VMEM((ATT_HEADS, tq), F32),
            pltpu.VMEM((ATT_HEADS, ATT_DH, tq), F32),
            pltpu.VMEM((2, ATT_HEADS, tk, tq), F32),
            pltpu.VMEM((2, ATT_HEADS, tq), F32),
        ],
        compiler_params=_params(("parallel", "arbitrary")),
        name="dsa",
    )(iq, ikw, ikw, aqk, aqk, av, akm, avm)


def _merge_kernel(h_ref, yr_ref, ya_ref, gr_ref, ga_ref, wr_ref, wa_ref, wm_ref, o_ref):
    yr = _dot(yr_ref[...], wr_ref[...])
    ya = _dot(ya_ref[...], wa_ref[...])
    merged = _sigmoid(gr_ref[...].astype(F32)) * yr + _sigmoid(ga_ref[...].astype(F32)) * ya
    o_ref[...] = h_ref[...] + _dot(merged.astype(BF16), wm_ref[...])


def _merge(h, yr, ya, gates, wr, wa, wm, *, tm):
    M, D = h.shape
    row = lambda i: (i, 0)
    fixed = lambda i: (0, 0)
    return pl.pallas_call(
        _merge_kernel,
        grid=(M // tm,),
        in_specs=[
            pl.BlockSpec((tm, D), row),
            pl.BlockSpec((tm, RET_V), row),
            pl.BlockSpec((tm, ATT_W), row),
            pl.BlockSpec((tm, D), lambda i: (i, 0)),
            pl.BlockSpec((tm, D), lambda i: (i, 1)),
            pl.BlockSpec((RET_V, D), fixed),
            pl.BlockSpec((ATT_W, D), fixed),
            pl.BlockSpec((D, D), fixed),
        ],
        out_specs=pl.BlockSpec((tm, D), row),
        out_shape=jax.ShapeDtypeStruct((M, D), F32),
        compiler_params=_params(("parallel",)),
        name="merge",
    )(h, yr, ya, gates, gates, wr, wa, wm)


def _rope_tables(pos, d, width):
    inv_freq = ROPE_THETA ** (-jnp.arange(0, d, 2, dtype=F32) / d)
    ang = pos.astype(F32)[:, None] * inv_freq[None, :]
    cos, sin = jnp.cos(ang), jnp.sin(ang)
    cos = jnp.concatenate([cos, cos], axis=-1)
    sin = jnp.concatenate([-sin, sin], axis=-1)
    rep = width // d
    return jnp.tile(cos, (1, rep)), jnp.tile(sin, (1, rep))


def _tile(n, cap):
    t = min(n, cap)
    assert n % t == 0, (n, t)
    return t


class _Plan(NamedTuple):
    rows: int
    rows_wide: int
    rows_merge: int
    cols: int
    ffn_slab: int
    att: int
    chunk: int


def _plan(S, M):
    return _Plan(rows=_tile(S, 4 * MXU_DIM), rows_wide=_tile(M, 8 * MXU_DIM), rows_merge=_tile(M, 2 * MXU_DIM),
                 cols=4 * MXU_DIM, ffn_slab=MXU_DIM, att=_tile(S, MXU_DIM), chunk=_tile(S, 4 * RET_CHUNK))


def kernel(x, meta_tokens, ffn1_norm, ffn1_w_gate, ffn1_w_up, ffn1_w_down, mix_norm, w_in, w_ret_out,
           w_att_out, w_mix_out, ffn2_norm, ffn2_w_gate, ffn2_w_up, ffn2_w_down, final_norm):
    B, S, D = x.shape
    assert ffn1_norm.shape[0] == 1, "single layer only"
    assert meta_tokens.shape[0] == N_META
    k_top = min(TOPK_MAX, S // 4)
    M = B * S
    plan = _plan(S, M)
    bf = lambda a: a.astype(BF16)
    row = lambda a: a.reshape(1, -1)

    w32 = w_in[0].T
    offs = np.cumsum([0, RET_QK, RET_QK, RET_V, RET_V, ATT_W, ATT_W, ATT_W, IDX_Q, IDX_DH, IDX_HEADS, D, D])
    tn = plan.cols
    assert all(int(o) % tn == 0 for o in offs[:8])
    c_rq, c_rk, c_rv, c_rg, c_aq, c_ak, c_av = (int(o) // tn for o in offs[:7])
    w_idx = jnp.pad(bf(w32[offs[7]:offs[10]].T), ((0, 0), (0, LANES - IDX_DH - IDX_HEADS)))
    w_gate = w32[offs[10]:offs[12]]

    pos_meta = jnp.arange(N_META, dtype=jnp.int32)
    pos_real = N_META + jnp.arange(S, dtype=jnp.int32)
    ffn1 = (row(ffn1_norm[0]), bf(ffn1_w_gate[0]), bf(ffn1_w_up[0]), bf(ffn1_w_down[0]), row(mix_norm[0]))
    ffn2 = (row(ffn2_norm[0]), bf(ffn2_w_gate[0]), bf(ffn2_w_up[0]), bf(ffn2_w_down[0]), row(final_norm))
    tf = plan.ffn_slab
    k_scale = RET_DK ** -0.5

    _, un_m = _ffn(meta_tokens.astype(F32), *ffn1, tm=N_META, tf=tf, emit_h=True, n_dtype=BF16)
    pm = dict(tm=N_META, tn=tn)
    rk_m = _proj(un_m, w32, col0=c_rk, n_tiles=RET_QK // tn, mode="rope", d=RET_DK, k_scale=k_scale,
                 tabs=_rope_tables(pos_meta, RET_DK, RET_DK), **pm)
    rv_m = _proj(un_m, w32, col0=c_rv, n_tiles=RET_V // tn, mode="plain", **pm)
    ak_m = _proj(un_m, w32, col0=c_ak, n_tiles=ATT_W // tn, mode="rope", d=ATT_DH,
                 tabs=_rope_tables(pos_meta, ATT_DH, LANES), **pm)
    av_m = _proj(un_m, w32, col0=c_av, n_tiles=ATT_W // tn, mode="plain", **pm)

    tm = plan.rows
    h1, un = _ffn(x.reshape(M, D), *ffn1, tm=tm, tf=tf, emit_h=True, n_dtype=BF16)
    pr = dict(tm=tm, tn=tn)
    pw = dict(tm=plan.rows_wide, tn=tn)
    rqk = _proj(un, w32, col0=c_rq, n_tiles=2 * RET_QK // tn, mode="rope", d=RET_DK, k_scale=k_scale,
                n_q=RET_QK // tn, tabs=_rope_tables(pos_real, RET_DK, RET_DK), **pr)
    rv = _proj(un, w32, col0=c_rv, n_tiles=RET_V // tn, mode="plain", **pw)
    rg = _proj(un, w32, col0=c_rg, n_tiles=RET_V // tn, mode="silu", **pw)
    aqk = _proj(un, w32, col0=c_aq, n_tiles=2 * ATT_W // tn, mode="rope", d=ATT_DH, q_scale=ATT_EXP_SCALE,
                n_q=ATT_W // tn, tabs=_rope_tables(pos_real, ATT_DH, LANES), **pr)
    av = _proj(un, w32, col0=c_av, n_tiles=ATT_W // tn, mode="plain", **pw)
    gates = _proj(un, w_gate, col0=0, n_tiles=2 * D // tn, mode="plain", **pw)
    iq, ikw = _idx_proj(un, w_idx, _rope_tables(pos_real, IDX_DH, LANES), tm=tm)

    heads = jnp.arange(RET_HEADS, dtype=F32)
    log_g = jnp.log1p(-(2.0 ** (-5.0 - heads)))
    C = plan.chunk
    b3 = lambda a: a.reshape(B, S, a.shape[-1])
    yr = _retention(log_g, jnp.exp(log_g * C), b3(rqk), b3(rv), b3(rg), rk_m, rv_m, C=C)
    tk = plan.att
    pad_meta = lambda a: jnp.pad(a, ((0, tk - N_META), (0, 0)))
    ya = _dsa(b3(iq), b3(ikw), b3(aqk), b3(av), pad_meta(ak_m), pad_meta(av_m),
              tq=plan.att, tk=tk, k_top=k_top)

    h2 = _merge(h1, yr.reshape(M, RET_V), ya.reshape(M, ATT_W), gates, bf(w_ret_out[0]), bf(w_att_out[0]),
                bf(w_mix_out[0]), tm=plan.rows_merge)
    (y,) = _ffn(h2, *ffn2, tm=tm, tf=tf, emit_h=False, n_dtype=F32)
    return y.reshape(B, S, D)
```

```python
import functools
from typing import NamedTuple

import jax
import jax.numpy as jnp
import numpy as np
from jax import lax
from jax.experimental import pallas as pl
from jax.experimental.pallas import tpu as pltpu

N_META = 16
RET_HEADS = 4
RET_DK = 256
RET_DV = 512
RET_CHUNK = 128
ATT_HEADS = 8
ATT_DH = 128
IDX_HEADS = 8
IDX_DH = 64
TOPK_MAX = 256
ROPE_THETA = 10000.0
EPS = 1e-6
RET_QK = RET_HEADS * RET_DK
RET_V = RET_HEADS * RET_DV
ATT_W = ATT_HEADS * ATT_DH
IDX_Q = IDX_HEADS * IDX_DH
ATT_EXP_SCALE = (ATT_DH ** -0.5) * float(np.log2(np.e))

LANES = 128
SUBLANES = 8
MXU_DIM = 256
PROJ_SUB_ROWS = MXU_DIM
VMEM_LIMIT = 48 * 1024 * 1024
FFN_VMEM_LIMIT = 56 * 1024 * 1024

BF16 = jnp.bfloat16
F32 = jnp.float32
NT_DIMS = (((1,), (1,)), ((), ()))


def _dot(a, b):
    return jnp.dot(a, b, preferred_element_type=F32)


def _dot_nt(a, b):
    return lax.dot_general(a, b, NT_DIMS, preferred_element_type=F32)


def _rms(x, g):
    return x * lax.rsqrt(jnp.mean(x * x, axis=-1, keepdims=True) + EPS) * g


def _sigmoid(x):
    return 0.5 * jnp.tanh(0.5 * x) + 0.5


def _params(sem):
    return pltpu.CompilerParams(dimension_semantics=sem, vmem_limit_bytes=VMEM_LIMIT)


def _ffn_kernel(x_ref, g_ref, wg_ref, wu_ref, wd_ref, g2_ref, *rest, nf, tf, emit_h):
    if emit_h:
        h_ref, n_ref, xn_sc, acc_sc, act_sc = rest
    else:
        n_ref, xn_sc, acc_sc, act_sc = rest

    def down(f):
        rows = pl.ds(pl.multiple_of(f * tf, tf), tf)
        return _dot(act_sc[...], wd_ref[rows, :])

    def up(f):
        cols = pl.ds(pl.multiple_of(f * tf, tf), tf)
        xn = xn_sc[...]
        gate = _dot(xn, wg_ref[:, cols])
        lift = _dot(xn, wu_ref[:, cols])
        act_sc[...] = ((gate * _sigmoid(gate)) * lift).astype(BF16)

    xn_sc[...] = _rms(x_ref[...], g_ref[...]).astype(BF16)
    acc_sc[...] = jnp.zeros_like(acc_sc)
    up(0)

    def slab(f, carry):
        acc_sc[...] += down(f - 1)
        up(f)
        return carry

    lax.fori_loop(1, nf, slab, 0, unroll=True)
    h = x_ref[...] + 0.5 * (acc_sc[...] + down(nf - 1))
    if emit_h:
        h_ref[...] = h
    n_ref[...] = _rms(h, g2_ref[...]).astype(n_ref.dtype)


def _ffn(x, g, wg, wu, wd, g2, *, tm, tf, emit_h, n_dtype):
    M, D = x.shape
    F = wg.shape[1]
    nf = F // tf
    assert nf >= 2
    out_shape = [jax.ShapeDtypeStruct((M, D), n_dtype)]
    out_specs = [pl.BlockSpec((tm, D), lambda i: (i, 0))]
    if emit_h:
        out_shape = [jax.ShapeDtypeStruct((M, D), F32)] + out_shape
        out_specs = [pl.BlockSpec((tm, D), lambda i: (i, 0))] + out_specs
    resident = lambda shape: pl.BlockSpec(shape, lambda i: (0, 0), pipeline_mode=pl.Buffered(1))
    return pl.pallas_call(
        functools.partial(_ffn_kernel, nf=nf, tf=tf, emit_h=emit_h),
        grid=(M // tm,),
        in_specs=[
            pl.BlockSpec((tm, D), lambda i: (i, 0)),
            pl.BlockSpec((1, D), lambda i: (0, 0)),
            resident((D, F)),
            resident((D, F)),
            resident((F, D)),
            pl.BlockSpec((1, D), lambda i: (0, 0)),
        ],
        out_specs=out_specs,
        out_shape=out_shape,
        scratch_shapes=[pltpu.VMEM((tm, D), BF16), pltpu.VMEM((tm, D), F32), pltpu.VMEM((tm, tf), BF16)],
        compiler_params=pltpu.CompilerParams(
            dimension_semantics=("arbitrary",), vmem_limit_bytes=FFN_VMEM_LIMIT),
        name="ffn",
    )(x, g, wg, wu, wd, g2)


def _rot_half(y, d):
    w = y.shape[-1]
    if d >= w:
        return jnp.concatenate([y[:, w // 2:], y[:, : w // 2]], axis=-1)
    if d == LANES:
        return pltpu.roll(y, d // 2, axis=1)
    fwd = pltpu.roll(y, d // 2, axis=1)
    bwd = pltpu.roll(y, w - d // 2, axis=1)
    lane = lax.broadcasted_iota(jnp.int32, y.shape, 1)
    return jnp.where((lane % d) < d // 2, bwd, fwd)


def _rope_tile(y, cos, sin, d):
    tw = cos.shape[-1]
    outs = []
    for t in range(y.shape[-1] // tw):
        yt = y[:, t * tw:(t + 1) * tw]
        outs.append(yt * cos + _rot_half(yt, d) * sin)
    return outs[0] if len(outs) == 1 else jnp.concatenate(outs, axis=-1)


def _proj_kernel(x_ref, w_ref, *rest, mode, d, q_scale, k_scale, n_q):
    if mode == "rope":
        cos_ref, sin_ref, o_ref, w_sc = rest
    else:
        o_ref, w_sc = rest

    @pl.when(pl.program_id(1) == 0)
    def _():
        w_sc[...] = w_ref[...].T.astype(BF16)

    tm = x_ref.shape[0]
    sub = min(tm, PROJ_SUB_ROWS)
    for r0 in range(0, tm, sub):
        rows = slice(r0, r0 + sub)
        y = _dot(x_ref[rows, :], w_sc[...])
        if mode == "rope":
            y = _rope_tile(y, cos_ref[rows, :], sin_ref[rows, :], d)
            if q_scale != 1.0 or k_scale != 1.0:
                y = y * jnp.where(pl.program_id(0) >= n_q, k_scale, q_scale)
        elif mode == "silu":
            y = y * _sigmoid(y)
        o_ref[rows, :] = y.astype(o_ref.dtype)


def _proj(x, w_t, *, col0, n_tiles, tm, tn, mode, out_dtype=None, tabs=None, d=None, q_scale=1.0,
          k_scale=1.0, n_q=0):
    M, K = x.shape
    in_specs = [
        pl.BlockSpec((tm, K), lambda j, i: (i, 0)),
        pl.BlockSpec((tn, K), lambda j, i: (col0 + j, 0)),
    ]
    args = [x, w_t]
    if mode == "rope":
        cos, sin = tabs
        npos = cos.shape[0] // tm
        tw = cos.shape[1]
        in_specs += [pl.BlockSpec((tm, tw), lambda j, i: (i % npos, 0))] * 2
        args += [cos, sin]
    return pl.pallas_call(
        functools.partial(_proj_kernel, mode=mode, d=d, q_scale=q_scale, k_scale=k_scale, n_q=n_q),
        grid=(n_tiles, M // tm),
        in_specs=in_specs,
        out_specs=pl.BlockSpec((tm, tn), lambda j, i: (i, j)),
        out_shape=jax.ShapeDtypeStruct((M, n_tiles * tn), out_dtype or BF16),
        scratch_shapes=[pltpu.VMEM((K, tn), BF16)],
        compiler_params=_params(("arbitrary", "arbitrary")),
        name="proj_" + mode,
    )(*args)


def _idx_proj_kernel(x_ref, w_ref, cos_ref, sin_ref, q_ref, kw_ref):
    y = _dot(x_ref[...], w_ref[...])
    cos, sin = cos_ref[...], sin_ref[...]
    q_ref[...] = _rope_tile(y[:, :IDX_Q], cos, sin, IDX_DH)
    tail = y[:, IDX_Q:]
    lane = lax.broadcasted_iota(jnp.int32, tail.shape, 1)
    kw_ref[...] = jnp.where(lane < IDX_DH, _rope_tile(tail, cos, sin, IDX_DH), tail)


def _idx_proj(x, w, tabs, *, tm):
    M, K = x.shape
    N = w.shape[1]
    cos, sin = tabs
    npos = cos.shape[0] // tm
    return pl.pallas_call(
        _idx_proj_kernel,
        grid=(M // tm,),
        in_specs=[
            pl.BlockSpec((tm, K), lambda i: (i, 0)),
            pl.BlockSpec((K, N), lambda i: (0, 0)),
            pl.BlockSpec((tm, LANES), lambda i: (i % npos, 0)),
            pl.BlockSpec((tm, LANES), lambda i: (i % npos, 0)),
        ],
        out_specs=[
            pl.BlockSpec((tm, IDX_Q), lambda i: (i, 0)),
            pl.BlockSpec((tm, LANES), lambda i: (i, 0)),
        ],
        out_shape=[
            jax.ShapeDtypeStruct((M, IDX_Q), F32),
            jax.ShapeDtypeStruct((M, LANES), F32),
        ],
        compiler_params=_params(("parallel",)),
        name="proj_idx",
    )(x, w, cos, sin)


def _ret_kernel(lg_ref, gc_ref, q_ref, k_ref, v_ref, g_ref, km_ref, vm_ref, o_ref, st_ref, decay_ref, *, C):
    c = pl.program_id(1)
    dk = lambda h: slice(h * RET_DK, (h + 1) * RET_DK)
    dv = lambda h: slice(h * RET_DV, (h + 1) * RET_DV)

    @pl.when(c == 0)
    def _():
        pm = lax.broadcasted_iota(jnp.int32, (N_META, 1), 0).astype(F32)
        ri = lax.broadcasted_iota(jnp.int32, (C, C), 0)
        ci = lax.broadcasted_iota(jnp.int32, (C, C), 1)
        diff = (ri - ci).astype(F32)
        for h in range(RET_HEADS):
            lg = lg_ref[h]
            zeta_m = jnp.exp(lg * (N_META - 1.0 - pm))
            kz = km_ref[:, dk(h)].astype(F32) * zeta_m
            st_ref[h] = _dot(kz.T.astype(BF16), vm_ref[:, dv(h)])
            decay_ref[h] = jnp.where(diff >= 0, jnp.exp(lg * jnp.maximum(diff, 0.0)), 0.0)

    pos = lax.broadcasted_iota(jnp.int32, (C, 1), 0).astype(F32)
    for h in range(RET_HEADS):
        lg = lg_ref[h]
        q = q_ref[0, :, dk(h)]
        k = k_ref[0, :, dk(h)]
        v = v_ref[0, :, dv(h)]
        xi = jnp.exp(lg * (pos + 1.0))
        zeta = jnp.exp(lg * (C - 1.0 - pos))
        state = st_ref[h]
        inner = _dot((_dot_nt(q, k) * decay_ref[h]).astype(BF16), v)
        cross = _dot(q, state.astype(BF16)) * xi
        y = inner + cross
        y = y * lax.rsqrt(jnp.mean(y * y, axis=-1, keepdims=True) + EPS)
        o_ref[0, :, dv(h)] = (g_ref[0, :, dv(h)] * y).astype(o_ref.dtype)
        kz = k.astype(F32) * zeta
        st_ref[h] = gc_ref[h] * state + _dot(kz.T.astype(BF16), v)


def _retention(lg, gc, rqk, rv, rg, km, vm, *, C):
    B, S, _ = rv.shape
    smem = pl.BlockSpec(memory_space=pltpu.SMEM)
    return pl.pallas_call(
        functools.partial(_ret_kernel, C=C),
        grid=(B, S // C),
        in_specs=[
            smem,
            smem,
            pl.BlockSpec((1, C, RET_QK), lambda b, c: (b, c, 0)),
            pl.BlockSpec((1, C, RET_QK), lambda b, c: (b, c, 1)),
            pl.BlockSpec((1, C, RET_V), lambda b, c: (b, c, 0)),
            pl.BlockSpec((1, C, RET_V), lambda b, c: (b, c, 0)),
            pl.BlockSpec((N_META, RET_QK), lambda b, c: (0, 0)),
            pl.BlockSpec((N_META, RET_V), lambda b, c: (0, 0)),
        ],
        out_specs=pl.BlockSpec((1, C, RET_V), lambda b, c: (b, c, 0)),
        out_shape=jax.ShapeDtypeStruct((B, S, RET_V), BF16),
        scratch_shapes=[pltpu.VMEM((RET_HEADS, RET_DK, RET_DV), F32), pltpu.VMEM((RET_HEADS, C, C), F32)],
        compiler_params=_params(("parallel", "arbitrary")),
        name="retention",
    )(lg, gc, rqk, rqk, rv, rg, km, vm)


TINY = float(np.finfo(np.float32).tiny)
TN_DIMS = (((0,), (0,)), ((), ()))


def _dsa_kernel(iq_ref, ikw_ref, wq_ref, aq_ref, ak_ref, av_ref, akm_ref, avm_ref, o_ref,
                sc_ref, tau_ref, m_ref, l_ref, acc_ref, s_ref, bm_ref, *, nq, tq, tk, k_top):
    qi = pl.program_id(1)
    nkb = (qi + 1) * (tq // tk)
    kf = float(k_top)
    neg_inf = -jnp.inf
    qpos = qi * tq + lax.broadcasted_iota(jnp.int32, (1, tq), 1)

    part_shape = (SUBLANES, tq)

    def fold(x):
        return x.reshape(tk // SUBLANES, SUBLANES, tq).sum(axis=0)

    idx_scale = (IDX_DH ** -0.5) * (IDX_HEADS ** -0.5)
    iq = iq_ref[0].astype(BF16)
    w_t = wq_ref[0].T
    iq_h = [iq[:, h * IDX_DH:(h + 1) * IDX_DH] for h in range(IDX_HEADS)]
    w_h = [w_t[IDX_DH + h:IDX_DH + h + 1, :] for h in range(IDX_HEADS)]

    def score_rows(kb, n_blocks, carry):
        rmin, rmax = carry
        rows = n_blocks * tk
        ik = ikw_ref[0, pl.ds(pl.multiple_of(kb * tk, tk), rows), :][:, :IDX_DH].astype(BF16)
        acc = jnp.zeros((rows, tq), F32)
        for h in range(IDX_HEADS):
            acc = acc + w_h[h] * jnp.maximum(_dot_nt(ik, iq_h[h]), 0.0)
        ok = (lax.broadcasted_iota(jnp.int32, (rows, 1), 0) + kb * tk) <= qpos
        s = jnp.where(ok, acc * idx_scale, neg_inf)
        for i in range(n_blocks):
            sc_ref[kb + i] = s[i * tk:(i + 1) * tk]
        rmin = jnp.minimum(rmin, jnp.min(jnp.where(ok, s, jnp.inf), axis=0, keepdims=True))
        rmax = jnp.maximum(rmax, jnp.max(s, axis=0, keepdims=True))
        return rmin, rmax

    carry = (jnp.full((1, tq), jnp.inf, F32), jnp.full((1, tq), neg_inf, F32))
    carry = lax.fori_loop(0, nkb // 4, lambda j, c: score_rows(4 * j, 4, c), carry)
    carry = lax.cond(nkb % 4 >= 2, lambda c: score_rows((nkb // 4) * 4, 2, c), lambda c: c, carry)
    rmin, rmax = lax.cond(nkb % 2 == 1, lambda c: score_rows(nkb - 1, 1, c), lambda c: c, carry)

    def count(pred):
        def body(kb, c):
            return c + fold(jnp.where(pred(sc_ref[kb]), 1.0, 0.0))
        return jnp.sum(lax.fori_loop(0, nkb, body, jnp.zeros(part_shape, F32)), axis=0, keepdims=True)

    def any_query(flag):
        return jnp.max(jnp.where(flag, 1.0, 0.0))

    c_max = count(lambda s: s >= rmax)
    top_tied = c_max >= kf
    lo0 = jnp.where(top_tied, rmax, rmin)
    cl0 = jnp.where(top_tied, c_max, (qpos + 1).astype(F32))

    def midpoint(lo, hi):
        a = jnp.maximum(jnp.abs(lo), TINY)
        b = jnp.maximum(jnp.abs(hi), TINY)
        geo = jnp.sqrt(a) * jnp.sqrt(b)
        one_sign = (lo >= 0.0) | (hi <= 0.0)
        far = jnp.maximum(a, b) > 4.0 * jnp.minimum(a, b)
        mid = jnp.where(one_sign & far, jnp.where(lo >= 0.0, geo, -geo), 0.5 * lo + 0.5 * hi)
        mid = jnp.where(lo == 0.0, TINY, mid)
        mid = jnp.where(hi == 0.0, -TINY, mid)
        return jnp.where((lo < 0.0) & (hi > 0.0), 0.0, mid)

    def active(lo, hi, cl):
        mid = midpoint(lo, hi)
        return (cl > kf) & (mid > lo) & (mid < hi)

    def go_on(st):
        return st[3] > 0.0

    def threshold(n_blocks, lo, hi, cl):
        blocks = range(n_blocks)

        def over_keys(fn, init, combine):
            acc = init
            for kb in blocks:
                acc = combine(acc, fn(sc_ref[kb]))
            return acc

        def count(pred):
            c = over_keys(lambda s: fold(jnp.where(pred(s), 1.0, 0.0)), jnp.zeros(part_shape, F32), jnp.add)
            return jnp.sum(c, axis=0, keepdims=True)

        def shrink_once(lo, hi, cl):
            mid = midpoint(lo, hi)
            act = active(lo, hi, cl)
            c = count(lambda s: s >= mid)
            up = act & (c >= kf)
            dn = act & (c < kf)
            return jnp.where(up, mid, lo), jnp.where(dn, mid, hi), jnp.where(up, c, cl)

        def shrink(st):
            pending = any_query(active(*st[:3]))
            return (*shrink_once(*shrink_once(*st[:3])), pending)

        lo, hi, cl, _ = lax.while_loop(go_on, shrink, (lo, hi, cl, jnp.float32(1.0)))
        tau_ref[...] = lo
        tie = cl > kf

        @pl.when(any_query(tie) > 0.0)
        def _():
            def max_below(bound):
                part = over_keys(
                    lambda s: jnp.max(jnp.where(s < bound, s, neg_inf).reshape(tk // SUBLANES, SUBLANES, tq), axis=0),
                    jnp.full(part_shape, neg_inf, F32), jnp.maximum)
                return jnp.max(part, axis=0, keepdims=True)

            hi_d = jnp.where(lo >= hi, jnp.inf, hi)
            v0 = max_below(hi_d)
            ge0 = count(lambda s: s >= v0)

            def walk_on(st):
                return st[3] > 0.0

            def walk(st):
                hi_w, v, ge, _ = st
                hi_w = jnp.where(tie & (ge < kf), v, hi_w)
                v = max_below(hi_w)
                ge = count(lambda s: s >= v)
                return hi_w, v, ge, any_query(tie & (ge < kf))

            _, v, _, _ = lax.while_loop(walk_on, walk, (hi_d, v0, ge0, any_query(tie & (ge0 < kf))))
            keep = kf - count(lambda s: s > v)
            ri = lax.broadcasted_iota(jnp.int32, (tk, tk), 0)
            ci = lax.broadcasted_iota(jnp.int32, (tk, tk), 1)
            earlier = jnp.where(ci < ri, 1.0, 0.0).astype(BF16)
            seen = jnp.zeros((1, tq), F32)
            for kb in blocks:
                s = sc_ref[kb]
                eq = tie & (s == v)
                eqf = jnp.where(eq, 1.0, 0.0)
                rank = seen + _dot(earlier, eqf.astype(BF16))
                sc_ref[kb] = jnp.where(eq & (rank >= keep), neg_inf, s)
                seen = seen + jnp.sum(eqf, axis=0, keepdims=True)
            tau_ref[...] = jnp.where(tie, v, lo)

        return jnp.int32(0)

    branches = [functools.partial(threshold, (i + 1) * (tq // tk)) for i in range(nq)]
    lax.switch(qi, branches, lo0, rmax, cl0)

    tau = tau_ref[...]
    meta_bias = jnp.where(lax.broadcasted_iota(jnp.int32, (tk, 1), 0) < N_META, 0.0, neg_inf)

    def head(ref, h, rows=None):
        if rows is None:
            return ref[:, h * ATT_DH:(h + 1) * ATT_DH]
        return ref[0, rows, h * ATT_DH:(h + 1) * ATT_DH]

    def keys_of(kb):
        if kb is None:
            return (lambda h: head(akm_ref, h)), (lambda h: head(avm_ref, h)), meta_bias
        rows = pl.ds(pl.multiple_of(kb * tk, tk), tk)
        bias = jnp.where(sc_ref[kb] >= tau, 0.0, neg_inf)
        return (lambda h: head(ak_ref, h, rows)), (lambda h: head(av_ref, h, rows)), bias

    def score_phase(kb, buf):
        k_of, _, bias = keys_of(kb)
        for h in range(ATT_HEADS):
            s = _dot_nt(k_of(h), head(aq_ref, h, slice(None))) + bias
            s_ref[buf, h] = s
            bm_ref[buf, h:h + 1, :] = jnp.max(s, axis=0, keepdims=True)

    def value_phase(kb, buf):
        _, v_of, _ = keys_of(kb)
        for h in range(ATT_HEADS):
            m_new = jnp.maximum(m_ref[h:h + 1, :], bm_ref[buf, h:h + 1, :])
            p = jnp.exp2(s_ref[buf, h] - m_new)
            p_sum = jnp.sum(p, axis=0, keepdims=True)
            pv = lax.dot_general(v_of(h), p.astype(BF16), TN_DIMS, preferred_element_type=F32)
            alpha = jnp.exp2(m_ref[h:h + 1, :] - m_new)
            l_ref[h:h + 1, :] = alpha * l_ref[h:h + 1, :] + p_sum
            acc_ref[h] = alpha * acc_ref[h] + pv
            m_ref[h:h + 1, :] = m_new

    m_ref[...] = jnp.full_like(m_ref, neg_inf)
    l_ref[...] = jnp.zeros_like(l_ref)
    acc_ref[...] = jnp.zeros_like(acc_ref)
    score_phase(None, 0)

    value_phase(None, 0)
    score_phase(0, 1)

    def att_pair(j, carry):
        kb = 2 * j
        value_phase(kb, 1)
        score_phase(kb + 1, 0)

        value_phase(kb + 1, 0)
        score_phase(jnp.minimum(kb + 2, nkb - 1), 1)
        return carry

    lax.fori_loop(0, nkb // 2, att_pair, 0)

    @pl.when(nkb % 2 == 1)
    def _():
        value_phase(nkb - 1, 1)

    for h in range(ATT_HEADS):
        out_t = acc_ref[h] / l_ref[h:h + 1, :]
        o_ref[0, :, h * ATT_DH:(h + 1) * ATT_DH] = out_t.T.astype(o_ref.dtype)


def _dsa(iq, ikw, aqk, av, akm, avm, *, tq, tk, k_top):
    B, S, _ = av.shape
    nq = S // tq
    assert tq % tk == 0
    return pl.pallas_call(
        functools.partial(_dsa_kernel, nq=nq, tq=tq, tk=tk, k_top=k_top),
        grid=(B, nq),
        in_specs=[
            pl.BlockSpec((1, tq, IDX_Q), lambda b, q: (b, q, 0)),
            pl.BlockSpec((1, S, LANES), lambda b, q: (b, 0, 0)),
            pl.BlockSpec((1, tq, LANES), lambda b, q: (b, q, 0)),
            pl.BlockSpec((1, tq, ATT_W), lambda b, q: (b, q, 0)),
            pl.BlockSpec((1, S, ATT_W), lambda b, q: (b, 0, 1)),
            pl.BlockSpec((1, S, ATT_W), lambda b, q: (b, 0, 0)),
            pl.BlockSpec((tk, ATT_W), lambda b, q: (0, 0)),
            pl.BlockSpec((tk, ATT_W), lambda b, q: (0, 0)),
        ],
        out_specs=pl.BlockSpec((1, tq, ATT_W), lambda b, q: (b, q, 0)),
        out_shape=jax.ShapeDtypeStruct((B, S, ATT_W), BF16),
        scratch_shapes=[
            pltpu.VMEM((S // tk, tk, tq), F32),
            pltpu.VMEM((1, tq), F32),
            pltpu.VMEM((ATT_HEADS, tq), F32),
            pltpu.VMEM((ATT_HEADS, tq), F32),
            pltpu.VMEM((ATT_HEADS, ATT_DH, tq), F32),
            pltpu.VMEM((2, ATT_HEADS, tk, tq), F32),
            pltpu.VMEM((2, ATT_HEADS, tq), F32),
        ],
        compiler_params=_params(("parallel", "arbitrary")),
        name="dsa",
    )(iq, ikw, ikw, aqk, aqk, av, akm, avm)


def _merge_kernel(h_ref, yr_ref, ya_ref, gr_ref, ga_ref, wr_ref, wa_ref, wm_ref, o_ref):
    yr = _dot(yr_ref[...], wr_ref[...])
    ya = _dot(ya_ref[...], wa_ref[...])
    merged = _sigmoid(gr_ref[...].astype(F32)) * yr + _sigmoid(ga_ref[...].astype(F32)) * ya
    o_ref[...] = h_ref[...] + _dot(merged.astype(BF16), wm_ref[...])


def _merge(h, yr, ya, gates, wr, wa, wm, *, tm):
    M, D = h.shape
    row = lambda i: (i, 0)
    fixed = lambda i: (0, 0)
    return pl.pallas_call(
        _merge_kernel,
        grid=(M // tm,),
        in_specs=[
            pl.BlockSpec((tm, D), row),
            pl.BlockSpec((tm, RET_V), row),
            pl.BlockSpec((tm, ATT_W), row),
            pl.BlockSpec((tm, D), lambda i: (i, 0)),
            pl.BlockSpec((tm, D), lambda i: (i, 1)),
            pl.BlockSpec((RET_V, D), fixed),
            pl.BlockSpec((ATT_W, D), fixed),
            pl.BlockSpec((D, D), fixed),
        ],
        out_specs=pl.BlockSpec((tm, D), row),
        out_shape=jax.ShapeDtypeStruct((M, D), F32),
        compiler_params=_params(("parallel",)),
        name="merge",
    )(h, yr, ya, gates, gates, wr, wa, wm)


def _rope_tables(pos, d, width):
    inv_freq = ROPE_THETA ** (-jnp.arange(0, d, 2, dtype=F32) / d)
    ang = pos.astype(F32)[:, None] * inv_freq[None, :]
    cos, sin = jnp.cos(ang), jnp.sin(ang)
    cos = jnp.concatenate([cos, cos], axis=-1)
    sin = jnp.concatenate([-sin, sin], axis=-1)
    rep = width // d
    return jnp.tile(cos, (1, rep)), jnp.tile(sin, (1, rep))


def _tile(n, cap):
    t = min(n, cap)
    assert n % t == 0, (n, t)
    return t


class _Plan(NamedTuple):
    rows: int
    rows_tab: int
    rows_wide: int
    rows_merge: int
    cols: int
    ffn_slab: int
    att: int
    chunk: int


def _plan(S, M):
    return _Plan(rows=_tile(S, 4 * MXU_DIM), rows_tab=_tile(S, 8 * MXU_DIM), rows_wide=_tile(M, 8 * MXU_DIM),
                 rows_merge=_tile(M, 2 * MXU_DIM), cols=4 * MXU_DIM, ffn_slab=MXU_DIM, att=_tile(S, MXU_DIM),
                 chunk=_tile(S, 4 * RET_CHUNK))


def kernel(x, meta_tokens, ffn1_norm, ffn1_w_gate, ffn1_w_up, ffn1_w_down, mix_norm, w_in, w_ret_out,
           w_att_out, w_mix_out, ffn2_norm, ffn2_w_gate, ffn2_w_up, ffn2_w_down, final_norm):
    B, S, D = x.shape
    assert ffn1_norm.shape[0] == 1, "single layer only"
    assert meta_tokens.shape[0] == N_META
    k_top = min(TOPK_MAX, S // 4)
    M = B * S
    plan = _plan(S, M)
    bf = lambda a: a.astype(BF16)
    row = lambda a: a.reshape(1, -1)

    w32 = w_in[0].T
    offs = np.cumsum([0, RET_QK, RET_QK, RET_V, RET_V, ATT_W, ATT_W, ATT_W, IDX_Q, IDX_DH, IDX_HEADS, D, D])
    tn = plan.cols
    assert all(int(o) % tn == 0 for o in offs[:8])
    c_rq, c_rk, c_rv, c_rg, c_aq, c_ak, c_av = (int(o) // tn for o in offs[:7])
    w_idx = jnp.pad(bf(w32[offs[7]:offs[10]].T), ((0, 0), (0, LANES - IDX_DH - IDX_HEADS)))
    w_gate = w32[offs[10]:offs[12]]

    pos_meta = jnp.arange(N_META, dtype=jnp.int32)
    pos_real = N_META + jnp.arange(S, dtype=jnp.int32)
    ffn1 = (row(ffn1_norm[0]), bf(ffn1_w_gate[0]), bf(ffn1_w_up[0]), bf(ffn1_w_down[0]), row(mix_norm[0]))
    ffn2 = (row(ffn2_norm[0]), bf(ffn2_w_gate[0]), bf(ffn2_w_up[0]), bf(ffn2_w_down[0]), row(final_norm))
    tf = plan.ffn_slab
    k_scale = RET_DK ** -0.5

    _, un_m = _ffn(meta_tokens.astype(F32), *ffn1, tm=N_META, tf=tf, emit_h=True, n_dtype=BF16)
    pm = dict(tm=N_META, tn=tn)
    rk_m = _proj(un_m, w32, col0=c_rk, n_tiles=RET_QK // tn, mode="rope", d=RET_DK, k_scale=k_scale,
                 tabs=_rope_tables(pos_meta, RET_DK, RET_DK), **pm)
    rv_m = _proj(un_m, w32, col0=c_rv, n_tiles=RET_V // tn, mode="plain", **pm)
    ak_m = _proj(un_m, w32, col0=c_ak, n_tiles=ATT_W // tn, mode="rope", d=ATT_DH,
                 tabs=_rope_tables(pos_meta, ATT_DH, LANES), **pm)
    av_m = _proj(un_m, w32, col0=c_av, n_tiles=ATT_W // tn, mode="plain", **pm)

    tm = plan.rows
    h1, un = _ffn(x.reshape(M, D), *ffn1, tm=tm, tf=tf, emit_h=True, n_dtype=BF16)
    pr = dict(tm=plan.rows_tab, tn=tn)
    pw = dict(tm=plan.rows_wide, tn=tn)
    rqk = _proj(un, w32, col0=c_rq, n_tiles=2 * RET_QK // tn, mode="rope", d=RET_DK, k_scale=k_scale,
                n_q=RET_QK // tn, tabs=_rope_tables(pos_real, RET_DK, RET_DK), **pr)
    rv = _proj(un, w32, col0=c_rv, n_tiles=RET_V // tn, mode="plain", **pw)
    rg = _proj(un, w32, col0=c_rg, n_tiles=RET_V // tn, mode="silu", **pw)
    aqk = _proj(un, w32, col0=c_aq, n_tiles=2 * ATT_W // tn, mode="rope", d=ATT_DH, q_scale=ATT_EXP_SCALE,
                n_q=ATT_W // tn, tabs=_rope_tables(pos_real, ATT_DH, LANES), **pr)
    av = _proj(un, w32, col0=c_av, n_tiles=ATT_W // tn, mode="plain", **pw)
    gates = _proj(un, w_gate, col0=0, n_tiles=2 * D // tn, mode="plain", **pw)
    iq, ikw = _idx_proj(un, w_idx, _rope_tables(pos_real, IDX_DH, LANES), tm=tm)

    heads = jnp.arange(RET_HEADS, dtype=F32)
    log_g = jnp.log1p(-(2.0 ** (-5.0 - heads)))
    C = plan.chunk
    b3 = lambda a: a.reshape(B, S, a.shape[-1])
    yr = _retention(log_g, jnp.exp(log_g * C), b3(rqk), b3(rv), b3(rg), rk_m, rv_m, C=C)
    tk = plan.att
    pad_meta = lambda a: jnp.pad(a, ((0, tk - N_META), (0, 0)))
    ya = _dsa(b3(iq), b3(ikw), b3(aqk), b3(av), pad_meta(ak_m), pad_meta(av_m),
              tq=plan.att, tk=tk, k_top=k_top)

    h2 = _merge(h1, yr.reshape(M, RET_V), ya.reshape(M, ATT_W), gates, bf(w_ret_out[0]), bf(w_att_out[0]),
                bf(w_mix_out[0]), tm=plan.rows_merge)
    (y,) = _ffn(h2, *ffn2, tm=tm, tf=tf, emit_h=False, n_dtype=F32)
    return y.reshape(B, S, D)
```

```python
import functools
from typing import NamedTuple

import jax
import jax.numpy as jnp
import numpy as np
from jax import lax
from jax.experimental import pallas as pl
from jax.experimental.pallas import tpu as pltpu

N_META = 16
RET_HEADS = 4
RET_DK = 256
RET_DV = 512
RET_CHUNK = 128
ATT_HEADS = 8
ATT_DH = 128
IDX_HEADS = 8
IDX_DH = 64
TOPK_MAX = 256
ROPE_THETA = 10000.0
EPS = 1e-6
RET_QK = RET_HEADS * RET_DK
RET_V = RET_HEADS * RET_DV
ATT_W = ATT_HEADS * ATT_DH
IDX_Q = IDX_HEADS * IDX_DH
ATT_EXP_SCALE = (ATT_DH ** -0.5) * float(np.log2(np.e))

LANES = 128
SUBLANES = 8
MXU_DIM = 256
PROJ_SUB_ROWS = MXU_DIM
VMEM_LIMIT = 48 * 1024 * 1024
FFN_VMEM_LIMIT = 56 * 1024 * 1024

BF16 = jnp.bfloat16
F32 = jnp.float32
NT_DIMS = (((1,), (1,)), ((), ()))


def _dot(a, b):
    return jnp.dot(a, b, preferred_element_type=F32)


def _dot_nt(a, b):
    return lax.dot_general(a, b, NT_DIMS, preferred_element_type=F32)


def _rms(x, g):
    return x * lax.rsqrt(jnp.mean(x * x, axis=-1, keepdims=True) + EPS) * g


def _sigmoid(x):
    return 0.5 * jnp.tanh(0.5 * x) + 0.5


def _params(sem):
    return pltpu.CompilerParams(dimension_semantics=sem, vmem_limit_bytes=VMEM_LIMIT)


def _ffn_kernel(x_ref, g_ref, wg_ref, wu_ref, wd_ref, g2_ref, *rest, nf, tf, emit_h):
    if emit_h:
        h_ref, n_ref, xn_sc, acc_sc, act_sc = rest
    else:
        n_ref, xn_sc, acc_sc, act_sc = rest

    def down(f):
        rows = pl.ds(pl.multiple_of(f * tf, tf), tf)
        return _dot(act_sc[...], wd_ref[rows, :])

    def up(f):
        cols = pl.ds(pl.multiple_of(f * tf, tf), tf)
        xn = xn_sc[...]
        gate = _dot(xn, wg_ref[:, cols])
        lift = _dot(xn, wu_ref[:, cols])
        act_sc[...] = ((gate * _sigmoid(gate)) * lift).astype(BF16)

    xn_sc[...] = _rms(x_ref[...], g_ref[...]).astype(BF16)
    acc_sc[...] = jnp.zeros_like(acc_sc)
    up(0)

    def slab(f, carry):
        acc_sc[...] += down(f - 1)
        up(f)
        return carry

    lax.fori_loop(1, nf, slab, 0, unroll=True)
    h = x_ref[...] + 0.5 * (acc_sc[...] + down(nf - 1))
    if emit_h:
        h_ref[...] = h
    n_ref[...] = _rms(h, g2_ref[...]).astype(n_ref.dtype)


def _ffn(x, g, wg, wu, wd, g2, *, tm, tf, emit_h, n_dtype):
    M, D = x.shape
    F = wg.shape[1]
    nf = F // tf
    assert nf >= 2
    out_shape = [jax.ShapeDtypeStruct((M, D), n_dtype)]
    out_specs = [pl.BlockSpec((tm, D), lambda i: (i, 0))]
    if emit_h:
        out_shape = [jax.ShapeDtypeStruct((M, D), F32)] + out_shape
        out_specs = [pl.BlockSpec((tm, D), lambda i: (i, 0))] + out_specs
    resident = lambda shape: pl.BlockSpec(shape, lambda i: (0, 0), pipeline_mode=pl.Buffered(1))
    return pl.pallas_call(
        functools.partial(_ffn_kernel, nf=nf, tf=tf, emit_h=emit_h),
        grid=(M // tm,),
        in_specs=[
            pl.BlockSpec((tm, D), lambda i: (i, 0)),
            pl.BlockSpec((1, D), lambda i: (0, 0)),
            resident((D, F)),
            resident((D, F)),
            resident((F, D)),
            pl.BlockSpec((1, D), lambda i: (0, 0)),
        ],
        out_specs=out_specs,
        out_shape=out_shape,
        scratch_shapes=[pltpu.VMEM((tm, D), BF16), pltpu.VMEM((tm, D), F32), pltpu.VMEM((tm, tf), BF16)],
        compiler_params=pltpu.CompilerParams(
            dimension_semantics=("arbitrary",), vmem_limit_bytes=FFN_VMEM_LIMIT),
        name="ffn",
    )(x, g, wg, wu, wd, g2)


def _rot_half(y, d):
    w = y.shape[-1]
    if d >= w:
        return jnp.concatenate([y[:, w // 2:], y[:, : w // 2]], axis=-1)
    if d == LANES:
        return pltpu.roll(y, d // 2, axis=1)
    fwd = pltpu.roll(y, d // 2, axis=1)
    bwd = pltpu.roll(y, w - d // 2, axis=1)
    lane = lax.broadcasted_iota(jnp.int32, y.shape, 1)
    return jnp.where((lane % d) < d // 2, bwd, fwd)


def _rope_tile(y, cos, sin, d):
    tw = cos.shape[-1]
    outs = []
    for t in range(y.shape[-1] // tw):
        yt = y[:, t * tw:(t + 1) * tw]
        outs.append(yt * cos + _rot_half(yt, d) * sin)
    return outs[0] if len(outs) == 1 else jnp.concatenate(outs, axis=-1)


def _proj_kernel(x_ref, w_ref, *rest, mode, d, q_scale, k_scale, n_q):
    if mode == "rope":
        cos_ref, sin_ref, o_ref, w_sc = rest
    else:
        o_ref, w_sc = rest

    @pl.when(pl.program_id(1) == 0)
    def _():
        w_sc[...] = w_ref[...].T.astype(BF16)

    tm = x_ref.shape[0]
    sub = min(tm, PROJ_SUB_ROWS)
    for r0 in range(0, tm, sub):
        rows = slice(r0, r0 + sub)
        y = _dot(x_ref[rows, :], w_sc[...])
        if mode == "rope":
            y = _rope_tile(y, cos_ref[rows, :], sin_ref[rows, :], d)
            if q_scale != 1.0 or k_scale != 1.0:
                y = y * jnp.where(pl.program_id(0) >= n_q, k_scale, q_scale)
        elif mode == "silu":
            y = y * _sigmoid(y)
        o_ref[rows, :] = y.astype(o_ref.dtype)


def _proj(x, w_t, *, col0, n_tiles, tm, tn, mode, out_dtype=None, tabs=None, d=None, q_scale=1.0,
          k_scale=1.0, n_q=0):
    M, K = x.shape
    in_specs = [
        pl.BlockSpec((tm, K), lambda j, i: (i, 0)),
        pl.BlockSpec((tn, K), lambda j, i: (col0 + j, 0)),
    ]
    args = [x, w_t]
    if mode == "rope":
        cos, sin = tabs
        npos = cos.shape[0] // tm
        tw = cos.shape[1]
        in_specs += [pl.BlockSpec((tm, tw), lambda j, i: (i % npos, 0))] * 2
        args += [cos, sin]
    return pl.pallas_call(
        functools.partial(_proj_kernel, mode=mode, d=d, q_scale=q_scale, k_scale=k_scale, n_q=n_q),
        grid=(n_tiles, M // tm),
        in_specs=in_specs,
        out_specs=pl.BlockSpec((tm, tn), lambda j, i: (i, j)),
        out_shape=jax.ShapeDtypeStruct((M, n_tiles * tn), out_dtype or BF16),
        scratch_shapes=[pltpu.VMEM((K, tn), BF16)],
        compiler_params=_params(("arbitrary", "arbitrary")),
        name="proj_" + mode,
    )(*args)


def _idx_proj_kernel(x_ref, w_ref, cos_ref, sin_ref, q_ref, kw_ref):
    tm = x_ref.shape[0]
    sub = min(tm, PROJ_SUB_ROWS)
    for r0 in range(0, tm, sub):
        rows = slice(r0, r0 + sub)
        y = _dot(x_ref[rows, :], w_ref[...])
        cos, sin = cos_ref[rows, :], sin_ref[rows, :]
        q_ref[rows, :] = _rope_tile(y[:, :IDX_Q], cos, sin, IDX_DH)
        tail = y[:, IDX_Q:]
        lane = lax.broadcasted_iota(jnp.int32, tail.shape, 1)
        kw_ref[rows, :] = jnp.where(lane < IDX_DH, _rope_tile(tail, cos, sin, IDX_DH), tail)


def _idx_proj(x, w, tabs, *, tm):
    M, K = x.shape
    N = w.shape[1]
    cos, sin = tabs
    npos = cos.shape[0] // tm
    return pl.pallas_call(
        _idx_proj_kernel,
        grid=(M // tm,),
        in_specs=[
            pl.BlockSpec((tm, K), lambda i: (i, 0)),
            pl.BlockSpec((K, N), lambda i: (0, 0)),
            pl.BlockSpec((tm, LANES), lambda i: (i % npos, 0)),
            pl.BlockSpec((tm, LANES), lambda i: (i % npos, 0)),
        ],
        out_specs=[
            pl.BlockSpec((tm, IDX_Q), lambda i: (i, 0)),
            pl.BlockSpec((tm, LANES), lambda i: (i, 0)),
        ],
        out_shape=[
            jax.ShapeDtypeStruct((M, IDX_Q), F32),
            jax.ShapeDtypeStruct((M, LANES), F32),
        ],
        compiler_params=_params(("parallel",)),
        name="proj_idx",
    )(x, w, cos, sin)


def _ret_kernel(lg_ref, gc_ref, q_ref, k_ref, v_ref, g_ref, km_ref, vm_ref, o_ref, st_ref, decay_ref, *, C):
    c = pl.program_id(1)
    dk = lambda h: slice(h * RET_DK, (h + 1) * RET_DK)
    dv = lambda h: slice(h * RET_DV, (h + 1) * RET_DV)

    @pl.when(c == 0)
    def _():
        pm = lax.broadcasted_iota(jnp.int32, (N_META, 1), 0).astype(F32)
        ri = lax.broadcasted_iota(jnp.int32, (C, C), 0)
        ci = lax.broadcasted_iota(jnp.int32, (C, C), 1)
        diff = (ri - ci).astype(F32)
        for h in range(RET_HEADS):
            lg = lg_ref[h]
            zeta_m = jnp.exp(lg * (N_META - 1.0 - pm))
            kz = km_ref[:, dk(h)].astype(F32) * zeta_m
            st_ref[h] = _dot(kz.T.astype(BF16), vm_ref[:, dv(h)])
            decay_ref[h] = jnp.where(diff >= 0, jnp.exp(lg * jnp.maximum(diff, 0.0)), 0.0)

    pos = lax.broadcasted_iota(jnp.int32, (C, 1), 0).astype(F32)
    for h in range(RET_HEADS):
        lg = lg_ref[h]
        xi = jnp.exp(lg * (pos + 1.0))
        zeta = jnp.exp(lg * (C - 1.0 - pos))
        state = st_ref[h]
        for r0 in range(0, q_ref.shape[1], C):
            rows = slice(r0, r0 + C)
            q = q_ref[0, rows, dk(h)]
            k = k_ref[0, rows, dk(h)]
            v = v_ref[0, rows, dv(h)]
            inner = _dot((_dot_nt(q, k) * decay_ref[h]).astype(BF16), v)
            cross = _dot(q, state.astype(BF16)) * xi
            y = inner + cross
            y = y * lax.rsqrt(jnp.mean(y * y, axis=-1, keepdims=True) + EPS)
            o_ref[0, rows, dv(h)] = (g_ref[0, rows, dv(h)] * y).astype(o_ref.dtype)
            kz = k.astype(F32) * zeta
            state = gc_ref[h] * state + _dot(kz.T.astype(BF16), v)
        st_ref[h] = state


def _retention(lg, gc, rqk, rv, rg, km, vm, *, C, R):
    B, S, _ = rv.shape
    smem = pl.BlockSpec(memory_space=pltpu.SMEM)
    return pl.pallas_call(
        functools.partial(_ret_kernel, C=C),
        grid=(B, S // R),
        in_specs=[
            smem,
            smem,
            pl.BlockSpec((1, R, RET_QK), lambda b, c: (b, c, 0)),
            pl.BlockSpec((1, R, RET_QK), lambda b, c: (b, c, 1)),
            pl.BlockSpec((1, R, RET_V), lambda b, c: (b, c, 0)),
            pl.BlockSpec((1, R, RET_V), lambda b, c: (b, c, 0)),
            pl.BlockSpec((N_META, RET_QK), lambda b, c: (0, 0)),
            pl.BlockSpec((N_META, RET_V), lambda b, c: (0, 0)),
        ],
        out_specs=pl.BlockSpec((1, R, RET_V), lambda b, c: (b, c, 0)),
        out_shape=jax.ShapeDtypeStruct((B, S, RET_V), BF16),
        scratch_shapes=[pltpu.VMEM((RET_HEADS, RET_DK, RET_DV), F32), pltpu.VMEM((RET_HEADS, C, C), F32)],
        compiler_params=_params(("parallel", "arbitrary")),
        name="retention",
    )(lg, gc, rqk, rqk, rv, rg, km, vm)


TINY = float(np.finfo(np.float32).tiny)
TN_DIMS = (((0,), (0,)), ((), ()))


def _dsa_kernel(iq_ref, ikw_ref, wq_ref, aq_ref, ak_ref, av_ref, akm_ref, avm_ref, o_ref,
                sc_ref, tau_ref, m_ref, l_ref, acc_ref, s_ref, bm_ref, *, nq, tq, tk, k_top):
    qi = pl.program_id(1)
    nkb = (qi + 1) * (tq // tk)
    kf = float(k_top)
    neg_inf = -jnp.inf
    qpos = qi * tq + lax.broadcasted_iota(jnp.int32, (1, tq), 1)

    part_shape = (SUBLANES, tq)

    def fold(x):
        return x.reshape(tk // SUBLANES, SUBLANES, tq).sum(axis=0)

    idx_scale = (IDX_DH ** -0.5) * (IDX_HEADS ** -0.5)
    iq = iq_ref[0].astype(BF16)
    w_t = wq_ref[0].T
    iq_h = [iq[:, h * IDX_DH:(h + 1) * IDX_DH] for h in range(IDX_HEADS)]
    w_h = [w_t[IDX_DH + h:IDX_DH + h + 1, :] for h in range(IDX_HEADS)]

    def score_rows(kb, n_blocks, carry):
        rmin, rmax = carry
        rows = n_blocks * tk
        ik = ikw_ref[0, pl.ds(pl.multiple_of(kb * tk, tk), rows), :][:, :IDX_DH].astype(BF16)
        acc = jnp.zeros((rows, tq), F32)
        for h in range(IDX_HEADS):
            acc = acc + w_h[h] * jnp.maximum(_dot_nt(ik, iq_h[h]), 0.0)
        ok = (lax.broadcasted_iota(jnp.int32, (rows, 1), 0) + kb * tk) <= qpos
        s = jnp.where(ok, acc * idx_scale, neg_inf)
        for i in range(n_blocks):
            sc_ref[kb + i] = s[i * tk:(i + 1) * tk]
        rmin = jnp.minimum(rmin, jnp.min(jnp.where(ok, s, jnp.inf), axis=0, keepdims=True))
        rmax = jnp.maximum(rmax, jnp.max(s, axis=0, keepdims=True))
        return rmin, rmax

    carry = (jnp.full((1, tq), jnp.inf, F32), jnp.full((1, tq), neg_inf, F32))
    carry = lax.fori_loop(0, nkb // 4, lambda j, c: score_rows(4 * j, 4, c), carry)
    carry = lax.cond(nkb % 4 >= 2, lambda c: score_rows((nkb // 4) * 4, 2, c), lambda c: c, carry)
    rmin, rmax = lax.cond(nkb % 2 == 1, lambda c: score_rows(nkb - 1, 1, c), lambda c: c, carry)

    def count(pred):
        def body(kb, c):
            return c + fold(jnp.where(pred(sc_ref[kb]), 1.0, 0.0))
        return jnp.sum(lax.fori_loop(0, nkb, body, jnp.zeros(part_shape, F32)), axis=0, keepdims=True)

    def any_query(flag):
        return jnp.max(jnp.where(flag, 1.0, 0.0))

    c_max = count(lambda s: s >= rmax)
    top_tied = c_max >= kf
    lo0 = jnp.where(top_tied, rmax, rmin)
    cl0 = jnp.where(top_tied, c_max, (qpos + 1).astype(F32))

    def midpoint(lo, hi):
        a = jnp.maximum(jnp.abs(lo), TINY)
        b = jnp.maximum(jnp.abs(hi), TINY)
        geo = jnp.sqrt(a) * jnp.sqrt(b)
        one_sign = (lo >= 0.0) | (hi <= 0.0)
        far = jnp.maximum(a, b) > 4.0 * jnp.minimum(a, b)
        mid = jnp.where(one_sign & far, jnp.where(lo >= 0.0, geo, -geo), 0.5 * lo + 0.5 * hi)
        mid = jnp.where(lo == 0.0, TINY, mid)
        mid = jnp.where(hi == 0.0, -TINY, mid)
        return jnp.where((lo < 0.0) & (hi > 0.0), 0.0, mid)

    def active(lo, hi, cl):
        mid = midpoint(lo, hi)
        return (cl > kf) & (mid > lo) & (mid < hi)

    def go_on(st):
        return st[3] > 0.0

    def threshold(n_blocks, lo, hi, cl):
        blocks = range(n_blocks)

        def over_keys(fn, init, combine):
            acc = init
            for kb in blocks:
                acc = combine(acc, fn(sc_ref[kb]))
            return acc

        def count(pred):
            c = over_keys(lambda s: fold(jnp.where(pred(s), 1.0, 0.0)), jnp.zeros(part_shape, F32), jnp.add)
            return jnp.sum(c, axis=0, keepdims=True)

        def shrink_once(lo, hi, cl):
            mid = midpoint(lo, hi)
            act = active(lo, hi, cl)
            c = count(lambda s: s >= mid)
            up = act & (c >= kf)
            dn = act & (c < kf)
            return jnp.where(up, mid, lo), jnp.where(dn, mid, hi), jnp.where(up, c, cl)

        def shrink(st):
            pending = any_query(active(*st[:3]))
            return (*shrink_once(*shrink_once(*st[:3])), pending)

        lo, hi, cl, _ = lax.while_loop(go_on, shrink, (lo, hi, cl, jnp.float32(1.0)))
        tau_ref[...] = lo
        tie = cl > kf

        @pl.when(any_query(tie) > 0.0)
        def _():
            def max_below(bound):
                part = over_keys(
                    lambda s: jnp.max(jnp.where(s < bound, s, neg_inf).reshape(tk // SUBLANES, SUBLANES, tq), axis=0),
                    jnp.full(part_shape, neg_inf, F32), jnp.maximum)
                return jnp.max(part, axis=0, keepdims=True)

            hi_d = jnp.where(lo >= hi, jnp.inf, hi)
            v0 = max_below(hi_d)
            ge0 = count(lambda s: s >= v0)

            def walk_on(st):
                return st[3] > 0.0

            def walk(st):
                hi_w, v, ge, _ = st
                hi_w = jnp.where(tie & (ge < kf), v, hi_w)
                v = max_below(hi_w)
                ge = count(lambda s: s >= v)
                return hi_w, v, ge, any_query(tie & (ge < kf))

            _, v, _, _ = lax.while_loop(walk_on, walk, (hi_d, v0, ge0, any_query(tie & (ge0 < kf))))
            keep = kf - count(lambda s: s > v)
            ri = lax.broadcasted_iota(jnp.int32, (tk, tk), 0)
            ci = lax.broadcasted_iota(jnp.int32, (tk, tk), 1)
            earlier = jnp.where(ci < ri, 1.0, 0.0).astype(BF16)
            seen = jnp.zeros((1, tq), F32)
            for kb in blocks:
                s = sc_ref[kb]
                eq = tie & (s == v)
                eqf = jnp.where(eq, 1.0, 0.0)
                rank = seen + _dot(earlier, eqf.astype(BF16))
                sc_ref[kb] = jnp.where(eq & (rank >= keep), neg_inf, s)
                seen = seen + jnp.sum(eqf, axis=0, keepdims=True)
            tau_ref[...] = jnp.where(tie, v, lo)

        return jnp.int32(0)

    branches = [functools.partial(threshold, (i + 1) * (tq // tk)) for i in range(nq)]
    lax.switch(qi, branches, lo0, rmax, cl0)

    tau = tau_ref[...]
    meta_bias = jnp.where(lax.broadcasted_iota(jnp.int32, (tk, 1), 0) < N_META, 0.0, neg_inf)

    def head(ref, h, rows=None):
        if rows is None:
            return ref[:, h * ATT_DH:(h + 1) * ATT_DH]
        return ref[0, rows, h * ATT_DH:(h + 1) * ATT_DH]

    def keys_of(kb):
        if kb is None:
            return (lambda h: head(akm_ref, h)), (lambda h: head(avm_ref, h)), meta_bias
        rows = pl.ds(pl.multiple_of(kb * tk, tk), tk)
        bias = jnp.where(sc_ref[kb] >= tau, 0.0, neg_inf)
        return (lambda h: head(ak_ref, h, rows)), (lambda h: head(av_ref, h, rows)), bias

    def score_phase(kb, buf):
        k_of, _, bias = keys_of(kb)
        for h in range(ATT_HEADS):
            s = _dot_nt(k_of(h), head(aq_ref, h, slice(None))) + bias
            s_ref[buf, h] = s
            bm_ref[buf, h:h + 1, :] = jnp.max(s, axis=0, keepdims=True)

    def value_phase(kb, buf):
        _, v_of, _ = keys_of(kb)
        for h in range(ATT_HEADS):
            m_new = jnp.maximum(m_ref[h:h + 1, :], bm_ref[buf, h:h + 1, :])
            p = jnp.exp2(s_ref[buf, h] - m_new)
            p_sum = jnp.sum(p, axis=0, keepdims=True)
            pv = lax.dot_general(v_of(h), p.astype(BF16), TN_DIMS, preferred_element_type=F32)
            alpha = jnp.exp2(m_ref[h:h + 1, :] - m_new)
            l_ref[h:h + 1, :] = alpha * l_ref[h:h + 1, :] + p_sum
            acc_ref[h] = alpha * acc_ref[h] + pv
            m_ref[h:h + 1, :] = m_new

    m_ref[...] = jnp.full_like(m_ref, neg_inf)
    l_ref[...] = jnp.zeros_like(l_ref)
    acc_ref[...] = jnp.zeros_like(acc_ref)
    score_phase(None, 0)

    value_phase(None, 0)
    score_phase(0, 1)

    def att_pair(j, carry):
        kb = 2 * j
        value_phase(kb, 1)
        score_phase(kb + 1, 0)

        value_phase(kb + 1, 0)
        score_phase(jnp.minimum(kb + 2, nkb - 1), 1)
        return carry

    lax.fori_loop(0, nkb // 2, att_pair, 0)

    @pl.when(nkb % 2 == 1)
    def _():
        value_phase(nkb - 1, 1)

    for h in range(ATT_HEADS):
        out_t = acc_ref[h] / l_ref[h:h + 1, :]
        o_ref[0, :, h * ATT_DH:(h + 1) * ATT_DH] = out_t.T.astype(o_ref.dtype)


def _dsa(iq, ikw, aqk, av, akm, avm, *, tq, tk, k_top):
    B, S, _ = av.shape
    nq = S // tq
    assert tq % tk == 0
    return pl.pallas_call(
        functools.partial(_dsa_kernel, nq=nq, tq=tq, tk=tk, k_top=k_top),
        grid=(B, nq),
        in_specs=[
            pl.BlockSpec((1, tq, IDX_Q), lambda b, q: (b, q, 0)),
            pl.BlockSpec((1, S, LANES), lambda b, q: (b, 0, 0)),
            pl.BlockSpec((1, tq, LANES), lambda b, q: (b, q, 0)),
            pl.BlockSpec((1, tq, ATT_W), lambda b, q: (b, q, 0)),
            pl.BlockSpec((1, S, ATT_W), lambda b, q: (b, 0, 1)),
            pl.BlockSpec((1, S, ATT_W), lambda b, q: (b, 0, 0)),
            pl.BlockSpec((tk, ATT_W), lambda b, q: (0, 0)),
            pl.BlockSpec((tk, ATT_W), lambda b, q: (0, 0)),
        ],
        out_specs=pl.BlockSpec((1, tq, ATT_W), lambda b, q: (b, q, 0)),
        out_shape=jax.ShapeDtypeStruct((B, S, ATT_W), BF16),
        scratch_shapes=[
            pltpu.VMEM((S // tk, tk, tq), F32),
            pltpu.VMEM((1, tq), F32),
            pltpu.VMEM((ATT_HEADS, tq), F32),
            pltpu.VMEM((ATT_HEADS, tq), F32),
            pltpu.VMEM((ATT_HEADS, ATT_DH, tq), F32),
            pltpu.VMEM((2, ATT_HEADS, tk, tq), F32),
            pltpu.VMEM((2, ATT_HEADS, tq), F32),
        ],
        compiler_params=_params(("parallel", "arbitrary")),
        name="dsa",
    )(iq, ikw, ikw, aqk, aqk, av, akm, avm)


def _merge_kernel(h_ref, yr_ref, ya_ref, gr_ref, ga_ref, wr_ref, wa_ref, wm_ref, o_ref):
    yr = _dot(yr_ref[...], wr_ref[...])
    ya = _dot(ya_ref[...], wa_ref[...])
    merged = _sigmoid(gr_ref[...].astype(F32)) * yr + _sigmoid(ga_ref[...].astype(F32)) * ya
    o_ref[...] = h_ref[...] + _dot(merged.astype(BF16), wm_ref[...])


def _merge(h, yr, ya, gates, wr, wa, wm, *, tm):
    M, D = h.shape
    row = lambda i: (i, 0)
    fixed = lambda i: (0, 0)
    return pl.pallas_call(
        _merge_kernel,
        grid=(M // tm,),
        in_specs=[
            pl.BlockSpec((tm, D), row),
            pl.BlockSpec((tm, RET_V), row),
            pl.BlockSpec((tm, ATT_W), row),
            pl.BlockSpec((tm, D), lambda i: (i, 0)),
            pl.BlockSpec((tm, D), lambda i: (i, 1)),
            pl.BlockSpec((RET_V, D), fixed),
            pl.BlockSpec((ATT_W, D), fixed),
            pl.BlockSpec((D, D), fixed),
        ],
        out_specs=pl.BlockSpec((tm, D), row),
        out_shape=jax.ShapeDtypeStruct((M, D), F32),
        compiler_params=_params(("parallel",)),
        name="merge",
    )(h, yr, ya, gates, gates, wr, wa, wm)


def _rope_tables(pos, d, width):
    inv_freq = ROPE_THETA ** (-jnp.arange(0, d, 2, dtype=F32) / d)
    ang = pos.astype(F32)[:, None] * inv_freq[None, :]
    cos, sin = jnp.cos(ang), jnp.sin(ang)
    cos = jnp.concatenate([cos, cos], axis=-1)
    sin = jnp.concatenate([-sin, sin], axis=-1)
    rep = width // d
    return jnp.tile(cos, (1, rep)), jnp.tile(sin, (1, rep))


def _tile(n, cap):
    t = min(n, cap)
    assert n % t == 0, (n, t)
    return t


class _Plan(NamedTuple):
    rows: int
    rows_tab: int
    rows_wide: int
    rows_merge: int
    cols: int
    ffn_slab: int
    att: int
    chunk: int
    ret_rows: int


def _plan(S, M):
    return _Plan(rows=_tile(S, 4 * MXU_DIM), rows_tab=_tile(S, 8 * MXU_DIM), rows_wide=_tile(M, 8 * MXU_DIM),
                 rows_merge=_tile(M, 2 * MXU_DIM), cols=4 * MXU_DIM, ffn_slab=MXU_DIM, att=_tile(S, MXU_DIM),
                 chunk=_tile(S, 4 * RET_CHUNK), ret_rows=_tile(S, 4 * RET_CHUNK))


def kernel(x, meta_tokens, ffn1_norm, ffn1_w_gate, ffn1_w_up, ffn1_w_down, mix_norm, w_in, w_ret_out,
           w_att_out, w_mix_out, ffn2_norm, ffn2_w_gate, ffn2_w_up, ffn2_w_down, final_norm):
    B, S, D = x.shape
    assert ffn1_norm.shape[0] == 1, "single layer only"
    assert meta_tokens.shape[0] == N_META
    k_top = min(TOPK_MAX, S // 4)
    M = B * S
    plan = _plan(S, M)
    bf = lambda a: a.astype(BF16)
    row = lambda a: a.reshape(1, -1)

    w32 = w_in[0].T
    offs = np.cumsum([0, RET_QK, RET_QK, RET_V, RET_V, ATT_W, ATT_W, ATT_W, IDX_Q, IDX_DH, IDX_HEADS, D, D])
    tn = plan.cols
    assert all(int(o) % tn == 0 for o in offs[:8])
    c_rq, c_rk, c_rv, c_rg, c_aq, c_ak, c_av = (int(o) // tn for o in offs[:7])
    w_idx = jnp.pad(bf(w32[offs[7]:offs[10]].T), ((0, 0), (0, LANES - IDX_DH - IDX_HEADS)))
    w_gate = w32[offs[10]:offs[12]]

    pos_meta = jnp.arange(N_META, dtype=jnp.int32)
    pos_real = N_META + jnp.arange(S, dtype=jnp.int32)
    ffn1 = (row(ffn1_norm[0]), bf(ffn1_w_gate[0]), bf(ffn1_w_up[0]), bf(ffn1_w_down[0]), row(mix_norm[0]))
    ffn2 = (row(ffn2_norm[0]), bf(ffn2_w_gate[0]), bf(ffn2_w_up[0]), bf(ffn2_w_down[0]), row(final_norm))
    tf = plan.ffn_slab
    k_scale = RET_DK ** -0.5

    _, un_m = _ffn(meta_tokens.astype(F32), *ffn1, tm=N_META, tf=tf, emit_h=True, n_dtype=BF16)
    pm = dict(tm=N_META, tn=tn)
    rk_m = _proj(un_m, w32, col0=c_rk, n_tiles=RET_QK // tn, mode="rope", d=RET_DK, k_scale=k_scale,
                 tabs=_rope_tables(pos_meta, RET_DK, RET_DK), **pm)
    rv_m = _proj(un_m, w32, col0=c_rv, n_tiles=RET_V // tn, mode="plain", **pm)
    ak_m = _proj(un_m, w32, col0=c_ak, n_tiles=ATT_W // tn, mode="rope", d=ATT_DH,
                 tabs=_rope_tables(pos_meta, ATT_DH, LANES), **pm)
    av_m = _proj(un_m, w32, col0=c_av, n_tiles=ATT_W // tn, mode="plain", **pm)

    tm = plan.rows
    h1, un = _ffn(x.reshape(M, D), *ffn1, tm=tm, tf=tf, emit_h=True, n_dtype=BF16)
    pr = dict(tm=plan.rows_tab, tn=tn)
    pw = dict(tm=plan.rows_wide, tn=tn)
    rqk = _proj(un, w32, col0=c_rq, n_tiles=2 * RET_QK // tn, mode="rope", d=RET_DK, k_scale=k_scale,
                n_q=RET_QK // tn, tabs=_rope_tables(pos_real, RET_DK, RET_DK), **pr)
    rv = _proj(un, w32, col0=c_rv, n_tiles=RET_V // tn, mode="plain", **pw)
    rg = _proj(un, w32, col0=c_rg, n_tiles=RET_V // tn, mode="silu", **pw)
    aqk = _proj(un, w32, col0=c_aq, n_tiles=2 * ATT_W // tn, mode="rope", d=ATT_DH, q_scale=ATT_EXP_SCALE,
                n_q=ATT_W // tn, tabs=_rope_tables(pos_real, ATT_DH, LANES), **pr)
    av = _proj(un, w32, col0=c_av, n_tiles=ATT_W // tn, mode="plain", **pw)
    gates = _proj(un, w_gate, col0=0, n_tiles=2 * D // tn, mode="plain", **pw)
    iq, ikw = _idx_proj(un, w_idx, _rope_tables(pos_real, IDX_DH, LANES), tm=tm)

    heads = jnp.arange(RET_HEADS, dtype=F32)
    log_g = jnp.log1p(-(2.0 ** (-5.0 - heads)))
    C = plan.chunk
    b3 = lambda a: a.reshape(B, S, a.shape[-1])
    yr = _retention(log_g, jnp.exp(log_g * C), b3(rqk), b3(rv), b3(rg), rk_m, rv_m, C=C, R=plan.ret_rows)
    tk = plan.att
    pad_meta = lambda a: jnp.pad(a, ((0, tk - N_META), (0, 0)))
    ya = _dsa(b3(iq), b3(ikw), b3(aqk), b3(av), pad_meta(ak_m), pad_meta(av_m),
              tq=plan.att, tk=tk, k_top=k_top)

    h2 = _merge(h1, yr.reshape(M, RET_V), ya.reshape(M, ATT_W), gates, bf(w_ret_out[0]), bf(w_att_out[0]),
                bf(w_mix_out[0]), tm=plan.rows_merge)
    (y,) = _ffn(h2, *ffn2, tm=tm, tf=tf, emit_h=False, n_dtype=F32)
    return y.reshape(B, S, D)
```

```python
import functools
from typing import NamedTuple

import jax
import jax.numpy as jnp
import numpy as np
from jax import lax
from jax.experimental import pallas as pl
from jax.experimental.pallas import tpu as pltpu

N_META = 16
RET_HEADS = 4
RET_DK = 256
RET_DV = 512
RET_CHUNK = 128
ATT_HEADS = 8
ATT_DH = 128
IDX_HEADS = 8
IDX_DH = 64
TOPK_MAX = 256
ROPE_THETA = 10000.0
EPS = 1e-6
RET_QK = RET_HEADS * RET_DK
RET_V = RET_HEADS * RET_DV
ATT_W = ATT_HEADS * ATT_DH
IDX_Q = IDX_HEADS * IDX_DH
ATT_EXP_SCALE = (ATT_DH ** -0.5) * float(np.log2(np.e))

LANES = 128
SUBLANES = 8
MXU_DIM = 256
PROJ_SUB_ROWS = MXU_DIM
VMEM_LIMIT = 48 * 1024 * 1024
FFN_VMEM_LIMIT = 56 * 1024 * 1024

BF16 = jnp.bfloat16
F32 = jnp.float32
NT_DIMS = (((1,), (1,)), ((), ()))


def _dot(a, b):
    return jnp.dot(a, b, preferred_element_type=F32)


def _dot_nt(a, b):
    return lax.dot_general(a, b, NT_DIMS, preferred_element_type=F32)


def _rms(x, g):
    return x * lax.rsqrt(jnp.mean(x * x, axis=-1, keepdims=True) + EPS) * g


def _sigmoid(x):
    return 0.5 * jnp.tanh(0.5 * x) + 0.5


def _params(sem):
    return pltpu.CompilerParams(dimension_semantics=sem, vmem_limit_bytes=VMEM_LIMIT)


def _ffn_kernel(x_ref, g_ref, wg_ref, wu_ref, wd_ref, g2_ref, *rest, nf, tf, emit_h):
    if emit_h:
        h_ref, n_ref, xn_sc, acc_sc, act_sc = rest
    else:
        n_ref, xn_sc, acc_sc, act_sc = rest

    def down(f):
        rows = pl.ds(pl.multiple_of(f * tf, tf), tf)
        return _dot(act_sc[...], wd_ref[rows, :])

    def up(f):
        cols = pl.ds(pl.multiple_of(f * tf, tf), tf)
        xn = xn_sc[...]
        gate = _dot(xn, wg_ref[:, cols])
        lift = _dot(xn, wu_ref[:, cols])
        act_sc[...] = ((gate * _sigmoid(gate)) * lift).astype(BF16)

    xn_sc[...] = _rms(x_ref[...], g_ref[...]).astype(BF16)
    acc_sc[...] = jnp.zeros_like(acc_sc)
    up(0)

    def slab(f, carry):
        acc_sc[...] += down(f - 1)
        up(f)
        return carry

    lax.fori_loop(1, nf, slab, 0, unroll=True)
    h = x_ref[...] + 0.5 * (acc_sc[...] + down(nf - 1))
    if emit_h:
        h_ref[...] = h
    n_ref[...] = _rms(h, g2_ref[...]).astype(n_ref.dtype)


def _ffn(x, g, wg, wu, wd, g2, *, tm, tf, emit_h, n_dtype):
    M, D = x.shape
    F = wg.shape[1]
    nf = F // tf
    assert nf >= 2
    out_shape = [jax.ShapeDtypeStruct((M, D), n_dtype)]
    out_specs = [pl.BlockSpec((tm, D), lambda i: (i, 0))]
    if emit_h:
        out_shape = [jax.ShapeDtypeStruct((M, D), F32)] + out_shape
        out_specs = [pl.BlockSpec((tm, D), lambda i: (i, 0))] + out_specs
    resident = lambda shape: pl.BlockSpec(shape, lambda i: (0, 0), pipeline_mode=pl.Buffered(1))
    return pl.pallas_call(
        functools.partial(_ffn_kernel, nf=nf, tf=tf, emit_h=emit_h),
        grid=(M // tm,),
        in_specs=[
            pl.BlockSpec((tm, D), lambda i: (i, 0)),
            pl.BlockSpec((1, D), lambda i: (0, 0)),
            resident((D, F)),
            resident((D, F)),
            resident((F, D)),
            pl.BlockSpec((1, D), lambda i: (0, 0)),
        ],
        out_specs=out_specs,
        out_shape=out_shape,
        scratch_shapes=[pltpu.VMEM((tm, D), BF16), pltpu.VMEM((tm, D), F32), pltpu.VMEM((tm, tf), BF16)],
        compiler_params=pltpu.CompilerParams(
            dimension_semantics=("arbitrary",), vmem_limit_bytes=FFN_VMEM_LIMIT),
        name="ffn",
    )(x, g, wg, wu, wd, g2)


def _rot_half(y, d):
    w = y.shape[-1]
    if d >= w:
        return jnp.concatenate([y[:, w // 2:], y[:, : w // 2]], axis=-1)
    if d == LANES:
        return pltpu.roll(y, d // 2, axis=1)
    fwd = pltpu.roll(y, d // 2, axis=1)
    bwd = pltpu.roll(y, w - d // 2, axis=1)
    lane = lax.broadcasted_iota(jnp.int32, y.shape, 1)
    return jnp.where((lane % d) < d // 2, bwd, fwd)


def _rope_tile(y, cos, sin, d):
    tw = cos.shape[-1]
    outs = []
    for t in range(y.shape[-1] // tw):
        yt = y[:, t * tw:(t + 1) * tw]
        outs.append(yt * cos + _rot_half(yt, d) * sin)
    return outs[0] if len(outs) == 1 else jnp.concatenate(outs, axis=-1)


def _proj_kernel(x_ref, w_ref, *rest, mode, d, q_scale, k_scale, n_q):
    if mode == "rope":
        cos_ref, sin_ref, o_ref, w_sc = rest
    else:
        o_ref, w_sc = rest

    @pl.when(pl.program_id(1) == 0)
    def _():
        w_sc[...] = w_ref[...].T.astype(BF16)

    tm = x_ref.shape[0]
    sub = min(tm, PROJ_SUB_ROWS)
    for r0 in range(0, tm, sub):
        rows = slice(r0, r0 + sub)
        y = _dot(x_ref[rows, :], w_sc[...])
        if mode == "rope":
            y = _rope_tile(y, cos_ref[rows, :], sin_ref[rows, :], d)
            if q_scale != 1.0 or k_scale != 1.0:
                y = y * jnp.where(pl.program_id(0) >= n_q, k_scale, q_scale)
        elif mode == "silu":
            y = y * _sigmoid(y)
        o_ref[rows, :] = y.astype(o_ref.dtype)


def _proj(x, w_t, *, col0, n_tiles, tm, tn, mode, out_dtype=None, tabs=None, d=None, q_scale=1.0,
          k_scale=1.0, n_q=0):
    M, K = x.shape
    in_specs = [
        pl.BlockSpec((tm, K), lambda j, i: (i, 0)),
        pl.BlockSpec((tn, K), lambda j, i: (col0 + j, 0)),
    ]
    args = [x, w_t]
    if mode == "rope":
        cos, sin = tabs
        npos = cos.shape[0] // tm
        tw = cos.shape[1]
        in_specs += [pl.BlockSpec((tm, tw), lambda j, i: (i % npos, 0))] * 2
        args += [cos, sin]
    return pl.pallas_call(
        functools.partial(_proj_kernel, mode=mode, d=d, q_scale=q_scale, k_scale=k_scale, n_q=n_q),
        grid=(n_tiles, M // tm),
        in_specs=in_specs,
        out_specs=pl.BlockSpec((tm, tn), lambda j, i: (i, j)),
        out_shape=jax.ShapeDtypeStruct((M, n_tiles * tn), out_dtype or BF16),
        scratch_shapes=[pltpu.VMEM((K, tn), BF16)],
        compiler_params=_params(("arbitrary", "arbitrary")),
        name="proj_" + mode,
    )(*args)


def _idx_proj_kernel(x_ref, w_ref, cos_ref, sin_ref, q_ref, kw_ref):
    tm = x_ref.shape[0]
    sub = min(tm, PROJ_SUB_ROWS)
    for r0 in range(0, tm, sub):
        rows = slice(r0, r0 + sub)
        y = _dot(x_ref[rows, :], w_ref[...])
        cos, sin = cos_ref[rows, :], sin_ref[rows, :]
        q_ref[rows, :] = _rope_tile(y[:, :IDX_Q], cos, sin, IDX_DH)
        tail = y[:, IDX_Q:]
        lane = lax.broadcasted_iota(jnp.int32, tail.shape, 1)
        kw_ref[rows, :] = jnp.where(lane < IDX_DH, _rope_tile(tail, cos, sin, IDX_DH), tail)


def _idx_proj(x, w, tabs, *, tm):
    M, K = x.shape
    N = w.shape[1]
    cos, sin = tabs
    npos = cos.shape[0] // tm
    return pl.pallas_call(
        _idx_proj_kernel,
        grid=(M // tm,),
        in_specs=[
            pl.BlockSpec((tm, K), lambda i: (i, 0)),
            pl.BlockSpec((K, N), lambda i: (0, 0)),
            pl.BlockSpec((tm, LANES), lambda i: (i % npos, 0)),
            pl.BlockSpec((tm, LANES), lambda i: (i % npos, 0)),
        ],
        out_specs=[
            pl.BlockSpec((tm, IDX_Q), lambda i: (i, 0)),
            pl.BlockSpec((tm, LANES), lambda i: (i, 0)),
        ],
        out_shape=[
            jax.ShapeDtypeStruct((M, IDX_Q), F32),
            jax.ShapeDtypeStruct((M, LANES), F32),
        ],
        compiler_params=_params(("parallel",)),
        name="proj_idx",
    )(x, w, cos, sin)


def _ret_kernel(lg_ref, gc_ref, q_ref, k_ref, v_ref, g_ref, km_ref, vm_ref, o_ref, st_ref, decay_ref, *, C):
    c = pl.program_id(1)
    dk = lambda h: slice(h * RET_DK, (h + 1) * RET_DK)
    dv = lambda h: slice(h * RET_DV, (h + 1) * RET_DV)

    @pl.when(c == 0)
    def _():
        pm = lax.broadcasted_iota(jnp.int32, (N_META, 1), 0).astype(F32)
        ri = lax.broadcasted_iota(jnp.int32, (C, C), 0)
        ci = lax.broadcasted_iota(jnp.int32, (C, C), 1)
        diff = (ri - ci).astype(F32)
        for h in range(RET_HEADS):
            lg = lg_ref[h]
            zeta_m = jnp.exp(lg * (N_META - 1.0 - pm))
            kz = km_ref[:, dk(h)].astype(F32) * zeta_m
            st_ref[h] = _dot(kz.T.astype(BF16), vm_ref[:, dv(h)])
            decay_ref[h] = jnp.where(diff >= 0, jnp.exp(lg * jnp.maximum(diff, 0.0)), 0.0)

    pos = lax.broadcasted_iota(jnp.int32, (C, 1), 0).astype(F32)
    for h in range(RET_HEADS):
        lg = lg_ref[h]
        xi = jnp.exp(lg * (pos + 1.0))
        zeta = jnp.exp(lg * (C - 1.0 - pos))
        state = st_ref[h]
        for r0 in range(0, q_ref.shape[1], C):
            rows = slice(r0, r0 + C)
            q = q_ref[0, rows, dk(h)]
            k = k_ref[0, rows, dk(h)]
            v = v_ref[0, rows, dv(h)]
            inner = _dot((_dot_nt(q, k) * decay_ref[h]).astype(BF16), v)
            cross = _dot(q, state.astype(BF16)) * xi
            y = inner + cross
            y = y * lax.rsqrt(jnp.mean(y * y, axis=-1, keepdims=True) + EPS)
            o_ref[0, rows, dv(h)] = (g_ref[0, rows, dv(h)] * y).astype(o_ref.dtype)
            kz = k.astype(F32) * zeta
            state = gc_ref[h] * state + _dot(kz.T.astype(BF16), v)
        st_ref[h] = state


def _retention(lg, gc, rqk, rv, rg, km, vm, *, C, R):
    B, S, _ = rv.shape
    smem = pl.BlockSpec(memory_space=pltpu.SMEM)
    return pl.pallas_call(
        functools.partial(_ret_kernel, C=C),
        grid=(B, S // R),
        in_specs=[
            smem,
            smem,
            pl.BlockSpec((1, R, RET_QK), lambda b, c: (b, c, 0)),
            pl.BlockSpec((1, R, RET_QK), lambda b, c: (b, c, 1)),
            pl.BlockSpec((1, R, RET_V), lambda b, c: (b, c, 0)),
            pl.BlockSpec((1, R, RET_V), lambda b, c: (b, c, 0)),
            pl.BlockSpec((N_META, RET_QK), lambda b, c: (0, 0)),
            pl.BlockSpec((N_META, RET_V), lambda b, c: (0, 0)),
        ],
        out_specs=pl.BlockSpec((1, R, RET_V), lambda b, c: (b, c, 0)),
        out_shape=jax.ShapeDtypeStruct((B, S, RET_V), BF16),
        scratch_shapes=[pltpu.VMEM((RET_HEADS, RET_DK, RET_DV), F32), pltpu.VMEM((RET_HEADS, C, C), F32)],
        compiler_params=_params(("parallel", "arbitrary")),
        name="retention",
    )(lg, gc, rqk, rqk, rv, rg, km, vm)


TINY = float(np.finfo(np.float32).tiny)
TN_DIMS = (((0,), (0,)), ((), ()))


def _dsa_kernel(iq_ref, ikw_ref, wq_ref, aq_ref, ak_ref, av_ref, akm_ref, avm_ref, o_ref,
                sc_ref, tau_ref, m_ref, l_ref, acc_ref, s_ref, bm_ref, *, nq, tq, tk, k_top):
    qi = pl.program_id(1)
    nkb = (qi + 1) * (tq // tk)
    kf = float(k_top)
    neg_inf = -jnp.inf
    qpos = qi * tq + lax.broadcasted_iota(jnp.int32, (1, tq), 1)

    part_shape = (SUBLANES, tq)

    def fold(x):
        return x.reshape(tk // SUBLANES, SUBLANES, tq).sum(axis=0)

    idx_scale = (IDX_DH ** -0.5) * (IDX_HEADS ** -0.5)
    iq = iq_ref[0].astype(BF16)
    w_t = wq_ref[0].T
    iq_h = [iq[:, h * IDX_DH:(h + 1) * IDX_DH] for h in range(IDX_HEADS)]
    w_h = [w_t[IDX_DH + h:IDX_DH + h + 1, :] for h in range(IDX_HEADS)]

    def score_rows(kb, n_blocks, carry):
        rmin, rmax = carry
        rows = n_blocks * tk
        ik = ikw_ref[0, pl.ds(pl.multiple_of(kb * tk, tk), rows), :][:, :IDX_DH].astype(BF16)
        acc = jnp.zeros((rows, tq), F32)
        for h in range(IDX_HEADS):
            acc = acc + w_h[h] * jnp.maximum(_dot_nt(ik, iq_h[h]), 0.0)
        ok = (lax.broadcasted_iota(jnp.int32, (rows, 1), 0) + kb * tk) <= qpos
        s = jnp.where(ok, acc * idx_scale, neg_inf)
        for i in range(n_blocks):
            sc_ref[kb + i] = s[i * tk:(i + 1) * tk]
        rmin = jnp.minimum(rmin, jnp.min(jnp.where(ok, s, jnp.inf), axis=0, keepdims=True))
        rmax = jnp.maximum(rmax, jnp.max(s, axis=0, keepdims=True))
        return rmin, rmax

    carry = (jnp.full((1, tq), jnp.inf, F32), jnp.full((1, tq), neg_inf, F32))
    carry = lax.fori_loop(0, nkb // 4, lambda j, c: score_rows(4 * j, 4, c), carry)
    carry = lax.cond(nkb % 4 >= 2, lambda c: score_rows((nkb // 4) * 4, 2, c), lambda c: c, carry)
    rmin, rmax = lax.cond(nkb % 2 == 1, lambda c: score_rows(nkb - 1, 1, c), lambda c: c, carry)

    def any_query(flag):
        return jnp.max(jnp.where(flag, 1.0, 0.0))

    def midpoint(lo, hi):
        a = jnp.maximum(jnp.abs(lo), TINY)
        b = jnp.maximum(jnp.abs(hi), TINY)
        geo = jnp.sqrt(a) * jnp.sqrt(b)
        one_sign = (lo >= 0.0) | (hi <= 0.0)
        far = jnp.maximum(a, b) > 4.0 * jnp.minimum(a, b)
        mid = jnp.where(one_sign & far, jnp.where(lo >= 0.0, geo, -geo), 0.5 * lo + 0.5 * hi)
        mid = jnp.where(lo == 0.0, TINY, mid)
        mid = jnp.where(hi == 0.0, -TINY, mid)
        return jnp.where((lo < 0.0) & (hi > 0.0), 0.0, mid)

    def active(lo, hi, cl):
        mid = midpoint(lo, hi)
        return (cl > kf) & (mid > lo) & (mid < hi)

    def go_on(st):
        return st[3] > 0.0

    def threshold(n_blocks, rmin, rmax):
        blocks = range(n_blocks)

        def over_keys(fn, init, combine):
            acc = init
            for kb in blocks:
                acc = combine(acc, fn(sc_ref[kb]))
            return acc

        def count(pred):
            c = over_keys(lambda s: fold(jnp.where(pred(s), 1.0, 0.0)), jnp.zeros(part_shape, F32), jnp.add)
            return jnp.sum(c, axis=0, keepdims=True)

        c_max = count(lambda s: s >= rmax)
        top_tied = c_max >= kf
        lo = jnp.where(top_tied, rmax, rmin)
        hi = rmax
        cl = jnp.where(top_tied, c_max, (qpos + 1).astype(F32))

        def shrink_once(lo, hi, cl):
            mid = midpoint(lo, hi)
            act = active(lo, hi, cl)
            c = count(lambda s: s >= mid)
            up = act & (c >= kf)
            dn = act & (c < kf)
            return jnp.where(up, mid, lo), jnp.where(dn, mid, hi), jnp.where(up, c, cl)

        def shrink(st):
            pending = any_query(active(*st[:3]))
            return (*shrink_once(*shrink_once(*st[:3])), pending)

        lo, hi, cl, _ = lax.while_loop(go_on, shrink, (lo, hi, cl, jnp.float32(1.0)))
        tau_ref[...] = lo
        tie = cl > kf

        @pl.when(any_query(tie) > 0.0)
        def _():
            def max_below(bound):
                part = over_keys(
                    lambda s: jnp.max(jnp.where(s < bound, s, neg_inf).reshape(tk // SUBLANES, SUBLANES, tq), axis=0),
                    jnp.full(part_shape, neg_inf, F32), jnp.maximum)
                return jnp.max(part, axis=0, keepdims=True)

            hi_d = jnp.where(lo >= hi, jnp.inf, hi)
            v0 = max_below(hi_d)
            ge0 = count(lambda s: s >= v0)

            def walk_on(st):
                return st[3] > 0.0

            def walk(st):
                hi_w, v, ge, _ = st
                hi_w = jnp.where(tie & (ge < kf), v, hi_w)
                v = max_below(hi_w)
                ge = count(lambda s: s >= v)
                return hi_w, v, ge, any_query(tie & (ge < kf))

            _, v, _, _ = lax.while_loop(walk_on, walk, (hi_d, v0, ge0, any_query(tie & (ge0 < kf))))
            keep = kf - count(lambda s: s > v)
            ri = lax.broadcasted_iota(jnp.int32, (tk, tk), 0)
            ci = lax.broadcasted_iota(jnp.int32, (tk, tk), 1)
            earlier = jnp.where(ci < ri, 1.0, 0.0).astype(BF16)
            seen = jnp.zeros((1, tq), F32)
            for kb in blocks:
                s = sc_ref[kb]
                eq = tie & (s == v)
                eqf = jnp.where(eq, 1.0, 0.0)
                rank = seen + _dot(earlier, eqf.astype(BF16))
                sc_ref[kb] = jnp.where(eq & (rank >= keep), neg_inf, s)
                seen = seen + jnp.sum(eqf, axis=0, keepdims=True)
            tau_ref[...] = jnp.where(tie, v, lo)

        return jnp.int32(0)

    branches = [functools.partial(threshold, (i + 1) * (tq // tk)) for i in range(nq)]
    lax.switch(qi, branches, rmin, rmax)

    tau = tau_ref[...]
    meta_bias = jnp.where(lax.broadcasted_iota(jnp.int32, (tk, 1), 0) < N_META, 0.0, neg_inf)

    def head(ref, h, rows=None):
        if rows is None:
            return ref[:, h * ATT_DH:(h + 1) * ATT_DH]
        return ref[0, rows, h * ATT_DH:(h + 1) * ATT_DH]

    def keys_of(kb):
        if kb is None:
            return (lambda h: head(akm_ref, h)), (lambda h: head(avm_ref, h)), meta_bias
        rows = pl.ds(pl.multiple_of(kb * tk, tk), tk)
        bias = jnp.where(sc_ref[kb] >= tau, 0.0, neg_inf)
        return (lambda h: head(ak_ref, h, rows)), (lambda h: head(av_ref, h, rows)), bias

    def score_phase(kb, buf):
        k_of, _, bias = keys_of(kb)
        for h in range(ATT_HEADS):
            s = _dot_nt(k_of(h), head(aq_ref, h, slice(None))) + bias
            s_ref[buf, h] = s
            bm_ref[buf, h:h + 1, :] = jnp.max(s, axis=0, keepdims=True)

    def value_phase(kb, buf):
        _, v_of, _ = keys_of(kb)
        for h in range(ATT_HEADS):
            m_new = jnp.maximum(m_ref[h:h + 1, :], bm_ref[buf, h:h + 1, :])
            p = jnp.exp2(s_ref[buf, h] - m_new)
            p_sum = jnp.sum(p, axis=0, keepdims=True)
            pv = lax.dot_general(v_of(h), p.astype(BF16), TN_DIMS, preferred_element_type=F32)
            alpha = jnp.exp2(m_ref[h:h + 1, :] - m_new)
            l_ref[h:h + 1, :] = alpha * l_ref[h:h + 1, :] + p_sum
            acc_ref[h] = alpha * acc_ref[h] + pv
            m_ref[h:h + 1, :] = m_new

    m_ref[...] = jnp.full_like(m_ref, neg_inf)
    l_ref[...] = jnp.zeros_like(l_ref)
    acc_ref[...] = jnp.zeros_like(acc_ref)
    score_phase(None, 0)

    value_phase(None, 0)
    score_phase(0, 1)

    def att_blocks(kb, n):
        for i in range(n):
            buf = 1 - i % 2
            value_phase(kb + i, buf)
            score_phase(jnp.minimum(kb + i + 1, nkb - 1), 1 - buf)

    def att_quad(j, carry):
        att_blocks(4 * j, 4)
        return carry

    lax.fori_loop(0, nkb // 4, att_quad, 0)

    @pl.when(nkb % 4 >= 2)
    def _():
        att_blocks((nkb // 4) * 4, 2)

    @pl.when(nkb % 2 == 1)
    def _():
        value_phase(nkb - 1, 1)

    for h in range(ATT_HEADS):
        out_t = acc_ref[h] / l_ref[h:h + 1, :]
        o_ref[0, :, h * ATT_DH:(h + 1) * ATT_DH] = out_t.T.astype(o_ref.dtype)


def _dsa(iq, ikw, aqk, av, akm, avm, *, tq, tk, k_top):
    B, S, _ = av.shape
    nq = S // tq
    assert tq % tk == 0
    return pl.pallas_call(
        functools.partial(_dsa_kernel, nq=nq, tq=tq, tk=tk, k_top=k_top),
        grid=(B, nq),
        in_specs=[
            pl.BlockSpec((1, tq, IDX_Q), lambda b, q: (b, q, 0)),
            pl.BlockSpec((1, S, LANES), lambda b, q: (b, 0, 0)),
            pl.BlockSpec((1, tq, LANES), lambda b, q: (b, q, 0)),
            pl.BlockSpec((1, tq, ATT_W), lambda b, q: (b, q, 0)),
            pl.BlockSpec((1, S, ATT_W), lambda b, q: (b, 0, 1)),
            pl.BlockSpec((1, S, ATT_W), lambda b, q: (b, 0, 0)),
            pl.BlockSpec((tk, ATT_W), lambda b, q: (0, 0)),
            pl.BlockSpec((tk, ATT_W), lambda b, q: (0, 0)),
        ],
        out_specs=pl.BlockSpec((1, tq, ATT_W), lambda b, q: (b, q, 0)),
        out_shape=jax.ShapeDtypeStruct((B, S, ATT_W), BF16),
        scratch_shapes=[
            pltpu.VMEM((S // tk, tk, tq), F32),
            pltpu.VMEM((1, tq), F32),
            pltpu.VMEM((ATT_HEADS, tq), F32),
            pltpu.VMEM((ATT_HEADS, tq), F32),
            pltpu.VMEM((ATT_HEADS, ATT_DH, tq), F32),
            pltpu.VMEM((2, ATT_HEADS, tk, tq), F32),
            pltpu.VMEM((2, ATT_HEADS, tq), F32),
        ],
        compiler_params=_params(("parallel", "arbitrary")),
        name="dsa",
    )(iq, ikw, ikw, aqk, aqk, av, akm, avm)


def _merge_kernel(h_ref, yr_ref, ya_ref, gr_ref, ga_ref, wr_ref, wa_ref, wm_ref, o_ref):
    yr = _dot(yr_ref[...], wr_ref[...])
    ya = _dot(ya_ref[...], wa_ref[...])
    merged = _sigmoid(gr_ref[...].astype(F32)) * yr + _sigmoid(ga_ref[...].astype(F32)) * ya
    o_ref[...] = h_ref[...] + _dot(merged.astype(BF16), wm_ref[...])


def _merge(h, yr, ya, gates, wr, wa, wm, *, tm):
    M, D = h.shape
    row = lambda i: (i, 0)
    fixed = lambda i: (0, 0)
    return pl.pallas_call(
        _merge_kernel,
        grid=(M // tm,),
        in_specs=[
            pl.BlockSpec((tm, D), row),
            pl.BlockSpec((tm, RET_V), row),
            pl.BlockSpec((tm, ATT_W), row),
            pl.BlockSpec((tm, D), lambda i: (i, 0)),
            pl.BlockSpec((tm, D), lambda i: (i, 1)),
            pl.BlockSpec((RET_V, D), fixed),
            pl.BlockSpec((ATT_W, D), fixed),
            pl.BlockSpec((D, D), fixed),
        ],
        out_specs=pl.BlockSpec((tm, D), row),
        out_shape=jax.ShapeDtypeStruct((M, D), F32),
        compiler_params=_params(("parallel",)),
        name="merge",
    )(h, yr, ya, gates, gates, wr, wa, wm)


def _rope_tables(pos, d, width):
    inv_freq = ROPE_THETA ** (-jnp.arange(0, d, 2, dtype=F32) / d)
    ang = pos.astype(F32)[:, None] * inv_freq[None, :]
    cos, sin = jnp.cos(ang), jnp.sin(ang)
    cos = jnp.concatenate([cos, cos], axis=-1)
    sin = jnp.concatenate([-sin, sin], axis=-1)
    rep = width // d
    return jnp.tile(cos, (1, rep)), jnp.tile(sin, (1, rep))


def _tile(n, cap):
    t = min(n, cap)
    assert n % t == 0, (n, t)
    return t


class _Plan(NamedTuple):
    rows: int
    rows_tab: int
    rows_wide: int
    rows_merge: int
    cols: int
    ffn_slab: int
    att: int
    chunk: int
    ret_rows: int


def _plan(S, M):
    return _Plan(rows=_tile(S, 4 * MXU_DIM), rows_tab=_tile(S, 8 * MXU_DIM), rows_wide=_tile(M, 8 * MXU_DIM),
                 rows_merge=_tile(M, 2 * MXU_DIM), cols=4 * MXU_DIM, ffn_slab=MXU_DIM, att=_tile(S, MXU_DIM),
                 chunk=_tile(S, 4 * RET_CHUNK), ret_rows=_tile(S, 4 * RET_CHUNK))


def kernel(x, meta_tokens, ffn1_norm, ffn1_w_gate, ffn1_w_up, ffn1_w_down, mix_norm, w_in, w_ret_out,
           w_att_out, w_mix_out, ffn2_norm, ffn2_w_gate, ffn2_w_up, ffn2_w_down, final_norm):
    B, S, D = x.shape
    assert ffn1_norm.shape[0] == 1, "single layer only"
    assert meta_tokens.shape[0] == N_META
    k_top = min(TOPK_MAX, S // 4)
    M = B * S
    plan = _plan(S, M)
    bf = lambda a: a.astype(BF16)
    row = lambda a: a.reshape(1, -1)

    w32 = w_in[0].T
    offs = np.cumsum([0, RET_QK, RET_QK, RET_V, RET_V, ATT_W, ATT_W, ATT_W, IDX_Q, IDX_DH, IDX_HEADS, D, D])
    tn = plan.cols
    assert all(int(o) % tn == 0 for o in offs[:8])
    c_rq, c_rk, c_rv, c_rg, c_aq, c_ak, c_av = (int(o) // tn for o in offs[:7])
    w_idx = jnp.pad(bf(w32[offs[7]:offs[10]].T), ((0, 0), (0, LANES - IDX_DH - IDX_HEADS)))
    w_gate = w32[offs[10]:offs[12]]

    pos_meta = jnp.arange(N_META, dtype=jnp.int32)
    pos_real = N_META + jnp.arange(S, dtype=jnp.int32)
    ffn1 = (row(ffn1_norm[0]), bf(ffn1_w_gate[0]), bf(ffn1_w_up[0]), bf(ffn1_w_down[0]), row(mix_norm[0]))
    ffn2 = (row(ffn2_norm[0]), bf(ffn2_w_gate[0]), bf(ffn2_w_up[0]), bf(ffn2_w_down[0]), row(final_norm))
    tf = plan.ffn_slab
    k_scale = RET_DK ** -0.5

    _, un_m = _ffn(meta_tokens.astype(F32), *ffn1, tm=N_META, tf=tf, emit_h=True, n_dtype=BF16)
    pm = dict(tm=N_META, tn=tn)
    rk_m = _proj(un_m, w32, col0=c_rk, n_tiles=RET_QK // tn, mode="rope", d=RET_DK, k_scale=k_scale,
                 tabs=_rope_tables(pos_meta, RET_DK, RET_DK), **pm)
    rv_m = _proj(un_m, w32, col0=c_rv, n_tiles=RET_V // tn, mode="plain", **pm)
    ak_m = _proj(un_m, w32, col0=c_ak, n_tiles=ATT_W // tn, mode="rope", d=ATT_DH,
                 tabs=_rope_tables(pos_meta, ATT_DH, LANES), **pm)
    av_m = _proj(un_m, w32, col0=c_av, n_tiles=ATT_W // tn, mode="plain", **pm)

    tm = plan.rows
    h1, un = _ffn(x.reshape(M, D), *ffn1, tm=tm, tf=tf, emit_h=True, n_dtype=BF16)
    pr = dict(tm=plan.rows_tab, tn=tn)
    pw = dict(tm=plan.rows_wide, tn=tn)
    rqk = _proj(un, w32, col0=c_rq, n_tiles=2 * RET_QK // tn, mode="rope", d=RET_DK, k_scale=k_scale,
                n_q=RET_QK // tn, tabs=_rope_tables(pos_real, RET_DK, RET_DK), **pr)
    rv = _proj(un, w32, col0=c_rv, n_tiles=RET_V // tn, mode="plain", **pw)
    rg = _proj(un, w32, col0=c_rg, n_tiles=RET_V // tn, mode="silu", **pw)
    aqk = _proj(un, w32, col0=c_aq, n_tiles=2 * ATT_W // tn, mode="rope", d=ATT_DH, q_scale=ATT_EXP_SCALE,
                n_q=ATT_W // tn, tabs=_rope_tables(pos_real, ATT_DH, LANES), **pr)
    av = _proj(un, w32, col0=c_av, n_tiles=ATT_W // tn, mode="plain", **pw)
    gates = _proj(un, w_gate, col0=0, n_tiles=2 * D // tn, mode="plain", **pw)
    iq, ikw = _idx_proj(un, w_idx, _rope_tables(pos_real, IDX_DH, LANES), tm=tm)

    heads = jnp.arange(RET_HEADS, dtype=F32)
    log_g = jnp.log1p(-(2.0 ** (-5.0 - heads)))
    C = plan.chunk
    b3 = lambda a: a.reshape(B, S, a.shape[-1])
    yr = _retention(log_g, jnp.exp(log_g * C), b3(rqk), b3(rv), b3(rg), rk_m, rv_m, C=C, R=plan.ret_rows)
    tk = plan.att
    pad_meta = lambda a: jnp.pad(a, ((0, tk - N_META), (0, 0)))
    ya = _dsa(b3(iq), b3(ikw), b3(aqk), b3(av), pad_meta(ak_m), pad_meta(av_m),
              tq=plan.att, tk=tk, k_top=k_top)

    h2 = _merge(h1, yr.reshape(M, RET_V), ya.reshape(M, ATT_W), gates, bf(w_ret_out[0]), bf(w_att_out[0]),
                bf(w_mix_out[0]), tm=plan.rows_merge)
    (y,) = _ffn(h2, *ffn2, tm=tm, tf=tf, emit_h=False, n_dtype=F32)
    return y.reshape(B, S, D)
```

```python
import functools
from typing import NamedTuple

import jax
import jax.numpy as jnp
import numpy as np
from jax import lax
from jax.experimental import pallas as pl
from jax.experimental.pallas import tpu as pltpu

N_META = 16
RET_HEADS = 4
RET_DK = 256
RET_DV = 512
RET_CHUNK = 128
ATT_HEADS = 8
ATT_DH = 128
IDX_HEADS = 8
IDX_DH = 64
TOPK_MAX = 256
ROPE_THETA = 10000.0
EPS = 1e-6
RET_QK = RET_HEADS * RET_DK
RET_V = RET_HEADS * RET_DV
ATT_W = ATT_HEADS * ATT_DH
IDX_Q = IDX_HEADS * IDX_DH
ATT_EXP_SCALE = (ATT_DH ** -0.5) * float(np.log2(np.e))

LANES = 128
SUBLANES = 8
MXU_DIM = 256
PROJ_SUB_ROWS = MXU_DIM
VMEM_LIMIT = 48 * 1024 * 1024
FFN_VMEM_LIMIT = 56 * 1024 * 1024

BF16 = jnp.bfloat16
F32 = jnp.float32
NT_DIMS = (((1,), (1,)), ((), ()))


def _dot(a, b):
    return jnp.dot(a, b, preferred_element_type=F32)


def _dot_nt(a, b):
    return lax.dot_general(a, b, NT_DIMS, preferred_element_type=F32)


def _rms(x, g):
    return x * lax.rsqrt(jnp.mean(x * x, axis=-1, keepdims=True) + EPS) * g


def _sigmoid(x):
    return 0.5 * jnp.tanh(0.5 * x) + 0.5


def _params(sem):
    return pltpu.CompilerParams(dimension_semantics=sem, vmem_limit_bytes=VMEM_LIMIT)


def _ffn_kernel(x_ref, g_ref, wg_ref, wu_ref, wd_ref, g2_ref, *rest, nf, tf, emit_h):
    if emit_h:
        h_ref, n_ref, xn_sc, acc_sc, act_sc = rest
    else:
        n_ref, xn_sc, acc_sc, act_sc = rest

    def down(f):
        rows = pl.ds(pl.multiple_of(f * tf, tf), tf)
        return _dot(act_sc[...], wd_ref[rows, :])

    def up(f):
        cols = pl.ds(pl.multiple_of(f * tf, tf), tf)
        xn = xn_sc[...]
        gate = _dot(xn, wg_ref[:, cols])
        lift = _dot(xn, wu_ref[:, cols])
        act_sc[...] = ((gate * _sigmoid(gate)) * lift).astype(BF16)

    xn_sc[...] = _rms(x_ref[...], g_ref[...]).astype(BF16)
    acc_sc[...] = jnp.zeros_like(acc_sc)
    up(0)

    def slab(f, carry):
        acc_sc[...] += down(f - 1)
        up(f)
        return carry

    lax.fori_loop(1, nf, slab, 0, unroll=True)
    h = x_ref[...] + 0.5 * (acc_sc[...] + down(nf - 1))
    if emit_h:
        h_ref[...] = h
    n_ref[...] = _rms(h, g2_ref[...]).astype(n_ref.dtype)


def _ffn(x, g, wg, wu, wd, g2, *, tm, tf, emit_h, n_dtype):
    M, D = x.shape
    F = wg.shape[1]
    nf = F // tf
    assert nf >= 2
    out_shape = [jax.ShapeDtypeStruct((M, D), n_dtype)]
    out_specs = [pl.BlockSpec((tm, D), lambda i: (i, 0))]
    if emit_h:
        out_shape = [jax.ShapeDtypeStruct((M, D), F32)] + out_shape
        out_specs = [pl.BlockSpec((tm, D), lambda i: (i, 0))] + out_specs
    resident = lambda shape: pl.BlockSpec(shape, lambda i: (0, 0), pipeline_mode=pl.Buffered(1))
    return pl.pallas_call(
        functools.partial(_ffn_kernel, nf=nf, tf=tf, emit_h=emit_h),
        grid=(M // tm,),
        in_specs=[
            pl.BlockSpec((tm, D), lambda i: (i, 0)),
            pl.BlockSpec((1, D), lambda i: (0, 0)),
            resident((D, F)),
            resident((D, F)),
            resident((F, D)),
            pl.BlockSpec((1, D), lambda i: (0, 0)),
        ],
        out_specs=out_specs,
        out_shape=out_shape,
        scratch_shapes=[pltpu.VMEM((tm, D), BF16), pltpu.VMEM((tm, D), F32), pltpu.VMEM((tm, tf), BF16)],
        compiler_params=pltpu.CompilerParams(
            dimension_semantics=("arbitrary",), vmem_limit_bytes=FFN_VMEM_LIMIT),
        name="ffn",
    )(x, g, wg, wu, wd, g2)


def _rot_half(y, d):
    w = y.shape[-1]
    if d >= w:
        return jnp.concatenate([y[:, w // 2:], y[:, : w // 2]], axis=-1)
    if d == LANES:
        return pltpu.roll(y, d // 2, axis=1)
    fwd = pltpu.roll(y, d // 2, axis=1)
    bwd = pltpu.roll(y, w - d // 2, axis=1)
    lane = lax.broadcasted_iota(jnp.int32, y.shape, 1)
    return jnp.where((lane % d) < d // 2, bwd, fwd)


def _rope_tile(y, cos, sin, d):
    tw = cos.shape[-1]
    outs = []
    for t in range(y.shape[-1] // tw):
        yt = y[:, t * tw:(t + 1) * tw]
        outs.append(yt * cos + _rot_half(yt, d) * sin)
    return outs[0] if len(outs) == 1 else jnp.concatenate(outs, axis=-1)


def _proj_kernel(x_ref, w_ref, *rest, mode, d, q_scale, k_scale, n_q):
    if mode == "rope":
        cos_ref, sin_ref, o_ref, w_sc = rest
    else:
        o_ref, w_sc = rest

    @pl.when(pl.program_id(1) == 0)
    def _():
        w_sc[...] = w_ref[...].T.astype(BF16)

    tm = x_ref.shape[0]
    sub = min(tm, PROJ_SUB_ROWS) if mode == "rope" else tm
    for r0 in range(0, tm, sub):
        rows = slice(r0, r0 + sub)
        y = _dot(x_ref[rows, :], w_sc[...])
        if mode == "rope":
            y = _rope_tile(y, cos_ref[rows, :], sin_ref[rows, :], d)
            if q_scale != 1.0 or k_scale != 1.0:
                y = y * jnp.where(pl.program_id(0) >= n_q, k_scale, q_scale)
        elif mode == "silu":
            y = y * _sigmoid(y)
        o_ref[rows, :] = y.astype(o_ref.dtype)


def _proj(x, w_t, *, col0, n_tiles, tm, tn, mode, out_dtype=None, tabs=None, d=None, q_scale=1.0,
          k_scale=1.0, n_q=0):
    M, K = x.shape
    in_specs = [
        pl.BlockSpec((tm, K), lambda j, i: (i, 0)),
        pl.BlockSpec((tn, K), lambda j, i: (col0 + j, 0)),
    ]
    args = [x, w_t]
    if mode == "rope":
        cos, sin = tabs
        npos = cos.shape[0] // tm
        tw = cos.shape[1]
        in_specs += [pl.BlockSpec((tm, tw), lambda j, i: (i % npos, 0))] * 2
        args += [cos, sin]
    return pl.pallas_call(
        functools.partial(_proj_kernel, mode=mode, d=d, q_scale=q_scale, k_scale=k_scale, n_q=n_q),
        grid=(n_tiles, M // tm),
        in_specs=in_specs,
        out_specs=pl.BlockSpec((tm, tn), lambda j, i: (i, j)),
        out_shape=jax.ShapeDtypeStruct((M, n_tiles * tn), out_dtype or BF16),
        scratch_shapes=[pltpu.VMEM((K, tn), BF16)],
        compiler_params=_params(("arbitrary", "arbitrary")),
        name="proj_" + mode,
    )(*args)


def _idx_proj_kernel(x_ref, w_ref, cos_ref, sin_ref, q_ref, kw_ref):
    tm = x_ref.shape[0]
    sub = min(tm, PROJ_SUB_ROWS)
    for r0 in range(0, tm, sub):
        rows = slice(r0, r0 + sub)
        y = _dot(x_ref[rows, :], w_ref[...])
        cos, sin = cos_ref[rows, :], sin_ref[rows, :]
        q_ref[rows, :] = _rope_tile(y[:, :IDX_Q], cos, sin, IDX_DH)
        tail = y[:, IDX_Q:]
        lane = lax.broadcasted_iota(jnp.int32, tail.shape, 1)
        kw_ref[rows, :] = jnp.where(lane < IDX_DH, _rope_tile(tail, cos, sin, IDX_DH), tail)


def _idx_proj(x, w, tabs, *, tm):
    M, K = x.shape
    N = w.shape[1]
    cos, sin = tabs
    npos = cos.shape[0] // tm
    return pl.pallas_call(
        _idx_proj_kernel,
        grid=(M // tm,),
        in_specs=[
            pl.BlockSpec((tm, K), lambda i: (i, 0)),
            pl.BlockSpec((K, N), lambda i: (0, 0)),
            pl.BlockSpec((tm, LANES), lambda i: (i % npos, 0)),
            pl.BlockSpec((tm, LANES), lambda i: (i % npos, 0)),
        ],
        out_specs=[
            pl.BlockSpec((tm, IDX_Q), lambda i: (i, 0)),
            pl.BlockSpec((tm, LANES), lambda i: (i, 0)),
        ],
        out_shape=[
            jax.ShapeDtypeStruct((M, IDX_Q), F32),
            jax.ShapeDtypeStruct((M, LANES), F32),
        ],
        compiler_params=_params(("parallel",)),
        name="proj_idx",
    )(x, w, cos, sin)


def _ret_kernel(lg_ref, gc_ref, q_ref, k_ref, v_ref, g_ref, km_ref, vm_ref, o_ref, st_ref, decay_ref, *, C):
    c = pl.program_id(1)
    dk = lambda h: slice(h * RET_DK, (h + 1) * RET_DK)
    dv = lambda h: slice(h * RET_DV, (h + 1) * RET_DV)

    @pl.when(c == 0)
    def _():
        pm = lax.broadcasted_iota(jnp.int32, (N_META, 1), 0).astype(F32)
        ri = lax.broadcasted_iota(jnp.int32, (C, C), 0)
        ci = lax.broadcasted_iota(jnp.int32, (C, C), 1)
        diff = (ri - ci).astype(F32)
        for h in range(RET_HEADS):
            lg = lg_ref[h]
            zeta_m = jnp.exp(lg * (N_META - 1.0 - pm))
            kz = km_ref[:, dk(h)].astype(F32) * zeta_m
            st_ref[h] = _dot(kz.T.astype(BF16), vm_ref[:, dv(h)])
            decay_ref[h] = jnp.where(diff >= 0, jnp.exp(lg * jnp.maximum(diff, 0.0)), 0.0)

    pos = lax.broadcasted_iota(jnp.int32, (C, 1), 0).astype(F32)
    for h in range(RET_HEADS):
        lg = lg_ref[h]
        xi = jnp.exp(lg * (pos + 1.0))
        zeta = jnp.exp(lg * (C - 1.0 - pos))
        state = st_ref[h]
        for r0 in range(0, q_ref.shape[1], C):
            rows = slice(r0, r0 + C)
            q = q_ref[0, rows, dk(h)]
            k = k_ref[0, rows, dk(h)]
            v = v_ref[0, rows, dv(h)]
            inner = _dot((_dot_nt(q, k) * decay_ref[h]).astype(BF16), v)
            cross = _dot(q, state.astype(BF16)) * xi
            y = inner + cross
            y = y * lax.rsqrt(jnp.mean(y * y, axis=-1, keepdims=True) + EPS)
            o_ref[0, rows, dv(h)] = (g_ref[0, rows, dv(h)] * y).astype(o_ref.dtype)
            kz = k.astype(F32) * zeta
            state = gc_ref[h] * state + _dot(kz.T.astype(BF16), v)
        st_ref[h] = state


def _retention(lg, gc, rqk, rv, rg, km, vm, *, C, R):
    B, S, _ = rv.shape
    smem = pl.BlockSpec(memory_space=pltpu.SMEM)
    return pl.pallas_call(
        functools.partial(_ret_kernel, C=C),
        grid=(B, S // R),
        in_specs=[
            smem,
            smem,
            pl.BlockSpec((1, R, RET_QK), lambda b, c: (b, c, 0)),
            pl.BlockSpec((1, R, RET_QK), lambda b, c: (b, c, 1)),
            pl.BlockSpec((1, R, RET_V), lambda b, c: (b, c, 0)),
            pl.BlockSpec((1, R, RET_V), lambda b, c: (b, c, 0)),
            pl.BlockSpec((N_META, RET_QK), lambda b, c: (0, 0)),
            pl.BlockSpec((N_META, RET_V), lambda b, c: (0, 0)),
        ],
        out_specs=pl.BlockSpec((1, R, RET_V), lambda b, c: (b, c, 0)),
        out_shape=jax.ShapeDtypeStruct((B, S, RET_V), BF16),
        scratch_shapes=[pltpu.VMEM((RET_HEADS, RET_DK, RET_DV), F32), pltpu.VMEM((RET_HEADS, C, C), F32)],
        compiler_params=_params(("parallel", "arbitrary")),
        name="retention",
    )(lg, gc, rqk, rqk, rv, rg, km, vm)


TINY = float(np.finfo(np.float32).tiny)
TN_DIMS = (((0,), (0,)), ((), ()))


def _dsa_kernel(iq_ref, ikw_ref, wq_ref, aq_ref, ak_ref, av_ref, akm_ref, avm_ref, o_ref,
                sc_ref, tau_ref, m_ref, l_ref, acc_ref, s_ref, bm_ref, *, nq, tq, tk, k_top):
    qi = pl.program_id(1)
    nkb = (qi + 1) * (tq // tk)
    kf = float(k_top)
    neg_inf = -jnp.inf
    qpos = qi * tq + lax.broadcasted_iota(jnp.int32, (1, tq), 1)

    part_shape = (SUBLANES, tq)

    def fold(x):
        return x.reshape(tk // SUBLANES, SUBLANES, tq).sum(axis=0)

    idx_scale = (IDX_DH ** -0.5) * (IDX_HEADS ** -0.5)
    iq = iq_ref[0].astype(BF16)
    w_t = wq_ref[0].T
    iq_h = [iq[:, h * IDX_DH:(h + 1) * IDX_DH] for h in range(IDX_HEADS)]
    w_h = [w_t[IDX_DH + h:IDX_DH + h + 1, :] for h in range(IDX_HEADS)]

    def score_rows(kb, n_blocks, carry):
        rmin, rmax = carry
        rows = n_blocks * tk
        ik = ikw_ref[0, pl.ds(pl.multiple_of(kb * tk, tk), rows), :][:, :IDX_DH].astype(BF16)
        acc = jnp.zeros((rows, tq), F32)
        for h in range(IDX_HEADS):
            acc = acc + w_h[h] * jnp.maximum(_dot_nt(ik, iq_h[h]), 0.0)
        ok = (lax.broadcasted_iota(jnp.int32, (rows, 1), 0) + kb * tk) <= qpos
        s = jnp.where(ok, acc * idx_scale, neg_inf)
        for i in range(n_blocks):
            sc_ref[kb + i] = s[i * tk:(i + 1) * tk]
        rmin = jnp.minimum(rmin, jnp.min(jnp.where(ok, s, jnp.inf), axis=0, keepdims=True))
        rmax = jnp.maximum(rmax, jnp.max(s, axis=0, keepdims=True))
        return rmin, rmax

    carry = (jnp.full((1, tq), jnp.inf, F32), jnp.full((1, tq), neg_inf, F32))
    carry = lax.fori_loop(0, nkb // 4, lambda j, c: score_rows(4 * j, 4, c), carry)
    carry = lax.cond(nkb % 4 >= 2, lambda c: score_rows((nkb // 4) * 4, 2, c), lambda c: c, carry)
    rmin, rmax = lax.cond(nkb % 2 == 1, lambda c: score_rows(nkb - 1, 1, c), lambda c: c, carry)

    def any_query(flag):
        return jnp.max(jnp.where(flag, 1.0, 0.0))

    def midpoint(lo, hi):
        a = jnp.maximum(jnp.abs(lo), TINY)
        b = jnp.maximum(jnp.abs(hi), TINY)
        geo = jnp.sqrt(a) * jnp.sqrt(b)
        one_sign = (lo >= 0.0) | (hi <= 0.0)
        far = jnp.maximum(a, b) > 4.0 * jnp.minimum(a, b)
        mid = jnp.where(one_sign & far, jnp.where(lo >= 0.0, geo, -geo), 0.5 * lo + 0.5 * hi)
        mid = jnp.where(lo == 0.0, TINY, mid)
        mid = jnp.where(hi == 0.0, -TINY, mid)
        return jnp.where((lo < 0.0) & (hi > 0.0), 0.0, mid)

    def active(lo, hi, cl):
        mid = midpoint(lo, hi)
        return (cl > kf) & (mid > lo) & (mid < hi)

    def go_on(st):
        return st[3] > 0.0

    def threshold(n_blocks, rmin, rmax):
        blocks = range(n_blocks)

        def over_keys(fn, init, combine):
            acc = init
            for kb in blocks:
                acc = combine(acc, fn(sc_ref[kb]))
            return acc

        def count(pred):
            c = over_keys(lambda s: fold(jnp.where(pred(s), 1.0, 0.0)), jnp.zeros(part_shape, F32), jnp.add)
            return jnp.sum(c, axis=0, keepdims=True)

        c_max = count(lambda s: s >= rmax)
        top_tied = c_max >= kf
        lo = jnp.where(top_tied, rmax, rmin)
        hi = rmax
        cl = jnp.where(top_tied, c_max, (qpos + 1).astype(F32))

        def shrink_once(lo, hi, cl):
            mid = midpoint(lo, hi)
            act = active(lo, hi, cl)
            c = count(lambda s: s >= mid)
            up = act & (c >= kf)
            dn = act & (c < kf)
            return jnp.where(up, mid, lo), jnp.where(dn, mid, hi), jnp.where(up, c, cl)

        def shrink(st):
            pending = any_query(active(*st[:3]))
            return (*shrink_once(*shrink_once(*st[:3])), pending)

        lo, hi, cl, _ = lax.while_loop(go_on, shrink, (lo, hi, cl, jnp.float32(1.0)))
        tau_ref[...] = lo
        tie = cl > kf

        @pl.when(any_query(tie) > 0.0)
        def _():
            def max_below(bound):
                part = over_keys(
                    lambda s: jnp.max(jnp.where(s < bound, s, neg_inf).reshape(tk // SUBLANES, SUBLANES, tq), axis=0),
                    jnp.full(part_shape, neg_inf, F32), jnp.maximum)
                return jnp.max(part, axis=0, keepdims=True)

            hi_d = jnp.where(lo >= hi, jnp.inf, hi)
            v0 = max_below(hi_d)
            ge0 = count(lambda s: s >= v0)

            def walk_on(st):
                return st[3] > 0.0

            def walk(st):
                hi_w, v, ge, _ = st
                hi_w = jnp.where(tie & (ge < kf), v, hi_w)
                v = max_below(hi_w)
                ge = count(lambda s: s >= v)
                return hi_w, v, ge, any_query(tie & (ge < kf))

            _, v, _, _ = lax.while_loop(walk_on, walk, (hi_d, v0, ge0, any_query(tie & (ge0 < kf))))
            keep = kf - count(lambda s: s > v)
            ri = lax.broadcasted_iota(jnp.int32, (tk, tk), 0)
            ci = lax.broadcasted_iota(jnp.int32, (tk, tk), 1)
            earlier = jnp.where(ci < ri, 1.0, 0.0).astype(BF16)
            seen = jnp.zeros((1, tq), F32)
            for kb in blocks:
                s = sc_ref[kb]
                eq = tie & (s == v)
                eqf = jnp.where(eq, 1.0, 0.0)
                rank = seen + _dot(earlier, eqf.astype(BF16))
                sc_ref[kb] = jnp.where(eq & (rank >= keep), neg_inf, s)
                seen = seen + jnp.sum(eqf, axis=0, keepdims=True)
            tau_ref[...] = jnp.where(tie, v, lo)

        return jnp.int32(0)

    branches = [functools.partial(threshold, (i + 1) * (tq // tk)) for i in range(nq)]
    lax.switch(qi, branches, rmin, rmax)

    tau = tau_ref[...]
    meta_bias = jnp.where(lax.broadcasted_iota(jnp.int32, (tk, 1), 0) < N_META, 0.0, neg_inf)

    def head(ref, h, rows=None):
        if rows is None:
            return ref[:, h * ATT_DH:(h + 1) * ATT_DH]
        return ref[0, rows, h * ATT_DH:(h + 1) * ATT_DH]

    def keys_of(kb):
        if kb is None:
            return (lambda h: head(akm_ref, h)), (lambda h: head(avm_ref, h)), meta_bias
        rows = pl.ds(pl.multiple_of(kb * tk, tk), tk)
        bias = jnp.where(sc_ref[kb] >= tau, 0.0, neg_inf)
        return (lambda h: head(ak_ref, h, rows)), (lambda h: head(av_ref, h, rows)), bias

    def score_phase(kb, buf):
        k_of, _, bias = keys_of(kb)
        for h in range(ATT_HEADS):
            s = _dot_nt(k_of(h), head(aq_ref, h, slice(None))) + bias
            s_ref[buf, h] = s
            bm_ref[buf, h:h + 1, :] = jnp.max(s, axis=0, keepdims=True)

    def value_phase(kb, buf):
        _, v_of, _ = keys_of(kb)
        for h in range(ATT_HEADS):
            m_new = jnp.maximum(m_ref[h:h + 1, :], bm_ref[buf, h:h + 1, :])
            p = jnp.exp2(s_ref[buf, h] - m_new)
            p_sum = jnp.sum(p, axis=0, keepdims=True)
            pv = lax.dot_general(v_of(h), p.astype(BF16), TN_DIMS, preferred_element_type=F32)
            alpha = jnp.exp2(m_ref[h:h + 1, :] - m_new)
            l_ref[h:h + 1, :] = alpha * l_ref[h:h + 1, :] + p_sum
            acc_ref[h] = alpha * acc_ref[h] + pv
            m_ref[h:h + 1, :] = m_new

    m_ref[...] = jnp.full_like(m_ref, neg_inf)
    l_ref[...] = jnp.zeros_like(l_ref)
    acc_ref[...] = jnp.zeros_like(acc_ref)
    score_phase(None, 0)

    value_phase(None, 0)
    score_phase(0, 1)

    def att_blocks(kb, n):
        for i in range(n):
            buf = 1 - i % 2
            value_phase(kb + i, buf)
            score_phase(jnp.minimum(kb + i + 1, nkb - 1), 1 - buf)

    def att_quad(j, carry):
        att_blocks(4 * j, 4)
        return carry

    lax.fori_loop(0, nkb // 4, att_quad, 0)

    @pl.when(nkb % 4 >= 2)
    def _():
        att_blocks((nkb // 4) * 4, 2)

    @pl.when(nkb % 2 == 1)
    def _():
        value_phase(nkb - 1, 1)

    for h in range(ATT_HEADS):
        out_t = acc_ref[h] / l_ref[h:h + 1, :]
        o_ref[0, :, h * ATT_DH:(h + 1) * ATT_DH] = out_t.T.astype(o_ref.dtype)


def _dsa(iq, ikw, aqk, av, akm, avm, *, tq, tk, k_top):
    B, S, _ = av.shape
    nq = S // tq
    assert tq % tk == 0
    return pl.pallas_call(
        functools.partial(_dsa_kernel, nq=nq, tq=tq, tk=tk, k_top=k_top),
        grid=(B, nq),
        in_specs=[
            pl.BlockSpec((1, tq, IDX_Q), lambda b, q: (b, q, 0)),
            pl.BlockSpec((1, S, LANES), lambda b, q: (b, 0, 0)),
            pl.BlockSpec((1, tq, LANES), lambda b, q: (b, q, 0)),
            pl.BlockSpec((1, tq, ATT_W), lambda b, q: (b, q, 0)),
            pl.BlockSpec((1, S, ATT_W), lambda b, q: (b, 0, 1)),
            pl.BlockSpec((1, S, ATT_W), lambda b, q: (b, 0, 0)),
            pl.BlockSpec((tk, ATT_W), lambda b, q: (0, 0)),
            pl.BlockSpec((tk, ATT_W), lambda b, q: (0, 0)),
        ],
        out_specs=pl.BlockSpec((1, tq, ATT_W), lambda b, q: (b, q, 0)),
        out_shape=jax.ShapeDtypeStruct((B, S, ATT_W), BF16),
        scratch_shapes=[
            pltpu.VMEM((S // tk, tk, tq), F32),
            pltpu.VMEM((1, tq), F32),
            pltpu.VMEM((ATT_HEADS, tq), F32),
            pltpu.VMEM((ATT_HEADS, tq), F32),
            pltpu.VMEM((ATT_HEADS, ATT_DH, tq), F32),
            pltpu.VMEM((2, ATT_HEADS, tk, tq), F32),
            pltpu.VMEM((2, ATT_HEADS, tq), F32),
        ],
        compiler_params=_params(("parallel", "arbitrary")),
        name="dsa",
    )(iq, ikw, ikw, aqk, aqk, av, akm, avm)


def _merge_kernel(h_ref, yr_ref, ya_ref, gr_ref, ga_ref, wr_ref, wa_ref, wm_ref, o_ref):
    yr = _dot(yr_ref[...], wr_ref[...])
    ya = _dot(ya_ref[...], wa_ref[...])
    merged = _sigmoid(gr_ref[...].astype(F32)) * yr + _sigmoid(ga_ref[...].astype(F32)) * ya
    o_ref[...] = h_ref[...] + _dot(merged.astype(BF16), wm_ref[...])


def _merge(h, yr, ya, gates, wr, wa, wm, *, tm):
    M, D = h.shape
    row = lambda i: (i, 0)
    fixed = lambda i: (0, 0)
    return pl.pallas_call(
        _merge_kernel,
        grid=(M // tm,),
        in_specs=[
            pl.BlockSpec((tm, D), row),
            pl.BlockSpec((tm, RET_V), row),
            pl.BlockSpec((tm, ATT_W), row),
            pl.BlockSpec((tm, D), lambda i: (i, 0)),
            pl.BlockSpec((tm, D), lambda i: (i, 1)),
            pl.BlockSpec((RET_V, D), fixed),
            pl.BlockSpec((ATT_W, D), fixed),
            pl.BlockSpec((D, D), fixed),
        ],
        out_specs=pl.BlockSpec((tm, D), row),
        out_shape=jax.ShapeDtypeStruct((M, D), F32),
        compiler_params=_params(("parallel",)),
        name="merge",
    )(h, yr, ya, gates, gates, wr, wa, wm)


def _rope_tables(pos, d, width):
    inv_freq = ROPE_THETA ** (-jnp.arange(0, d, 2, dtype=F32) / d)
    ang = pos.astype(F32)[:, None] * inv_freq[None, :]
    cos, sin = jnp.cos(ang), jnp.sin(ang)
    cos = jnp.concatenate([cos, cos], axis=-1)
    sin = jnp.concatenate([-sin, sin], axis=-1)
    rep = width // d
    return jnp.tile(cos, (1, rep)), jnp.tile(sin, (1, rep))


def _tile(n, cap):
    t = min(n, cap)
    assert n % t == 0, (n, t)
    return t


class _Plan(NamedTuple):
    rows: int
    rows_tab: int
    rows_wide: int
    rows_merge: int
    cols: int
    ffn_slab: int
    att: int
    chunk: int
    ret_rows: int


def _plan(S, M):
    return _Plan(rows=_tile(S, 4 * MXU_DIM), rows_tab=_tile(S, 8 * MXU_DIM), rows_wide=_tile(M, 8 * MXU_DIM),
                 rows_merge=_tile(M, 2 * MXU_DIM), cols=4 * MXU_DIM, ffn_slab=MXU_DIM, att=_tile(S, MXU_DIM),
                 chunk=_tile(S, 4 * RET_CHUNK), ret_rows=_tile(S, 4 * RET_CHUNK))


def kernel(x, meta_tokens, ffn1_norm, ffn1_w_gate, ffn1_w_up, ffn1_w_down, mix_norm, w_in, w_ret_out,
           w_att_out, w_mix_out, ffn2_norm, ffn2_w_gate, ffn2_w_up, ffn2_w_down, final_norm):
    B, S, D = x.shape
    assert ffn1_norm.shape[0] == 1, "single layer only"
    assert meta_tokens.shape[0] == N_META
    k_top = min(TOPK_MAX, S // 4)
    M = B * S
    plan = _plan(S, M)
    bf = lambda a: a.astype(BF16)
    row = lambda a: a.reshape(1, -1)

    w32 = w_in[0].T
    offs = np.cumsum([0, RET_QK, RET_QK, RET_V, RET_V, ATT_W, ATT_W, ATT_W, IDX_Q, IDX_DH, IDX_HEADS, D, D])
    tn = plan.cols
    assert all(int(o) % tn == 0 for o in offs[:8])
    c_rq, c_rk, c_rv, c_rg, c_aq, c_ak, c_av = (int(o) // tn for o in offs[:7])
    w_idx = jnp.pad(bf(w32[offs[7]:offs[10]].T), ((0, 0), (0, LANES - IDX_DH - IDX_HEADS)))
    w_gate = w32[offs[10]:offs[12]]

    pos_meta = jnp.arange(N_META, dtype=jnp.int32)
    pos_real = N_META + jnp.arange(S, dtype=jnp.int32)
    ffn1 = (row(ffn1_norm[0]), bf(ffn1_w_gate[0]), bf(ffn1_w_up[0]), bf(ffn1_w_down[0]), row(mix_norm[0]))
    ffn2 = (row(ffn2_norm[0]), bf(ffn2_w_gate[0]), bf(ffn2_w_up[0]), bf(ffn2_w_down[0]), row(final_norm))
    tf = plan.ffn_slab
    k_scale = RET_DK ** -0.5

    _, un_m = _ffn(meta_tokens.astype(F32), *ffn1, tm=N_META, tf=tf, emit_h=True, n_dtype=BF16)
    pm = dict(tm=N_META, tn=tn)
    rk_m = _proj(un_m, w32, col0=c_rk, n_tiles=RET_QK // tn, mode="rope", d=RET_DK, k_scale=k_scale,
                 tabs=_rope_tables(pos_meta, RET_DK, RET_DK), **pm)
    rv_m = _proj(un_m, w32, col0=c_rv, n_tiles=RET_V // tn, mode="plain", **pm)
    ak_m = _proj(un_m, w32, col0=c_ak, n_tiles=ATT_W // tn, mode="rope", d=ATT_DH,
                 tabs=_rope_tables(pos_meta, ATT_DH, LANES), **pm)
    av_m = _proj(un_m, w32, col0=c_av, n_tiles=ATT_W // tn, mode="plain", **pm)

    tm = plan.rows
    h1, un = _ffn(x.reshape(M, D), *ffn1, tm=tm, tf=tf, emit_h=True, n_dtype=BF16)
    pr = dict(tm=plan.rows_tab, tn=tn)
    pw = dict(tm=plan.rows_wide, tn=tn)
    rqk = _proj(un, w32, col0=c_rq, n_tiles=2 * RET_QK // tn, mode="rope", d=RET_DK, k_scale=k_scale,
                n_q=RET_QK // tn, tabs=_rope_tables(pos_real, RET_DK, RET_DK), **pr)
    rv = _proj(un, w32, col0=c_rv, n_tiles=RET_V // tn, mode="plain", **pw)
    rg = _proj(un, w32, col0=c_rg, n_tiles=RET_V // tn, mode="silu", **pw)
    aqk = _proj(un, w32, col0=c_aq, n_tiles=2 * ATT_W // tn, mode="rope", d=ATT_DH, q_scale=ATT_EXP_SCALE,
                n_q=ATT_W // tn, tabs=_rope_tables(pos_real, ATT_DH, LANES), **pr)
    av = _proj(un, w32, col0=c_av, n_tiles=ATT_W // tn, mode="plain", **pw)
    gates = _proj(un, w_gate, col0=0, n_tiles=2 * D // tn, mode="plain", **pw)
    iq, ikw = _idx_proj(un, w_idx, _rope_tables(pos_real, IDX_DH, LANES), tm=tm)

    heads = jnp.arange(RET_HEADS, dtype=F32)
    log_g = jnp.log1p(-(2.0 ** (-5.0 - heads)))
    C = plan.chunk
    b3 = lambda a: a.reshape(B, S, a.shape[-1])
    yr = _retention(log_g, jnp.exp(log_g * C), b3(rqk), b3(rv), b3(rg), rk_m, rv_m, C=C, R=plan.ret_rows)
    tk = plan.att
    pad_meta = lambda a: jnp.pad(a, ((0, tk - N_META), (0, 0)))
    ya = _dsa(b3(iq), b3(ikw), b3(aqk), b3(av), pad_meta(ak_m), pad_meta(av_m),
              tq=plan.att, tk=tk, k_top=k_top)

    h2 = _merge(h1, yr.reshape(M, RET_V), ya.reshape(M, ATT_W), gates, bf(w_ret_out[0]), bf(w_att_out[0]),
                bf(w_mix_out[0]), tm=plan.rows_merge)
    (y,) = _ffn(h2, *ffn2, tm=tm, tf=tf, emit_h=False, n_dtype=F32)
    return y.reshape(B, S, D)
```

```python
import functools
from typing import NamedTuple

import jax
import jax.numpy as jnp
import numpy as np
from jax import lax
from jax.experimental import pallas as pl
from jax.experimental.pallas import tpu as pltpu

N_META = 16
RET_HEADS = 4
RET_DK = 256
RET_DV = 512
RET_CHUNK = 128
ATT_HEADS = 8
ATT_DH = 128
IDX_HEADS = 8
IDX_DH = 64
TOPK_MAX = 256
ROPE_THETA = 10000.0
EPS = 1e-6
RET_QK = RET_HEADS * RET_DK
RET_V = RET_HEADS * RET_DV
ATT_W = ATT_HEADS * ATT_DH
IDX_Q = IDX_HEADS * IDX_DH
ATT_EXP_SCALE = (ATT_DH ** -0.5) * float(np.log2(np.e))

LANES = 128
SUBLANES = 8
MXU_DIM = 256
PROJ_SUB_ROWS = MXU_DIM
VMEM_LIMIT = 48 * 1024 * 1024
FFN_VMEM_LIMIT = 56 * 1024 * 1024

BF16 = jnp.bfloat16
F32 = jnp.float32
NT_DIMS = (((1,), (1,)), ((), ()))


def _dot(a, b):
    return jnp.dot(a, b, preferred_element_type=F32)


def _dot_nt(a, b):
    return lax.dot_general(a, b, NT_DIMS, preferred_element_type=F32)


def _rms(x, g):
    return x * lax.rsqrt(jnp.mean(x * x, axis=-1, keepdims=True) + EPS) * g


def _sigmoid(x):
    return 0.5 * jnp.tanh(0.5 * x) + 0.5


def _params(sem):
    return pltpu.CompilerParams(dimension_semantics=sem, vmem_limit_bytes=VMEM_LIMIT)


def _ffn_kernel(x_ref, g_ref, wg_ref, wu_ref, wd_ref, g2_ref, *rest, nf, tf, emit_h):
    if emit_h:
        h_ref, n_ref, xn_sc, acc_sc, act_sc = rest
    else:
        n_ref, xn_sc, acc_sc, act_sc = rest

    def down(f):
        rows = pl.ds(pl.multiple_of(f * tf, tf), tf)
        return _dot(act_sc[...], wd_ref[rows, :])

    def up(f):
        cols = pl.ds(pl.multiple_of(f * tf, tf), tf)
        xn = xn_sc[...]
        gate = _dot(xn, wg_ref[:, cols])
        lift = _dot(xn, wu_ref[:, cols])
        act_sc[...] = ((gate * _sigmoid(gate)) * lift).astype(BF16)

    xn_sc[...] = _rms(x_ref[...], g_ref[...]).astype(BF16)
    acc_sc[...] = jnp.zeros_like(acc_sc)
    up(0)

    def slab(f, carry):
        acc_sc[...] += down(f - 1)
        up(f)
        return carry

    lax.fori_loop(1, nf, slab, 0, unroll=True)
    h = x_ref[...] + 0.5 * (acc_sc[...] + down(nf - 1))
    if emit_h:
        h_ref[...] = h
    n_ref[...] = _rms(h, g2_ref[...]).astype(n_ref.dtype)


def _ffn(x, g, wg, wu, wd, g2, *, tm, tf, emit_h, n_dtype):
    M, D = x.shape
    F = wg.shape[1]
    nf = F // tf
    assert nf >= 2
    out_shape = [jax.ShapeDtypeStruct((M, D), n_dtype)]
    out_specs = [pl.BlockSpec((tm, D), lambda i: (i, 0))]
    if emit_h:
        out_shape = [jax.ShapeDtypeStruct((M, D), F32)] + out_shape
        out_specs = [pl.BlockSpec((tm, D), lambda i: (i, 0))] + out_specs
    resident = lambda shape: pl.BlockSpec(shape, lambda i: (0, 0), pipeline_mode=pl.Buffered(1))
    return pl.pallas_call(
        functools.partial(_ffn_kernel, nf=nf, tf=tf, emit_h=emit_h),
        grid=(M // tm,),
        in_specs=[
            pl.BlockSpec((tm, D), lambda i: (i, 0)),
            pl.BlockSpec((1, D), lambda i: (0, 0)),
            resident((D, F)),
            resident((D, F)),
            resident((F, D)),
            pl.BlockSpec((1, D), lambda i: (0, 0)),
        ],
        out_specs=out_specs,
        out_shape=out_shape,
        scratch_shapes=[pltpu.VMEM((tm, D), BF16), pltpu.VMEM((tm, D), F32), pltpu.VMEM((tm, tf), BF16)],
        compiler_params=pltpu.CompilerParams(
            dimension_semantics=("arbitrary",), vmem_limit_bytes=FFN_VMEM_LIMIT),
        name="ffn",
    )(x, g, wg, wu, wd, g2)


def _rot_half(y, d):
    w = y.shape[-1]
    if d >= w:
        return jnp.concatenate([y[:, w // 2:], y[:, : w // 2]], axis=-1)
    if d == LANES:
        return pltpu.roll(y, d // 2, axis=1)
    fwd = pltpu.roll(y, d // 2, axis=1)
    bwd = pltpu.roll(y, w - d // 2, axis=1)
    lane = lax.broadcasted_iota(jnp.int32, y.shape, 1)
    return jnp.where((lane % d) < d // 2, bwd, fwd)


def _rope_tile(y, cos, sin, d):
    tw = cos.shape[-1]
    outs = []
    for t in range(y.shape[-1] // tw):
        yt = y[:, t * tw:(t + 1) * tw]
        outs.append(yt * cos + _rot_half(yt, d) * sin)
    return outs[0] if len(outs) == 1 else jnp.concatenate(outs, axis=-1)


def _proj_kernel(x_ref, w_ref, *rest, mode, d, q_scale, k_scale, n_q):
    if mode == "rope":
        cos_ref, sin_ref, o_ref, w_sc = rest
    else:
        o_ref, w_sc = rest

    @pl.when(pl.program_id(1) == 0)
    def _():
        w_sc[...] = w_ref[...].T.astype(BF16)

    tm = x_ref.shape[0]
    sub = min(tm, PROJ_SUB_ROWS) if mode == "rope" else tm
    for r0 in range(0, tm, sub):
        rows = slice(r0, r0 + sub)
        y = _dot(x_ref[rows, :], w_sc[...])
        if mode == "rope":
            y = _rope_tile(y, cos_ref[rows, :], sin_ref[rows, :], d)
            if q_scale != 1.0 or k_scale != 1.0:
                y = y * jnp.where(pl.program_id(0) >= n_q, k_scale, q_scale)
        elif mode == "silu":
            y = y * _sigmoid(y)
        o_ref[rows, :] = y.astype(o_ref.dtype)


def _proj(x, w_t, *, col0, n_tiles, tm, tn, mode, out_dtype=None, tabs=None, d=None, q_scale=1.0,
          k_scale=1.0, n_q=0):
    M, K = x.shape
    in_specs = [
        pl.BlockSpec((tm, K), lambda j, i: (i, 0)),
        pl.BlockSpec((tn, K), lambda j, i: (col0 + j, 0)),
    ]
    args = [x, w_t]
    if mode == "rope":
        cos, sin = tabs
        npos = cos.shape[0] // tm
        tw = cos.shape[1]
        in_specs += [pl.BlockSpec((tm, tw), lambda j, i: (i % npos, 0))] * 2
        args += [cos, sin]
    return pl.pallas_call(
        functools.partial(_proj_kernel, mode=mode, d=d, q_scale=q_scale, k_scale=k_scale, n_q=n_q),
        grid=(n_tiles, M // tm),
        in_specs=in_specs,
        out_specs=pl.BlockSpec((tm, tn), lambda j, i: (i, j)),
        out_shape=jax.ShapeDtypeStruct((M, n_tiles * tn), out_dtype or BF16),
        scratch_shapes=[pltpu.VMEM((K, tn), BF16)],
        compiler_params=_params(("arbitrary", "arbitrary")),
        name="proj_" + mode,
    )(*args)


def _idx_proj_kernel(x_ref, w_ref, cos_ref, sin_ref, q_ref, kw_ref):
    tm = x_ref.shape[0]
    sub = min(tm, PROJ_SUB_ROWS)
    for r0 in range(0, tm, sub):
        rows = slice(r0, r0 + sub)
        y = _dot(x_ref[rows, :], w_ref[...])
        cos, sin = cos_ref[rows, :], sin_ref[rows, :]
        q_ref[rows, :] = _rope_tile(y[:, :IDX_Q], cos, sin, IDX_DH)
        tail = y[:, IDX_Q:]
        lane = lax.broadcasted_iota(jnp.int32, tail.shape, 1)
        kw_ref[rows, :] = jnp.where(lane < IDX_DH, _rope_tile(tail, cos, sin, IDX_DH), tail)


def _idx_proj(x, w, tabs, *, tm):
    M, K = x.shape
    N = w.shape[1]
    cos, sin = tabs
    npos = cos.shape[0] // tm
    return pl.pallas_call(
        _idx_proj_kernel,
        grid=(M // tm,),
        in_specs=[
            pl.BlockSpec((tm, K), lambda i: (i, 0)),
            pl.BlockSpec((K, N), lambda i: (0, 0)),
            pl.BlockSpec((tm, LANES), lambda i: (i % npos, 0)),
            pl.BlockSpec((tm, LANES), lambda i: (i % npos, 0)),
        ],
        out_specs=[
            pl.BlockSpec((tm, IDX_Q), lambda i: (i, 0)),
            pl.BlockSpec((tm, LANES), lambda i: (i, 0)),
        ],
        out_shape=[
            jax.ShapeDtypeStruct((M, IDX_Q), F32),
            jax.ShapeDtypeStruct((M, LANES), F32),
        ],
        compiler_params=_params(("parallel",)),
        name="proj_idx",
    )(x, w, cos, sin)


def _ret_kernel(lg_ref, gc_ref, q_ref, k_ref, v_ref, g_ref, km_ref, vm_ref, o_ref, st_ref, decay_ref, *, C):
    c = pl.program_id(1)
    dk = lambda h: slice(h * RET_DK, (h + 1) * RET_DK)
    dv = lambda h: slice(h * RET_DV, (h + 1) * RET_DV)

    @pl.when(c == 0)
    def _():
        pm = lax.broadcasted_iota(jnp.int32, (N_META, 1), 0).astype(F32)
        ri = lax.broadcasted_iota(jnp.int32, (C, C), 0)
        ci = lax.broadcasted_iota(jnp.int32, (C, C), 1)
        diff = (ri - ci).astype(F32)
        for h in range(RET_HEADS):
            lg = lg_ref[h]
            zeta_m = jnp.exp(lg * (N_META - 1.0 - pm))
            kz = km_ref[:, dk(h)].astype(F32) * zeta_m
            st_ref[h] = _dot(kz.T.astype(BF16), vm_ref[:, dv(h)])
            decay_ref[h] = jnp.where(diff >= 0, jnp.exp(lg * jnp.maximum(diff, 0.0)), 0.0)

    pos = lax.broadcasted_iota(jnp.int32, (C, 1), 0).astype(F32)
    for h in range(RET_HEADS):
        lg = lg_ref[h]
        xi = jnp.exp(lg * (pos + 1.0))
        zeta = jnp.exp(lg * (C - 1.0 - pos))
        state = st_ref[h]
        for r0 in range(0, q_ref.shape[1], C):
            rows = slice(r0, r0 + C)
            q = q_ref[0, rows, dk(h)]
            k = k_ref[0, rows, dk(h)]
            v = v_ref[0, rows, dv(h)]
            inner = _dot((_dot_nt(q, k) * decay_ref[h]).astype(BF16), v)
            cross = _dot(q, state.astype(BF16)) * xi
            y = inner + cross
            y = y * lax.rsqrt(jnp.mean(y * y, axis=-1, keepdims=True) + EPS)
            o_ref[0, rows, dv(h)] = (g_ref[0, rows, dv(h)] * y).astype(o_ref.dtype)
            kz = k.astype(F32) * zeta
            state = gc_ref[h] * state + _dot(kz.T.astype(BF16), v)
        st_ref[h] = state


def _retention(lg, gc, rqk, rv, rg, km, vm, *, C, R):
    B, S, _ = rv.shape
    smem = pl.BlockSpec(memory_space=pltpu.SMEM)
    return pl.pallas_call(
        functools.partial(_ret_kernel, C=C),
        grid=(B, S // R),
        in_specs=[
            smem,
            smem,
            pl.BlockSpec((1, R, RET_QK), lambda b, c: (b, c, 0)),
            pl.BlockSpec((1, R, RET_QK), lambda b, c: (b, c, 1)),
            pl.BlockSpec((1, R, RET_V), lambda b, c: (b, c, 0)),
            pl.BlockSpec((1, R, RET_V), lambda b, c: (b, c, 0)),
            pl.BlockSpec((N_META, RET_QK), lambda b, c: (0, 0)),
            pl.BlockSpec((N_META, RET_V), lambda b, c: (0, 0)),
        ],
        out_specs=pl.BlockSpec((1, R, RET_V), lambda b, c: (b, c, 0)),
        out_shape=jax.ShapeDtypeStruct((B, S, RET_V), BF16),
        scratch_shapes=[pltpu.VMEM((RET_HEADS, RET_DK, RET_DV), F32), pltpu.VMEM((RET_HEADS, C, C), F32)],
        compiler_params=_params(("parallel", "arbitrary")),
        name="retention",
    )(lg, gc, rqk, rqk, rv, rg, km, vm)


TINY = float(np.finfo(np.float32).tiny)
TN_DIMS = (((0,), (0,)), ((), ()))


def _dsa_kernel(iq_ref, ikw_ref, wq_ref, aq_ref, ak_ref, av_ref, akm_ref, avm_ref, o_ref,
                sc_ref, tau_ref, m_ref, l_ref, acc_ref, s_ref, bm_ref, *, nq, tq, tk, k_top):
    qi = pl.program_id(1)
    nkb = (qi + 1) * (tq // tk)
    kf = float(k_top)
    neg_inf = -jnp.inf
    qpos = qi * tq + lax.broadcasted_iota(jnp.int32, (1, tq), 1)

    part_shape = (SUBLANES, tq)

    def fold(x):
        return x.reshape(tk // SUBLANES, SUBLANES, tq).sum(axis=0)

    idx_scale = (IDX_DH ** -0.5) * (IDX_HEADS ** -0.5)
    iq = iq_ref[0].astype(BF16)
    w_t = wq_ref[0].T
    iq_h = [iq[:, h * IDX_DH:(h + 1) * IDX_DH] for h in range(IDX_HEADS)]
    w_h = [w_t[IDX_DH + h:IDX_DH + h + 1, :] for h in range(IDX_HEADS)]

    def score_rows(kb, n_blocks, carry):
        rmin, rmax = carry
        rows = n_blocks * tk
        ik = ikw_ref[0, pl.ds(pl.multiple_of(kb * tk, tk), rows), :][:, :IDX_DH].astype(BF16)
        acc = jnp.zeros((rows, tq), F32)
        for h in range(IDX_HEADS):
            acc = acc + w_h[h] * jnp.maximum(_dot_nt(ik, iq_h[h]), 0.0)
        ok = (lax.broadcasted_iota(jnp.int32, (rows, 1), 0) + kb * tk) <= qpos
        s = jnp.where(ok, acc * idx_scale, neg_inf)
        for i in range(n_blocks):
            sc_ref[kb + i] = s[i * tk:(i + 1) * tk]
        rmin = jnp.minimum(rmin, jnp.min(jnp.where(ok, s, jnp.inf), axis=0, keepdims=True))
        rmax = jnp.maximum(rmax, jnp.max(s, axis=0, keepdims=True))
        return rmin, rmax

    carry = (jnp.full((1, tq), jnp.inf, F32), jnp.full((1, tq), neg_inf, F32))
    carry = lax.fori_loop(0, nkb // 4, lambda j, c: score_rows(4 * j, 4, c), carry)
    carry = lax.cond(nkb % 4 >= 2, lambda c: score_rows((nkb // 4) * 4, 2, c), lambda c: c, carry)
    rmin, rmax = lax.cond(nkb % 2 == 1, lambda c: score_rows(nkb - 1, 1, c), lambda c: c, carry)

    def any_query(flag):
        return jnp.max(jnp.where(flag, 1.0, 0.0))

    def midpoint(lo, hi):
        a = jnp.maximum(jnp.abs(lo), TINY)
        b = jnp.maximum(jnp.abs(hi), TINY)
        geo = jnp.sqrt(a) * jnp.sqrt(b)
        one_sign = (lo >= 0.0) | (hi <= 0.0)
        far = jnp.maximum(a, b) > 4.0 * jnp.minimum(a, b)
        mid = jnp.where(one_sign & far, jnp.where(lo >= 0.0, geo, -geo), 0.5 * lo + 0.5 * hi)
        mid = jnp.where(lo == 0.0, TINY, mid)
        mid = jnp.where(hi == 0.0, -TINY, mid)
        return jnp.where((lo < 0.0) & (hi > 0.0), 0.0, mid)

    def active(lo, hi, cl):
        mid = midpoint(lo, hi)
        return (cl > kf) & (mid > lo) & (mid < hi)

    def go_on(st):
        return st[3] > 0.0

    def threshold(n_blocks, rmin, rmax):
        blocks = range(n_blocks)

        def over_keys(fn, init, combine):
            acc = init
            for kb in blocks:
                acc = combine(acc, fn(sc_ref[kb]))
            return acc

        def count(pred):
            c = over_keys(lambda s: fold(jnp.where(pred(s), 1.0, 0.0)), jnp.zeros(part_shape, F32), jnp.add)
            return jnp.sum(c, axis=0, keepdims=True)

        c_max = count(lambda s: s >= rmax)
        top_tied = c_max >= kf
        lo = jnp.where(top_tied, rmax, rmin)
        hi = rmax
        cl = jnp.where(top_tied, c_max, (qpos + 1).astype(F32))

        def shrink_once(lo, hi, cl):
            mid = midpoint(lo, hi)
            act = active(lo, hi, cl)
            c = count(lambda s: s >= mid)
            up = act & (c >= kf)
            dn = act & (c < kf)
            return jnp.where(up, mid, lo), jnp.where(dn, mid, hi), jnp.where(up, c, cl)

        def shrink(st):
            pending = any_query(active(*st[:3]))
            return (*shrink_once(*shrink_once(*st[:3])), pending)

        lo, hi, cl, _ = lax.while_loop(go_on, shrink, (lo, hi, cl, jnp.float32(1.0)))
        tau_ref[...] = lo
        tie = cl > kf

        @pl.when(any_query(tie) > 0.0)
        def _():
            def max_below(bound):
                part = over_keys(
                    lambda s: jnp.max(jnp.where(s < bound, s, neg_inf).reshape(tk // SUBLANES, SUBLANES, tq), axis=0),
                    jnp.full(part_shape, neg_inf, F32), jnp.maximum)
                return jnp.max(part, axis=0, keepdims=True)

            hi_d = jnp.where(lo >= hi, jnp.inf, hi)
            v0 = max_below(hi_d)
            ge0 = count(lambda s: s >= v0)

            def walk_on(st):
                return st[3] > 0.0

            def walk(st):
                hi_w, v, ge, _ = st
                hi_w = jnp.where(tie & (ge < kf), v, hi_w)
                v = max_below(hi_w)
                ge = count(lambda s: s >= v)
                return hi_w, v, ge, any_query(tie & (ge < kf))

            _, v, _, _ = lax.while_loop(walk_on, walk, (hi_d, v0, ge0, any_query(tie & (ge0 < kf))))
            keep = kf - count(lambda s: s > v)
            ri = lax.broadcasted_iota(jnp.int32, (tk, tk), 0)
            ci = lax.broadcasted_iota(jnp.int32, (tk, tk), 1)
            earlier = jnp.where(ci < ri, 1.0, 0.0).astype(BF16)
            seen = jnp.zeros((1, tq), F32)
            for kb in blocks:
                s = sc_ref[kb]
                eq = tie & (s == v)
                eqf = jnp.where(eq, 1.0, 0.0)
                rank = seen + _dot(earlier, eqf.astype(BF16))
                sc_ref[kb] = jnp.where(eq & (rank >= keep), neg_inf, s)
                seen = seen + jnp.sum(eqf, axis=0, keepdims=True)
            tau_ref[...] = jnp.where(tie, v, lo)

        return jnp.int32(0)

    branches = [functools.partial(threshold, (i + 1) * (tq // tk)) for i in range(nq)]
    lax.switch(qi, branches, rmin, rmax)

    tau = tau_ref[...]

    def head(ref, h, rows=None):
        if rows is None:
            return ref[:, h * ATT_DH:(h + 1) * ATT_DH]
        return ref[0, rows, h * ATT_DH:(h + 1) * ATT_DH]

    def score_phase(kb, buf):
        rows = pl.ds(pl.multiple_of(kb * tk, tk), tk)
        bias = jnp.where(sc_ref[kb] >= tau, 0.0, neg_inf)
        for h in range(ATT_HEADS):
            s = _dot_nt(head(ak_ref, h, rows), head(aq_ref, h, slice(None))) + bias
            s_ref[buf, h] = s
            bm_ref[buf, h:h + 1, :] = jnp.max(s, axis=0, keepdims=True)

    def value_phase(kb, buf):
        rows = pl.ds(pl.multiple_of(kb * tk, tk), tk)
        for h in range(ATT_HEADS):
            m_new = jnp.maximum(m_ref[h:h + 1, :], bm_ref[buf, h:h + 1, :])
            p = jnp.exp2(s_ref[buf, h] - m_new)
            p_sum = jnp.sum(p, axis=0, keepdims=True)
            pv = lax.dot_general(head(av_ref, h, rows), p.astype(BF16), TN_DIMS, preferred_element_type=F32)
            alpha = jnp.exp2(m_ref[h:h + 1, :] - m_new)
            l_ref[h:h + 1, :] = alpha * l_ref[h:h + 1, :] + p_sum
            acc_ref[h] = alpha * acc_ref[h] + pv
            m_ref[h:h + 1, :] = m_new

    for h in range(ATT_HEADS):
        s = _dot_nt(head(akm_ref, h), head(aq_ref, h, slice(None)))
        m = jnp.max(s, axis=0, keepdims=True)
        p = jnp.exp2(s - m)
        m_ref[h:h + 1, :] = m
        l_ref[h:h + 1, :] = jnp.sum(p, axis=0, keepdims=True)
        acc_ref[h] = lax.dot_general(head(avm_ref, h), p.astype(BF16), TN_DIMS, preferred_element_type=F32)
    score_phase(0, 1)

    def att_blocks(kb, n):
        for i in range(n):
            buf = 1 - i % 2
            value_phase(kb + i, buf)
            score_phase(jnp.minimum(kb + i + 1, nkb - 1), 1 - buf)

    def att_quad(j, carry):
        att_blocks(4 * j, 4)
        return carry

    lax.fori_loop(0, nkb // 4, att_quad, 0)

    @pl.when(nkb % 4 >= 2)
    def _():
        att_blocks((nkb // 4) * 4, 2)

    @pl.when(nkb % 2 == 1)
    def _():
        value_phase(nkb - 1, 1)

    for h in range(ATT_HEADS):
        out_t = acc_ref[h] / l_ref[h:h + 1, :]
        o_ref[0, :, h * ATT_DH:(h + 1) * ATT_DH] = out_t.T.astype(o_ref.dtype)


def _dsa(iq, ikw, aqk, av, akm, avm, *, tq, tk, k_top):
    B, S, _ = av.shape
    nq = S // tq
    assert tq % tk == 0
    return pl.pallas_call(
        functools.partial(_dsa_kernel, nq=nq, tq=tq, tk=tk, k_top=k_top),
        grid=(B, nq),
        in_specs=[
            pl.BlockSpec((1, tq, IDX_Q), lambda b, q: (b, q, 0)),
            pl.BlockSpec((1, S, LANES), lambda b, q: (b, 0, 0)),
            pl.BlockSpec((1, tq, LANES), lambda b, q: (b, q, 0)),
            pl.BlockSpec((1, tq, ATT_W), lambda b, q: (b, q, 0)),
            pl.BlockSpec((1, S, ATT_W), lambda b, q: (b, 0, 1)),
            pl.BlockSpec((1, S, ATT_W), lambda b, q: (b, 0, 0)),
            pl.BlockSpec((N_META, ATT_W), lambda b, q: (0, 0)),
            pl.BlockSpec((N_META, ATT_W), lambda b, q: (0, 0)),
        ],
        out_specs=pl.BlockSpec((1, tq, ATT_W), lambda b, q: (b, q, 0)),
        out_shape=jax.ShapeDtypeStruct((B, S, ATT_W), BF16),
        scratch_shapes=[
            pltpu.VMEM((S // tk, tk, tq), F32),
            pltpu.VMEM((1, tq), F32),
            pltpu.VMEM((ATT_HEADS, tq), F32),
            pltpu.VMEM((ATT_HEADS, tq), F32),
            pltpu.VMEM((ATT_HEADS, ATT_DH, tq), F32),
            pltpu.VMEM((2, ATT_HEADS, tk, tq), F32),
            pltpu.VMEM((2, ATT_HEADS, tq), F32),
        ],
        compiler_params=_params(("parallel", "arbitrary")),
        name="dsa",
    )(iq, ikw, ikw, aqk, aqk, av, akm, avm)


def _merge_kernel(h_ref, yr_ref, ya_ref, gr_ref, ga_ref, wr_ref, wa_ref, wm_ref, o_ref):
    yr = _dot(yr_ref[...], wr_ref[...])
    ya = _dot(ya_ref[...], wa_ref[...])
    merged = _sigmoid(gr_ref[...].astype(F32)) * yr + _sigmoid(ga_ref[...].astype(F32)) * ya
    o_ref[...] = h_ref[...] + _dot(merged.astype(BF16), wm_ref[...])


def _merge(h, yr, ya, gates, wr, wa, wm, *, tm):
    M, D = h.shape
    row = lambda i: (i, 0)
    fixed = lambda i: (0, 0)
    return pl.pallas_call(
        _merge_kernel,
        grid=(M // tm,),
        in_specs=[
            pl.BlockSpec((tm, D), row),
            pl.BlockSpec((tm, RET_V), row),
            pl.BlockSpec((tm, ATT_W), row),
            pl.BlockSpec((tm, D), lambda i: (i, 0)),
            pl.BlockSpec((tm, D), lambda i: (i, 1)),
            pl.BlockSpec((RET_V, D), fixed),
            pl.BlockSpec((ATT_W, D), fixed),
            pl.BlockSpec((D, D), fixed),
        ],
        out_specs=pl.BlockSpec((tm, D), row),
        out_shape=jax.ShapeDtypeStruct((M, D), F32),
        compiler_params=_params(("parallel",)),
        name="merge",
    )(h, yr, ya, gates, gates, wr, wa, wm)


def _rope_tables(pos, d, width):
    inv_freq = ROPE_THETA ** (-jnp.arange(0, d, 2, dtype=F32) / d)
    ang = pos.astype(F32)[:, None] * inv_freq[None, :]
    cos, sin = jnp.cos(ang), jnp.sin(ang)
    cos = jnp.concatenate([cos, cos], axis=-1)
    sin = jnp.concatenate([-sin, sin], axis=-1)
    rep = width // d
    return jnp.tile(cos, (1, rep)), jnp.tile(sin, (1, rep))


def _tile(n, cap):
    t = min(n, cap)
    assert n % t == 0, (n, t)
    return t


class _Plan(NamedTuple):
    rows: int
    rows_tab: int
    rows_wide: int
    rows_merge: int
    cols: int
    ffn_slab: int
    att: int
    chunk: int
    ret_rows: int


def _plan(S, M):
    return _Plan(rows=_tile(S, 4 * MXU_DIM), rows_tab=_tile(S, 8 * MXU_DIM), rows_wide=_tile(M, 8 * MXU_DIM),
                 rows_merge=_tile(M, 2 * MXU_DIM), cols=4 * MXU_DIM, ffn_slab=MXU_DIM, att=_tile(S, MXU_DIM),
                 chunk=_tile(S, 4 * RET_CHUNK), ret_rows=_tile(S, 4 * RET_CHUNK))


def kernel(x, meta_tokens, ffn1_norm, ffn1_w_gate, ffn1_w_up, ffn1_w_down, mix_norm, w_in, w_ret_out,
           w_att_out, w_mix_out, ffn2_norm, ffn2_w_gate, ffn2_w_up, ffn2_w_down, final_norm):
    B, S, D = x.shape
    assert ffn1_norm.shape[0] == 1, "single layer only"
    assert meta_tokens.shape[0] == N_META
    k_top = min(TOPK_MAX, S // 4)
    M = B * S
    plan = _plan(S, M)
    bf = lambda a: a.astype(BF16)
    row = lambda a: a.reshape(1, -1)

    w32 = w_in[0].T
    offs = np.cumsum([0, RET_QK, RET_QK, RET_V, RET_V, ATT_W, ATT_W, ATT_W, IDX_Q, IDX_DH, IDX_HEADS, D, D])
    tn = plan.cols
    assert all(int(o) % tn == 0 for o in offs[:8])
    c_rq, c_rk, c_rv, c_rg, c_aq, c_ak, c_av = (int(o) // tn for o in offs[:7])
    w_idx = jnp.pad(bf(w32[offs[7]:offs[10]].T), ((0, 0), (0, LANES - IDX_DH - IDX_HEADS)))
    w_gate = w32[offs[10]:offs[12]]

    pos_meta = jnp.arange(N_META, dtype=jnp.int32)
    pos_real = N_META + jnp.arange(S, dtype=jnp.int32)
    ffn1 = (row(ffn1_norm[0]), bf(ffn1_w_gate[0]), bf(ffn1_w_up[0]), bf(ffn1_w_down[0]), row(mix_norm[0]))
    ffn2 = (row(ffn2_norm[0]), bf(ffn2_w_gate[0]), bf(ffn2_w_up[0]), bf(ffn2_w_down[0]), row(final_norm))
    tf = plan.ffn_slab
    k_scale = RET_DK ** -0.5

    _, un_m = _ffn(meta_tokens.astype(F32), *ffn1, tm=N_META, tf=tf, emit_h=True, n_dtype=BF16)
    pm = dict(tm=N_META, tn=tn)
    rk_m = _proj(un_m, w32, col0=c_rk, n_tiles=RET_QK // tn, mode="rope", d=RET_DK, k_scale=k_scale,
                 tabs=_rope_tables(pos_meta, RET_DK, RET_DK), **pm)
    rv_m = _proj(un_m, w32, col0=c_rv, n_tiles=RET_V // tn, mode="plain", **pm)
    ak_m = _proj(un_m, w32, col0=c_ak, n_tiles=ATT_W // tn, mode="rope", d=ATT_DH,
                 tabs=_rope_tables(pos_meta, ATT_DH, LANES), **pm)
    av_m = _proj(un_m, w32, col0=c_av, n_tiles=ATT_W // tn, mode="plain", **pm)

    tm = plan.rows
    h1, un = _ffn(x.reshape(M, D), *ffn1, tm=tm, tf=tf, emit_h=True, n_dtype=BF16)
    pr = dict(tm=plan.rows_tab, tn=tn)
    pw = dict(tm=plan.rows_wide, tn=tn)
    rqk = _proj(un, w32, col0=c_rq, n_tiles=2 * RET_QK // tn, mode="rope", d=RET_DK, k_scale=k_scale,
                n_q=RET_QK // tn, tabs=_rope_tables(pos_real, RET_DK, RET_DK), **pr)
    rv = _proj(un, w32, col0=c_rv, n_tiles=RET_V // tn, mode="plain", **pw)
    rg = _proj(un, w32, col0=c_rg, n_tiles=RET_V // tn, mode="silu", **pw)
    aqk = _proj(un, w32, col0=c_aq, n_tiles=2 * ATT_W // tn, mode="rope", d=ATT_DH, q_scale=ATT_EXP_SCALE,
                n_q=ATT_W // tn, tabs=_rope_tables(pos_real, ATT_DH, LANES), **pr)
    av = _proj(un, w32, col0=c_av, n_tiles=ATT_W // tn, mode="plain", **pw)
    gates = _proj(un, w_gate, col0=0, n_tiles=2 * D // tn, mode="plain", **pw)
    iq, ikw = _idx_proj(un, w_idx, _rope_tables(pos_real, IDX_DH, LANES), tm=tm)

    heads = jnp.arange(RET_HEADS, dtype=F32)
    log_g = jnp.log1p(-(2.0 ** (-5.0 - heads)))
    C = plan.chunk
    b3 = lambda a: a.reshape(B, S, a.shape[-1])
    yr = _retention(log_g, jnp.exp(log_g * C), b3(rqk), b3(rv), b3(rg), rk_m, rv_m, C=C, R=plan.ret_rows)
    ya = _dsa(b3(iq), b3(ikw), b3(aqk), b3(av), ak_m, av_m, tq=plan.att, tk=plan.att, k_top=k_top)

    h2 = _merge(h1, yr.reshape(M, RET_V), ya.reshape(M, ATT_W), gates, bf(w_ret_out[0]), bf(w_att_out[0]),
                bf(w_mix_out[0]), tm=plan.rows_merge)
    (y,) = _ffn(h2, *ffn2, tm=tm, tf=tf, emit_h=False, n_dtype=F32)
    return y.reshape(B, S, D)
```

```python
import functools
from typing import NamedTuple

import jax
import jax.numpy as jnp
import numpy as np
from jax import lax
from jax.experimental import pallas as pl
from jax.experimental.pallas import tpu as pltpu

N_META = 16
RET_HEADS = 4
RET_DK = 256
RET_DV = 512
RET_CHUNK = 128
ATT_HEADS = 8
ATT_DH = 128
IDX_HEADS = 8
IDX_DH = 64
TOPK_MAX = 256
ROPE_THETA = 10000.0
EPS = 1e-6
RET_QK = RET_HEADS * RET_DK
RET_V = RET_HEADS * RET_DV
ATT_W = ATT_HEADS * ATT_DH
IDX_Q = IDX_HEADS * IDX_DH
ATT_EXP_SCALE = (ATT_DH ** -0.5) * float(np.log2(np.e))

LANES = 128
SUBLANES = 8
MXU_DIM = 256
PROJ_SUB_ROWS = MXU_DIM
VMEM_LIMIT = 48 * 1024 * 1024
FFN_VMEM_LIMIT = 56 * 1024 * 1024

BF16 = jnp.bfloat16
F32 = jnp.float32
NT_DIMS = (((1,), (1,)), ((), ()))


def _dot(a, b):
    return jnp.dot(a, b, preferred_element_type=F32)


def _dot_nt(a, b):
    return lax.dot_general(a, b, NT_DIMS, preferred_element_type=F32)


def _rms(x, g):
    return x * lax.rsqrt(jnp.mean(x * x, axis=-1, keepdims=True) + EPS) * g


def _sigmoid(x):
    return 0.5 * jnp.tanh(0.5 * x) + 0.5


def _params(sem):
    return pltpu.CompilerParams(dimension_semantics=sem, vmem_limit_bytes=VMEM_LIMIT)


def _ffn_kernel(x_ref, g_ref, wg_ref, wu_ref, wd_ref, g2_ref, *rest, nf, tf, emit_h):
    if emit_h:
        h_ref, n_ref, xn_sc, acc_sc, act_sc = rest
    else:
        n_ref, xn_sc, acc_sc, act_sc = rest

    def down(f):
        rows = pl.ds(pl.multiple_of(f * tf, tf), tf)
        return _dot(act_sc[...], wd_ref[rows, :])

    def up(f):
        cols = pl.ds(pl.multiple_of(f * tf, tf), tf)
        xn = xn_sc[...]
        gate = _dot(xn, wg_ref[:, cols])
        lift = _dot(xn, wu_ref[:, cols])
        act_sc[...] = ((gate * _sigmoid(gate)) * lift).astype(BF16)

    xn_sc[...] = _rms(x_ref[...], g_ref[...]).astype(BF16)
    acc_sc[...] = jnp.zeros_like(acc_sc)
    up(0)

    def slab(f, carry):
        acc_sc[...] += down(f - 1)
        up(f)
        return carry

    lax.fori_loop(1, nf, slab, 0, unroll=True)
    h = x_ref[...] + 0.5 * (acc_sc[...] + down(nf - 1))
    if emit_h:
        h_ref[...] = h
    n_ref[...] = _rms(h, g2_ref[...]).astype(n_ref.dtype)


def _ffn(x, g, wg, wu, wd, g2, *, tm, tf, emit_h, n_dtype):
    M, D = x.shape
    F = wg.shape[1]
    nf = F // tf
    assert nf >= 2
    out_shape = [jax.ShapeDtypeStruct((M, D), n_dtype)]
    out_specs = [pl.BlockSpec((tm, D), lambda i: (i, 0))]
    if emit_h:
        out_shape = [jax.ShapeDtypeStruct((M, D), F32)] + out_shape
        out_specs = [pl.BlockSpec((tm, D), lambda i: (i, 0))] + out_specs
    resident = lambda shape: pl.BlockSpec(shape, lambda i: (0, 0), pipeline_mode=pl.Buffered(1))
    return pl.pallas_call(
        functools.partial(_ffn_kernel, nf=nf, tf=tf, emit_h=emit_h),
        grid=(M // tm,),
        in_specs=[
            pl.BlockSpec((tm, D), lambda i: (i, 0)),
            pl.BlockSpec((1, D), lambda i: (0, 0)),
            resident((D, F)),
            resident((D, F)),
            resident((F, D)),
            pl.BlockSpec((1, D), lambda i: (0, 0)),
        ],
        out_specs=out_specs,
        out_shape=out_shape,
        scratch_shapes=[pltpu.VMEM((tm, D), BF16), pltpu.VMEM((tm, D), F32), pltpu.VMEM((tm, tf), BF16)],
        compiler_params=pltpu.CompilerParams(
            dimension_semantics=("arbitrary",), vmem_limit_bytes=FFN_VMEM_LIMIT),
        name="ffn",
    )(x, g, wg, wu, wd, g2)


def _rot_half(y, d):
    w = y.shape[-1]
    if d >= w:
        return jnp.concatenate([y[:, w // 2:], y[:, : w // 2]], axis=-1)
    if d == LANES:
        return pltpu.roll(y, d // 2, axis=1)
    fwd = pltpu.roll(y, d // 2, axis=1)
    bwd = pltpu.roll(y, w - d // 2, axis=1)
    lane = lax.broadcasted_iota(jnp.int32, y.shape, 1)
    return jnp.where((lane % d) < d // 2, bwd, fwd)


def _rope_tile(y, cos, sin, d):
    tw = cos.shape[-1]
    outs = []
    for t in range(y.shape[-1] // tw):
        yt = y[:, t * tw:(t + 1) * tw]
        outs.append(yt * cos + _rot_half(yt, d) * sin)
    return outs[0] if len(outs) == 1 else jnp.concatenate(outs, axis=-1)


def _proj_kernel(x_ref, w_ref, *rest, mode, d, q_scale, k_scale, n_q):
    if mode == "rope":
        cos_ref, sin_ref, o_ref, w_sc = rest
    else:
        o_ref, w_sc = rest

    @pl.when(pl.program_id(1) == 0)
    def _():
        w_sc[...] = w_ref[...].T.astype(BF16)

    tm = x_ref.shape[0]
    sub = min(tm, PROJ_SUB_ROWS) if mode == "rope" else tm
    for r0 in range(0, tm, sub):
        rows = slice(r0, r0 + sub)
        y = _dot(x_ref[rows, :], w_sc[...])
        if mode == "rope":
            y = _rope_tile(y, cos_ref[rows, :], sin_ref[rows, :], d)
            if q_scale != 1.0 or k_scale != 1.0:
                y = y * jnp.where(pl.program_id(0) >= n_q, k_scale, q_scale)
        elif mode == "silu":
            y = y * _sigmoid(y)
        o_ref[rows, :] = y.astype(o_ref.dtype)


def _proj(x, w_t, *, col0, n_tiles, tm, tn, mode, out_dtype=None, tabs=None, d=None, q_scale=1.0,
          k_scale=1.0, n_q=0):
    M, K = x.shape
    in_specs = [
        pl.BlockSpec((tm, K), lambda j, i: (i, 0)),
        pl.BlockSpec((tn, K), lambda j, i: (col0 + j, 0)),
    ]
    args = [x, w_t]
    if mode == "rope":
        cos, sin = tabs
        npos = cos.shape[0] // tm
        tw = cos.shape[1]
        in_specs += [pl.BlockSpec((tm, tw), lambda j, i: (i % npos, 0))] * 2
        args += [cos, sin]
    return pl.pallas_call(
        functools.partial(_proj_kernel, mode=mode, d=d, q_scale=q_scale, k_scale=k_scale, n_q=n_q),
        grid=(n_tiles, M // tm),
        in_specs=in_specs,
        out_specs=pl.BlockSpec((tm, tn), lambda j, i: (i, j)),
        out_shape=jax.ShapeDtypeStruct((M, n_tiles * tn), out_dtype or BF16),
        scratch_shapes=[pltpu.VMEM((K, tn), BF16)],
        compiler_params=_params(("arbitrary", "arbitrary")),
        name="proj_" + mode,
    )(*args)


def _idx_proj_kernel(x_ref, w_ref, cos_ref, sin_ref, q_ref, kw_ref):
    tm = x_ref.shape[0]
    sub = min(tm, PROJ_SUB_ROWS)
    for r0 in range(0, tm, sub):
        rows = slice(r0, r0 + sub)
        y = _dot(x_ref[rows, :], w_ref[...])
        cos, sin = cos_ref[rows, :], sin_ref[rows, :]
        q_ref[rows, :] = _rope_tile(y[:, :IDX_Q], cos, sin, IDX_DH)
        tail = y[:, IDX_Q:]
        lane = lax.broadcasted_iota(jnp.int32, tail.shape, 1)
        kw_ref[rows, :] = jnp.where(lane < IDX_DH, _rope_tile(tail, cos, sin, IDX_DH), tail)


def _idx_proj(x, w, tabs, *, tm):
    M, K = x.shape
    N = w.shape[1]
    cos, sin = tabs
    npos = cos.shape[0] // tm
    return pl.pallas_call(
        _idx_proj_kernel,
        grid=(M // tm,),
        in_specs=[
            pl.BlockSpec((tm, K), lambda i: (i, 0)),
            pl.BlockSpec((K, N), lambda i: (0, 0)),
            pl.BlockSpec((tm, LANES), lambda i: (i % npos, 0)),
            pl.BlockSpec((tm, LANES), lambda i: (i % npos, 0)),
        ],
        out_specs=[
            pl.BlockSpec((tm, IDX_Q), lambda i: (i, 0)),
            pl.BlockSpec((tm, LANES), lambda i: (i, 0)),
        ],
        out_shape=[
            jax.ShapeDtypeStruct((M, IDX_Q), F32),
            jax.ShapeDtypeStruct((M, LANES), F32),
        ],
        compiler_params=_params(("parallel",)),
        name="proj_idx",
    )(x, w, cos, sin)


def _ret_kernel(lg_ref, gc_ref, q_ref, k_ref, v_ref, g_ref, km_ref, vm_ref, o_ref, st_ref, st0_ref, decay_ref, *, C):
    c = pl.program_id(1)
    dk = lambda h: slice(h * RET_DK, (h + 1) * RET_DK)
    dv = lambda h: slice(h * RET_DV, (h + 1) * RET_DV)

    @pl.when((pl.program_id(0) == 0) & (c == 0))
    def _():
        pm = lax.broadcasted_iota(jnp.int32, (N_META, 1), 0).astype(F32)
        ri = lax.broadcasted_iota(jnp.int32, (C, C), 0)
        ci = lax.broadcasted_iota(jnp.int32, (C, C), 1)
        diff = (ri - ci).astype(F32)
        for h in range(RET_HEADS):
            lg = lg_ref[h]
            zeta_m = jnp.exp(lg * (N_META - 1.0 - pm))
            kz = km_ref[:, dk(h)].astype(F32) * zeta_m
            st0_ref[h] = _dot(kz.T.astype(BF16), vm_ref[:, dv(h)])
            decay_ref[h] = jnp.where(diff >= 0, jnp.exp(lg * jnp.maximum(diff, 0.0)), 0.0)

    @pl.when(c == 0)
    def _():
        st_ref[...] = st0_ref[...]

    pos = lax.broadcasted_iota(jnp.int32, (C, 1), 0).astype(F32)
    for h in range(RET_HEADS):
        lg = lg_ref[h]
        xi = jnp.exp(lg * (pos + 1.0))
        zeta = jnp.exp(lg * (C - 1.0 - pos))
        state = st_ref[h]
        for r0 in range(0, q_ref.shape[1], C):
            rows = slice(r0, r0 + C)
            q = q_ref[0, rows, dk(h)]
            k = k_ref[0, rows, dk(h)]
            v = v_ref[0, rows, dv(h)]
            inner = _dot((_dot_nt(q, k) * decay_ref[h]).astype(BF16), v)
            cross = _dot(q, state.astype(BF16)) * xi
            y = inner + cross
            y = y * lax.rsqrt(jnp.mean(y * y, axis=-1, keepdims=True) + EPS)
            o_ref[0, rows, dv(h)] = (g_ref[0, rows, dv(h)] * y).astype(o_ref.dtype)
            kz = k.astype(F32) * zeta
            state = gc_ref[h] * state + _dot(kz.T.astype(BF16), v)
        st_ref[h] = state


def _retention(lg, gc, rqk, rv, rg, km, vm, *, C, R):
    B, S, _ = rv.shape
    smem = pl.BlockSpec(memory_space=pltpu.SMEM)
    return pl.pallas_call(
        functools.partial(_ret_kernel, C=C),
        grid=(B, S // R),
        in_specs=[
            smem,
            smem,
            pl.BlockSpec((1, R, RET_QK), lambda b, c: (b, c, 0)),
            pl.BlockSpec((1, R, RET_QK), lambda b, c: (b, c, 1)),
            pl.BlockSpec((1, R, RET_V), lambda b, c: (b, c, 0)),
            pl.BlockSpec((1, R, RET_V), lambda b, c: (b, c, 0)),
            pl.BlockSpec((N_META, RET_QK), lambda b, c: (0, 0)),
            pl.BlockSpec((N_META, RET_V), lambda b, c: (0, 0)),
        ],
        out_specs=pl.BlockSpec((1, R, RET_V), lambda b, c: (b, c, 0)),
        out_shape=jax.ShapeDtypeStruct((B, S, RET_V), BF16),
        scratch_shapes=[pltpu.VMEM((RET_HEADS, RET_DK, RET_DV), F32), pltpu.VMEM((RET_HEADS, RET_DK, RET_DV), F32),
                        pltpu.VMEM((RET_HEADS, C, C), F32)],
        compiler_params=_params(("arbitrary", "arbitrary")),
        name="retention",
    )(lg, gc, rqk, rqk, rv, rg, km, vm)


TINY = float(np.finfo(np.float32).tiny)
TN_DIMS = (((0,), (0,)), ((), ()))


def _dsa_kernel(iq_ref, ikw_ref, wq_ref, aq_ref, ak_ref, av_ref, akm_ref, avm_ref, o_ref,
                sc_ref, tau_ref, m_ref, l_ref, acc_ref, s_ref, bm_ref, *, nq, tq, tk, k_top):
    qi = pl.program_id(1)
    nkb = (qi + 1) * (tq // tk)
    kf = float(k_top)
    neg_inf = -jnp.inf
    qpos = qi * tq + lax.broadcasted_iota(jnp.int32, (1, tq), 1)

    part_shape = (SUBLANES, tq)

    def fold(x):
        return x.reshape(tk // SUBLANES, SUBLANES, tq).sum(axis=0)

    idx_scale = (IDX_DH ** -0.5) * (IDX_HEADS ** -0.5)
    iq = iq_ref[0].astype(BF16)
    w_t = wq_ref[0].T
    iq_h = [iq[:, h * IDX_DH:(h + 1) * IDX_DH] for h in range(IDX_HEADS)]
    w_h = [w_t[IDX_DH + h:IDX_DH + h + 1, :] for h in range(IDX_HEADS)]

    def score_rows(kb, n_blocks, carry):
        rmin, rmax = carry
        rows = n_blocks * tk
        ik = ikw_ref[0, pl.ds(pl.multiple_of(kb * tk, tk), rows), :][:, :IDX_DH].astype(BF16)
        acc = jnp.zeros((rows, tq), F32)
        for h in range(IDX_HEADS):
            acc = acc + w_h[h] * jnp.maximum(_dot_nt(ik, iq_h[h]), 0.0)
        ok = (lax.broadcasted_iota(jnp.int32, (rows, 1), 0) + kb * tk) <= qpos
        s = jnp.where(ok, acc * idx_scale, neg_inf)
        for i in range(n_blocks):
            sc_ref[kb + i] = s[i * tk:(i + 1) * tk]
        rmin = jnp.minimum(rmin, jnp.min(jnp.where(ok, s, jnp.inf), axis=0, keepdims=True))
        rmax = jnp.maximum(rmax, jnp.max(s, axis=0, keepdims=True))
        return rmin, rmax

    carry = (jnp.full((1, tq), jnp.inf, F32), jnp.full((1, tq), neg_inf, F32))
    carry = lax.fori_loop(0, nkb // 4, lambda j, c: score_rows(4 * j, 4, c), carry)
    carry = lax.cond(nkb % 4 >= 2, lambda c: score_rows((nkb // 4) * 4, 2, c), lambda c: c, carry)
    rmin, rmax = lax.cond(nkb % 2 == 1, lambda c: score_rows(nkb - 1, 1, c), lambda c: c, carry)

    def any_query(flag):
        return jnp.max(jnp.where(flag, 1.0, 0.0))

    def midpoint(lo, hi):
        a = jnp.maximum(jnp.abs(lo), TINY)
        b = jnp.maximum(jnp.abs(hi), TINY)
        geo = jnp.sqrt(a) * jnp.sqrt(b)
        one_sign = (lo >= 0.0) | (hi <= 0.0)
        far = jnp.maximum(a, b) > 4.0 * jnp.minimum(a, b)
        mid = jnp.where(one_sign & far, jnp.where(lo >= 0.0, geo, -geo), 0.5 * lo + 0.5 * hi)
        mid = jnp.where(lo == 0.0, TINY, mid)
        mid = jnp.where(hi == 0.0, -TINY, mid)
        return jnp.where((lo < 0.0) & (hi > 0.0), 0.0, mid)

    def active(lo, hi, cl):
        mid = midpoint(lo, hi)
        return (cl > kf) & (mid > lo) & (mid < hi)

    def go_on(st):
        return st[3] > 0.0

    def threshold(n_blocks, rmin, rmax):
        blocks = range(n_blocks)

        def over_keys(fn, init, combine):
            acc = init
            for kb in blocks:
                acc = combine(acc, fn(sc_ref[kb]))
            return acc

        def count(pred):
            c = over_keys(lambda s: fold(jnp.where(pred(s), 1.0, 0.0)), jnp.zeros(part_shape, F32), jnp.add)
            return jnp.sum(c, axis=0, keepdims=True)

        c_max = count(lambda s: s >= rmax)
        top_tied = c_max >= kf
        lo = jnp.where(top_tied, rmax, rmin)
        hi = rmax
        cl = jnp.where(top_tied, c_max, (qpos + 1).astype(F32))

        def shrink_once(lo, hi, cl):
            mid = midpoint(lo, hi)
            act = active(lo, hi, cl)
            c = count(lambda s: s >= mid)
            up = act & (c >= kf)
            dn = act & (c < kf)
            return jnp.where(up, mid, lo), jnp.where(dn, mid, hi), jnp.where(up, c, cl)

        def shrink(st):
            pending = any_query(active(*st[:3]))
            return (*shrink_once(*shrink_once(*st[:3])), pending)

        lo, hi, cl, _ = lax.while_loop(go_on, shrink, (lo, hi, cl, jnp.float32(1.0)))
        tau_ref[...] = lo
        tie = cl > kf

        @pl.when(any_query(tie) > 0.0)
        def _():
            def max_below(bound):
                part = over_keys(
                    lambda s: jnp.max(jnp.where(s < bound, s, neg_inf).reshape(tk // SUBLANES, SUBLANES, tq), axis=0),
                    jnp.full(part_shape, neg_inf, F32), jnp.maximum)
                return jnp.max(part, axis=0, keepdims=True)

            hi_d = jnp.where(lo >= hi, jnp.inf, hi)
            v0 = max_below(hi_d)
            ge0 = count(lambda s: s >= v0)

            def walk_on(st):
                return st[3] > 0.0

            def walk(st):
                hi_w, v, ge, _ = st
                hi_w = jnp.where(tie & (ge < kf), v, hi_w)
                v = max_below(hi_w)
                ge = count(lambda s: s >= v)
                return hi_w, v, ge, any_query(tie & (ge < kf))

            _, v, _, _ = lax.while_loop(walk_on, walk, (hi_d, v0, ge0, any_query(tie & (ge0 < kf))))
            keep = kf - count(lambda s: s > v)
            ri = lax.broadcasted_iota(jnp.int32, (tk, tk), 0)
            ci = lax.broadcasted_iota(jnp.int32, (tk, tk), 1)
            earlier = jnp.where(ci < ri, 1.0, 0.0).astype(BF16)
            seen = jnp.zeros((1, tq), F32)
            for kb in blocks:
                s = sc_ref[kb]
                eq = tie & (s == v)
                eqf = jnp.where(eq, 1.0, 0.0)
                rank = seen + _dot(earlier, eqf.astype(BF16))
                sc_ref[kb] = jnp.where(eq & (rank >= keep), neg_inf, s)
                seen = seen + jnp.sum(eqf, axis=0, keepdims=True)
            tau_ref[...] = jnp.where(tie, v, lo)

        return jnp.int32(0)

    branches = [functools.partial(threshold, (i + 1) * (tq // tk)) for i in range(nq)]
    lax.switch(qi, branches, rmin, rmax)

    tau = tau_ref[...]

    def head(ref, h, rows=None):
        if rows is None:
            return ref[:, h * ATT_DH:(h + 1) * ATT_DH]
        return ref[0, rows, h * ATT_DH:(h + 1) * ATT_DH]

    def score_phase(kb, buf):
        rows = pl.ds(pl.multiple_of(kb * tk, tk), tk)
        bias = jnp.where(sc_ref[kb] >= tau, 0.0, neg_inf)
        for h in range(ATT_HEADS):
            s = _dot_nt(head(ak_ref, h, rows), head(aq_ref, h, slice(None))) + bias
            s_ref[buf, h] = s
            bm_ref[buf, h:h + 1, :] = jnp.max(s, axis=0, keepdims=True)

    def value_phase(kb, buf):
        rows = pl.ds(pl.multiple_of(kb * tk, tk), tk)
        for h in range(ATT_HEADS):
            m_new = jnp.maximum(m_ref[h:h + 1, :], bm_ref[buf, h:h + 1, :])
            p = jnp.exp2(s_ref[buf, h] - m_new)
            p_sum = jnp.sum(p, axis=0, keepdims=True)
            pv = lax.dot_general(head(av_ref, h, rows), p.astype(BF16), TN_DIMS, preferred_element_type=F32)
            alpha = jnp.exp2(m_ref[h:h + 1, :] - m_new)
            l_ref[h:h + 1, :] = alpha * l_ref[h:h + 1, :] + p_sum
            acc_ref[h] = alpha * acc_ref[h] + pv
            m_ref[h:h + 1, :] = m_new

    for h in range(ATT_HEADS):
        s = _dot_nt(head(akm_ref, h), head(aq_ref, h, slice(None)))
        m = jnp.max(s, axis=0, keepdims=True)
        p = jnp.exp2(s - m)
        m_ref[h:h + 1, :] = m
        l_ref[h:h + 1, :] = jnp.sum(p, axis=0, keepdims=True)
        acc_ref[h] = lax.dot_general(head(avm_ref, h), p.astype(BF16), TN_DIMS, preferred_element_type=F32)
    score_phase(0, 1)

    def att_blocks(kb, n):
        for i in range(n):
            buf = 1 - i % 2
            value_phase(kb + i, buf)
            score_phase(jnp.minimum(kb + i + 1, nkb - 1), 1 - buf)

    def att_quad(j, carry):
        att_blocks(4 * j, 4)
        return carry

    lax.fori_loop(0, nkb // 4, att_quad, 0)

    @pl.when(nkb % 4 >= 2)
    def _():
        att_blocks((nkb // 4) * 4, 2)

    @pl.when(nkb % 2 == 1)
    def _():
        value_phase(nkb - 1, 1)

    for h in range(ATT_HEADS):
        out_t = acc_ref[h] / l_ref[h:h + 1, :]
        o_ref[0, :, h * ATT_DH:(h + 1) * ATT_DH] = out_t.T.astype(o_ref.dtype)


def _dsa(iq, ikw, aqk, av, akm, avm, *, tq, tk, k_top):
    B, S, _ = av.shape
    nq = S // tq
    assert tq % tk == 0
    return pl.pallas_call(
        functools.partial(_dsa_kernel, nq=nq, tq=tq, tk=tk, k_top=k_top),
        grid=(B, nq),
        in_specs=[
            pl.BlockSpec((1, tq, IDX_Q), lambda b, q: (b, q, 0)),
            pl.BlockSpec((1, S, LANES), lambda b, q: (b, 0, 0)),
            pl.BlockSpec((1, tq, LANES), lambda b, q: (b, q, 0)),
            pl.BlockSpec((1, tq, ATT_W), lambda b, q: (b, q, 0)),
            pl.BlockSpec((1, S, ATT_W), lambda b, q: (b, 0, 1)),
            pl.BlockSpec((1, S, ATT_W), lambda b, q: (b, 0, 0)),
            pl.BlockSpec((N_META, ATT_W), lambda b, q: (0, 0)),
            pl.BlockSpec((N_META, ATT_W), lambda b, q: (0, 0)),
        ],
        out_specs=pl.BlockSpec((1, tq, ATT_W), lambda b, q: (b, q, 0)),
        out_shape=jax.ShapeDtypeStruct((B, S, ATT_W), BF16),
        scratch_shapes=[
            pltpu.VMEM((S // tk, tk, tq), F32),
            pltpu.VMEM((1, tq), F32),
            pltpu.VMEM((ATT_HEADS, tq), F32),
            pltpu.VMEM((ATT_HEADS, tq), F32),
            pltpu.VMEM((ATT_HEADS, ATT_DH, tq), F32),
            pltpu.VMEM((2, ATT_HEADS, tk, tq), F32),
            pltpu.VMEM((2, ATT_HEADS, tq), F32),
        ],
        compiler_params=_params(("parallel", "arbitrary")),
        name="dsa",
    )(iq, ikw, ikw, aqk, aqk, av, akm, avm)


def _merge_kernel(h_ref, yr_ref, ya_ref, gr_ref, ga_ref, wr_ref, wa_ref, wm_ref, o_ref):
    yr = _dot(yr_ref[...], wr_ref[...])
    ya = _dot(ya_ref[...], wa_ref[...])
    merged = _sigmoid(gr_ref[...].astype(F32)) * yr + _sigmoid(ga_ref[...].astype(F32)) * ya
    o_ref[...] = h_ref[...] + _dot(merged.astype(BF16), wm_ref[...])


def _merge(h, yr, ya, gates, wr, wa, wm, *, tm):
    M, D = h.shape
    row = lambda i: (i, 0)
    fixed = lambda i: (0, 0)
    return pl.pallas_call(
        _merge_kernel,
        grid=(M // tm,),
        in_specs=[
            pl.BlockSpec((tm, D), row),
            pl.BlockSpec((tm, RET_V), row),
            pl.BlockSpec((tm, ATT_W), row),
            pl.BlockSpec((tm, D), lambda i: (i, 0)),
            pl.BlockSpec((tm, D), lambda i: (i, 1)),
            pl.BlockSpec((RET_V, D), fixed),
            pl.BlockSpec((ATT_W, D), fixed),
            pl.BlockSpec((D, D), fixed),
        ],
        out_specs=pl.BlockSpec((tm, D), row),
        out_shape=jax.ShapeDtypeStruct((M, D), F32),
        compiler_params=_params(("parallel",)),
        name="merge",
    )(h, yr, ya, gates, gates, wr, wa, wm)


def _rope_tables(pos, d, width):
    inv_freq = ROPE_THETA ** (-jnp.arange(0, d, 2, dtype=F32) / d)
    ang = pos.astype(F32)[:, None] * inv_freq[None, :]
    cos, sin = jnp.cos(ang), jnp.sin(ang)
    cos = jnp.concatenate([cos, cos], axis=-1)
    sin = jnp.concatenate([-sin, sin], axis=-1)
    rep = width // d
    return jnp.tile(cos, (1, rep)), jnp.tile(sin, (1, rep))


def _tile(n, cap):
    t = min(n, cap)
    assert n % t == 0, (n, t)
    return t


class _Plan(NamedTuple):
    rows: int
    rows_tab: int
    rows_wide: int
    rows_merge: int
    cols: int
    ffn_slab: int
    att: int
    chunk: int
    ret_rows: int


def _plan(S, M):
    return _Plan(rows=_tile(S, 4 * MXU_DIM), rows_tab=_tile(S, 8 * MXU_DIM), rows_wide=_tile(M, 8 * MXU_DIM),
                 rows_merge=_tile(M, 2 * MXU_DIM), cols=4 * MXU_DIM, ffn_slab=MXU_DIM, att=_tile(S, MXU_DIM),
                 chunk=_tile(S, 4 * RET_CHUNK), ret_rows=_tile(S, 4 * RET_CHUNK))


def kernel(x, meta_tokens, ffn1_norm, ffn1_w_gate, ffn1_w_up, ffn1_w_down, mix_norm, w_in, w_ret_out,
           w_att_out, w_mix_out, ffn2_norm, ffn2_w_gate, ffn2_w_up, ffn2_w_down, final_norm):
    B, S, D = x.shape
    assert ffn1_norm.shape[0] == 1, "single layer only"
    assert meta_tokens.shape[0] == N_META
    k_top = min(TOPK_MAX, S // 4)
    M = B * S
    plan = _plan(S, M)
    bf = lambda a: a.astype(BF16)
    row = lambda a: a.reshape(1, -1)

    w32 = w_in[0].T
    offs = np.cumsum([0, RET_QK, RET_QK, RET_V, RET_V, ATT_W, ATT_W, ATT_W, IDX_Q, IDX_DH, IDX_HEADS, D, D])
    tn = plan.cols
    assert all(int(o) % tn == 0 for o in offs[:8])
    c_rq, c_rk, c_rv, c_rg, c_aq, c_ak, c_av = (int(o) // tn for o in offs[:7])
    w_idx = jnp.pad(bf(w32[offs[7]:offs[10]].T), ((0, 0), (0, LANES - IDX_DH - IDX_HEADS)))
    w_gate = w32[offs[10]:offs[12]]

    pos_meta = jnp.arange(N_META, dtype=jnp.int32)
    pos_real = N_META + jnp.arange(S, dtype=jnp.int32)
    ffn1 = (row(ffn1_norm[0]), bf(ffn1_w_gate[0]), bf(ffn1_w_up[0]), bf(ffn1_w_down[0]), row(mix_norm[0]))
    ffn2 = (row(ffn2_norm[0]), bf(ffn2_w_gate[0]), bf(ffn2_w_up[0]), bf(ffn2_w_down[0]), row(final_norm))
    tf = plan.ffn_slab
    k_scale = RET_DK ** -0.5

    _, un_m = _ffn(meta_tokens.astype(F32), *ffn1, tm=N_META, tf=tf, emit_h=True, n_dtype=BF16)
    pm = dict(tm=N_META, tn=tn)
    rk_m = _proj(un_m, w32, col0=c_rk, n_tiles=RET_QK // tn, mode="rope", d=RET_DK, k_scale=k_scale,
                 tabs=_rope_tables(pos_meta, RET_DK, RET_DK), **pm)
    rv_m = _proj(un_m, w32, col0=c_rv, n_tiles=RET_V // tn, mode="plain", **pm)
    ak_m = _proj(un_m, w32, col0=c_ak, n_tiles=ATT_W // tn, mode="rope", d=ATT_DH,
                 tabs=_rope_tables(pos_meta, ATT_DH, LANES), **pm)
    av_m = _proj(un_m, w32, col0=c_av, n_tiles=ATT_W // tn, mode="plain", **pm)

    tm = plan.rows
    h1, un = _ffn(x.reshape(M, D), *ffn1, tm=tm, tf=tf, emit_h=True, n_dtype=BF16)
    pr = dict(tm=plan.rows_tab, tn=tn)
    pw = dict(tm=plan.rows_wide, tn=tn)
    rqk = _proj(un, w32, col0=c_rq, n_tiles=2 * RET_QK // tn, mode="rope", d=RET_DK, k_scale=k_scale,
                n_q=RET_QK // tn, tabs=_rope_tables(pos_real, RET_DK, RET_DK), **pr)
    rv = _proj(un, w32, col0=c_rv, n_tiles=RET_V // tn, mode="plain", **pw)
    rg = _proj(un, w32, col0=c_rg, n_tiles=RET_V // tn, mode="silu", **pw)
    aqk = _proj(un, w32, col0=c_aq, n_tiles=2 * ATT_W // tn, mode="rope", d=ATT_DH, q_scale=ATT_EXP_SCALE,
                n_q=ATT_W // tn, tabs=_rope_tables(pos_real, ATT_DH, LANES), **pr)
    av = _proj(un, w32, col0=c_av, n_tiles=ATT_W // tn, mode="plain", **pw)
    gates = _proj(un, w_gate, col0=0, n_tiles=2 * D // tn, mode="plain", **pw)
    iq, ikw = _idx_proj(un, w_idx, _rope_tables(pos_real, IDX_DH, LANES), tm=tm)

    heads = jnp.arange(RET_HEADS, dtype=F32)
    log_g = jnp.log1p(-(2.0 ** (-5.0 - heads)))
    C = plan.chunk
    b3 = lambda a: a.reshape(B, S, a.shape[-1])
    yr = _retention(log_g, jnp.exp(log_g * C), b3(rqk), b3(rv), b3(rg), rk_m, rv_m, C=C, R=plan.ret_rows)
    ya = _dsa(b3(iq), b3(ikw), b3(aqk), b3(av), ak_m, av_m, tq=plan.att, tk=plan.att, k_top=k_top)

    h2 = _merge(h1, yr.reshape(M, RET_V), ya.reshape(M, ATT_W), gates, bf(w_ret_out[0]), bf(w_att_out[0]),
                bf(w_mix_out[0]), tm=plan.rows_merge)
    (y,) = _ffn(h2, *ffn2, tm=tm, tf=tf, emit_h=False, n_dtype=F32)
    return y.reshape(B, S, D)
```

```python
import functools
from typing import NamedTuple

import jax
import jax.numpy as jnp
import numpy as np
from jax import lax
from jax.experimental import pallas as pl
from jax.experimental.pallas import tpu as pltpu

N_META = 16
RET_HEADS = 4
RET_DK = 256
RET_DV = 512
RET_CHUNK = 128
ATT_HEADS = 8
ATT_DH = 128
IDX_HEADS = 8
IDX_DH = 64
TOPK_MAX = 256
ROPE_THETA = 10000.0
EPS = 1e-6
RET_QK = RET_HEADS * RET_DK
RET_V = RET_HEADS * RET_DV
ATT_W = ATT_HEADS * ATT_DH
IDX_Q = IDX_HEADS * IDX_DH
ATT_EXP_SCALE = (ATT_DH ** -0.5) * float(np.log2(np.e))

LANES = 128
SUBLANES = 8
MXU_DIM = 256
PROJ_SUB_ROWS = MXU_DIM
VMEM_LIMIT = 48 * 1024 * 1024
FFN_VMEM_LIMIT = 56 * 1024 * 1024

BF16 = jnp.bfloat16
F32 = jnp.float32
NT_DIMS = (((1,), (1,)), ((), ()))


def _dot(a, b):
    return jnp.dot(a, b, preferred_element_type=F32)


def _dot_nt(a, b):
    return lax.dot_general(a, b, NT_DIMS, preferred_element_type=F32)


def _rms(x, g):
    return x * lax.rsqrt(jnp.mean(x * x, axis=-1, keepdims=True) + EPS) * g


def _sigmoid(x):
    return 0.5 * jnp.tanh(0.5 * x) + 0.5


def _params(sem):
    return pltpu.CompilerParams(dimension_semantics=sem, vmem_limit_bytes=VMEM_LIMIT)


def _ffn_kernel(x_ref, g_ref, wg_ref, wu_ref, wd_ref, g2_ref, *rest, nf, tf, emit_h):
    if emit_h:
        h_ref, n_ref, xn_sc, acc_sc, act_sc = rest
    else:
        n_ref, xn_sc, acc_sc, act_sc = rest

    def down(f):
        rows = pl.ds(pl.multiple_of(f * tf, tf), tf)
        return _dot(act_sc[...], wd_ref[rows, :])

    def up(f):
        cols = pl.ds(pl.multiple_of(f * tf, tf), tf)
        xn = xn_sc[...]
        gate = _dot(xn, wg_ref[:, cols])
        lift = _dot(xn, wu_ref[:, cols])
        act_sc[...] = ((gate * _sigmoid(gate)) * lift).astype(BF16)

    xn_sc[...] = _rms(x_ref[...], g_ref[...]).astype(BF16)
    acc_sc[...] = jnp.zeros_like(acc_sc)
    up(0)

    def slab(f, carry):
        acc_sc[...] += down(f - 1)
        up(f)
        return carry

    lax.fori_loop(1, nf, slab, 0, unroll=True)
    h = x_ref[...] + 0.5 * (acc_sc[...] + down(nf - 1))
    if emit_h:
        h_ref[...] = h
    n_ref[...] = _rms(h, g2_ref[...]).astype(n_ref.dtype)


def _ffn(x, g, wg, wu, wd, g2, *, tm, tf, emit_h, n_dtype):
    M, D = x.shape
    F = wg.shape[1]
    nf = F // tf
    assert nf >= 2
    out_shape = [jax.ShapeDtypeStruct((M, D), n_dtype)]
    out_specs = [pl.BlockSpec((tm, D), lambda i: (i, 0))]
    if emit_h:
        out_shape = [jax.ShapeDtypeStruct((M, D), F32)] + out_shape
        out_specs = [pl.BlockSpec((tm, D), lambda i: (i, 0))] + out_specs
    resident = lambda shape: pl.BlockSpec(shape, lambda i: (0, 0), pipeline_mode=pl.Buffered(1))
    return pl.pallas_call(
        functools.partial(_ffn_kernel, nf=nf, tf=tf, emit_h=emit_h),
        grid=(M // tm,),
        in_specs=[
            pl.BlockSpec((tm, D), lambda i: (i, 0)),
            pl.BlockSpec((1, D), lambda i: (0, 0)),
            resident((D, F)),
            resident((D, F)),
            resident((F, D)),
            pl.BlockSpec((1, D), lambda i: (0, 0)),
        ],
        out_specs=out_specs,
        out_shape=out_shape,
        scratch_shapes=[pltpu.VMEM((tm, D), BF16), pltpu.VMEM((tm, D), F32), pltpu.VMEM((tm, tf), BF16)],
        compiler_params=pltpu.CompilerParams(
            dimension_semantics=("arbitrary",), vmem_limit_bytes=FFN_VMEM_LIMIT),
        name="ffn",
    )(x, g, wg, wu, wd, g2)


def _rot_half(y, d):
    w = y.shape[-1]
    if d >= w:
        return jnp.concatenate([y[:, w // 2:], y[:, : w // 2]], axis=-1)
    if d == LANES:
        return pltpu.roll(y, d // 2, axis=1)
    fwd = pltpu.roll(y, d // 2, axis=1)
    bwd = pltpu.roll(y, w - d // 2, axis=1)
    lane = lax.broadcasted_iota(jnp.int32, y.shape, 1)
    return jnp.where((lane % d) < d // 2, bwd, fwd)


def _rope_tile(y, cos, sin, d):
    tw = cos.shape[-1]
    outs = []
    for t in range(y.shape[-1] // tw):
        yt = y[:, t * tw:(t + 1) * tw]
        outs.append(yt * cos + _rot_half(yt, d) * sin)
    return outs[0] if len(outs) == 1 else jnp.concatenate(outs, axis=-1)


def _proj_kernel(x_ref, w_ref, *rest, mode, d, q_scale, k_scale, n_q):
    if mode == "rope":
        cos_ref, sin_ref, o_ref, w_sc = rest
    else:
        o_ref, w_sc = rest

    @pl.when(pl.program_id(1) == 0)
    def _():
        w_sc[...] = w_ref[...].T.astype(BF16)

    tm = x_ref.shape[0]
    sub = min(tm, PROJ_SUB_ROWS) if mode == "rope" else tm
    for r0 in range(0, tm, sub):
        rows = slice(r0, r0 + sub)
        y = _dot(x_ref[rows, :], w_sc[...])
        if mode == "rope":
            y = _rope_tile(y, cos_ref[rows, :], sin_ref[rows, :], d)
            if q_scale != 1.0 or k_scale != 1.0:
                y = y * jnp.where(pl.program_id(0) >= n_q, k_scale, q_scale)
        elif mode == "silu":
            y = y * _sigmoid(y)
        o_ref[rows, :] = y.astype(o_ref.dtype)


def _proj(x, w_t, *, col0, n_tiles, tm, tn, mode, out_dtype=None, tabs=None, d=None, q_scale=1.0,
          k_scale=1.0, n_q=0):
    M, K = x.shape
    in_specs = [
        pl.BlockSpec((tm, K), lambda j, i: (i, 0)),
        pl.BlockSpec((tn, K), lambda j, i: (col0 + j, 0)),
    ]
    args = [x, w_t]
    if mode == "rope":
        cos, sin = tabs
        npos = cos.shape[0] // tm
        tw = cos.shape[1]
        in_specs += [pl.BlockSpec((tm, tw), lambda j, i: (i % npos, 0))] * 2
        args += [cos, sin]
    return pl.pallas_call(
        functools.partial(_proj_kernel, mode=mode, d=d, q_scale=q_scale, k_scale=k_scale, n_q=n_q),
        grid=(n_tiles, M // tm),
        in_specs=in_specs,
        out_specs=pl.BlockSpec((tm, tn), lambda j, i: (i, j)),
        out_shape=jax.ShapeDtypeStruct((M, n_tiles * tn), out_dtype or BF16),
        scratch_shapes=[pltpu.VMEM((K, tn), BF16)],
        compiler_params=_params(("arbitrary", "arbitrary")),
        name="proj_" + mode,
    )(*args)


def _idx_proj_kernel(x_ref, w_ref, cos_ref, sin_ref, q_ref, kw_ref):
    tm = x_ref.shape[0]
    sub = min(tm, PROJ_SUB_ROWS)
    for r0 in range(0, tm, sub):
        rows = slice(r0, r0 + sub)
        y = _dot(x_ref[rows, :], w_ref[...])
        cos, sin = cos_ref[rows, :], sin_ref[rows, :]
        q_ref[rows, :] = _rope_tile(y[:, :IDX_Q], cos, sin, IDX_DH)
        tail = y[:, IDX_Q:]
        lane = lax.broadcasted_iota(jnp.int32, tail.shape, 1)
        kw_ref[rows, :] = jnp.where(lane < IDX_DH, _rope_tile(tail, cos, sin, IDX_DH), tail)


def _idx_proj(x, w, tabs, *, tm):
    M, K = x.shape
    N = w.shape[1]
    cos, sin = tabs
    npos = cos.shape[0] // tm
    return pl.pallas_call(
        _idx_proj_kernel,
        grid=(M // tm,),
        in_specs=[
            pl.BlockSpec((tm, K), lambda i: (i, 0)),
            pl.BlockSpec((K, N), lambda i: (0, 0)),
            pl.BlockSpec((tm, LANES), lambda i: (i % npos, 0)),
            pl.BlockSpec((tm, LANES), lambda i: (i % npos, 0)),
        ],
        out_specs=[
            pl.BlockSpec((tm, IDX_Q), lambda i: (i, 0)),
            pl.BlockSpec((tm, LANES), lambda i: (i, 0)),
        ],
        out_shape=[
            jax.ShapeDtypeStruct((M, IDX_Q), F32),
            jax.ShapeDtypeStruct((M, LANES), F32),
        ],
        compiler_params=_params(("parallel",)),
        name="proj_idx",
    )(x, w, cos, sin)


def _ret_kernel(lg_ref, gc_ref, q_ref, k_ref, v_ref, g_ref, km_ref, vm_ref, o_ref, st_ref, st0_ref, decay_ref, *, C):
    c = pl.program_id(1)
    dk = lambda h: slice(h * RET_DK, (h + 1) * RET_DK)
    dv = lambda h: slice(h * RET_DV, (h + 1) * RET_DV)

    @pl.when((pl.program_id(0) == 0) & (c == 0))
    def _():
        pm = lax.broadcasted_iota(jnp.int32, (N_META, 1), 0).astype(F32)
        ri = lax.broadcasted_iota(jnp.int32, (C, C), 0)
        ci = lax.broadcasted_iota(jnp.int32, (C, C), 1)
        diff = (ri - ci).astype(F32)
        for h in range(RET_HEADS):
            lg = lg_ref[h]
            zeta_m = jnp.exp(lg * (N_META - 1.0 - pm))
            kz = km_ref[:, dk(h)].astype(F32) * zeta_m
            st0_ref[h] = _dot(kz.T.astype(BF16), vm_ref[:, dv(h)])
            decay_ref[h] = jnp.where(diff >= 0, jnp.exp(lg * jnp.maximum(diff, 0.0)), 0.0)

    @pl.when(c == 0)
    def _():
        st_ref[...] = st0_ref[...]

    pos = lax.broadcasted_iota(jnp.int32, (C, 1), 0).astype(F32)
    for h in range(RET_HEADS):
        lg = lg_ref[h]
        xi = jnp.exp(lg * (pos + 1.0))
        zeta = jnp.exp(lg * (C - 1.0 - pos))
        state = st_ref[h]
        for r0 in range(0, q_ref.shape[1], C):
            rows = slice(r0, r0 + C)
            q = q_ref[0, rows, dk(h)]
            k = k_ref[0, rows, dk(h)]
            v = v_ref[0, rows, dv(h)]
            inner = _dot((_dot_nt(q, k) * decay_ref[h]).astype(BF16), v)
            cross = _dot(q, state.astype(BF16)) * xi
            y = inner + cross
            y = y * lax.rsqrt(jnp.mean(y * y, axis=-1, keepdims=True) + EPS)
            o_ref[0, rows, dv(h)] = (g_ref[0, rows, dv(h)] * y).astype(o_ref.dtype)
            kz = k.astype(F32) * zeta
            state = gc_ref[h] * state + _dot(kz.T.astype(BF16), v)
        st_ref[h] = state


def _retention(lg, gc, rqk, rv, rg, km, vm, *, C, R):
    B, S, _ = rv.shape
    smem = pl.BlockSpec(memory_space=pltpu.SMEM)
    return pl.pallas_call(
        functools.partial(_ret_kernel, C=C),
        grid=(B, S // R),
        in_specs=[
            smem,
            smem,
            pl.BlockSpec((1, R, RET_QK), lambda b, c: (b, c, 0)),
            pl.BlockSpec((1, R, RET_QK), lambda b, c: (b, c, 1)),
            pl.BlockSpec((1, R, RET_V), lambda b, c: (b, c, 0)),
            pl.BlockSpec((1, R, RET_V), lambda b, c: (b, c, 0)),
            pl.BlockSpec((N_META, RET_QK), lambda b, c: (0, 0)),
            pl.BlockSpec((N_META, RET_V), lambda b, c: (0, 0)),
        ],
        out_specs=pl.BlockSpec((1, R, RET_V), lambda b, c: (b, c, 0)),
        out_shape=jax.ShapeDtypeStruct((B, S, RET_V), BF16),
        scratch_shapes=[pltpu.VMEM((RET_HEADS, RET_DK, RET_DV), F32), pltpu.VMEM((RET_HEADS, RET_DK, RET_DV), F32),
                        pltpu.VMEM((RET_HEADS, C, C), F32)],
        compiler_params=_params(("arbitrary", "arbitrary")),
        name="retention",
    )(lg, gc, rqk, rqk, rv, rg, km, vm)


TINY = float(np.finfo(np.float32).tiny)
TN_DIMS = (((0,), (0,)), ((), ()))


def _dsa_kernel(iq_ref, ikw_ref, wq_ref, aq_ref, ak_ref, av_ref, akm_ref, avm_ref, o_ref,
                sc_ref, tau_ref, m_ref, l_ref, acc_ref, s_ref, bm_ref, *, nq, tq, tk, k_top):
    qi = pl.program_id(1)
    nkb = (qi + 1) * (tq // tk)
    kf = float(k_top)
    neg_inf = -jnp.inf
    qpos = qi * tq + lax.broadcasted_iota(jnp.int32, (1, tq), 1)

    part_shape = (SUBLANES, tq)

    def fold(x):
        return x.reshape(tk // SUBLANES, SUBLANES, tq).sum(axis=0)

    idx_scale = (IDX_DH ** -0.5) * (IDX_HEADS ** -0.5)
    iq = iq_ref[0].astype(BF16)
    w_t = wq_ref[0].T
    iq_h = [iq[:, h * IDX_DH:(h + 1) * IDX_DH] for h in range(IDX_HEADS)]
    w_h = [w_t[IDX_DH + h:IDX_DH + h + 1, :] for h in range(IDX_HEADS)]

    def score_rows(kb, n_blocks, carry):
        rmin, rmax = carry
        rows = n_blocks * tk
        ik = ikw_ref[0, pl.ds(pl.multiple_of(kb * tk, tk), rows), :][:, :IDX_DH].astype(BF16)
        acc = jnp.zeros((rows, tq), F32)
        for h in range(IDX_HEADS):
            acc = acc + w_h[h] * jnp.maximum(_dot_nt(ik, iq_h[h]), 0.0)
        ok = (lax.broadcasted_iota(jnp.int32, (rows, 1), 0) + kb * tk) <= qpos
        s = jnp.where(ok, acc * idx_scale, neg_inf)
        for i in range(n_blocks):
            sc_ref[kb + i] = s[i * tk:(i + 1) * tk]
        rmin = jnp.minimum(rmin, jnp.min(jnp.where(ok, s, jnp.inf), axis=0, keepdims=True))
        rmax = jnp.maximum(rmax, jnp.max(s, axis=0, keepdims=True))
        return rmin, rmax

    carry = (jnp.full((1, tq), jnp.inf, F32), jnp.full((1, tq), neg_inf, F32))
    carry = lax.fori_loop(0, nkb // 4, lambda j, c: score_rows(4 * j, 4, c), carry)
    carry = lax.cond(nkb % 4 >= 2, lambda c: score_rows((nkb // 4) * 4, 2, c), lambda c: c, carry)
    rmin, rmax = lax.cond(nkb % 2 == 1, lambda c: score_rows(nkb - 1, 1, c), lambda c: c, carry)

    def any_query(flag):
        return jnp.max(jnp.where(flag, 1.0, 0.0))

    def midpoint(lo, hi):
        a = jnp.maximum(jnp.abs(lo), TINY)
        b = jnp.maximum(jnp.abs(hi), TINY)
        geo = jnp.sqrt(a) * jnp.sqrt(b)
        one_sign = (lo >= 0.0) | (hi <= 0.0)
        far = jnp.maximum(a, b) > 4.0 * jnp.minimum(a, b)
        mid = jnp.where(one_sign & far, jnp.where(lo >= 0.0, geo, -geo), 0.5 * lo + 0.5 * hi)
        mid = jnp.where(lo == 0.0, TINY, mid)
        mid = jnp.where(hi == 0.0, -TINY, mid)
        return jnp.where((lo < 0.0) & (hi > 0.0), 0.0, mid)

    def active(lo, hi, cl):
        mid = midpoint(lo, hi)
        return (cl > kf) & (mid > lo) & (mid < hi)

    def go_on(st):
        return st[3] > 0.0

    def threshold(n_blocks, rmin, rmax):
        blocks = range(n_blocks)

        def over_keys(fn, init, combine):
            acc = init
            for kb in blocks:
                acc = combine(acc, fn(sc_ref[kb]))
            return acc

        def count(pred):
            c = over_keys(lambda s: fold(jnp.where(pred(s), 1.0, 0.0)), jnp.zeros(part_shape, F32), jnp.add)
            return jnp.sum(c, axis=0, keepdims=True)

        c_max = count(lambda s: s >= rmax)
        top_tied = c_max >= kf
        lo = jnp.where(top_tied, rmax, rmin)
        hi = rmax
        cl = jnp.where(top_tied, c_max, (qpos + 1).astype(F32))

        def shrink_once(lo, hi, cl):
            mid = midpoint(lo, hi)
            act = active(lo, hi, cl)
            c = count(lambda s: s >= mid)
            up = act & (c >= kf)
            dn = act & (c < kf)
            return jnp.where(up, mid, lo), jnp.where(dn, mid, hi), jnp.where(up, c, cl)

        def shrink(st):
            lo, hi, cl = shrink_once(*st[:3])
            pending = any_query(active(lo, hi, cl))
            return (*shrink_once(lo, hi, cl), pending)

        lo, hi, cl, _ = lax.while_loop(go_on, shrink, (lo, hi, cl, jnp.float32(1.0)))
        tau_ref[...] = lo
        tie = cl > kf

        @pl.when(any_query(tie) > 0.0)
        def _():
            def max_below(bound):
                part = over_keys(
                    lambda s: jnp.max(jnp.where(s < bound, s, neg_inf).reshape(tk // SUBLANES, SUBLANES, tq), axis=0),
                    jnp.full(part_shape, neg_inf, F32), jnp.maximum)
                return jnp.max(part, axis=0, keepdims=True)

            hi_d = jnp.where(lo >= hi, jnp.inf, hi)
            v0 = max_below(hi_d)
            ge0 = count(lambda s: s >= v0)

            def walk_on(st):
                return st[3] > 0.0

            def walk(st):
                hi_w, v, ge, _ = st
                hi_w = jnp.where(tie & (ge < kf), v, hi_w)
                v = max_below(hi_w)
                ge = count(lambda s: s >= v)
                return hi_w, v, ge, any_query(tie & (ge < kf))

            _, v, _, _ = lax.while_loop(walk_on, walk, (hi_d, v0, ge0, any_query(tie & (ge0 < kf))))
            keep = kf - count(lambda s: s > v)
            ri = lax.broadcasted_iota(jnp.int32, (tk, tk), 0)
            ci = lax.broadcasted_iota(jnp.int32, (tk, tk), 1)
            earlier = jnp.where(ci < ri, 1.0, 0.0).astype(BF16)
            seen = jnp.zeros((1, tq), F32)
            for kb in blocks:
                s = sc_ref[kb]
                eq = tie & (s == v)
                eqf = jnp.where(eq, 1.0, 0.0)
                rank = seen + _dot(earlier, eqf.astype(BF16))
                sc_ref[kb] = jnp.where(eq & (rank >= keep), neg_inf, s)
                seen = seen + jnp.sum(eqf, axis=0, keepdims=True)
            tau_ref[...] = jnp.where(tie, v, lo)

        return jnp.int32(0)

    branches = [functools.partial(threshold, (i + 1) * (tq // tk)) for i in range(nq)]
    lax.switch(qi, branches, rmin, rmax)

    tau = tau_ref[...]

    def head(ref, h, rows=None):
        if rows is None:
            return ref[:, h * ATT_DH:(h + 1) * ATT_DH]
        return ref[0, rows, h * ATT_DH:(h + 1) * ATT_DH]

    def score_phase(kb, buf):
        rows = pl.ds(pl.multiple_of(kb * tk, tk), tk)
        bias = jnp.where(sc_ref[kb] >= tau, 0.0, neg_inf)
        for h in range(ATT_HEADS):
            s = _dot_nt(head(ak_ref, h, rows), head(aq_ref, h, slice(None))) + bias
            s_ref[buf, h] = s
            bm_ref[buf, h:h + 1, :] = jnp.max(s, axis=0, keepdims=True)

    def value_phase(kb, buf):
        rows = pl.ds(pl.multiple_of(kb * tk, tk), tk)
        for h in range(ATT_HEADS):
            m_new = jnp.maximum(m_ref[h:h + 1, :], bm_ref[buf, h:h + 1, :])
            p = jnp.exp2(s_ref[buf, h] - m_new)
            p_sum = jnp.sum(p, axis=0, keepdims=True)
            pv = lax.dot_general(head(av_ref, h, rows), p.astype(BF16), TN_DIMS, preferred_element_type=F32)
            alpha = jnp.exp2(m_ref[h:h + 1, :] - m_new)
            l_ref[h:h + 1, :] = alpha * l_ref[h:h + 1, :] + p_sum
            acc_ref[h] = alpha * acc_ref[h] + pv
            m_ref[h:h + 1, :] = m_new

    for h in range(ATT_HEADS):
        s = _dot_nt(head(akm_ref, h), head(aq_ref, h, slice(None)))
        m = jnp.max(s, axis=0, keepdims=True)
        p = jnp.exp2(s - m)
        m_ref[h:h + 1, :] = m
        l_ref[h:h + 1, :] = jnp.sum(p, axis=0, keepdims=True)
        acc_ref[h] = lax.dot_general(head(avm_ref, h), p.astype(BF16), TN_DIMS, preferred_element_type=F32)
    score_phase(0, 1)

    def att_blocks(kb, n):
        for i in range(n):
            buf = 1 - i % 2
            value_phase(kb + i, buf)
            score_phase(jnp.minimum(kb + i + 1, nkb - 1), 1 - buf)

    def att_quad(j, carry):
        att_blocks(4 * j, 4)
        return carry

    lax.fori_loop(0, nkb // 4, att_quad, 0)

    @pl.when(nkb % 4 >= 2)
    def _():
        att_blocks((nkb // 4) * 4, 2)

    @pl.when(nkb % 2 == 1)
    def _():
        value_phase(nkb - 1, 1)

    for h in range(ATT_HEADS):
        out_t = acc_ref[h] / l_ref[h:h + 1, :]
        o_ref[0, :, h * ATT_DH:(h + 1) * ATT_DH] = out_t.T.astype(o_ref.dtype)


def _dsa(iq, ikw, aqk, av, akm, avm, *, tq, tk, k_top):
    B, S, _ = av.shape
    nq = S // tq
    assert tq % tk == 0
    return pl.pallas_call(
        functools.partial(_dsa_kernel, nq=nq, tq=tq, tk=tk, k_top=k_top),
        grid=(B, nq),
        in_specs=[
            pl.BlockSpec((1, tq, IDX_Q), lambda b, q: (b, q, 0)),
            pl.BlockSpec((1, S, LANES), lambda b, q: (b, 0, 0)),
            pl.BlockSpec((1, tq, LANES), lambda b, q: (b, q, 0)),
            pl.BlockSpec((1, tq, ATT_W), lambda b, q: (b, q, 0)),
            pl.BlockSpec((1, S, ATT_W), lambda b, q: (b, 0, 1)),
            pl.BlockSpec((1, S, ATT_W), lambda b, q: (b, 0, 0)),
            pl.BlockSpec((N_META, ATT_W), lambda b, q: (0, 0)),
            pl.BlockSpec((N_META, ATT_W), lambda b, q: (0, 0)),
        ],
        out_specs=pl.BlockSpec((1, tq, ATT_W), lambda b, q: (b, q, 0)),
        out_shape=jax.ShapeDtypeStruct((B, S, ATT_W), BF16),
        scratch_shapes=[
            pltpu.VMEM((S // tk, tk, tq), F32),
            pltpu.VMEM((1, tq), F32),
            pltpu.VMEM((ATT_HEADS, tq), F32),
            pltpu.VMEM((ATT_HEADS, tq), F32),
            pltpu.VMEM((ATT_HEADS, ATT_DH, tq), F32),
            pltpu.VMEM((2, ATT_HEADS, tk, tq), F32),
            pltpu.VMEM((2, ATT_HEADS, tq), F32),
        ],
        compiler_params=_params(("parallel", "arbitrary")),
        name="dsa",
    )(iq, ikw, ikw, aqk, aqk, av, akm, avm)


def _merge_kernel(h_ref, yr_ref, ya_ref, gr_ref, ga_ref, wr_ref, wa_ref, wm_ref, o_ref):
    yr = _dot(yr_ref[...], wr_ref[...])
    ya = _dot(ya_ref[...], wa_ref[...])
    merged = _sigmoid(gr_ref[...].astype(F32)) * yr + _sigmoid(ga_ref[...].astype(F32)) * ya
    o_ref[...] = h_ref[...] + _dot(merged.astype(BF16), wm_ref[...])


def _merge(h, yr, ya, gates, wr, wa, wm, *, tm):
    M, D = h.shape
    row = lambda i: (i, 0)
    fixed = lambda i: (0, 0)
    return pl.pallas_call(
        _merge_kernel,
        grid=(M // tm,),
        in_specs=[
            pl.BlockSpec((tm, D), row),
            pl.BlockSpec((tm, RET_V), row),
            pl.BlockSpec((tm, ATT_W), row),
            pl.BlockSpec((tm, D), lambda i: (i, 0)),
            pl.BlockSpec((tm, D), lambda i: (i, 1)),
            pl.BlockSpec((RET_V, D), fixed),
            pl.BlockSpec((ATT_W, D), fixed),
            pl.BlockSpec((D, D), fixed),
        ],
        out_specs=pl.BlockSpec((tm, D), row),
        out_shape=jax.ShapeDtypeStruct((M, D), F32),
        compiler_params=_params(("parallel",)),
        name="merge",
    )(h, yr, ya, gates, gates, wr, wa, wm)


def _rope_tables(pos, d, width):
    inv_freq = ROPE_THETA ** (-jnp.arange(0, d, 2, dtype=F32) / d)
    ang = pos.astype(F32)[:, None] * inv_freq[None, :]
    cos, sin = jnp.cos(ang), jnp.sin(ang)
    cos = jnp.concatenate([cos, cos], axis=-1)
    sin = jnp.concatenate([-sin, sin], axis=-1)
    rep = width // d
    return jnp.tile(cos, (1, rep)), jnp.tile(sin, (1, rep))


def _tile(n, cap):
    t = min(n, cap)
    assert n % t == 0, (n, t)
    return t


class _Plan(NamedTuple):
    rows: int
    rows_tab: int
    rows_wide: int
    rows_merge: int
    cols: int
    ffn_slab: int
    att: int
    chunk: int
    ret_rows: int


def _plan(S, M):
    return _Plan(rows=_tile(S, 4 * MXU_DIM), rows_tab=_tile(S, 8 * MXU_DIM), rows_wide=_tile(M, 8 * MXU_DIM),
                 rows_merge=_tile(M, 2 * MXU_DIM), cols=4 * MXU_DIM, ffn_slab=MXU_DIM, att=_tile(S, MXU_DIM),
                 chunk=_tile(S, 4 * RET_CHUNK), ret_rows=_tile(S, 4 * RET_CHUNK))


def kernel(x, meta_tokens, ffn1_norm, ffn1_w_gate, ffn1_w_up, ffn1_w_down, mix_norm, w_in, w_ret_out,
           w_att_out, w_mix_out, ffn2_norm, ffn2_w_gate, ffn2_w_up, ffn2_w_down, final_norm):
    B, S, D = x.shape
    assert ffn1_norm.shape[0] == 1, "single layer only"
    assert meta_tokens.shape[0] == N_META
    k_top = min(TOPK_MAX, S // 4)
    M = B * S
    plan = _plan(S, M)
    bf = lambda a: a.astype(BF16)
    row = lambda a: a.reshape(1, -1)

    w32 = w_in[0].T
    offs = np.cumsum([0, RET_QK, RET_QK, RET_V, RET_V, ATT_W, ATT_W, ATT_W, IDX_Q, IDX_DH, IDX_HEADS, D, D])
    tn = plan.cols
    assert all(int(o) % tn == 0 for o in offs[:8])
    c_rq, c_rk, c_rv, c_rg, c_aq, c_ak, c_av = (int(o) // tn for o in offs[:7])
    w_idx = jnp.pad(bf(w32[offs[7]:offs[10]].T), ((0, 0), (0, LANES - IDX_DH - IDX_HEADS)))
    w_gate = w32[offs[10]:offs[12]]

    pos_meta = jnp.arange(N_META, dtype=jnp.int32)
    pos_real = N_META + jnp.arange(S, dtype=jnp.int32)
    ffn1 = (row(ffn1_norm[0]), bf(ffn1_w_gate[0]), bf(ffn1_w_up[0]), bf(ffn1_w_down[0]), row(mix_norm[0]))
    ffn2 = (row(ffn2_norm[0]), bf(ffn2_w_gate[0]), bf(ffn2_w_up[0]), bf(ffn2_w_down[0]), row(final_norm))
    tf = plan.ffn_slab
    k_scale = RET_DK ** -0.5

    _, un_m = _ffn(meta_tokens.astype(F32), *ffn1, tm=N_META, tf=tf, emit_h=True, n_dtype=BF16)
    pm = dict(tm=N_META, tn=tn)
    rk_m = _proj(un_m, w32, col0=c_rk, n_tiles=RET_QK // tn, mode="rope", d=RET_DK, k_scale=k_scale,
                 tabs=_rope_tables(pos_meta, RET_DK, RET_DK), **pm)
    rv_m = _proj(un_m, w32, col0=c_rv, n_tiles=RET_V // tn, mode="plain", **pm)
    ak_m = _proj(un_m, w32, col0=c_ak, n_tiles=ATT_W // tn, mode="rope", d=ATT_DH,
                 tabs=_rope_tables(pos_meta, ATT_DH, LANES), **pm)
    av_m = _proj(un_m, w32, col0=c_av, n_tiles=ATT_W // tn, mode="plain", **pm)

    tm = plan.rows
    h1, un = _ffn(x.reshape(M, D), *ffn1, tm=tm, tf=tf, emit_h=True, n_dtype=BF16)
    pr = dict(tm=plan.rows_tab, tn=tn)
    pw = dict(tm=plan.rows_wide, tn=tn)
    rqk = _proj(un, w32, col0=c_rq, n_tiles=2 * RET_QK // tn, mode="rope", d=RET_DK, k_scale=k_scale,
                n_q=RET_QK // tn, tabs=_rope_tables(pos_real, RET_DK, RET_DK), **pr)
    rv = _proj(un, w32, col0=c_rv, n_tiles=RET_V // tn, mode="plain", **pw)
    rg = _proj(un, w32, col0=c_rg, n_tiles=RET_V // tn, mode="silu", **pw)
    aqk = _proj(un, w32, col0=c_aq, n_tiles=2 * ATT_W // tn, mode="rope", d=ATT_DH, q_scale=ATT_EXP_SCALE,
                n_q=ATT_W // tn, tabs=_rope_tables(pos_real, ATT_DH, LANES), **pr)
    av = _proj(un, w32, col0=c_av, n_tiles=ATT_W // tn, mode="plain", **pw)
    gates = _proj(un, w_gate, col0=0, n_tiles=2 * D // tn, mode="plain", **pw)
    iq, ikw = _idx_proj(un, w_idx, _rope_tables(pos_real, IDX_DH, LANES), tm=tm)

    heads = jnp.arange(RET_HEADS, dtype=F32)
    log_g = jnp.log1p(-(2.0 ** (-5.0 - heads)))
    C = plan.chunk
    b3 = lambda a: a.reshape(B, S, a.shape[-1])
    yr = _retention(log_g, jnp.exp(log_g * C), b3(rqk), b3(rv), b3(rg), rk_m, rv_m, C=C, R=plan.ret_rows)
    ya = _dsa(b3(iq), b3(ikw), b3(aqk), b3(av), ak_m, av_m, tq=plan.att, tk=plan.att, k_top=k_top)

    h2 = _merge(h1, yr.reshape(M, RET_V), ya.reshape(M, ATT_W), gates, bf(w_ret_out[0]), bf(w_att_out[0]),
                bf(w_mix_out[0]), tm=plan.rows_merge)
    (y,) = _ffn(h2, *ffn2, tm=tm, tf=tf, emit_h=False, n_dtype=F32)
    return y.reshape(B, S, D)
```

```python
import functools
from typing import NamedTuple

import jax
import jax.numpy as jnp
import numpy as np
from jax import lax
from jax.experimental import pallas as pl
from jax.experimental.pallas import tpu as pltpu

N_META = 16
RET_HEADS = 4
RET_DK = 256
RET_DV = 512
RET_CHUNK = 128
ATT_HEADS = 8
ATT_DH = 128
IDX_HEADS = 8
IDX_DH = 64
TOPK_MAX = 256
ROPE_THETA = 10000.0
EPS = 1e-6
RET_QK = RET_HEADS * RET_DK
RET_V = RET_HEADS * RET_DV
ATT_W = ATT_HEADS * ATT_DH
IDX_Q = IDX_HEADS * IDX_DH
ATT_EXP_SCALE = (ATT_DH ** -0.5) * float(np.log2(np.e))

LANES = 128
SUBLANES = 8
MXU_DIM = 256
PROJ_SUB_ROWS = MXU_DIM
VMEM_LIMIT = 48 * 1024 * 1024
FFN_VMEM_LIMIT = 56 * 1024 * 1024

BF16 = jnp.bfloat16
F32 = jnp.float32
NT_DIMS = (((1,), (1,)), ((), ()))


def _dot(a, b):
    return jnp.dot(a, b, preferred_element_type=F32)


def _dot_nt(a, b):
    return lax.dot_general(a, b, NT_DIMS, preferred_element_type=F32)


def _rms(x, g):
    return x * lax.rsqrt(jnp.mean(x * x, axis=-1, keepdims=True) + EPS) * g


def _sigmoid(x):
    return 0.5 * jnp.tanh(0.5 * x) + 0.5


def _params(sem):
    return pltpu.CompilerParams(dimension_semantics=sem, vmem_limit_bytes=VMEM_LIMIT)


def _ffn_kernel(x_ref, g_ref, wg_ref, wu_ref, wd_ref, g2_ref, *rest, nf, tf, emit_h):
    if emit_h:
        h_ref, n_ref, xn_sc, acc_sc, act_sc = rest
    else:
        n_ref, xn_sc, acc_sc, act_sc = rest

    def down(f):
        rows = pl.ds(pl.multiple_of(f * tf, tf), tf)
        return _dot(act_sc[...], wd_ref[rows, :])

    def up(f):
        cols = pl.ds(pl.multiple_of(f * tf, tf), tf)
        xn = xn_sc[...]
        gate = _dot(xn, wg_ref[:, cols])
        lift = _dot(xn, wu_ref[:, cols])
        act_sc[...] = ((gate * _sigmoid(gate)) * lift).astype(BF16)

    xn_sc[...] = _rms(x_ref[...], g_ref[...]).astype(BF16)
    acc_sc[...] = jnp.zeros_like(acc_sc)
    up(0)

    def slab(f, carry):
        acc_sc[...] += down(f - 1)
        up(f)
        return carry

    lax.fori_loop(1, nf, slab, 0, unroll=True)
    h = x_ref[...] + 0.5 * (acc_sc[...] + down(nf - 1))
    if emit_h:
        h_ref[...] = h
    n_ref[...] = _rms(h, g2_ref[...]).astype(n_ref.dtype)


def _ffn(x, g, wg, wu, wd, g2, *, tm, tf, emit_h, n_dtype):
    M, D = x.shape
    F = wg.shape[1]
    nf = F // tf
    assert nf >= 2
    out_shape = [jax.ShapeDtypeStruct((M, D), n_dtype)]
    out_specs = [pl.BlockSpec((tm, D), lambda i: (i, 0))]
    if emit_h:
        out_shape = [jax.ShapeDtypeStruct((M, D), F32)] + out_shape
        out_specs = [pl.BlockSpec((tm, D), lambda i: (i, 0))] + out_specs
    resident = lambda shape: pl.BlockSpec(shape, lambda i: (0, 0), pipeline_mode=pl.Buffered(1))
    return pl.pallas_call(
        functools.partial(_ffn_kernel, nf=nf, tf=tf, emit_h=emit_h),
        grid=(M // tm,),
        in_specs=[
            pl.BlockSpec((tm, D), lambda i: (i, 0)),
            pl.BlockSpec((1, D), lambda i: (0, 0)),
            resident((D, F)),
            resident((D, F)),
            resident((F, D)),
            pl.BlockSpec((1, D), lambda i: (0, 0)),
        ],
        out_specs=out_specs,
        out_shape=out_shape,
        scratch_shapes=[pltpu.VMEM((tm, D), BF16), pltpu.VMEM((tm, D), F32), pltpu.VMEM((tm, tf), BF16)],
        compiler_params=pltpu.CompilerParams(
            dimension_semantics=("arbitrary",), vmem_limit_bytes=FFN_VMEM_LIMIT),
        name="ffn",
    )(x, g, wg, wu, wd, g2)


def _rot_half(y, d):
    w = y.shape[-1]
    if d >= w:
        return jnp.concatenate([y[:, w // 2:], y[:, : w // 2]], axis=-1)
    if d == LANES:
        return pltpu.roll(y, d // 2, axis=1)
    fwd = pltpu.roll(y, d // 2, axis=1)
    bwd = pltpu.roll(y, w - d // 2, axis=1)
    lane = lax.broadcasted_iota(jnp.int32, y.shape, 1)
    return jnp.where((lane % d) < d // 2, bwd, fwd)


def _rope_tile(y, cos, sin, d):
    tw = cos.shape[-1]
    outs = []
    for t in range(y.shape[-1] // tw):
        yt = y[:, t * tw:(t + 1) * tw]
        outs.append(yt * cos + _rot_half(yt, d) * sin)
    return outs[0] if len(outs) == 1 else jnp.concatenate(outs, axis=-1)


def _proj_kernel(x_ref, w_ref, *rest, mode, d, q_scale, k_scale, n_q):
    if mode == "rope":
        cos_ref, sin_ref, o_ref, w_sc = rest
    else:
        o_ref, w_sc = rest

    @pl.when(pl.program_id(1) == 0)
    def _():
        w_sc[...] = w_ref[...].T.astype(BF16)

    tm = x_ref.shape[0]
    sub = min(tm, PROJ_SUB_ROWS) if mode == "rope" else tm
    for r0 in range(0, tm, sub):
        rows = slice(r0, r0 + sub)
        y = _dot(x_ref[rows, :], w_sc[...])
        if mode == "rope":
            y = _rope_tile(y, cos_ref[rows, :], sin_ref[rows, :], d)
            if q_scale != 1.0 or k_scale != 1.0:
                y = y * jnp.where(pl.program_id(0) >= n_q, k_scale, q_scale)
        elif mode == "silu":
            y = y * _sigmoid(y)
        o_ref[rows, :] = y.astype(o_ref.dtype)


def _proj(x, w_t, *, col0, n_tiles, tm, tn, mode, out_dtype=None, tabs=None, d=None, q_scale=1.0,
          k_scale=1.0, n_q=0):
    M, K = x.shape
    in_specs = [
        pl.BlockSpec((tm, K), lambda j, i: (i, 0)),
        pl.BlockSpec((tn, K), lambda j, i: (col0 + j, 0)),
    ]
    args = [x, w_t]
    if mode == "rope":
        cos, sin = tabs
        npos = cos.shape[0] // tm
        tw = cos.shape[1]
        in_specs += [pl.BlockSpec((tm, tw), lambda j, i: (i % npos, 0))] * 2
        args += [cos, sin]
    return pl.pallas_call(
        functools.partial(_proj_kernel, mode=mode, d=d, q_scale=q_scale, k_scale=k_scale, n_q=n_q),
        grid=(n_tiles, M // tm),
        in_specs=in_specs,
        out_specs=pl.BlockSpec((tm, tn), lambda j, i: (i, j)),
        out_shape=jax.ShapeDtypeStruct((M, n_tiles * tn), out_dtype or BF16),
        scratch_shapes=[pltpu.VMEM((K, tn), BF16)],
        compiler_params=_params(("arbitrary", "arbitrary")),
        name="proj_" + mode,
    )(*args)


def _idx_proj_kernel(x_ref, w_ref, cos_ref, sin_ref, q_ref, kw_ref):
    tm = x_ref.shape[0]
    sub = min(tm, PROJ_SUB_ROWS)
    for r0 in range(0, tm, sub):
        rows = slice(r0, r0 + sub)
        y = _dot(x_ref[rows, :], w_ref[...])
        cos, sin = cos_ref[rows, :], sin_ref[rows, :]
        q_ref[rows, :] = _rope_tile(y[:, :IDX_Q], cos, sin, IDX_DH)
        tail = y[:, IDX_Q:]
        lane = lax.broadcasted_iota(jnp.int32, tail.shape, 1)
        kw_ref[rows, :] = jnp.where(lane < IDX_DH, _rope_tile(tail, cos, sin, IDX_DH), tail)


def _idx_proj(x, w, tabs, *, tm):
    M, K = x.shape
    N = w.shape[1]
    cos, sin = tabs
    npos = cos.shape[0] // tm
    return pl.pallas_call(
        _idx_proj_kernel,
        grid=(M // tm,),
        in_specs=[
            pl.BlockSpec((tm, K), lambda i: (i, 0)),
            pl.BlockSpec((K, N), lambda i: (0, 0)),
            pl.BlockSpec((tm, LANES), lambda i: (i % npos, 0)),
            pl.BlockSpec((tm, LANES), lambda i: (i % npos, 0)),
        ],
        out_specs=[
            pl.BlockSpec((tm, IDX_Q), lambda i: (i, 0)),
            pl.BlockSpec((tm, LANES), lambda i: (i, 0)),
        ],
        out_shape=[
            jax.ShapeDtypeStruct((M, IDX_Q), F32),
            jax.ShapeDtypeStruct((M, LANES), F32),
        ],
        compiler_params=_params(("parallel",)),
        name="proj_idx",
    )(x, w, cos, sin)


def _ret_kernel(lg_ref, gc_ref, q_ref, k_ref, v_ref, g_ref, km_ref, vm_ref, o_ref, st_ref, st0_ref, decay_ref, *, C):
    c = pl.program_id(1)
    dk = lambda h: slice(h * RET_DK, (h + 1) * RET_DK)
    dv = lambda h: slice(h * RET_DV, (h + 1) * RET_DV)

    @pl.when((pl.program_id(0) == 0) & (c == 0))
    def _():
        pm = lax.broadcasted_iota(jnp.int32, (N_META, 1), 0).astype(F32)
        ri = lax.broadcasted_iota(jnp.int32, (C, C), 0)
        ci = lax.broadcasted_iota(jnp.int32, (C, C), 1)
        diff = (ri - ci).astype(F32)
        for h in range(RET_HEADS):
            lg = lg_ref[h]
            zeta_m = jnp.exp(lg * (N_META - 1.0 - pm))
            kz = km_ref[:, dk(h)].astype(F32) * zeta_m
            st0_ref[h] = _dot(kz.T.astype(BF16), vm_ref[:, dv(h)])
            decay_ref[h] = jnp.where(diff >= 0, jnp.exp(lg * jnp.maximum(diff, 0.0)), 0.0)

    @pl.when(c == 0)
    def _():
        st_ref[...] = st0_ref[...]

    pos = lax.broadcasted_iota(jnp.int32, (C, 1), 0).astype(F32)
    for h in range(RET_HEADS):
        lg = lg_ref[h]
        xi = jnp.exp(lg * (pos + 1.0))
        zeta = jnp.exp(lg * (C - 1.0 - pos))
        state = st_ref[h]
        for r0 in range(0, q_ref.shape[1], C):
            rows = slice(r0, r0 + C)
            q = q_ref[0, rows, dk(h)]
            k = k_ref[0, rows, dk(h)]
            v = v_ref[0, rows, dv(h)]
            inner = _dot((_dot_nt(q, k) * decay_ref[h]).astype(BF16), v)
            cross = _dot(q, state.astype(BF16)) * xi
            y = inner + cross
            y = y * lax.rsqrt(jnp.mean(y * y, axis=-1, keepdims=True) + EPS)
            o_ref[0, rows, dv(h)] = (g_ref[0, rows, dv(h)] * y).astype(o_ref.dtype)
            kz = k.astype(F32) * zeta
            state = gc_ref[h] * state + _dot(kz.T.astype(BF16), v)
        st_ref[h] = state


def _retention(lg, gc, rqk, rv, rg, km, vm, *, C, R):
    B, S, _ = rv.shape
    smem = pl.BlockSpec(memory_space=pltpu.SMEM)
    return pl.pallas_call(
        functools.partial(_ret_kernel, C=C),
        grid=(B, S // R),
        in_specs=[
            smem,
            smem,
            pl.BlockSpec((1, R, RET_QK), lambda b, c: (b, c, 0)),
            pl.BlockSpec((1, R, RET_QK), lambda b, c: (b, c, 1)),
            pl.BlockSpec((1, R, RET_V), lambda b, c: (b, c, 0)),
            pl.BlockSpec((1, R, RET_V), lambda b, c: (b, c, 0)),
            pl.BlockSpec((N_META, RET_QK), lambda b, c: (0, 0)),
            pl.BlockSpec((N_META, RET_V), lambda b, c: (0, 0)),
        ],
        out_specs=pl.BlockSpec((1, R, RET_V), lambda b, c: (b, c, 0)),
        out_shape=jax.ShapeDtypeStruct((B, S, RET_V), BF16),
        scratch_shapes=[pltpu.VMEM((RET_HEADS, RET_DK, RET_DV), F32), pltpu.VMEM((RET_HEADS, RET_DK, RET_DV), F32),
                        pltpu.VMEM((RET_HEADS, C, C), F32)],
        compiler_params=_params(("arbitrary", "arbitrary")),
        name="retention",
    )(lg, gc, rqk, rqk, rv, rg, km, vm)


TINY = float(np.finfo(np.float32).tiny)
TN_DIMS = (((0,), (0,)), ((), ()))


def _dsa_kernel(iq_ref, ikw_ref, wq_ref, aq_ref, ak_ref, av_ref, akm_ref, avm_ref, o_ref,
                sc_ref, tau_ref, m_ref, l_ref, acc_ref, s_ref, bm_ref, *, nq, tq, tk, k_top):
    qi = pl.program_id(1)
    nkb = (qi + 1) * (tq // tk)
    kf = float(k_top)
    neg_inf = -jnp.inf
    qpos = qi * tq + lax.broadcasted_iota(jnp.int32, (1, tq), 1)

    part_shape = (SUBLANES, tq)

    def fold(x):
        return x.reshape(tk // SUBLANES, SUBLANES, tq).sum(axis=0)

    idx_scale = (IDX_DH ** -0.5) * (IDX_HEADS ** -0.5)
    iq = iq_ref[0].astype(BF16)
    w_t = wq_ref[0].T
    iq_h = [iq[:, h * IDX_DH:(h + 1) * IDX_DH] for h in range(IDX_HEADS)]
    w_h = [w_t[IDX_DH + h:IDX_DH + h + 1, :] for h in range(IDX_HEADS)]

    def score_rows(kb, n_blocks, carry):
        rmin, rmax = carry
        rows = n_blocks * tk
        ik = ikw_ref[0, pl.ds(pl.multiple_of(kb * tk, tk), rows), :][:, :IDX_DH].astype(BF16)
        acc = jnp.zeros((rows, tq), F32)
        for h in range(IDX_HEADS):
            acc = acc + w_h[h] * jnp.maximum(_dot_nt(ik, iq_h[h]), 0.0)
        ok = (lax.broadcasted_iota(jnp.int32, (rows, 1), 0) + kb * tk) <= qpos
        s = jnp.where(ok, acc * idx_scale, neg_inf)
        for i in range(n_blocks):
            sc_ref[kb + i] = s[i * tk:(i + 1) * tk]
        rmin = jnp.minimum(rmin, jnp.min(jnp.where(ok, s, jnp.inf), axis=0, keepdims=True))
        rmax = jnp.maximum(rmax, jnp.max(s, axis=0, keepdims=True))
        return rmin, rmax

    carry = (jnp.full((1, tq), jnp.inf, F32), jnp.full((1, tq), neg_inf, F32))
    carry = lax.fori_loop(0, nkb // 4, lambda j, c: score_rows(4 * j, 4, c), carry)
    carry = lax.cond(nkb % 4 >= 2, lambda c: score_rows((nkb // 4) * 4, 2, c), lambda c: c, carry)
    rmin, rmax = lax.cond(nkb % 2 == 1, lambda c: score_rows(nkb - 1, 1, c), lambda c: c, carry)

    def any_query(flag):
        return jnp.max(jnp.where(flag, 1.0, 0.0))

    def midpoint(lo, hi):
        a = jnp.maximum(jnp.abs(lo), TINY)
        b = jnp.maximum(jnp.abs(hi), TINY)
        geo = jnp.sqrt(a) * jnp.sqrt(b)
        one_sign = (lo >= 0.0) | (hi <= 0.0)
        far = jnp.maximum(a, b) > 4.0 * jnp.minimum(a, b)
        mid = jnp.where(one_sign & far, jnp.where(lo >= 0.0, geo, -geo), 0.5 * lo + 0.5 * hi)
        mid = jnp.where(lo == 0.0, TINY, mid)
        mid = jnp.where(hi == 0.0, -TINY, mid)
        return jnp.where((lo < 0.0) & (hi > 0.0), 0.0, mid)

    def active(lo, hi, cl):
        mid = midpoint(lo, hi)
        return (cl > kf) & (mid > lo) & (mid < hi)

    def go_on(st):
        return st[3] > 0.0

    def threshold(n_blocks, rmin, rmax):
        blocks = range(n_blocks)

        def over_keys(fn, init, combine):
            acc = init
            for kb in blocks:
                acc = combine(acc, fn(sc_ref[kb]))
            return acc

        def count(pred):
            c = over_keys(lambda s: fold(jnp.where(pred(s), 1.0, 0.0)), jnp.zeros(part_shape, F32), jnp.add)
            return jnp.sum(c, axis=0, keepdims=True)

        c_max = count(lambda s: s >= rmax)
        top_tied = c_max >= kf
        lo = jnp.where(top_tied, rmax, rmin)
        hi = rmax
        cl = jnp.where(top_tied, c_max, (qpos + 1).astype(F32))

        def shrink_once(lo, hi, cl):
            mid = midpoint(lo, hi)
            act = active(lo, hi, cl)
            c = count(lambda s: s >= mid)
            up = act & (c >= kf)
            dn = act & (c < kf)
            return jnp.where(up, mid, lo), jnp.where(dn, mid, hi), jnp.where(up, c, cl)

        def shrink(st):
            lo, hi, cl = shrink_once(*st[:3])
            pending = any_query(active(lo, hi, cl))
            return (*shrink_once(lo, hi, cl), pending)

        lo, hi, cl, _ = lax.while_loop(go_on, shrink, (lo, hi, cl, jnp.float32(1.0)))
        tau_ref[...] = lo
        tie = cl > kf

        @pl.when(any_query(tie) > 0.0)
        def _():
            def max_below(bound):
                part = over_keys(
                    lambda s: jnp.max(jnp.where(s < bound, s, neg_inf).reshape(tk // SUBLANES, SUBLANES, tq), axis=0),
                    jnp.full(part_shape, neg_inf, F32), jnp.maximum)
                return jnp.max(part, axis=0, keepdims=True)

            hi_d = jnp.where(lo >= hi, jnp.inf, hi)
            v0 = max_below(hi_d)
            ge0 = count(lambda s: s >= v0)

            def walk_on(st):
                return st[3] > 0.0

            def walk(st):
                hi_w, v, ge, _ = st
                hi_w = jnp.where(tie & (ge < kf), v, hi_w)
                v = max_below(hi_w)
                ge = count(lambda s: s >= v)
                return hi_w, v, ge, any_query(tie & (ge < kf))

            _, v, _, _ = lax.while_loop(walk_on, walk, (hi_d, v0, ge0, any_query(tie & (ge0 < kf))))
            keep = kf - count(lambda s: s > v)
            ri = lax.broadcasted_iota(jnp.int32, (tk, tk), 0)
            ci = lax.broadcasted_iota(jnp.int32, (tk, tk), 1)
            earlier = jnp.where(ci < ri, 1.0, 0.0).astype(BF16)
            seen = jnp.zeros((1, tq), F32)
            for kb in blocks:
                s = sc_ref[kb]
                eq = tie & (s == v)
                eqf = jnp.where(eq, 1.0, 0.0)
                rank = seen + _dot(earlier, eqf.astype(BF16))
                sc_ref[kb] = jnp.where(eq & (rank >= keep), neg_inf, s)
                seen = seen + jnp.sum(eqf, axis=0, keepdims=True)
            tau_ref[...] = jnp.where(tie, v, lo)

        return jnp.int32(0)

    branches = [functools.partial(threshold, (i + 1) * (tq // tk)) for i in range(nq)]
    lax.switch(qi, branches, rmin, rmax)

    tau = tau_ref[...]

    def head(ref, h, rows=None):
        if rows is None:
            return ref[:, h * ATT_DH:(h + 1) * ATT_DH]
        return ref[0, rows, h * ATT_DH:(h + 1) * ATT_DH]

    def score_phase(kb, buf):
        rows = pl.ds(pl.multiple_of(kb * tk, tk), tk)
        bias = jnp.where(sc_ref[kb] >= tau, 0.0, neg_inf)
        for h in range(ATT_HEADS):
            s = _dot_nt(head(ak_ref, h, rows), head(aq_ref, h, slice(None))) + bias
            s_ref[buf, h] = s
            bm_ref[buf, h:h + 1, :] = jnp.max(s, axis=0, keepdims=True)

    def value_phase(kb, buf):
        rows = pl.ds(pl.multiple_of(kb * tk, tk), tk)
        for h in range(ATT_HEADS):
            m_new = jnp.maximum(m_ref[h:h + 1, :], bm_ref[buf, h:h + 1, :])
            p = jnp.exp2(s_ref[buf, h] - m_new)
            p_sum = jnp.sum(p, axis=0, keepdims=True)
            pv = lax.dot_general(head(av_ref, h, rows), p.astype(BF16), TN_DIMS, preferred_element_type=F32)
            alpha = jnp.exp2(m_ref[h:h + 1, :] - m_new)
            l_ref[h:h + 1, :] = alpha * l_ref[h:h + 1, :] + p_sum
            acc_ref[h] = alpha * acc_ref[h] + pv
            m_ref[h:h + 1, :] = m_new

    for h in range(ATT_HEADS):
        s = _dot_nt(head(akm_ref, h), head(aq_ref, h, slice(None)))
        m = jnp.max(s, axis=0, keepdims=True)
        p = jnp.exp2(s - m)
        m_ref[h:h + 1, :] = m
        l_ref[h:h + 1, :] = jnp.sum(p, axis=0, keepdims=True)
        acc_ref[h] = lax.dot_general(head(avm_ref, h), p.astype(BF16), TN_DIMS, preferred_element_type=F32)
    score_phase(0, 1)

    def att_blocks(kb, n):
        for i in range(n):
            buf = 1 - i % 2
            value_phase(kb + i, buf)
            score_phase(jnp.minimum(kb + i + 1, nkb - 1), 1 - buf)

    def att_quad(j, carry):
        att_blocks(4 * j, 4)
        return carry

    lax.fori_loop(0, nkb // 4, att_quad, 0)

    @pl.when(nkb % 4 >= 2)
    def _():
        att_blocks((nkb // 4) * 4, 2)

    @pl.when(nkb % 2 == 1)
    def _():
        value_phase(nkb - 1, 1)

    for h in range(ATT_HEADS):
        out_t = acc_ref[h] / l_ref[h:h + 1, :]
        o_ref[0, :, h * ATT_DH:(h + 1) * ATT_DH] = out_t.T.astype(o_ref.dtype)


def _dsa(iq, ikw, aqk, av, akm, avm, *, tq, tk, k_top):
    B, S, _ = av.shape
    nq = S // tq
    assert tq % tk == 0
    return pl.pallas_call(
        functools.partial(_dsa_kernel, nq=nq, tq=tq, tk=tk, k_top=k_top),
        grid=(B, nq),
        in_specs=[
            pl.BlockSpec((1, tq, IDX_Q), lambda b, q: (b, q, 0)),
            pl.BlockSpec((1, S, LANES), lambda b, q: (b, 0, 0)),
            pl.BlockSpec((1, tq, LANES), lambda b, q: (b, q, 0)),
            pl.BlockSpec((1, tq, ATT_W), lambda b, q: (b, q, 0)),
            pl.BlockSpec((1, S, ATT_W), lambda b, q: (b, 0, 1)),
            pl.BlockSpec((1, S, ATT_W), lambda b, q: (b, 0, 0)),
            pl.BlockSpec((N_META, ATT_W), lambda b, q: (0, 0)),
            pl.BlockSpec((N_META, ATT_W), lambda b, q: (0, 0)),
        ],
        out_specs=pl.BlockSpec((1, tq, ATT_W), lambda b, q: (b, q, 0)),
        out_shape=jax.ShapeDtypeStruct((B, S, ATT_W), BF16),
        scratch_shapes=[
            pltpu.VMEM((S // tk, tk, tq), F32),
            pltpu.VMEM((1, tq), F32),
            pltpu.VMEM((ATT_HEADS, tq), F32),
            pltpu.VMEM((ATT_HEADS, tq), F32),
            pltpu.VMEM((ATT_HEADS, ATT_DH, tq), F32),
            pltpu.VMEM((2, ATT_HEADS, tk, tq), F32),
            pltpu.VMEM((2, ATT_HEADS, tq), F32),
        ],
        compiler_params=_params(("parallel", "arbitrary")),
        name="dsa",
    )(iq, ikw, ikw, aqk, aqk, av, akm, avm)


def _merge_kernel(h_ref, yr_ref, ya_ref, gr_ref, ga_ref, wr_ref, wa_ref, wm_ref, o_ref):
    yr = _dot(yr_ref[...], wr_ref[...])
    ya = _dot(ya_ref[...], wa_ref[...])
    merged = _sigmoid(gr_ref[...].astype(F32)) * yr + _sigmoid(ga_ref[...].astype(F32)) * ya
    o_ref[...] = h_ref[...] + _dot(merged.astype(BF16), wm_ref[...])


def _merge(h, yr, ya, gates, wr, wa, wm, *, tm):
    M, D = h.shape
    row = lambda i: (i, 0)
    resident = lambda shape: pl.BlockSpec(shape, lambda i: (0, 0), pipeline_mode=pl.Buffered(1))
    return pl.pallas_call(
        _merge_kernel,
        grid=(M // tm,),
        in_specs=[
            pl.BlockSpec((tm, D), row),
            pl.BlockSpec((tm, RET_V), row),
            pl.BlockSpec((tm, ATT_W), row),
            pl.BlockSpec((tm, D), lambda i: (i, 0)),
            pl.BlockSpec((tm, D), lambda i: (i, 1)),
            resident((RET_V, D)),
            resident((ATT_W, D)),
            resident((D, D)),
        ],
        out_specs=pl.BlockSpec((tm, D), row),
        out_shape=jax.ShapeDtypeStruct((M, D), F32),
        compiler_params=_params(("parallel",)),
        name="merge",
    )(h, yr, ya, gates, gates, wr, wa, wm)


def _rope_tables(pos, d, width):
    inv_freq = ROPE_THETA ** (-jnp.arange(0, d, 2, dtype=F32) / d)
    ang = pos.astype(F32)[:, None] * inv_freq[None, :]
    cos, sin = jnp.cos(ang), jnp.sin(ang)
    cos = jnp.concatenate([cos, cos], axis=-1)
    sin = jnp.concatenate([-sin, sin], axis=-1)
    rep = width // d
    return jnp.tile(cos, (1, rep)), jnp.tile(sin, (1, rep))


def _tile(n, cap):
    t = min(n, cap)
    assert n % t == 0, (n, t)
    return t


class _Plan(NamedTuple):
    rows: int
    rows_tab: int
    rows_wide: int
    rows_merge: int
    cols: int
    ffn_slab: int
    att: int
    chunk: int
    ret_rows: int


def _plan(S, M):
    return _Plan(rows=_tile(S, 4 * MXU_DIM), rows_tab=_tile(S, 8 * MXU_DIM), rows_wide=_tile(M, 8 * MXU_DIM),
                 rows_merge=_tile(M, 4 * MXU_DIM), cols=4 * MXU_DIM, ffn_slab=MXU_DIM, att=_tile(S, MXU_DIM),
                 chunk=_tile(S, 4 * RET_CHUNK), ret_rows=_tile(S, 4 * RET_CHUNK))


def kernel(x, meta_tokens, ffn1_norm, ffn1_w_gate, ffn1_w_up, ffn1_w_down, mix_norm, w_in, w_ret_out,
           w_att_out, w_mix_out, ffn2_norm, ffn2_w_gate, ffn2_w_up, ffn2_w_down, final_norm):
    B, S, D = x.shape
    assert ffn1_norm.shape[0] == 1, "single layer only"
    assert meta_tokens.shape[0] == N_META
    k_top = min(TOPK_MAX, S // 4)
    M = B * S
    plan = _plan(S, M)
    bf = lambda a: a.astype(BF16)
    row = lambda a: a.reshape(1, -1)

    w32 = w_in[0].T
    offs = np.cumsum([0, RET_QK, RET_QK, RET_V, RET_V, ATT_W, ATT_W, ATT_W, IDX_Q, IDX_DH, IDX_HEADS, D, D])
    tn = plan.cols
    assert all(int(o) % tn == 0 for o in offs[:8])
    c_rq, c_rk, c_rv, c_rg, c_aq, c_ak, c_av = (int(o) // tn for o in offs[:7])
    w_idx = jnp.pad(bf(w32[offs[7]:offs[10]].T), ((0, 0), (0, LANES - IDX_DH - IDX_HEADS)))
    w_gate = w32[offs[10]:offs[12]]

    pos_meta = jnp.arange(N_META, dtype=jnp.int32)
    pos_real = N_META + jnp.arange(S, dtype=jnp.int32)
    ffn1 = (row(ffn1_norm[0]), bf(ffn1_w_gate[0]), bf(ffn1_w_up[0]), bf(ffn1_w_down[0]), row(mix_norm[0]))
    ffn2 = (row(ffn2_norm[0]), bf(ffn2_w_gate[0]), bf(ffn2_w_up[0]), bf(ffn2_w_down[0]), row(final_norm))
    tf = plan.ffn_slab
    k_scale = RET_DK ** -0.5

    _, un_m = _ffn(meta_tokens.astype(F32), *ffn1, tm=N_META, tf=tf, emit_h=True, n_dtype=BF16)
    pm = dict(tm=N_META, tn=tn)
    rk_m = _proj(un_m, w32, col0=c_rk, n_tiles=RET_QK // tn, mode="rope", d=RET_DK, k_scale=k_scale,
                 tabs=_rope_tables(pos_meta, RET_DK, RET_DK), **pm)
    rv_m = _proj(un_m, w32, col0=c_rv, n_tiles=RET_V // tn, mode="plain", **pm)
    ak_m = _proj(un_m, w32, col0=c_ak, n_tiles=ATT_W // tn, mode="rope", d=ATT_DH,
                 tabs=_rope_tables(pos_meta, ATT_DH, LANES), **pm)
    av_m = _proj(un_m, w32, col0=c_av, n_tiles=ATT_W // tn, mode="plain", **pm)

    tm = plan.rows
    h1, un = _ffn(x.reshape(M, D), *ffn1, tm=tm, tf=tf, emit_h=True, n_dtype=BF16)
    pr = dict(tm=plan.rows_tab, tn=tn)
    pw = dict(tm=plan.rows_wide, tn=tn)
    rqk = _proj(un, w32, col0=c_rq, n_tiles=2 * RET_QK // tn, mode="rope", d=RET_DK, k_scale=k_scale,
                n_q=RET_QK // tn, tabs=_rope_tables(pos_real, RET_DK, RET_DK), **pr)
    rv = _proj(un, w32, col0=c_rv, n_tiles=RET_V // tn, mode="plain", **pw)
    rg = _proj(un, w32, col0=c_rg, n_tiles=RET_V // tn, mode="silu", **pw)
    aqk = _proj(un, w32, col0=c_aq, n_tiles=2 * ATT_W // tn, mode="rope", d=ATT_DH, q_scale=ATT_EXP_SCALE,
                n_q=ATT_W // tn, tabs=_rope_tables(pos_real, ATT_DH, LANES), **pr)
    av = _proj(un, w32, col0=c_av, n_tiles=ATT_W // tn, mode="plain", **pw)
    gates = _proj(un, w_gate, col0=0, n_tiles=2 * D // tn, mode="plain", **pw)
    iq, ikw = _idx_proj(un, w_idx, _rope_tables(pos_real, IDX_DH, LANES), tm=tm)

    heads = jnp.arange(RET_HEADS, dtype=F32)
    log_g = jnp.log1p(-(2.0 ** (-5.0 - heads)))
    C = plan.chunk
    b3 = lambda a: a.reshape(B, S, a.shape[-1])
    yr = _retention(log_g, jnp.exp(log_g * C), b3(rqk), b3(rv), b3(rg), rk_m, rv_m, C=C, R=plan.ret_rows)
    ya = _dsa(b3(iq), b3(ikw), b3(aqk), b3(av), ak_m, av_m, tq=plan.att, tk=plan.att, k_top=k_top)

    h2 = _merge(h1, yr.reshape(M, RET_V), ya.reshape(M, ATT_W), gates, bf(w_ret_out[0]), bf(w_att_out[0]),
                bf(w_mix_out[0]), tm=plan.rows_merge)
    (y,) = _ffn(h2, *ffn2, tm=tm, tf=tf, emit_h=False, n_dtype=F32)
    return y.reshape(B, S, D)
```

```python
import functools
from typing import NamedTuple

import jax
import jax.numpy as jnp
import numpy as np
from jax import lax
from jax.experimental import pallas as pl
from jax.experimental.pallas import tpu as pltpu

N_META = 16
RET_HEADS = 4
RET_DK = 256
RET_DV = 512
RET_CHUNK = 128
ATT_HEADS = 8
ATT_DH = 128
IDX_HEADS = 8
IDX_DH = 64
TOPK_MAX = 256
ROPE_THETA = 10000.0
EPS = 1e-6
RET_QK = RET_HEADS * RET_DK
RET_V = RET_HEADS * RET_DV
ATT_W = ATT_HEADS * ATT_DH
IDX_Q = IDX_HEADS * IDX_DH
ATT_EXP_SCALE = (ATT_DH ** -0.5) * float(np.log2(np.e))

LANES = 128
SUBLANES = 8
MXU_DIM = 256
PROJ_SUB_ROWS = MXU_DIM
VMEM_LIMIT = 48 * 1024 * 1024
FFN_VMEM_LIMIT = 56 * 1024 * 1024

BF16 = jnp.bfloat16
F32 = jnp.float32
NT_DIMS = (((1,), (1,)), ((), ()))


def _dot(a, b):
    return jnp.dot(a, b, preferred_element_type=F32)


def _dot_nt(a, b):
    return lax.dot_general(a, b, NT_DIMS, preferred_element_type=F32)


def _rms(x, g):
    return x * lax.rsqrt(jnp.mean(x * x, axis=-1, keepdims=True) + EPS) * g


def _sigmoid(x):
    return 0.5 * jnp.tanh(0.5 * x) + 0.5


def _params(sem):
    return pltpu.CompilerParams(dimension_semantics=sem, vmem_limit_bytes=VMEM_LIMIT)


def _ffn_kernel(x_ref, g_ref, wg_ref, wu_ref, wd_ref, g2_ref, *rest, nf, tf, emit_h):
    if emit_h:
        h_ref, n_ref, xn_sc, acc_sc, act_sc = rest
    else:
        n_ref, xn_sc, acc_sc, act_sc = rest

    def down(f):
        rows = pl.ds(pl.multiple_of(f * tf, tf), tf)
        return _dot(act_sc[...], wd_ref[rows, :])

    def up(f):
        cols = pl.ds(pl.multiple_of(f * tf, tf), tf)
        xn = xn_sc[...]
        gate = _dot(xn, wg_ref[:, cols])
        lift = _dot(xn, wu_ref[:, cols])
        act_sc[...] = ((gate * _sigmoid(gate)) * lift).astype(BF16)

    xn_sc[...] = _rms(x_ref[...], g_ref[...]).astype(BF16)
    acc_sc[...] = jnp.zeros_like(acc_sc)
    up(0)

    def slab(f, carry):
        acc_sc[...] += down(f - 1)
        up(f)
        return carry

    lax.fori_loop(1, nf, slab, 0, unroll=True)
    h = x_ref[...] + 0.5 * (acc_sc[...] + down(nf - 1))
    if emit_h:
        h_ref[...] = h
    n_ref[...] = _rms(h, g2_ref[...]).astype(n_ref.dtype)


def _ffn(x, g, wg, wu, wd, g2, *, tm, tf, emit_h, n_dtype):
    M, D = x.shape
    F = wg.shape[1]
    nf = F // tf
    assert nf >= 2
    out_shape = [jax.ShapeDtypeStruct((M, D), n_dtype)]
    out_specs = [pl.BlockSpec((tm, D), lambda i: (i, 0))]
    if emit_h:
        out_shape = [jax.ShapeDtypeStruct((M, D), F32)] + out_shape
        out_specs = [pl.BlockSpec((tm, D), lambda i: (i, 0))] + out_specs
    resident = lambda shape: pl.BlockSpec(shape, lambda i: (0, 0), pipeline_mode=pl.Buffered(1))
    return pl.pallas_call(
        functools.partial(_ffn_kernel, nf=nf, tf=tf, emit_h=emit_h),
        grid=(M // tm,),
        in_specs=[
            pl.BlockSpec((tm, D), lambda i: (i, 0)),
            pl.BlockSpec((1, D), lambda i: (0, 0)),
            resident((D, F)),
            resident((D, F)),
            resident((F, D)),
            pl.BlockSpec((1, D), lambda i: (0, 0)),
        ],
        out_specs=out_specs,
        out_shape=out_shape,
        scratch_shapes=[pltpu.VMEM((tm, D), BF16), pltpu.VMEM((tm, D), F32), pltpu.VMEM((tm, tf), BF16)],
        compiler_params=pltpu.CompilerParams(
            dimension_semantics=("arbitrary",), vmem_limit_bytes=FFN_VMEM_LIMIT),
        name="ffn",
    )(x, g, wg, wu, wd, g2)


def _rot_half(y, d):
    w = y.shape[-1]
    if d >= w:
        return jnp.concatenate([y[:, w // 2:], y[:, : w // 2]], axis=-1)
    if d == LANES:
        return pltpu.roll(y, d // 2, axis=1)
    fwd = pltpu.roll(y, d // 2, axis=1)
    bwd = pltpu.roll(y, w - d // 2, axis=1)
    lane = lax.broadcasted_iota(jnp.int32, y.shape, 1)
    return jnp.where((lane % d) < d // 2, bwd, fwd)


def _rope_tile(y, cos, sin, d):
    tw = cos.shape[-1]
    outs = []
    for t in range(y.shape[-1] // tw):
        yt = y[:, t * tw:(t + 1) * tw]
        outs.append(yt * cos + _rot_half(yt, d) * sin)
    return outs[0] if len(outs) == 1 else jnp.concatenate(outs, axis=-1)


def _proj_kernel(x_ref, w_ref, *rest, mode, d, q_scale, k_scale, n_q):
    if mode == "rope":
        cos_ref, sin_ref, o_ref, w_sc = rest
    else:
        o_ref, w_sc = rest

    @pl.when(pl.program_id(1) == 0)
    def _():
        w_sc[...] = w_ref[...].T.astype(BF16)

    tm = x_ref.shape[0]
    sub = min(tm, PROJ_SUB_ROWS) if mode == "rope" else tm
    for r0 in range(0, tm, sub):
        rows = slice(r0, r0 + sub)
        y = _dot(x_ref[rows, :], w_sc[...])
        if mode == "rope":
            y = _rope_tile(y, cos_ref[rows, :], sin_ref[rows, :], d)
            if q_scale != 1.0 or k_scale != 1.0:
                y = y * jnp.where(pl.program_id(0) >= n_q, k_scale, q_scale)
        elif mode == "silu":
            y = y * _sigmoid(y)
        o_ref[rows, :] = y.astype(o_ref.dtype)


def _proj(x, w_t, *, col0, n_tiles, tm, tn, mode, out_dtype=None, tabs=None, d=None, q_scale=1.0,
          k_scale=1.0, n_q=0):
    M, K = x.shape
    in_specs = [
        pl.BlockSpec((tm, K), lambda j, i: (i, 0)),
        pl.BlockSpec((tn, K), lambda j, i: (col0 + j, 0)),
    ]
    args = [x, w_t]
    if mode == "rope":
        cos, sin = tabs
        npos = cos.shape[0] // tm
        tw = cos.shape[1]
        in_specs += [pl.BlockSpec((tm, tw), lambda j, i: (i % npos, 0))] * 2
        args += [cos, sin]
    return pl.pallas_call(
        functools.partial(_proj_kernel, mode=mode, d=d, q_scale=q_scale, k_scale=k_scale, n_q=n_q),
        grid=(n_tiles, M // tm),
        in_specs=in_specs,
        out_specs=pl.BlockSpec((tm, tn), lambda j, i: (i, j)),
        out_shape=jax.ShapeDtypeStruct((M, n_tiles * tn), out_dtype or BF16),
        scratch_shapes=[pltpu.VMEM((K, tn), BF16)],
        compiler_params=_params(("arbitrary", "arbitrary")),
        name="proj_" + mode,
    )(*args)


def _idx_proj_kernel(x_ref, w_ref, cos_ref, sin_ref, q_ref, kw_ref):
    tm = x_ref.shape[0]
    sub = min(tm, PROJ_SUB_ROWS)
    for r0 in range(0, tm, sub):
        rows = slice(r0, r0 + sub)
        y = _dot(x_ref[rows, :], w_ref[...])
        cos, sin = cos_ref[rows, :], sin_ref[rows, :]
        q_ref[rows, :] = _rope_tile(y[:, :IDX_Q], cos, sin, IDX_DH)
        tail = y[:, IDX_Q:]
        lane = lax.broadcasted_iota(jnp.int32, tail.shape, 1)
        kw_ref[rows, :] = jnp.where(lane < IDX_DH, _rope_tile(tail, cos, sin, IDX_DH), tail)


def _idx_proj(x, w, tabs, *, tm):
    M, K = x.shape
    N = w.shape[1]
    cos, sin = tabs
    npos = cos.shape[0] // tm
    return pl.pallas_call(
        _idx_proj_kernel,
        grid=(M // tm,),
        in_specs=[
            pl.BlockSpec((tm, K), lambda i: (i, 0)),
            pl.BlockSpec((K, N), lambda i: (0, 0)),
            pl.BlockSpec((tm, LANES), lambda i: (i % npos, 0)),
            pl.BlockSpec((tm, LANES), lambda i: (i % npos, 0)),
        ],
        out_specs=[
            pl.BlockSpec((tm, IDX_Q), lambda i: (i, 0)),
            pl.BlockSpec((tm, LANES), lambda i: (i, 0)),
        ],
        out_shape=[
            jax.ShapeDtypeStruct((M, IDX_Q), F32),
            jax.ShapeDtypeStruct((M, LANES), F32),
        ],
        compiler_params=_params(("parallel",)),
        name="proj_idx",
    )(x, w, cos, sin)


def _ret_kernel(lg_ref, gc_ref, q_ref, k_ref, v_ref, g_ref, km_ref, vm_ref, o_ref, st_ref, st0_ref, decay_ref, *, C):
    c = pl.program_id(1)
    dk = lambda h: slice(h * RET_DK, (h + 1) * RET_DK)
    dv = lambda h: slice(h * RET_DV, (h + 1) * RET_DV)

    @pl.when((pl.program_id(0) == 0) & (c == 0))
    def _():
        pm = lax.broadcasted_iota(jnp.int32, (N_META, 1), 0).astype(F32)
        ri = lax.broadcasted_iota(jnp.int32, (C, C), 0)
        ci = lax.broadcasted_iota(jnp.int32, (C, C), 1)
        diff = (ri - ci).astype(F32)
        for h in range(RET_HEADS):
            lg = lg_ref[h]
            zeta_m = jnp.exp(lg * (N_META - 1.0 - pm))
            kz = km_ref[:, dk(h)].astype(F32) * zeta_m
            st0_ref[h] = _dot(kz.T.astype(BF16), vm_ref[:, dv(h)])
            decay_ref[h] = jnp.where(diff >= 0, jnp.exp(lg * jnp.maximum(diff, 0.0)), 0.0)

    @pl.when(c == 0)
    def _():
        st_ref[...] = st0_ref[...]

    pos = lax.broadcasted_iota(jnp.int32, (C, 1), 0).astype(F32)
    for h in range(RET_HEADS):
        lg = lg_ref[h]
        xi = jnp.exp(lg * (pos + 1.0))
        zeta = jnp.exp(lg * (C - 1.0 - pos))
        state = st_ref[h]
        for r0 in range(0, q_ref.shape[1], C):
            rows = slice(r0, r0 + C)
            q = q_ref[0, rows, dk(h)]
            k = k_ref[0, rows, dk(h)]
            v = v_ref[0, rows, dv(h)]
            inner = _dot((_dot_nt(q, k) * decay_ref[h]).astype(BF16), v)
            cross = _dot(q, state.astype(BF16)) * xi
            y = inner + cross
            y = y * lax.rsqrt(jnp.mean(y * y, axis=-1, keepdims=True) + EPS)
            o_ref[0, rows, dv(h)] = (g_ref[0, rows, dv(h)] * y).astype(o_ref.dtype)
            kz = k.astype(F32) * zeta
            state = gc_ref[h] * state + _dot(kz.T.astype(BF16), v)
        st_ref[h] = state


def _retention(lg, gc, rqk, rv, rg, km, vm, *, C, R):
    B, S, _ = rv.shape
    smem = pl.BlockSpec(memory_space=pltpu.SMEM)
    return pl.pallas_call(
        functools.partial(_ret_kernel, C=C),
        grid=(B, S // R),
        in_specs=[
            smem,
            smem,
            pl.BlockSpec((1, R, RET_QK), lambda b, c: (b, c, 0)),
            pl.BlockSpec((1, R, RET_QK), lambda b, c: (b, c, 1)),
            pl.BlockSpec((1, R, RET_V), lambda b, c: (b, c, 0)),
            pl.BlockSpec((1, R, RET_V), lambda b, c: (b, c, 0)),
            pl.BlockSpec((N_META, RET_QK), lambda b, c: (0, 0)),
            pl.BlockSpec((N_META, RET_V), lambda b, c: (0, 0)),
        ],
        out_specs=pl.BlockSpec((1, R, RET_V), lambda b, c: (b, c, 0)),
        out_shape=jax.ShapeDtypeStruct((B, S, RET_V), BF16),
        scratch_shapes=[pltpu.VMEM((RET_HEADS, RET_DK, RET_DV), F32), pltpu.VMEM((RET_HEADS, RET_DK, RET_DV), F32),
                        pltpu.VMEM((RET_HEADS, C, C), F32)],
        compiler_params=_params(("arbitrary", "arbitrary")),
        name="retention",
    )(lg, gc, rqk, rqk, rv, rg, km, vm)


TINY = float(np.finfo(np.float32).tiny)
TN_DIMS = (((0,), (0,)), ((), ()))


def _dsa_kernel(iq_ref, ikw_ref, wq_ref, aq_ref, ak_ref, av_ref, akm_ref, avm_ref, o_ref,
                sc_ref, tau_ref, m_ref, l_ref, acc_ref, s_ref, bm_ref, *, nq, tq, tk, k_top):
    qi = pl.program_id(1)
    nkb = (qi + 1) * (tq // tk)
    kf = float(k_top)
    neg_inf = -jnp.inf
    qpos = qi * tq + lax.broadcasted_iota(jnp.int32, (1, tq), 1)

    part_shape = (SUBLANES, tq)

    def fold(x):
        return x.reshape(tk // SUBLANES, SUBLANES, tq).sum(axis=0)

    idx_scale = (IDX_DH ** -0.5) * (IDX_HEADS ** -0.5)
    iq = iq_ref[0].astype(BF16)
    w_t = wq_ref[0].T
    iq_h = [iq[:, h * IDX_DH:(h + 1) * IDX_DH] for h in range(IDX_HEADS)]
    w_h = [w_t[IDX_DH + h:IDX_DH + h + 1, :] for h in range(IDX_HEADS)]

    def score_rows(kb, n_blocks, carry):
        rmin, rmax = carry
        rows = n_blocks * tk
        ik = ikw_ref[0, pl.ds(pl.multiple_of(kb * tk, tk), rows), :][:, :IDX_DH].astype(BF16)
        acc = jnp.zeros((rows, tq), F32)
        for h in range(IDX_HEADS):
            acc = acc + w_h[h] * jnp.maximum(_dot_nt(ik, iq_h[h]), 0.0)
        ok = (lax.broadcasted_iota(jnp.int32, (rows, 1), 0) + kb * tk) <= qpos
        s = jnp.where(ok, acc * idx_scale, neg_inf)
        for i in range(n_blocks):
            sc_ref[kb + i] = s[i * tk:(i + 1) * tk]
        rmin = jnp.minimum(rmin, jnp.min(jnp.where(ok, s, jnp.inf), axis=0, keepdims=True))
        rmax = jnp.maximum(rmax, jnp.max(s, axis=0, keepdims=True))
        return rmin, rmax

    carry = (jnp.full((1, tq), jnp.inf, F32), jnp.full((1, tq), neg_inf, F32))
    carry = lax.fori_loop(0, nkb // 4, lambda j, c: score_rows(4 * j, 4, c), carry)
    carry = lax.cond(nkb % 4 >= 2, lambda c: score_rows((nkb // 4) * 4, 2, c), lambda c: c, carry)
    rmin, rmax = lax.cond(nkb % 2 == 1, lambda c: score_rows(nkb - 1, 1, c), lambda c: c, carry)

    def any_query(flag):
        return jnp.max(jnp.where(flag, 1.0, 0.0))

    def midpoint(lo, hi):
        a = jnp.maximum(jnp.abs(lo), TINY)
        b = jnp.maximum(jnp.abs(hi), TINY)
        geo = jnp.sqrt(a) * jnp.sqrt(b)
        one_sign = (lo >= 0.0) | (hi <= 0.0)
        far = jnp.maximum(a, b) > 4.0 * jnp.minimum(a, b)
        mid = jnp.where(one_sign & far, jnp.where(lo >= 0.0, geo, -geo), 0.5 * lo + 0.5 * hi)
        mid = jnp.where(lo == 0.0, TINY, mid)
        mid = jnp.where(hi == 0.0, -TINY, mid)
        return jnp.where((lo < 0.0) & (hi > 0.0), 0.0, mid)

    def active(lo, hi, cl):
        mid = midpoint(lo, hi)
        return (cl > kf) & (mid > lo) & (mid < hi)

    def go_on(st):
        return st[3] > 0.0

    def threshold(n_blocks, rmin, rmax):
        blocks = range(n_blocks)

        def over_keys(fn, init, combine):
            acc = init
            for kb in blocks:
                acc = combine(acc, fn(sc_ref[kb]))
            return acc

        def count(pred):
            c = over_keys(lambda s: fold(jnp.where(pred(s), 1.0, 0.0)), jnp.zeros(part_shape, F32), jnp.add)
            return jnp.sum(c, axis=0, keepdims=True)

        c_max = count(lambda s: s >= rmax)
        top_tied = c_max >= kf
        lo = jnp.where(top_tied, rmax, rmin)
        hi = rmax
        cl = jnp.where(top_tied, c_max, (qpos + 1).astype(F32))

        def shrink_once(lo, hi, cl):
            mid = midpoint(lo, hi)
            act = active(lo, hi, cl)
            c = count(lambda s: s >= mid)
            up = act & (c >= kf)
            dn = act & (c < kf)
            return jnp.where(up, mid, lo), jnp.where(dn, mid, hi), jnp.where(up, c, cl)

        def shrink(st):
            lo, hi, cl = shrink_once(*st[:3])
            pending = any_query(active(lo, hi, cl))
            return (*shrink_once(lo, hi, cl), pending)

        lo, hi, cl, _ = lax.while_loop(go_on, shrink, (lo, hi, cl, jnp.float32(1.0)))
        tau_ref[...] = lo
        tie = cl > kf

        @pl.when(any_query(tie) > 0.0)
        def _():
            def max_below(bound):
                part = over_keys(
                    lambda s: jnp.max(jnp.where(s < bound, s, neg_inf).reshape(tk // SUBLANES, SUBLANES, tq), axis=0),
                    jnp.full(part_shape, neg_inf, F32), jnp.maximum)
                return jnp.max(part, axis=0, keepdims=True)

            hi_d = jnp.where(lo >= hi, jnp.inf, hi)
            v0 = max_below(hi_d)
            ge0 = count(lambda s: s >= v0)

            def walk_on(st):
                return st[3] > 0.0

            def walk(st):
                hi_w, v, ge, _ = st
                hi_w = jnp.where(tie & (ge < kf), v, hi_w)
                v = max_below(hi_w)
                ge = count(lambda s: s >= v)
                return hi_w, v, ge, any_query(tie & (ge < kf))

            _, v, _, _ = lax.while_loop(walk_on, walk, (hi_d, v0, ge0, any_query(tie & (ge0 < kf))))
            keep = kf - count(lambda s: s > v)
            ri = lax.broadcasted_iota(jnp.int32, (tk, tk), 0)
            ci = lax.broadcasted_iota(jnp.int32, (tk, tk), 1)
            earlier = jnp.where(ci < ri, 1.0, 0.0).astype(BF16)
            seen = jnp.zeros((1, tq), F32)
            for kb in blocks:
                s = sc_ref[kb]
                eq = tie & (s == v)
                eqf = jnp.where(eq, 1.0, 0.0)
                rank = seen + _dot(earlier, eqf.astype(BF16))
                sc_ref[kb] = jnp.where(eq & (rank >= keep), neg_inf, s)
                seen = seen + jnp.sum(eqf, axis=0, keepdims=True)
            tau_ref[...] = jnp.where(tie, v, lo)

        return jnp.int32(0)

    branches = [functools.partial(threshold, (i + 1) * (tq // tk)) for i in range(nq)]
    lax.switch(qi, branches, rmin, rmax)

    tau = tau_ref[...]

    def head(ref, h, rows=None):
        if rows is None:
            return ref[:, h * ATT_DH:(h + 1) * ATT_DH]
        return ref[0, rows, h * ATT_DH:(h + 1) * ATT_DH]

    def score_phase(kb, buf):
        rows = pl.ds(pl.multiple_of(kb * tk, tk), tk)
        bias = jnp.where(sc_ref[kb] >= tau, 0.0, neg_inf)
        for h in range(ATT_HEADS):
            s = _dot_nt(head(ak_ref, h, rows), head(aq_ref, h, slice(None))) + bias
            s_ref[buf, h] = s
            bm_ref[buf, h:h + 1, :] = jnp.max(s, axis=0, keepdims=True)

    def value_phase(kb, buf):
        rows = pl.ds(pl.multiple_of(kb * tk, tk), tk)
        for h in range(ATT_HEADS):
            m_new = jnp.maximum(m_ref[h:h + 1, :], bm_ref[buf, h:h + 1, :])
            halves = []
            p_sum = jnp.zeros((1, tq), F32)
            for r0 in range(0, tk, tk // 2):
                p = jnp.exp2(s_ref[buf, h, r0:r0 + tk // 2, :] - m_new)
                p_sum = p_sum + jnp.sum(p, axis=0, keepdims=True)
                halves.append(p.astype(BF16))
            p_bf = jnp.concatenate(halves, axis=0)
            pv = lax.dot_general(head(av_ref, h, rows), p_bf, TN_DIMS, preferred_element_type=F32)
            alpha = jnp.exp2(m_ref[h:h + 1, :] - m_new)
            l_ref[h:h + 1, :] = alpha * l_ref[h:h + 1, :] + p_sum
            acc_ref[h] = alpha * acc_ref[h] + pv
            m_ref[h:h + 1, :] = m_new

    for h in range(ATT_HEADS):
        s = _dot_nt(head(akm_ref, h), head(aq_ref, h, slice(None)))
        m = jnp.max(s, axis=0, keepdims=True)
        p = jnp.exp2(s - m)
        m_ref[h:h + 1, :] = m
        l_ref[h:h + 1, :] = jnp.sum(p, axis=0, keepdims=True)
        acc_ref[h] = lax.dot_general(head(avm_ref, h), p.astype(BF16), TN_DIMS, preferred_element_type=F32)
    score_phase(0, 1)

    def att_blocks(kb, n):
        for i in range(n):
            buf = 1 - i % 2
            value_phase(kb + i, buf)
            score_phase(jnp.minimum(kb + i + 1, nkb - 1), 1 - buf)

    def att_quad(j, carry):
        att_blocks(4 * j, 4)
        return carry

    lax.fori_loop(0, nkb // 4, att_quad, 0)

    @pl.when(nkb % 4 >= 2)
    def _():
        att_blocks((nkb // 4) * 4, 2)

    @pl.when(nkb % 2 == 1)
    def _():
        value_phase(nkb - 1, 1)

    for h in range(ATT_HEADS):
        out_t = acc_ref[h] / l_ref[h:h + 1, :]
        o_ref[0, :, h * ATT_DH:(h + 1) * ATT_DH] = out_t.T.astype(o_ref.dtype)


def _dsa(iq, ikw, aqk, av, akm, avm, *, tq, tk, k_top):
    B, S, _ = av.shape
    nq = S // tq
    assert tq % tk == 0
    return pl.pallas_call(
        functools.partial(_dsa_kernel, nq=nq, tq=tq, tk=tk, k_top=k_top),
        grid=(B, nq),
        in_specs=[
            pl.BlockSpec((1, tq, IDX_Q), lambda b, q: (b, q, 0)),
            pl.BlockSpec((1, S, LANES), lambda b, q: (b, 0, 0)),
            pl.BlockSpec((1, tq, LANES), lambda b, q: (b, q, 0)),
            pl.BlockSpec((1, tq, ATT_W), lambda b, q: (b, q, 0)),
            pl.BlockSpec((1, S, ATT_W), lambda b, q: (b, 0, 1)),
            pl.BlockSpec((1, S, ATT_W), lambda b, q: (b, 0, 0)),
            pl.BlockSpec((N_META, ATT_W), lambda b, q: (0, 0)),
            pl.BlockSpec((N_META, ATT_W), lambda b, q: (0, 0)),
        ],
        out_specs=pl.BlockSpec((1, tq, ATT_W), lambda b, q: (b, q, 0)),
        out_shape=jax.ShapeDtypeStruct((B, S, ATT_W), BF16),
        scratch_shapes=[
            pltpu.VMEM((S // tk, tk, tq), F32),
            pltpu.VMEM((1, tq), F32),
            pltpu.VMEM((ATT_HEADS, tq), F32),
            pltpu.VMEM((ATT_HEADS, tq), F32),
            pltpu.VMEM((ATT_HEADS, ATT_DH, tq), F32),
            pltpu.VMEM((2, ATT_HEADS, tk, tq), F32),
            pltpu.VMEM((2, ATT_HEADS, tq), F32),
        ],
        compiler_params=_params(("parallel", "arbitrary")),
        name="dsa",
    )(iq, ikw, ikw, aqk, aqk, av, akm, avm)


def _merge_kernel(h_ref, yr_ref, ya_ref, gr_ref, ga_ref, wr_ref, wa_ref, wm_ref, o_ref):
    yr = _dot(yr_ref[...], wr_ref[...])
    ya = _dot(ya_ref[...], wa_ref[...])
    merged = _sigmoid(gr_ref[...].astype(F32)) * yr + _sigmoid(ga_ref[...].astype(F32)) * ya
    o_ref[...] = h_ref[...] + _dot(merged.astype(BF16), wm_ref[...])


def _merge(h, yr, ya, gates, wr, wa, wm, *, tm):
    M, D = h.shape
    row = lambda i: (i, 0)
    fixed = lambda i: (0, 0)
    return pl.pallas_call(
        _merge_kernel,
        grid=(M // tm,),
        in_specs=[
            pl.BlockSpec((tm, D), row),
            pl.BlockSpec((tm, RET_V), row),
            pl.BlockSpec((tm, ATT_W), row),
            pl.BlockSpec((tm, D), lambda i: (i, 0)),
            pl.BlockSpec((tm, D), lambda i: (i, 1)),
            pl.BlockSpec((RET_V, D), fixed),
            pl.BlockSpec((ATT_W, D), fixed),
            pl.BlockSpec((D, D), fixed),
        ],
        out_specs=pl.BlockSpec((tm, D), row),
        out_shape=jax.ShapeDtypeStruct((M, D), F32),
        compiler_params=_params(("parallel",)),
        name="merge",
    )(h, yr, ya, gates, gates, wr, wa, wm)


def _rope_tables(pos, d, width):
    inv_freq = ROPE_THETA ** (-jnp.arange(0, d, 2, dtype=F32) / d)
    ang = pos.astype(F32)[:, None] * inv_freq[None, :]
    cos, sin = jnp.cos(ang), jnp.sin(ang)
    cos = jnp.concatenate([cos, cos], axis=-1)
    sin = jnp.concatenate([-sin, sin], axis=-1)
    rep = width // d
    return jnp.tile(cos, (1, rep)), jnp.tile(sin, (1, rep))


def _tile(n, cap):
    t = min(n, cap)
    assert n % t == 0, (n, t)
    return t


class _Plan(NamedTuple):
    rows: int
    rows_tab: int
    rows_wide: int
    rows_merge: int
    cols: int
    ffn_slab: int
    att: int
    chunk: int
    ret_rows: int


def _plan(S, M):
    return _Plan(rows=_tile(S, 4 * MXU_DIM), rows_tab=_tile(S, 8 * MXU_DIM), rows_wide=_tile(M, 8 * MXU_DIM),
                 rows_merge=_tile(M, 2 * MXU_DIM), cols=4 * MXU_DIM, ffn_slab=MXU_DIM, att=_tile(S, MXU_DIM),
                 chunk=_tile(S, 4 * RET_CHUNK), ret_rows=_tile(S, 4 * RET_CHUNK))


def kernel(x, meta_tokens, ffn1_norm, ffn1_w_gate, ffn1_w_up, ffn1_w_down, mix_norm, w_in, w_ret_out,
           w_att_out, w_mix_out, ffn2_norm, ffn2_w_gate, ffn2_w_up, ffn2_w_down, final_norm):
    B, S, D = x.shape
    assert ffn1_norm.shape[0] == 1, "single layer only"
    assert meta_tokens.shape[0] == N_META
    k_top = min(TOPK_MAX, S // 4)
    M = B * S
    plan = _plan(S, M)
    bf = lambda a: a.astype(BF16)
    row = lambda a: a.reshape(1, -1)

    w32 = w_in[0].T
    offs = np.cumsum([0, RET_QK, RET_QK, RET_V, RET_V, ATT_W, ATT_W, ATT_W, IDX_Q, IDX_DH, IDX_HEADS, D, D])
    tn = plan.cols
    assert all(int(o) % tn == 0 for o in offs[:8])
    c_rq, c_rk, c_rv, c_rg, c_aq, c_ak, c_av = (int(o) // tn for o in offs[:7])
    w_idx = jnp.pad(bf(w32[offs[7]:offs[10]].T), ((0, 0), (0, LANES - IDX_DH - IDX_HEADS)))
    w_gate = w32[offs[10]:offs[12]]

    pos_meta = jnp.arange(N_META, dtype=jnp.int32)
    pos_real = N_META + jnp.arange(S, dtype=jnp.int32)
    ffn1 = (row(ffn1_norm[0]), bf(ffn1_w_gate[0]), bf(ffn1_w_up[0]), bf(ffn1_w_down[0]), row(mix_norm[0]))
    ffn2 = (row(ffn2_norm[0]), bf(ffn2_w_gate[0]), bf(ffn2_w_up[0]), bf(ffn2_w_down[0]), row(final_norm))
    tf = plan.ffn_slab
    k_scale = RET_DK ** -0.5

    _, un_m = _ffn(meta_tokens.astype(F32), *ffn1, tm=N_META, tf=tf, emit_h=True, n_dtype=BF16)
    pm = dict(tm=N_META, tn=tn)
    rk_m = _proj(un_m, w32, col0=c_rk, n_tiles=RET_QK // tn, mode="rope", d=RET_DK, k_scale=k_scale,
                 tabs=_rope_tables(pos_meta, RET_DK, RET_DK), **pm)
    rv_m = _proj(un_m, w32, col0=c_rv, n_tiles=RET_V // tn, mode="plain", **pm)
    ak_m = _proj(un_m, w32, col0=c_ak, n_tiles=ATT_W // tn, mode="rope", d=ATT_DH,
                 tabs=_rope_tables(pos_meta, ATT_DH, LANES), **pm)
    av_m = _proj(un_m, w32, col0=c_av, n_tiles=ATT_W // tn, mode="plain", **pm)

    tm = plan.rows
    h1, un = _ffn(x.reshape(M, D), *ffn1, tm=tm, tf=tf, emit_h=True, n_dtype=BF16)
    pr = dict(tm=plan.rows_tab, tn=tn)
    pw = dict(tm=plan.rows_wide, tn=tn)
    rqk = _proj(un, w32, col0=c_rq, n_tiles=2 * RET_QK // tn, mode="rope", d=RET_DK, k_scale=k_scale,
                n_q=RET_QK // tn, tabs=_rope_tables(pos_real, RET_DK, RET_DK), **pr)
    rv = _proj(un, w32, col0=c_rv, n_tiles=RET_V // tn, mode="plain", **pw)
    rg = _proj(un, w32, col0=c_rg, n_tiles=RET_V // tn, mode="silu", **pw)
    aqk = _proj(un, w32, col0=c_aq, n_tiles=2 * ATT_W // tn, mode="rope", d=ATT_DH, q_scale=ATT_EXP_SCALE,
                n_q=ATT_W // tn, tabs=_rope_tables(pos_real, ATT_DH, LANES), **pr)
    av = _proj(un, w32, col0=c_av, n_tiles=ATT_W // tn, mode="plain", **pw)
    gates = _proj(un, w_gate, col0=0, n_tiles=2 * D // tn, mode="plain", **pw)
    iq, ikw = _idx_proj(un, w_idx, _rope_tables(pos_real, IDX_DH, LANES), tm=tm)

    heads = jnp.arange(RET_HEADS, dtype=F32)
    log_g = jnp.log1p(-(2.0 ** (-5.0 - heads)))
    C = plan.chunk
    b3 = lambda a: a.reshape(B, S, a.shape[-1])
    yr = _retention(log_g, jnp.exp(log_g * C), b3(rqk), b3(rv), b3(rg), rk_m, rv_m, C=C, R=plan.ret_rows)
    ya = _dsa(b3(iq), b3(ikw), b3(aqk), b3(av), ak_m, av_m, tq=plan.att, tk=plan.att, k_top=k_top)

    h2 = _merge(h1, yr.reshape(M, RET_V), ya.reshape(M, ATT_W), gates, bf(w_ret_out[0]), bf(w_att_out[0]),
                bf(w_mix_out[0]), tm=plan.rows_merge)
    (y,) = _ffn(h2, *ffn2, tm=tm, tf=tf, emit_h=False, n_dtype=F32)
    return y.reshape(B, S, D)
```

```python
import functools
from typing import NamedTuple

import jax
import jax.numpy as jnp
import numpy as np
from jax import lax
from jax.experimental import pallas as pl
from jax.experimental.pallas import tpu as pltpu

N_META = 16
RET_HEADS = 4
RET_DK = 256
RET_DV = 512
RET_CHUNK = 128
ATT_HEADS = 8
ATT_DH = 128
IDX_HEADS = 8
IDX_DH = 64
TOPK_MAX = 256
ROPE_THETA = 10000.0
EPS = 1e-6
RET_QK = RET_HEADS * RET_DK
RET_V = RET_HEADS * RET_DV
ATT_W = ATT_HEADS * ATT_DH
IDX_Q = IDX_HEADS * IDX_DH
ATT_EXP_SCALE = (ATT_DH ** -0.5) * float(np.log2(np.e))

LANES = 128
SUBLANES = 8
MXU_DIM = 256
PROJ_SUB_ROWS = MXU_DIM
VMEM_LIMIT = 48 * 1024 * 1024
FFN_VMEM_LIMIT = 56 * 1024 * 1024

BF16 = jnp.bfloat16
F32 = jnp.float32
NT_DIMS = (((1,), (1,)), ((), ()))


def _dot(a, b):
    return jnp.dot(a, b, preferred_element_type=F32)


def _dot_nt(a, b):
    return lax.dot_general(a, b, NT_DIMS, preferred_element_type=F32)


def _rms(x, g):
    return x * lax.rsqrt(jnp.mean(x * x, axis=-1, keepdims=True) + EPS) * g


def _sigmoid(x):
    return 0.5 * jnp.tanh(0.5 * x) + 0.5


def _params(sem):
    return pltpu.CompilerParams(dimension_semantics=sem, vmem_limit_bytes=VMEM_LIMIT)


def _ffn_kernel(x_ref, g_ref, wg_ref, wu_ref, wd_ref, g2_ref, *rest, nf, tf, emit_h):
    if emit_h:
        h_ref, n_ref, xn_sc, acc_sc, act_sc = rest
    else:
        n_ref, xn_sc, acc_sc, act_sc = rest

    def down(f):
        rows = pl.ds(pl.multiple_of(f * tf, tf), tf)
        return _dot(act_sc[...], wd_ref[rows, :])

    def up(f):
        cols = pl.ds(pl.multiple_of(f * tf, tf), tf)
        xn = xn_sc[...]
        gate = _dot(xn, wg_ref[:, cols])
        lift = _dot(xn, wu_ref[:, cols])
        act_sc[...] = ((gate * _sigmoid(gate)) * lift).astype(BF16)

    xn_sc[...] = _rms(x_ref[...], g_ref[...]).astype(BF16)
    acc_sc[...] = jnp.zeros_like(acc_sc)
    up(0)

    def slab(f, carry):
        acc_sc[...] += down(f - 1)
        up(f)
        return carry

    lax.fori_loop(1, nf, slab, 0, unroll=True)
    h = x_ref[...] + 0.5 * (acc_sc[...] + down(nf - 1))
    if emit_h:
        h_ref[...] = h
    n_ref[...] = _rms(h, g2_ref[...]).astype(n_ref.dtype)


def _ffn(x, g, wg, wu, wd, g2, *, tm, tf, emit_h, n_dtype):
    M, D = x.shape
    F = wg.shape[1]
    nf = F // tf
    assert nf >= 2
    out_shape = [jax.ShapeDtypeStruct((M, D), n_dtype)]
    out_specs = [pl.BlockSpec((tm, D), lambda i: (i, 0))]
    if emit_h:
        out_shape = [jax.ShapeDtypeStruct((M, D), F32)] + out_shape
        out_specs = [pl.BlockSpec((tm, D), lambda i: (i, 0))] + out_specs
    resident = lambda shape: pl.BlockSpec(shape, lambda i: (0, 0), pipeline_mode=pl.Buffered(1))
    return pl.pallas_call(
        functools.partial(_ffn_kernel, nf=nf, tf=tf, emit_h=emit_h),
        grid=(M // tm,),
        in_specs=[
            pl.BlockSpec((tm, D), lambda i: (i, 0)),
            pl.BlockSpec((1, D), lambda i: (0, 0)),
            resident((D, F)),
            resident((D, F)),
            resident((F, D)),
            pl.BlockSpec((1, D), lambda i: (0, 0)),
        ],
        out_specs=out_specs,
        out_shape=out_shape,
        scratch_shapes=[pltpu.VMEM((tm, D), BF16), pltpu.VMEM((tm, D), F32), pltpu.VMEM((tm, tf), BF16)],
        compiler_params=pltpu.CompilerParams(
            dimension_semantics=("arbitrary",), vmem_limit_bytes=FFN_VMEM_LIMIT),
        name="ffn",
    )(x, g, wg, wu, wd, g2)


def _rot_half(y, d):
    w = y.shape[-1]
    if d >= w:
        return jnp.concatenate([y[:, w // 2:], y[:, : w // 2]], axis=-1)
    if d == LANES:
        return pltpu.roll(y, d // 2, axis=1)
    fwd = pltpu.roll(y, d // 2, axis=1)
    bwd = pltpu.roll(y, w - d // 2, axis=1)
    lane = lax.broadcasted_iota(jnp.int32, y.shape, 1)
    return jnp.where((lane % d) < d // 2, bwd, fwd)


def _rope_tile(y, cos, sin, d):
    tw = cos.shape[-1]
    outs = []
    for t in range(y.shape[-1] // tw):
        yt = y[:, t * tw:(t + 1) * tw]
        outs.append(yt * cos + _rot_half(yt, d) * sin)
    return outs[0] if len(outs) == 1 else jnp.concatenate(outs, axis=-1)


def _proj_kernel(x_ref, w_ref, *rest, mode, d, q_scale, k_scale, n_q):
    if mode == "rope":
        cos_ref, sin_ref, o_ref, w_sc = rest
    else:
        o_ref, w_sc = rest

    @pl.when(pl.program_id(1) == 0)
    def _():
        w_sc[...] = w_ref[...].T.astype(BF16)

    tm = x_ref.shape[0]
    sub = min(tm, PROJ_SUB_ROWS) if mode == "rope" else tm
    for r0 in range(0, tm, sub):
        rows = slice(r0, r0 + sub)
        y = _dot(x_ref[rows, :], w_sc[...])
        if mode == "rope":
            y = _rope_tile(y, cos_ref[rows, :], sin_ref[rows, :], d)
            if q_scale != 1.0 or k_scale != 1.0:
                y = y * jnp.where(pl.program_id(0) >= n_q, k_scale, q_scale)
        elif mode == "silu":
            y = y * _sigmoid(y)
        o_ref[rows, :] = y.astype(o_ref.dtype)


def _proj(x, w_t, *, col0, n_tiles, tm, tn, mode, out_dtype=None, tabs=None, d=None, q_scale=1.0,
          k_scale=1.0, n_q=0):
    M, K = x.shape
    in_specs = [
        pl.BlockSpec((tm, K), lambda j, i: (i, 0)),
        pl.BlockSpec((tn, K), lambda j, i: (col0 + j, 0)),
    ]
    args = [x, w_t]
    if mode == "rope":
        cos, sin = tabs
        npos = cos.shape[0] // tm
        tw = cos.shape[1]
        in_specs += [pl.BlockSpec((tm, tw), lambda j, i: (i % npos, 0))] * 2
        args += [cos, sin]
    return pl.pallas_call(
        functools.partial(_proj_kernel, mode=mode, d=d, q_scale=q_scale, k_scale=k_scale, n_q=n_q),
        grid=(n_tiles, M // tm),
        in_specs=in_specs,
        out_specs=pl.BlockSpec((tm, tn), lambda j, i: (i, j)),
        out_shape=jax.ShapeDtypeStruct((M, n_tiles * tn), out_dtype or BF16),
        scratch_shapes=[pltpu.VMEM((K, tn), BF16)],
        compiler_params=_params(("arbitrary", "arbitrary")),
        name="proj_" + mode,
    )(*args)


def _idx_proj_kernel(x_ref, w_ref, cos_ref, sin_ref, q_ref, kw_ref):
    tm = x_ref.shape[0]
    sub = min(tm, PROJ_SUB_ROWS)
    for r0 in range(0, tm, sub):
        rows = slice(r0, r0 + sub)
        y = _dot(x_ref[rows, :], w_ref[...])
        cos, sin = cos_ref[rows, :], sin_ref[rows, :]
        q_ref[rows, :] = _rope_tile(y[:, :IDX_Q], cos, sin, IDX_DH)
        tail = y[:, IDX_Q:]
        lane = lax.broadcasted_iota(jnp.int32, tail.shape, 1)
        kw_ref[rows, :] = jnp.where(lane < IDX_DH, _rope_tile(tail, cos, sin, IDX_DH), tail)


def _idx_proj(x, w, tabs, *, tm):
    M, K = x.shape
    N = w.shape[1]
    cos, sin = tabs
    npos = cos.shape[0] // tm
    return pl.pallas_call(
        _idx_proj_kernel,
        grid=(M // tm,),
        in_specs=[
            pl.BlockSpec((tm, K), lambda i: (i, 0)),
            pl.BlockSpec((K, N), lambda i: (0, 0)),
            pl.BlockSpec((tm, LANES), lambda i: (i % npos, 0)),
            pl.BlockSpec((tm, LANES), lambda i: (i % npos, 0)),
        ],
        out_specs=[
            pl.BlockSpec((tm, IDX_Q), lambda i: (i, 0)),
            pl.BlockSpec((tm, LANES), lambda i: (i, 0)),
        ],
        out_shape=[
            jax.ShapeDtypeStruct((M, IDX_Q), F32),
            jax.ShapeDtypeStruct((M, LANES), F32),
        ],
        compiler_params=_params(("parallel",)),
        name="proj_idx",
    )(x, w, cos, sin)


def _ret_kernel(lg_ref, gc_ref, q_ref, k_ref, v_ref, g_ref, km_ref, vm_ref, o_ref, st_ref, st0_ref, decay_ref, *, C):
    c = pl.program_id(1)
    dk = lambda h: slice(h * RET_DK, (h + 1) * RET_DK)
    dv = lambda h: slice(h * RET_DV, (h + 1) * RET_DV)

    @pl.when((pl.program_id(0) == 0) & (c == 0))
    def _():
        pm = lax.broadcasted_iota(jnp.int32, (N_META, 1), 0).astype(F32)
        ri = lax.broadcasted_iota(jnp.int32, (C, C), 0)
        ci = lax.broadcasted_iota(jnp.int32, (C, C), 1)
        diff = (ri - ci).astype(F32)
        for h in range(RET_HEADS):
            lg = lg_ref[h]
            zeta_m = jnp.exp(lg * (N_META - 1.0 - pm))
            kz = km_ref[:, dk(h)].astype(F32) * zeta_m
            st0_ref[h] = _dot(kz.T.astype(BF16), vm_ref[:, dv(h)])
            decay_ref[h] = jnp.where(diff >= 0, jnp.exp(lg * jnp.maximum(diff, 0.0)), 0.0)

    @pl.when(c == 0)
    def _():
        st_ref[...] = st0_ref[...]

    pos = lax.broadcasted_iota(jnp.int32, (C, 1), 0).astype(F32)
    for h in range(RET_HEADS):
        lg = lg_ref[h]
        xi = jnp.exp(lg * (pos + 1.0))
        zeta = jnp.exp(lg * (C - 1.0 - pos))
        state = st_ref[h]
        for r0 in range(0, q_ref.shape[1], C):
            rows = slice(r0, r0 + C)
            q = q_ref[0, rows, dk(h)]
            k = k_ref[0, rows, dk(h)]
            v = v_ref[0, rows, dv(h)]
            inner = _dot((_dot_nt(q, k) * decay_ref[h]).astype(BF16), v)
            cross = _dot(q, state.astype(BF16)) * xi
            y = inner + cross
            y = y * lax.rsqrt(jnp.mean(y * y, axis=-1, keepdims=True) + EPS)
            o_ref[0, rows, dv(h)] = (g_ref[0, rows, dv(h)] * y).astype(o_ref.dtype)
            kz = k.astype(F32) * zeta
            state = gc_ref[h] * state + _dot(kz.T.astype(BF16), v)
        st_ref[h] = state


def _retention(lg, gc, rqk, rv, rg, km, vm, *, C, R):
    B, S, _ = rv.shape
    smem = pl.BlockSpec(memory_space=pltpu.SMEM)
    return pl.pallas_call(
        functools.partial(_ret_kernel, C=C),
        grid=(B, S // R),
        in_specs=[
            smem,
            smem,
            pl.BlockSpec((1, R, RET_QK), lambda b, c: (b, c, 0)),
            pl.BlockSpec((1, R, RET_QK), lambda b, c: (b, c, 1)),
            pl.BlockSpec((1, R, RET_V), lambda b, c: (b, c, 0)),
            pl.BlockSpec((1, R, RET_V), lambda b, c: (b, c, 0)),
            pl.BlockSpec((N_META, RET_QK), lambda b, c: (0, 0)),
            pl.BlockSpec((N_META, RET_V), lambda b, c: (0, 0)),
        ],
        out_specs=pl.BlockSpec((1, R, RET_V), lambda b, c: (b, c, 0)),
        out_shape=jax.ShapeDtypeStruct((B, S, RET_V), BF16),
        scratch_shapes=[pltpu.VMEM((RET_HEADS, RET_DK, RET_DV), F32), pltpu.VMEM((RET_HEADS, RET_DK, RET_DV), F32),
                        pltpu.VMEM((RET_HEADS, C, C), F32)],
        compiler_params=_params(("arbitrary", "arbitrary")),
        name="retention",
    )(lg, gc, rqk, rqk, rv, rg, km, vm)


TINY = float(np.finfo(np.float32).tiny)
TN_DIMS = (((0,), (0,)), ((), ()))


def _dsa_kernel(iq_ref, ikw_ref, wq_ref, aq_ref, ak_ref, av_ref, akm_ref, avm_ref, o_ref,
                sc_ref, tau_ref, m_ref, l_ref, acc_ref, s_ref, bm_ref, *, nq, tq, tk, k_top):
    qi = pl.program_id(1)
    nkb = (qi + 1) * (tq // tk)
    kf = float(k_top)
    neg_inf = -jnp.inf
    qpos = qi * tq + lax.broadcasted_iota(jnp.int32, (1, tq), 1)

    part_shape = (SUBLANES, tq)

    def fold(x):
        return x.reshape(tk // SUBLANES, SUBLANES, tq).sum(axis=0)

    idx_scale = (IDX_DH ** -0.5) * (IDX_HEADS ** -0.5)
    iq = iq_ref[0].astype(BF16)
    w_t = wq_ref[0].T
    iq_h = [iq[:, h * IDX_DH:(h + 1) * IDX_DH] for h in range(IDX_HEADS)]
    w_h = [w_t[IDX_DH + h:IDX_DH + h + 1, :] for h in range(IDX_HEADS)]

    def score_rows(kb, n_blocks, carry):
        rmin, rmax = carry
        rows = n_blocks * tk
        ik = ikw_ref[0, pl.ds(pl.multiple_of(kb * tk, tk), rows), :][:, :IDX_DH].astype(BF16)
        acc = jnp.zeros((rows, tq), F32)
        for h in range(IDX_HEADS):
            acc = acc + w_h[h] * jnp.maximum(_dot_nt(ik, iq_h[h]), 0.0)
        ok = (lax.broadcasted_iota(jnp.int32, (rows, 1), 0) + kb * tk) <= qpos
        s = jnp.where(ok, acc * idx_scale, neg_inf)
        for i in range(n_blocks):
            sc_ref[kb + i] = s[i * tk:(i + 1) * tk]
        rmin = jnp.minimum(rmin, jnp.min(jnp.where(ok, s, jnp.inf), axis=0, keepdims=True))
        rmax = jnp.maximum(rmax, jnp.max(s, axis=0, keepdims=True))
        return rmin, rmax

    carry = (jnp.full((1, tq), jnp.inf, F32), jnp.full((1, tq), neg_inf, F32))
    carry = lax.fori_loop(0, nkb // 4, lambda j, c: score_rows(4 * j, 4, c), carry)
    carry = lax.cond(nkb % 4 >= 2, lambda c: score_rows((nkb // 4) * 4, 2, c), lambda c: c, carry)
    rmin, rmax = lax.cond(nkb % 2 == 1, lambda c: score_rows(nkb - 1, 1, c), lambda c: c, carry)

    def any_query(flag):
        return jnp.max(jnp.where(flag, 1.0, 0.0))

    def midpoint(lo, hi):
        a = jnp.maximum(jnp.abs(lo), TINY)
        b = jnp.maximum(jnp.abs(hi), TINY)
        geo = jnp.sqrt(a) * jnp.sqrt(b)
        one_sign = (lo >= 0.0) | (hi <= 0.0)
        far = jnp.maximum(a, b) > 4.0 * jnp.minimum(a, b)
        mid = jnp.where(one_sign & far, jnp.where(lo >= 0.0, geo, -geo), 0.5 * lo + 0.5 * hi)
        mid = jnp.where(lo == 0.0, TINY, mid)
        mid = jnp.where(hi == 0.0, -TINY, mid)
        return jnp.where((lo < 0.0) & (hi > 0.0), 0.0, mid)

    def active(lo, hi, cl):
        mid = midpoint(lo, hi)
        return (cl > kf) & (mid > lo) & (mid < hi)

    def go_on(st):
        return st[3] > 0.0

    def threshold(n_blocks, rmin, rmax):
        blocks = range(n_blocks)

        def over_keys(fn, init, combine):
            acc = init
            for kb in blocks:
                acc = combine(acc, fn(sc_ref[kb]))
            return acc

        def count(pred):
            c = over_keys(lambda s: fold(jnp.where(pred(s), 1.0, 0.0)), jnp.zeros(part_shape, F32), jnp.add)
            return jnp.sum(c, axis=0, keepdims=True)

        c_max = count(lambda s: s >= rmax)
        top_tied = c_max >= kf
        lo = jnp.where(top_tied, rmax, rmin)
        hi = rmax
        cl = jnp.where(top_tied, c_max, (qpos + 1).astype(F32))

        def shrink_once(lo, hi, cl):
            mid = midpoint(lo, hi)
            act = active(lo, hi, cl)
            c = count(lambda s: s >= mid)
            up = act & (c >= kf)
            dn = act & (c < kf)
            return jnp.where(up, mid, lo), jnp.where(dn, mid, hi), jnp.where(up, c, cl)

        def shrink(st):
            lo, hi, cl = shrink_once(*shrink_once(*st[:3]))
            pending = any_query(active(lo, hi, cl))
            return (*shrink_once(lo, hi, cl), pending)

        lo, hi, cl, _ = lax.while_loop(go_on, shrink, (lo, hi, cl, jnp.float32(1.0)))
        tau_ref[...] = lo
        tie = cl > kf

        @pl.when(any_query(tie) > 0.0)
        def _():
            def max_below(bound):
                part = over_keys(
                    lambda s: jnp.max(jnp.where(s < bound, s, neg_inf).reshape(tk // SUBLANES, SUBLANES, tq), axis=0),
                    jnp.full(part_shape, neg_inf, F32), jnp.maximum)
                return jnp.max(part, axis=0, keepdims=True)

            hi_d = jnp.where(lo >= hi, jnp.inf, hi)
            v0 = max_below(hi_d)
            ge0 = count(lambda s: s >= v0)

            def walk_on(st):
                return st[3] > 0.0

            def walk(st):
                hi_w, v, ge, _ = st
                hi_w = jnp.where(tie & (ge < kf), v, hi_w)
                v = max_below(hi_w)
                ge = count(lambda s: s >= v)
                return hi_w, v, ge, any_query(tie & (ge < kf))

            _, v, _, _ = lax.while_loop(walk_on, walk, (hi_d, v0, ge0, any_query(tie & (ge0 < kf))))
            keep = kf - count(lambda s: s > v)
            ri = lax.broadcasted_iota(jnp.int32, (tk, tk), 0)
            ci = lax.broadcasted_iota(jnp.int32, (tk, tk), 1)
            earlier = jnp.where(ci < ri, 1.0, 0.0).astype(BF16)
            seen = jnp.zeros((1, tq), F32)
            for kb in blocks:
                s = sc_ref[kb]
                eq = tie & (s == v)
                eqf = jnp.where(eq, 1.0, 0.0)
                rank = seen + _dot(earlier, eqf.astype(BF16))
                sc_ref[kb] = jnp.where(eq & (rank >= keep), neg_inf, s)
                seen = seen + jnp.sum(eqf, axis=0, keepdims=True)
            tau_ref[...] = jnp.where(tie, v, lo)

        return jnp.int32(0)

    branches = [functools.partial(threshold, (i + 1) * (tq // tk)) for i in range(nq)]
    lax.switch(qi, branches, rmin, rmax)

    tau = tau_ref[...]

    def head(ref, h, rows=None):
        if rows is None:
            return ref[:, h * ATT_DH:(h + 1) * ATT_DH]
        return ref[0, rows, h * ATT_DH:(h + 1) * ATT_DH]

    def score_phase(kb, buf):
        rows = pl.ds(pl.multiple_of(kb * tk, tk), tk)
        bias = jnp.where(sc_ref[kb] >= tau, 0.0, neg_inf)
        for h in range(ATT_HEADS):
            s = _dot_nt(head(ak_ref, h, rows), head(aq_ref, h, slice(None))) + bias
            s_ref[buf, h] = s
            bm_ref[buf, h:h + 1, :] = jnp.max(s, axis=0, keepdims=True)

    def value_phase(kb, buf):
        rows = pl.ds(pl.multiple_of(kb * tk, tk), tk)
        for h in range(ATT_HEADS):
            m_new = jnp.maximum(m_ref[h:h + 1, :], bm_ref[buf, h:h + 1, :])
            p = jnp.exp2(s_ref[buf, h] - m_new)
            p_sum = jnp.sum(p, axis=0, keepdims=True)
            pv = lax.dot_general(head(av_ref, h, rows), p.astype(BF16), TN_DIMS, preferred_element_type=F32)
            alpha = jnp.exp2(m_ref[h:h + 1, :] - m_new)
            l_ref[h:h + 1, :] = alpha * l_ref[h:h + 1, :] + p_sum
            acc_ref[h] = alpha * acc_ref[h] + pv
            m_ref[h:h + 1, :] = m_new

    for h in range(ATT_HEADS):
        s = _dot_nt(head(akm_ref, h), head(aq_ref, h, slice(None)))
        m = jnp.max(s, axis=0, keepdims=True)
        p = jnp.exp2(s - m)
        m_ref[h:h + 1, :] = m
        l_ref[h:h + 1, :] = jnp.sum(p, axis=0, keepdims=True)
        acc_ref[h] = lax.dot_general(head(avm_ref, h), p.astype(BF16), TN_DIMS, preferred_element_type=F32)
    score_phase(0, 1)

    def att_blocks(kb, n):
        for i in range(n):
            buf = 1 - i % 2
            value_phase(kb + i, buf)
            score_phase(jnp.minimum(kb + i + 1, nkb - 1), 1 - buf)

    def att_quad(j, carry):
        att_blocks(4 * j, 4)
        return carry

    lax.fori_loop(0, nkb // 4, att_quad, 0)

    @pl.when(nkb % 4 >= 2)
    def _():
        att_blocks((nkb // 4) * 4, 2)

    @pl.when(nkb % 2 == 1)
    def _():
        value_phase(nkb - 1, 1)

    for h in range(ATT_HEADS):
        out_t = acc_ref[h] / l_ref[h:h + 1, :]
        o_ref[0, :, h * ATT_DH:(h + 1) * ATT_DH] = out_t.T.astype(o_ref.dtype)


def _dsa(iq, ikw, aqk, av, akm, avm, *, tq, tk, k_top):
    B, S, _ = av.shape
    nq = S // tq
    assert tq % tk == 0
    return pl.pallas_call(
        functools.partial(_dsa_kernel, nq=nq, tq=tq, tk=tk, k_top=k_top),
        grid=(B, nq),
        in_specs=[
            pl.BlockSpec((1, tq, IDX_Q), lambda b, q: (b, q, 0)),
            pl.BlockSpec((1, S, LANES), lambda b, q: (b, 0, 0)),
            pl.BlockSpec((1, tq, LANES), lambda b, q: (b, q, 0)),
            pl.BlockSpec((1, tq, ATT_W), lambda b, q: (b, q, 0)),
            pl.BlockSpec((1, S, ATT_W), lambda b, q: (b, 0, 1)),
            pl.BlockSpec((1, S, ATT_W), lambda b, q: (b, 0, 0)),
            pl.BlockSpec((N_META, ATT_W), lambda b, q: (0, 0)),
            pl.BlockSpec((N_META, ATT_W), lambda b, q: (0, 0)),
        ],
        out_specs=pl.BlockSpec((1, tq, ATT_W), lambda b, q: (b, q, 0)),
        out_shape=jax.ShapeDtypeStruct((B, S, ATT_W), BF16),
        scratch_shapes=[
            pltpu.VMEM((S // tk, tk, tq), F32),
            pltpu.VMEM((1, tq), F32),
            pltpu.VMEM((ATT_HEADS, tq), F32),
            pltpu.VMEM((ATT_HEADS, tq), F32),
            pltpu.VMEM((ATT_HEADS, ATT_DH, tq), F32),
            pltpu.VMEM((2, ATT_HEADS, tk, tq), F32),
            pltpu.VMEM((2, ATT_HEADS, tq), F32),
        ],
        compiler_params=_params(("parallel", "arbitrary")),
        name="dsa",
    )(iq, ikw, ikw, aqk, aqk, av, akm, avm)


def _merge_kernel(h_ref, yr_ref, ya_ref, gr_ref, ga_ref, wr_ref, wa_ref, wm_ref, o_ref):
    yr = _dot(yr_ref[...], wr_ref[...])
    ya = _dot(ya_ref[...], wa_ref[...])
    merged = _sigmoid(gr_ref[...].astype(F32)) * yr + _sigmoid(ga_ref[...].astype(F32)) * ya
    o_ref[...] = h_ref[...] + _dot(merged.astype(BF16), wm_ref[...])


def _merge(h, yr, ya, gates, wr, wa, wm, *, tm):
    M, D = h.shape
    row = lambda i: (i, 0)
    fixed = lambda i: (0, 0)
    return pl.pallas_call(
        _merge_kernel,
        grid=(M // tm,),
        in_specs=[
            pl.BlockSpec((tm, D), row),
            pl.BlockSpec((tm, RET_V), row),
            pl.BlockSpec((tm, ATT_W), row),
            pl.BlockSpec((tm, D), lambda i: (i, 0)),
            pl.BlockSpec((tm, D), lambda i: (i, 1)),
            pl.BlockSpec((RET_V, D), fixed),
            pl.BlockSpec((ATT_W, D), fixed),
            pl.BlockSpec((D, D), fixed),
        ],
        out_specs=pl.BlockSpec((tm, D), row),
        out_shape=jax.ShapeDtypeStruct((M, D), F32),
        compiler_params=_params(("parallel",)),
        name="merge",
    )(h, yr, ya, gates, gates, wr, wa, wm)


def _rope_tables(pos, d, width):
    inv_freq = ROPE_THETA ** (-jnp.arange(0, d, 2, dtype=F32) / d)
    ang = pos.astype(F32)[:, None] * inv_freq[None, :]
    cos, sin = jnp.cos(ang), jnp.sin(ang)
    cos = jnp.concatenate([cos, cos], axis=-1)
    sin = jnp.concatenate([-sin, sin], axis=-1)
    rep = width // d
    return jnp.tile(cos, (1, rep)), jnp.tile(sin, (1, rep))


def _tile(n, cap):
    t = min(n, cap)
    assert n % t == 0, (n, t)
    return t


class _Plan(NamedTuple):
    rows: int
    rows_tab: int
    rows_wide: int
    rows_merge: int
    cols: int
    ffn_slab: int
    att: int
    chunk: int
    ret_rows: int


def _plan(S, M):
    return _Plan(rows=_tile(S, 4 * MXU_DIM), rows_tab=_tile(S, 8 * MXU_DIM), rows_wide=_tile(M, 8 * MXU_DIM),
                 rows_merge=_tile(M, 2 * MXU_DIM), cols=4 * MXU_DIM, ffn_slab=MXU_DIM, att=_tile(S, MXU_DIM),
                 chunk=_tile(S, 4 * RET_CHUNK), ret_rows=_tile(S, 4 * RET_CHUNK))


def kernel(x, meta_tokens, ffn1_norm, ffn1_w_gate, ffn1_w_up, ffn1_w_down, mix_norm, w_in, w_ret_out,
           w_att_out, w_mix_out, ffn2_norm, ffn2_w_gate, ffn2_w_up, ffn2_w_down, final_norm):
    B, S, D = x.shape
    assert ffn1_norm.shape[0] == 1, "single layer only"
    assert meta_tokens.shape[0] == N_META
    k_top = min(TOPK_MAX, S // 4)
    M = B * S
    plan = _plan(S, M)
    bf = lambda a: a.astype(BF16)
    row = lambda a: a.reshape(1, -1)

    w32 = w_in[0].T
    offs = np.cumsum([0, RET_QK, RET_QK, RET_V, RET_V, ATT_W, ATT_W, ATT_W, IDX_Q, IDX_DH, IDX_HEADS, D, D])
    tn = plan.cols
    assert all(int(o) % tn == 0 for o in offs[:8])
    c_rq, c_rk, c_rv, c_rg, c_aq, c_ak, c_av = (int(o) // tn for o in offs[:7])
    w_idx = jnp.pad(bf(w32[offs[7]:offs[10]].T), ((0, 0), (0, LANES - IDX_DH - IDX_HEADS)))
    w_gate = w32[offs[10]:offs[12]]

    pos_meta = jnp.arange(N_META, dtype=jnp.int32)
    pos_real = N_META + jnp.arange(S, dtype=jnp.int32)
    ffn1 = (row(ffn1_norm[0]), bf(ffn1_w_gate[0]), bf(ffn1_w_up[0]), bf(ffn1_w_down[0]), row(mix_norm[0]))
    ffn2 = (row(ffn2_norm[0]), bf(ffn2_w_gate[0]), bf(ffn2_w_up[0]), bf(ffn2_w_down[0]), row(final_norm))
    tf = plan.ffn_slab
    k_scale = RET_DK ** -0.5

    _, un_m = _ffn(meta_tokens.astype(F32), *ffn1, tm=N_META, tf=tf, emit_h=True, n_dtype=BF16)
    pm = dict(tm=N_META, tn=tn)
    rk_m = _proj(un_m, w32, col0=c_rk, n_tiles=RET_QK // tn, mode="rope", d=RET_DK, k_scale=k_scale,
                 tabs=_rope_tables(pos_meta, RET_DK, RET_DK), **pm)
    rv_m = _proj(un_m, w32, col0=c_rv, n_tiles=RET_V // tn, mode="plain", **pm)
    ak_m = _proj(un_m, w32, col0=c_ak, n_tiles=ATT_W // tn, mode="rope", d=ATT_DH,
                 tabs=_rope_tables(pos_meta, ATT_DH, LANES), **pm)
    av_m = _proj(un_m, w32, col0=c_av, n_tiles=ATT_W // tn, mode="plain", **pm)

    tm = plan.rows
    h1, un = _ffn(x.reshape(M, D), *ffn1, tm=tm, tf=tf, emit_h=True, n_dtype=BF16)
    pr = dict(tm=plan.rows_tab, tn=tn)
    pw = dict(tm=plan.rows_wide, tn=tn)
    rqk = _proj(un, w32, col0=c_rq, n_tiles=2 * RET_QK // tn, mode="rope", d=RET_DK, k_scale=k_scale,
                n_q=RET_QK // tn, tabs=_rope_tables(pos_real, RET_DK, RET_DK), **pr)
    rv = _proj(un, w32, col0=c_rv, n_tiles=RET_V // tn, mode="plain", **pw)
    rg = _proj(un, w32, col0=c_rg, n_tiles=RET_V // tn, mode="silu", **pw)
    aqk = _proj(un, w32, col0=c_aq, n_tiles=2 * ATT_W // tn, mode="rope", d=ATT_DH, q_scale=ATT_EXP_SCALE,
                n_q=ATT_W // tn, tabs=_rope_tables(pos_real, ATT_DH, LANES), **pr)
    av = _proj(un, w32, col0=c_av, n_tiles=ATT_W // tn, mode="plain", **pw)
    gates = _proj(un, w_gate, col0=0, n_tiles=2 * D // tn, mode="plain", **pw)
    iq, ikw = _idx_proj(un, w_idx, _rope_tables(pos_real, IDX_DH, LANES), tm=tm)

    heads = jnp.arange(RET_HEADS, dtype=F32)
    log_g = jnp.log1p(-(2.0 ** (-5.0 - heads)))
    C = plan.chunk
    b3 = lambda a: a.reshape(B, S, a.shape[-1])
    yr = _retention(log_g, jnp.exp(log_g * C), b3(rqk), b3(rv), b3(rg), rk_m, rv_m, C=C, R=plan.ret_rows)
    ya = _dsa(b3(iq), b3(ikw), b3(aqk), b3(av), ak_m, av_m, tq=plan.att, tk=plan.att, k_top=k_top)

    h2 = _merge(h1, yr.reshape(M, RET_V), ya.reshape(M, ATT_W), gates, bf(w_ret_out[0]), bf(w_att_out[0]),
                bf(w_mix_out[0]), tm=plan.rows_merge)
    (y,) = _ffn(h2, *ffn2, tm=tm, tf=tf, emit_h=False, n_dtype=F32)
    return y.reshape(B, S, D)
```
